```python
import math
import jax, jax.numpy as jnp
from jax import lax
import numpy as np


D_MODEL = 2048
BATCH = 4
SEQ = 8192
DEPTH = 2
DEC_BATCH = 1
DEC_SEQ = 16384
PAST_LEN = 128

N_EVEN = (DEPTH + 1) // 2
N_ODD = DEPTH // 2
EPS = 1e-6
A_WIDTH = D_MODEL // 2
A_HEAD = 64
A_HEADS = A_WIDTH // A_HEAD
A_DECAY_LORA = 64
A_ICL_LORA = 64
A_GATE_LORA = 128
A_COLS = 3 * A_WIDTH + 2 * A_DECAY_LORA + 2 * A_ICL_LORA + A_GATE_LORA
A_SPLITS = (A_WIDTH, 2 * A_WIDTH, 3 * A_WIDTH, 3 * A_WIDTH + 2 * A_DECAY_LORA, 3 * A_WIDTH + 2 * A_DECAY_LORA + 2 * A_ICL_LORA)
RWKV_GN_EPS = 64e-5
DECAY_SCALE = math.exp(-0.5)
B_WIDTH = D_MODEL - A_WIDTH
B_GROUP = 16
B_GROUPS = B_WIDTH // B_GROUP
B_STATE = 64
EVEN_IN = A_COLS + B_WIDTH
C_HEADS = 8
C_NOPE = 64
C_ROPE = 32
C_V = 128
C_Q_RANK = 512
C_KV_RANK = 256
ROPE_THETA = 10000.0
D_HEADS = 8
D_HEAD = 64
D_V = 2 * D_HEAD
SUBLN_EPS = 1e-5
N_BUCKETS = 32
MAX_DISTANCE = 128
ODD_SPLITS = (C_Q_RANK, C_Q_RANK + C_KV_RANK, C_Q_RANK + C_KV_RANK + C_ROPE, C_Q_RANK + C_KV_RANK + C_ROPE + D_HEADS * 2 * D_HEAD, C_Q_RANK + C_KV_RANK + C_ROPE + 2 * D_HEADS * 2 * D_HEAD)
ODD_IN = C_Q_RANK + C_KV_RANK + C_ROPE + 2 * D_HEADS * 2 * D_HEAD + D_HEADS * D_V
Q_BLOCK = 128
FFN_HIDDEN = 5632
N_MOD = 6

kernel_name = 'hybrid_bidir_rwkv7_s5_mla_diffattn_convffn_adaln'


def _f32(t):
    return t.astype(jnp.float32)


def rmsnorm(x, g, eps=EPS):
    xf = _f32(x)
    y = xf * lax.rsqrt(jnp.mean(xf * xf, axis=-1, keepdims=True) + eps)
    return (y * _f32(g)).astype(x.dtype)


def centred_shift(x):
    xp = jnp.pad(x, ((0, 0), (1, 1), (0, 0)))
    return 0.5 * (xp[:, :-2] + xp[:, 2:])


def to_blocks(t):
    b, s = t.shape[0], t.shape[1]
    return jnp.moveaxis(t.reshape((b, s // Q_BLOCK, Q_BLOCK) + t.shape[2:]), 1, 0)


def from_blocks(t):
    t = jnp.moveaxis(t, 0, 1)
    return t.reshape((t.shape[0], t.shape[1] * t.shape[2]) + t.shape[3:])


def rwkv7_scan(r, w, k, v, kk, a, reverse):
    bsz, _, h, n = r.shape
    b = a * kk
    xs = tuple(jnp.moveaxis(t, 1, 0) for t in (r, w, k, v, kk, b))

    def step(state, inp):
        r_t, w_t, k_t, v_t, kk_t, b_t = inp
        sa = jnp.einsum('bhvk,bhk->bhv', state, kk_t)
        state = state * w_t[:, :, None, :] - sa[..., None] * b_t[:, :, None, :] + v_t[..., None] * k_t[:, :, None, :]
        return state, jnp.einsum('bhvk,bhk->bhv', state, r_t)

    s0 = jnp.zeros((bsz, h, n, n), jnp.float32)
    _, ys = lax.scan(step, s0, xs, reverse=reverse)
    return jnp.moveaxis(ys, 0, 1)


def rwkv7_mixer(pa, mu, w0, w_up, a0, a_up, g_up, k_k, k_a, r_k, lnx_g, lnx_b):
    bsz, s, _ = pa.shape
    pa = pa + mu * (centred_shift(pa) - pa)
    r, k, v, dw, da, dg = jnp.split(_f32(pa), A_SPLITS, axis=-1)
    dw = dw.reshape(bsz, s, 2, A_DECAY_LORA)
    da = da.reshape(bsz, s, 2, A_ICL_LORA)
    decay = jnp.exp(-DECAY_SCALE * jax.nn.sigmoid(_f32(w0) + jnp.einsum('bsdr,drc->bsdc', jnp.tanh(dw), _f32(w_up))))
    icl = jax.nn.sigmoid(_f32(a0) + jnp.einsum('bsdr,drc->bsdc', da, _f32(a_up)))
    g = jax.nn.sigmoid(dg) @ _f32(g_up)

    def heads(t):
        return t.reshape(bsz, s, A_HEADS, A_HEAD)

    kk = heads(k * _f32(k_k))
    kk = kk / jnp.maximum(jnp.sqrt(jnp.sum(kk * kk, axis=-1, keepdims=True)), 1e-12)
    r_h, v_h = heads(r), heads(v)
    y = jnp.zeros_like(r_h)
    bonus = jnp.zeros(r_h.shape[:-1] + (1,), jnp.float32)
    for d in range(2):
        k_dh = heads(k * (1.0 + (icl[:, :, d] - 1.0) * _f32(k_a)))
        y = y + rwkv7_scan(r_h, heads(decay[:, :, d]), k_dh, v_h, kk, heads(icl[:, :, d]), reverse=(d == 1))
        bonus = bonus + jnp.sum(r_h * k_dh * _f32(r_k), axis=-1, keepdims=True)
    mean = jnp.mean(y, axis=-1, keepdims=True)
    var = jnp.mean(jnp.square(y - mean), axis=-1, keepdims=True)
    y = ((y - mean) * lax.rsqrt(var + RWKV_GN_EPS)).reshape(bsz, s, A_WIDTH) * _f32(lnx_g) + _f32(lnx_b)
    y = y + (bonus * v_h).reshape(bsz, s, A_WIDTH)
    return (y * g).astype(pa.dtype)


def _complex_combine(e1, e2):
    a1r, a1i, b1r, b1i = e1
    a2r, a2i, b2r, b2i = e2
    return (a2r * a1r - a2i * a1i,
            a2r * a1i + a2i * a1r,
            a2r * b1r - a2i * b1i + b2r,
            a2r * b1i + a2i * b1r + b2i)


def s5_mixer(u, lam_re, lam_im, log_step, b_re, b_im, c_re, c_im, d_skip, w_glu, b_glu):
    bsz, s, _ = u.shape
    uf = _f32(u).reshape(bsz, s, B_GROUPS, B_GROUP)
    y = uf * _f32(d_skip).reshape(B_GROUPS, B_GROUP)
    for d in range(2):
        lr, li = _f32(lam_re[d]), _f32(lam_im[d])
        step = jnp.exp(_f32(log_step[d]))[:, None]
        mag = jnp.exp(lr * step)
        ar, ai = mag * jnp.cos(li * step), mag * jnp.sin(li * step)
        den = lr * lr + li * li
        nr, ni = ar - 1.0, ai
        fr, fi = (nr * lr + ni * li) / den, (ni * lr - nr * li) / den
        br, bi = _f32(b_re[d]), _f32(b_im[d])
        bbr = fr[..., None] * br - fi[..., None] * bi
        bbi = fr[..., None] * bi + fi[..., None] * br
        xr = jnp.einsum('bsgc,gpc->bsgp', uf, bbr)
        xi = jnp.einsum('bsgc,gpc->bsgp', uf, bbi)
        a_r = jnp.broadcast_to(ar, (1, s, B_GROUPS, B_STATE))
        a_i = jnp.broadcast_to(ai, (1, s, B_GROUPS, B_STATE))
        _, _, hr, hi = lax.associative_scan(_complex_combine, (a_r, a_i, xr, xi), reverse=(d == 1), axis=1)
        y = y + jnp.einsum('bsgp,gcp->bsgc', hr, _f32(c_re[d])) - jnp.einsum('bsgp,gcp->bsgc', hi, _f32(c_im[d]))
    z = jax.nn.gelu(y.reshape(bsz, s, B_WIDTH))
    z = z * jax.nn.sigmoid(z @ _f32(w_glu) + _f32(b_glu))
    return z.astype(u.dtype)


def rope_tables(s, dim):
    inv = 1.0 / (ROPE_THETA ** (jnp.arange(0, dim, 2, dtype=jnp.float32) / dim))
    ang = jnp.arange(s, dtype=jnp.float32)[:, None] * inv[None, :]
    return jnp.cos(ang), jnp.sin(ang)


def apply_rope(x, cos, sin):
    x1, x2 = jnp.split(_f32(x), 2, axis=-1)
    return jnp.concatenate([x1 * cos - x2 * sin, x1 * sin + x2 * cos], axis=-1).astype(x.dtype)


def mla_mixer(cq, ckv, kr, q_norm_g, kv_norm_g, w_uq, w_ukv):
    bsz, s, _ = cq.shape
    q = (rmsnorm(cq, q_norm_g) @ w_uq).reshape(bsz, s, C_HEADS, C_NOPE + C_ROPE)
    q_nope, q_rope = q[..., :C_NOPE], q[..., C_NOPE:]
    kv = (rmsnorm(ckv, kv_norm_g) @ w_ukv).reshape(bsz, s, C_HEADS, C_NOPE + C_V)
    k_nope, v = kv[..., :C_NOPE], kv[..., C_NOPE:]
    cos, sin = rope_tables(s, C_ROPE)
    q_rope = apply_rope(q_rope, cos[:, None, :], sin[:, None, :])
    k_rope = apply_rope(kr, cos, sin)
    scale = (C_NOPE + C_ROPE) ** -0.5

    def block(args):
        qn, qr = args
        sc = jnp.einsum('bqhd,bkhd->bhqk', qn, k_nope) + jnp.einsum('bqhd,bkd->bhqk', qr, k_rope)
        p = jax.nn.softmax(_f32(sc) * scale, axis=-1)
        return jnp.einsum('bhqk,bkhd->bqhd', p.astype(v.dtype), v)

    out = from_blocks(lax.map(block, (to_blocks(q_nope), to_blocks(q_rope))))
    return out.reshape(bsz, s, C_HEADS * C_V)


def t5_bucket(rel):
    half = N_BUCKETS // 2
    max_exact = half // 2
    n = jnp.abs(rel)
    large = max_exact + (jnp.log(jnp.maximum(n, 1).astype(jnp.float32) / max_exact) / math.log(MAX_DISTANCE / max_exact) * (half - max_exact)).astype(jnp.int32)
    large = jnp.minimum(large, half - 1)
    return jnp.where(rel > 0, half, 0) + jnp.where(n < max_exact, n, large)


def diff_mixer(q, k, v, rel_bias, lq1, lk1, lq2, lk2, subln_g, lambda_init):
    bsz, s = q.shape[0], q.shape[1]
    lam = jnp.exp(jnp.sum(_f32(lq1) * _f32(lk1))) - jnp.exp(jnp.sum(_f32(lq2) * _f32(lk2))) + lambda_init
    scale = D_HEAD ** -0.5
    kpos = jnp.arange(s, dtype=jnp.int32)

    def block(args):
        qb, blk = args
        qpos = blk * Q_BLOCK + jnp.arange(Q_BLOCK, dtype=jnp.int32)
        bias = jnp.moveaxis(_f32(rel_bias)[t5_bucket(kpos[None, :] - qpos[:, None])], -1, 0)
        sc = _f32(jnp.einsum('bqhmd,bkhmd->bhmqk', qb, k)) * scale + bias[None, :, None]
        p = jax.nn.softmax(sc, axis=-1)
        attn = p[:, :, 0] - lam * p[:, :, 1]
        return jnp.einsum('bhqk,bkhd->bqhd', attn.astype(v.dtype), v)

    nb = s // Q_BLOCK
    out = from_blocks(lax.map(block, (to_blocks(q), jnp.arange(nb, dtype=jnp.int32))))
    out = rmsnorm(out, subln_g, SUBLN_EPS) * (1.0 - lambda_init)
    return out.reshape(bsz, s, D_HEADS * D_V)


def conv_ffn(h, w_up, conv_w, conv_b, w_down):
    u = h @ w_up
    up = jnp.pad(u, ((0, 0), (1, 1), (0, 0)))
    u = up[:, :-2] * conv_w[0] + up[:, 1:-1] * conv_w[1] + up[:, 2:] * conv_w[2] + conv_b
    val, gate = jnp.split(u, 2, axis=-1)
    return (jax.nn.silu(gate) * val) @ w_down


def trunk(x, c, ada_w, ada_b, norm1_g, norm2_g, even_w_in, even_w_out,
          rwkv_mu, rwkv_w0, rwkv_w_up, rwkv_a0, rwkv_a_up, rwkv_g_up, rwkv_k_k, rwkv_k_a, rwkv_r_k, rwkv_lnx_g, rwkv_lnx_b,
          s5_lam_re, s5_lam_im, s5_log_step, s5_b_re, s5_b_im, s5_c_re, s5_c_im, s5_d, s5_w_glu, s5_b_glu,
          odd_w_in, odd_w_out, mla_q_norm_g, mla_kv_norm_g, mla_w_uq, mla_w_ukv,
          diff_lq1, diff_lk1, diff_lq2, diff_lk2, diff_subln_g, rel_bias,
          ffn_w_up, ffn_conv_w, ffn_conv_b, ffn_w_down, final_g):
    cs = jax.nn.silu(c)
    for i in range(DEPTH):
        mod = cs @ ada_w[i] + ada_b[i]
        sh1, sc1, g1, sh2, sc2, g2 = [m[:, None, :] for m in jnp.split(mod, N_MOD, axis=-1)]
        h = rmsnorm(x, norm1_g[i]) * (1.0 + sc1) + sh1
        j = i // 2
        if i % 2 == 0:
            p = h @ even_w_in[j]
            ya = rwkv7_mixer(p[..., :A_COLS], rwkv_mu[j], rwkv_w0[j], rwkv_w_up[j], rwkv_a0[j], rwkv_a_up[j], rwkv_g_up[j],
                             rwkv_k_k[j], rwkv_k_a[j], rwkv_r_k[j], rwkv_lnx_g[j], rwkv_lnx_b[j])
            yb = s5_mixer(p[..., A_COLS:], s5_lam_re[j], s5_lam_im[j], s5_log_step[j], s5_b_re[j], s5_b_im[j],
                          s5_c_re[j], s5_c_im[j], s5_d[j], s5_w_glu[j], s5_b_glu[j])
            mix = jnp.concatenate([ya, yb], axis=-1) @ even_w_out[j]
        else:
            p = h @ odd_w_in[j]
            bsz, s = p.shape[0], p.shape[1]
            cq, ckv, kr, dq, dk, dv = jnp.split(p, ODD_SPLITS, axis=-1)
            yc = mla_mixer(cq, ckv, kr, mla_q_norm_g[j], mla_kv_norm_g[j], mla_w_uq[j], mla_w_ukv[j])
            yd = diff_mixer(dq.reshape(bsz, s, D_HEADS, 2, D_HEAD), dk.reshape(bsz, s, D_HEADS, 2, D_HEAD),
                            dv.reshape(bsz, s, D_HEADS, D_V), rel_bias, diff_lq1[j], diff_lk1[j], diff_lq2[j], diff_lk2[j],
                            diff_subln_g[j], 0.8 - 0.6 * math.exp(-0.3 * i))
            mix = jnp.concatenate([yc, yd], axis=-1) @ odd_w_out[j]
        x = x + g1 * mix
        h = rmsnorm(x, norm2_g[i]) * (1.0 + sc2) + sh2
        x = x + g2 * conv_ffn(h, ffn_w_up[i], ffn_conv_w[i], ffn_conv_b[i], ffn_w_down[i])
    return rmsnorm(x, final_g)


def setup_inputs(seed: int = 0) -> dict:
    key = jax.random.key(seed)
    ks = iter(jax.random.split(key, 64))

    def nrm(shape, s):
        return jax.random.normal(next(ks), shape, jnp.float32) * s

    def uni(shape, lo, hi):
        return jax.random.uniform(next(ks), shape, jnp.float32, lo, hi)

    D = D_MODEL
    inp = {}
    inp['x_prompt'] = nrm((BATCH, SEQ, D), 1.0)
    inp['x_sample'] = nrm((DEC_BATCH, DEC_SEQ, D), 1.0)
    inp['c_prompt'] = nrm((BATCH, D), 1.0)
    inp['c_sample'] = nrm((DEC_BATCH, D), 1.0)
    inp['ada_w'] = nrm((DEPTH, D, N_MOD * D), 0.5 * D ** -0.5)
    inp['ada_b'] = nrm((DEPTH, N_MOD * D), 0.02)
    inp['norm1_g'] = 1.0 + nrm((DEPTH, D), 0.02)
    inp['norm2_g'] = 1.0 + nrm((DEPTH, D), 0.02)
    inp['even_w_in'] = nrm((N_EVEN, D, EVEN_IN), D ** -0.5)
    inp['even_w_out'] = nrm((N_EVEN, A_WIDTH + B_WIDTH, D), (A_WIDTH + B_WIDTH) ** -0.5)
    inp['rwkv_mu'] = uni((N_EVEN, A_COLS), 0.0, 1.0)
    inp['rwkv_w0'] = uni((N_EVEN, 2, A_WIDTH), -6.0, 1.0)
    inp['rwkv_w_up'] = nrm((N_EVEN, 2, A_DECAY_LORA, A_WIDTH), 0.1)
    inp['rwkv_a0'] = nrm((N_EVEN, 2, A_WIDTH), 0.1)
    inp['rwkv_a_up'] = nrm((N_EVEN, 2, A_ICL_LORA, A_WIDTH), 0.5 * A_ICL_LORA ** -0.5)
    inp['rwkv_g_up'] = nrm((N_EVEN, A_GATE_LORA, A_WIDTH), A_GATE_LORA ** -0.5)
    inp['rwkv_k_k'] = 0.85 + nrm((N_EVEN, A_WIDTH), 0.05)
    inp['rwkv_k_a'] = 1.0 + nrm((N_EVEN, A_WIDTH), 0.05)
    inp['rwkv_r_k'] = nrm((N_EVEN, A_HEADS, A_HEAD), 0.1)
    inp['rwkv_lnx_g'] = 1.0 + nrm((N_EVEN, A_WIDTH), 0.02)
    inp['rwkv_lnx_b'] = nrm((N_EVEN, A_WIDTH), 0.02)
    inp['s5_lam_re'] = -0.5 + nrm((N_EVEN, 2, B_GROUPS, B_STATE), 0.01)
    inp['s5_lam_im'] = jnp.pi * jnp.arange(B_STATE, dtype=jnp.float32) + nrm((N_EVEN, 2, B_GROUPS, B_STATE), 0.01)
    inp['s5_log_step'] = uni((N_EVEN, 2, B_GROUPS), math.log(1e-3), math.log(1e-1))
    inp['s5_b_re'] = nrm((N_EVEN, 2, B_GROUPS, B_STATE, B_GROUP), (2 * B_GROUP) ** -0.5)
    inp['s5_b_im'] = nrm((N_EVEN, 2, B_GROUPS, B_STATE, B_GROUP), (2 * B_GROUP) ** -0.5)
    inp['s5_c_re'] = nrm((N_EVEN, 2, B_GROUPS, B_GROUP, B_STATE), B_STATE ** -0.5)
    inp['s5_c_im'] = nrm((N_EVEN, 2, B_GROUPS, B_GROUP, B_STATE), B_STATE ** -0.5)
    inp['s5_d'] = nrm((N_EVEN, B_WIDTH), 1.0)
    inp['s5_w_glu'] = nrm((N_EVEN, B_WIDTH, B_WIDTH), B_WIDTH ** -0.5)
    inp['s5_b_glu'] = nrm((N_EVEN, B_WIDTH), 0.02)
    inp['odd_w_in'] = nrm((N_ODD, D, ODD_IN), D ** -0.5)
    inp['odd_w_out'] = nrm((N_ODD, C_HEADS * C_V + D_HEADS * D_V, D), (C_HEADS * C_V + D_HEADS * D_V) ** -0.5)
    inp['mla_q_norm_g'] = 1.0 + nrm((N_ODD, C_Q_RANK), 0.02)
    inp['mla_kv_norm_g'] = 1.0 + nrm((N_ODD, C_KV_RANK), 0.02)
    inp['mla_w_uq'] = nrm((N_ODD, C_Q_RANK, C_HEADS * (C_NOPE + C_ROPE)), C_Q_RANK ** -0.5)
    inp['mla_w_ukv'] = nrm((N_ODD, C_KV_RANK, C_HEADS * (C_NOPE + C_V)), C_KV_RANK ** -0.5)
    inp['diff_lq1'] = nrm((N_ODD, D_HEAD), 0.1)
    inp['diff_lk1'] = nrm((N_ODD, D_HEAD), 0.1)
    inp['diff_lq2'] = nrm((N_ODD, D_HEAD), 0.1)
    inp['diff_lk2'] = nrm((N_ODD, D_HEAD), 0.1)
    inp['diff_subln_g'] = 1.0 + nrm((N_ODD, D_V), 0.02)
    inp['rel_bias'] = nrm((N_BUCKETS, D_HEADS), 0.5)
    inp['ffn_w_up'] = nrm((DEPTH, D, 2 * FFN_HIDDEN), D ** -0.5)
    inp['ffn_conv_w'] = jnp.array([0.25, 0.5, 0.25], jnp.float32)[None, :, None] + nrm((DEPTH, 3, 2 * FFN_HIDDEN), 0.1)
    inp['ffn_conv_b'] = nrm((DEPTH, 2 * FFN_HIDDEN), 0.02)
    inp['ffn_w_down'] = nrm((DEPTH, FFN_HIDDEN, D), FFN_HIDDEN ** -0.5)
    inp['final_g'] = 1.0 + nrm((D,), 0.02)
    return inp


def reference(x_prompt, x_sample, c_prompt, c_sample, ada_w, ada_b, norm1_g, norm2_g, even_w_in, even_w_out,
              rwkv_mu, rwkv_w0, rwkv_w_up, rwkv_a0, rwkv_a_up, rwkv_g_up, rwkv_k_k, rwkv_k_a, rwkv_r_k, rwkv_lnx_g, rwkv_lnx_b,
              s5_lam_re, s5_lam_im, s5_log_step, s5_b_re, s5_b_im, s5_c_re, s5_c_im, s5_d, s5_w_glu, s5_b_glu,
              odd_w_in, odd_w_out, mla_q_norm_g, mla_kv_norm_g, mla_w_uq, mla_w_ukv,
              diff_lq1, diff_lk1, diff_lq2, diff_lk2, diff_subln_g, rel_bias,
              ffn_w_up, ffn_conv_w, ffn_conv_b, ffn_w_down, final_g):
    weights = (ada_w, ada_b, norm1_g, norm2_g, even_w_in, even_w_out,
               rwkv_mu, rwkv_w0, rwkv_w_up, rwkv_a0, rwkv_a_up, rwkv_g_up, rwkv_k_k, rwkv_k_a, rwkv_r_k, rwkv_lnx_g, rwkv_lnx_b,
               s5_lam_re, s5_lam_im, s5_log_step, s5_b_re, s5_b_im, s5_c_re, s5_c_im, s5_d, s5_w_glu, s5_b_glu,
               odd_w_in, odd_w_out, mla_q_norm_g, mla_kv_norm_g, mla_w_uq, mla_w_ukv,
               diff_lq1, diff_lk1, diff_lq2, diff_lk2, diff_subln_g, rel_bias,
               ffn_w_up, ffn_conv_w, ffn_conv_b, ffn_w_down, final_g)
    y_prompt = trunk(x_prompt, c_prompt, *weights)
    y_sample = trunk(x_sample, c_sample, *weights)
    return (y_prompt, y_sample)
```

```python
import functools
import math

import jax
import jax.numpy as jnp
from jax import lax
from jax.experimental import pallas as pl
from jax.experimental.pallas import tpu as pltpu

F32 = jnp.float32
BF16 = jnp.bfloat16

D_MODEL = 2048
DEPTH = 2
EPS = 1e-6
A_WIDTH = 1024
A_HEAD = 64
A_HEADS = 16
A_LORA = 64
A_GATE_LORA = 128
A_COLS = 3 * A_WIDTH + 2 * A_LORA + 2 * A_LORA + A_GATE_LORA
RWKV_GN_EPS = 64e-5
DECAY_SCALE = math.exp(-0.5)
B_WIDTH = 1024
B_GROUP = 16
B_GROUPS = 64
B_STATE = 64
C_HEADS = 8
C_NOPE = 64
C_ROPE = 32
C_V = 128
C_Q_RANK = 512
C_KV_RANK = 256
ROPE_THETA = 10000.0
D_HEADS = 8
D_HEAD = 64
D_V = 128
SUBLN_EPS = 1e-5
N_BUCKETS = 32
MAX_DISTANCE = 128
FFN_HIDDEN = 5632
N_MOD = 6

LANES = 128
SUBLANES = 8
VMEM_LIMIT_BYTES = 56 * 1024 * 1024

RWKV_CHUNK = 64
S5_CHUNK = 16
LOG2E = 1.4426950408889634

NN = (((1,), (0,)), ((), ()))
NT = (((1,), (1,)), ((), ()))
BATCH_NN = (((2,), (1,)), ((0,), (0,)))


def _params(*sem):
    return pltpu.CompilerParams(dimension_semantics=sem, vmem_limit_bytes=VMEM_LIMIT_BYTES)


def _dot(a, b, dims=NN):
    return lax.dot_general(a, b, dims, preferred_element_type=F32)


def _split2(x):
    hi = x.astype(BF16)
    lo = (x - hi.astype(F32)).astype(BF16)
    return hi, lo


def _dot3(a, b, dims=NN):
    ah, al = _split2(a)
    bh, bl = _split2(b)
    return _dot(ah, bh, dims) + _dot(ah, bl, dims) + _dot(al, bh, dims)


def _dot_exact_rhs(a, b):
    ah, al = _split2(a)
    return _dot(ah, b) + _dot(al, b)


def _ada_kernel(c_ref, w_ref, b_ref, o_ref):
    c = c_ref[...]
    cs = c * jax.nn.sigmoid(c)
    o_ref[0] = _dot(cs.astype(BF16), w_ref[0].astype(BF16)) + b_ref[0]


def _ada_mod(c_all, ada_w, ada_b):
    n = N_MOD * D_MODEL
    tn = 1024
    return pl.pallas_call(
        _ada_kernel,
        grid=(DEPTH, n // tn),
        in_specs=[
            pl.BlockSpec((SUBLANES, D_MODEL), lambda l, j: (0, 0)),
            pl.BlockSpec((1, D_MODEL, tn), lambda l, j: (l, 0, j)),
            pl.BlockSpec((1, 1, tn), lambda l, j: (l, 0, j)),
        ],
        out_specs=pl.BlockSpec((1, SUBLANES, tn), lambda l, j: (l, 0, j)),
        out_shape=jax.ShapeDtypeStruct((DEPTH, SUBLANES, n), F32),
        compiler_params=_params("parallel", "parallel"),
        name="ada_mod",
    )(c_all, ada_w, ada_b.reshape(DEPTH, 1, n))


def _modnorm(x, g, sc, sh):
    y = x * lax.rsqrt(jnp.mean(x * x, axis=-1, keepdims=True) + EPS)
    return (y * g) * (1.0 + sc) + sh


def _normmod_mm_kernel(x_ref, g_ref, sc_ref, sh_ref, w_ref, o_ref, h_scr):
    @pl.when(pl.program_id(2) == 0)
    def _():
        h_scr[...] = _modnorm(x_ref[0], g_ref[...], sc_ref[0], sh_ref[0]).astype(BF16)

    o_ref[0] = _dot(h_scr[...], w_ref[...]).astype(o_ref.dtype)


def _normmod_mm(x, g, sc, sh, w, tn, out_dtype=F32):
    b, s, d = x.shape
    n = w.shape[1]
    tm = min(512, s)
    return pl.pallas_call(
        _normmod_mm_kernel,
        grid=(b, s // tm, n // tn),
        in_specs=[
            pl.BlockSpec((1, tm, d), lambda bi, i, j: (bi, i, 0)),
            pl.BlockSpec((1, d), lambda bi, i, j: (0, 0)),
            pl.BlockSpec((1, 1, d), lambda bi, i, j: (bi, 0, 0)),
            pl.BlockSpec((1, 1, d), lambda bi, i, j: (bi, 0, 0)),
            pl.BlockSpec((d, tn), lambda bi, i, j: (0, j)),
        ],
        out_specs=pl.BlockSpec((1, tm, tn), lambda bi, i, j: (bi, i, j)),
        out_shape=jax.ShapeDtypeStruct((b, s, n), out_dtype),
        scratch_shapes=[pltpu.VMEM((tm, d), BF16)],
        compiler_params=_params("parallel", "parallel", "arbitrary"),
        name="normmod_mm",
    )(x, g.reshape(1, d), sc, sh, w)


def _rwkv_prep_kernel(p_ref, pp_ref, pn_ref, mu_ref, w0_ref, wup_ref, a0_ref, aup_ref, gup_ref, kk_ref, ka_ref,
                      rk_ref, ones_ref, r_out, v_out, kkn_out, lw_out, kd_out, bd_out, bv_out, g_out):
    i = pl.program_id(1)
    last = pl.num_programs(1) - 1
    pa = p_ref[0]
    tm = pa.shape[0]
    row = lax.broadcasted_iota(jnp.int32, (tm, 1), 0)
    prev_row = jnp.where(i == 0, 0.0, pp_ref[0][SUBLANES - 1:SUBLANES, :])
    next_row = jnp.where(i == last, 0.0, pn_ref[0][0:1, :])
    p_prev = jnp.where(row == 0, prev_row, pltpu.roll(pa, 1, 0))
    p_next = jnp.where(row == tm - 1, next_row, pltpu.roll(pa, tm - 1, 0))
    pa = pa + mu_ref[...] * (0.5 * (p_prev + p_next) - pa)

    w = A_WIDTH
    r = pa[:, 0:w]
    k = pa[:, w:2 * w]
    v = pa[:, 2 * w:3 * w]
    dw = pa[:, 3 * w:3 * w + 2 * A_LORA]
    da = pa[:, 3 * w + 2 * A_LORA:3 * w + 4 * A_LORA]
    dg = pa[:, 3 * w + 4 * A_LORA:A_COLS]

    lw = -DECAY_SCALE * jax.nn.sigmoid(w0_ref[...] + _dot(jnp.tanh(dw).astype(BF16), wup_ref[...]))
    icl = jax.nn.sigmoid(a0_ref[...] + _dot(da.astype(BF16), aup_ref[...]))
    g = _dot(jax.nn.sigmoid(dg).astype(BF16), gup_ref[...])

    ones_bd = ones_ref[...]
    kkr = k * kk_ref[...]
    ss = _dot_exact_rhs(kkr * kkr, ones_bd)
    kkn = kkr / jnp.maximum(jnp.sqrt(ss), 1e-12)

    r_out[0] = r
    v_out[0] = v
    kkn_out[0] = kkn
    lw_out[0] = lw
    g_out[0] = g
    bonus = jnp.zeros_like(r)
    for d in range(2):
        icl_d = icl[:, d * w:(d + 1) * w]
        k_d = k * (1.0 + (icl_d - 1.0) * ka_ref[...])
        kd_out[0, :, d * w:(d + 1) * w] = k_d
        bd_out[0, :, d * w:(d + 1) * w] = icl_d * kkn
        bonus = bonus + _dot_exact_rhs(r * k_d * rk_ref[...], ones_bd)
    bv_out[0] = bonus * v


def _rwkv_prep(pa, wts):
    b, s, _ = pa.shape
    tm = min(256, s)
    nh = tm // SUBLANES
    w = A_WIDTH
    full = lambda shape: pl.BlockSpec(shape, lambda bi, i: (0,) * len(shape))
    tok = lambda n: pl.BlockSpec((1, tm, n), lambda bi, i: (bi, i, 0))
    out_shapes = [jax.ShapeDtypeStruct((b, s, n), F32) for n in (w, w, w, 2 * w, 2 * w, 2 * w, w, w)]
    return pl.pallas_call(
        _rwkv_prep_kernel,
        grid=(b, s // tm),
        in_specs=[
            tok(A_COLS),
            pl.BlockSpec((1, SUBLANES, A_COLS), lambda bi, i: (bi, jnp.maximum(i * nh - 1, 0), 0)),
            pl.BlockSpec((1, SUBLANES, A_COLS), lambda bi, i: (bi, jnp.minimum((i + 1) * nh, s // SUBLANES - 1), 0)),
            full((1, A_COLS)), full((1, 2 * w)), full((2 * A_LORA, 2 * w)), full((1, 2 * w)),
            full((2 * A_LORA, 2 * w)), full((A_GATE_LORA, w)), full((1, w)), full((1, w)), full((1, w)),
            full((w, w)),
        ],
        out_specs=[tok(sd.shape[-1]) for sd in out_shapes],
        out_shape=out_shapes,
        compiler_params=_params("parallel", "parallel"),
        name="rwkv_prep",
    )(pa, pa, pa, wts["mu"], wts["w0"], wts["wup"], wts["a0"], wts["aup"], wts["gup"], wts["k_k"], wts["k_a"],
      wts["r_k"], wts["ones_bd"])


def _rwkv_scan_kernel(r_ref, v_ref, kk_ref, lw_ref, kd_ref, bd_ref, y_ref, s_scr):
    d = pl.program_id(1)
    t = RWKV_CHUNK

    @pl.when(pl.program_id(3) == 0)
    def _():
        s_scr[...] = jnp.zeros_like(s_scr)

    sign = 1 - 2 * d
    lw = lw_ref[0]
    ri = lax.broadcasted_iota(jnp.int32, (t, t), 0)
    ci = lax.broadcasted_iota(jnp.int32, (t, t), 1)
    tri = jnp.where((ri - ci) * sign >= 0, 1.0, 0.0).astype(BF16)
    lw_hi = lw.astype(BF16)
    rem = lw - lw_hi.astype(F32)
    lw_mid = rem.astype(BF16)
    lw_lo = (rem - lw_mid.astype(F32)).astype(BF16)
    cum = _dot(tri, lw_hi) + _dot(tri, lw_mid) + _dot(tri, lw_lo)
    tot = jnp.sum(lw, axis=0, keepdims=True)
    gam = jnp.exp(cum)
    gam_ex = jnp.exp(cum - lw)
    ginv = jnp.exp(-cum)
    gend = jnp.exp(tot - cum)
    gtot = jnp.exp(tot)

    lane = lax.broadcasted_iota(jnp.int32, (1, LANES), 1)
    m0 = lane < A_HEAD

    def stack2(x):
        return jnp.concatenate([jnp.where(m0, x, 0.0), jnp.where(m0, 0.0, x)], axis=0)

    r = r_ref[0]
    v = v_ref[0]
    kk = kk_ref[0]
    kd = kd_ref[0]
    bd = bd_ref[0]
    q_r = stack2(r * gam)
    q_k = stack2(kk * gam_ex)
    k_k = stack2(kd * ginv)
    k_b = stack2(bd * ginv)
    v2 = stack2(v)
    k_k_end = stack2(kd * gend)
    k_b_end = stack2(bd * gend)

    n2 = 2 * t
    a = _dot3(jnp.concatenate([q_r, q_k], axis=0), jnp.concatenate([k_k, k_b], axis=0), NT)
    ri2 = lax.broadcasted_iota(jnp.int32, (n2, n2), 0)
    ci2 = lax.broadcasted_iota(jnp.int32, (n2, n2), 1)
    order = (ri2 - ci2) * sign
    incl = order >= 0
    strict = order > 0
    a_rk = jnp.where(incl, a[:n2, :n2], 0.0)
    a_rb = jnp.where(incl, a[:n2, n2:], 0.0)
    a_kk = jnp.where(strict, a[n2:, :n2], 0.0)
    a_bk = jnp.where(strict, a[n2:, n2:], 0.0)

    eye = jnp.where(ri2 == ci2, 1.0, 0.0)
    inv = eye - a_bk
    npow = a_bk
    for _ in range(5):
        npow = _dot3(npow, npow)
        inv = inv + _dot3(inv, npow)

    s2 = s_scr[...]
    rhs = -_dot3(q_k, s2, NT) - _dot3(a_kk, v2)
    p2 = _dot3(inv, rhs)
    y2 = _dot3(q_r, s2, NT) + _dot3(a_rb, p2) + _dot3(a_rk, v2)
    y_ref[0] = y2[:t] + y2[t:]
    s_scr[...] = s2 * gtot + _dot3(v2.T, k_k_end) + _dot3(p2.T, k_b_end)


def _rwkv_scan(r, v, kk, lw, kd, bd):
    b, s, w = r.shape
    t = RWKV_CHUNK
    nc = s // t
    npair = w // LANES

    def tok(shared):
        def idx(bi, d, p, c):
            ce = c + d * (nc - 1 - 2 * c)
            return (bi, ce, p if shared else d * npair + p)
        return pl.BlockSpec((1, t, LANES), idx)

    return pl.pallas_call(
        _rwkv_scan_kernel,
        grid=(b, 2, npair, nc),
        in_specs=[tok(True), tok(True), tok(True), tok(False), tok(False), tok(False)],
        out_specs=tok(False),
        out_shape=jax.ShapeDtypeStruct((b, s, 2 * w), F32),
        scratch_shapes=[pltpu.VMEM((LANES, LANES), F32)],
        compiler_params=_params("parallel", "parallel", "parallel", "arbitrary"),
        name="rwkv_scan",
    )(r, v, kk, lw, kd, bd)


def _rwkv_post_kernel(y_ref, bv_ref, g_ref, lng_ref, lnb_ref, ones_ref, o_ref):
    w = A_WIDTH
    y2 = y_ref[0]
    y = y2[:, :w] + y2[:, w:]
    ones_bd = ones_ref[...]
    mean = _dot_exact_rhs(y, ones_bd) * (1.0 / A_HEAD)
    yc = y - mean
    var = _dot_exact_rhs(yc * yc, ones_bd) * (1.0 / A_HEAD)
    yn = yc * lax.rsqrt(var + RWKV_GN_EPS) * lng_ref[...] + lnb_ref[...]
    o_ref[0] = ((yn + bv_ref[0]) * g_ref[0]).astype(o_ref.dtype)


def _rwkv_post(y2, bv, g, lng, lnb, ones_bd):
    b, s, w = bv.shape
    tm = min(512, s)
    full = lambda shape: pl.BlockSpec(shape, lambda bi, i: (0,) * len(shape))
    tok = lambda n: pl.BlockSpec((1, tm, n), lambda bi, i: (bi, i, 0))
    return pl.pallas_call(
        _rwkv_post_kernel,
        grid=(b, s // tm),
        in_specs=[tok(2 * w), tok(w), tok(w), full((1, w)), full((1, w)), full((w, w))],
        out_specs=tok(w),
        out_shape=jax.ShapeDtypeStruct((b, s, w), BF16),
        compiler_params=_params("parallel", "parallel"),
        name="rwkv_post",
    )(y2, bv, g, lng, lnb, ones_bd)


def _s5_kernel(u_ref, m_ref, w_ref, ws_ref, v_ref, a_ref, y_ref, x_scr, xs_scr, h_scr, carry_scr, *, rows, reverse):
    g = B_GROUPS

    @pl.when(pl.program_id(1) == 0)
    def _():
        carry_scr[...] = jnp.zeros_like(carry_scr)

    u = u_ref[0]
    x_scr[...] = lax.dot_general(u, w_ref[...], BATCH_NN, preferred_element_type=F32).reshape(g * rows, LANES)
    xs_scr[...] = lax.dot_general(u, ws_ref[...], BATCH_NN, preferred_element_type=F32).reshape(g * rows, LANES)
    a1 = a_ref[0]
    a2 = a_ref[1]
    a2s = a_ref[2]

    def step(i, carry):
        h, hs = carry
        r = (rows - 1 - i) if reverse else i
        idx = pl.ds(r, g, stride=rows)
        h_scr[idx, :] = h
        hn = a1 * h + a2 * hs + x_scr[idx, :]
        hsn = a1 * hs + a2s * h + xs_scr[idx, :]
        return hn, hsn

    h, hs = lax.fori_loop(0, rows, step, (carry_scr[0], carry_scr[1]))
    carry_scr[0] = h
    carry_scr[1] = hs
    hprev = h_scr[...].reshape(g, rows, LANES).astype(BF16)
    y_ref[0] = (lax.dot_general(u, m_ref[...], BATCH_NN, preferred_element_type=F32)
                + lax.dot_general(hprev, v_ref[...], BATCH_NN, preferred_element_type=F32))


def _s5_scan(u_g, mats, reverse):
    b, g, nrows, cw = u_g.shape
    rows = min(64, nrows)
    nsb = nrows // rows
    whole = pl.BlockSpec(memory_space=pltpu.VMEM)
    idx = (lambda bi, i: (bi, 0, nsb - 1 - i, 0)) if reverse else (lambda bi, i: (bi, 0, i, 0))
    return pl.pallas_call(
        functools.partial(_s5_kernel, rows=rows, reverse=reverse),
        grid=(b, nsb),
        in_specs=[pl.BlockSpec((1, g, rows, cw), idx), whole, whole, whole, whole, whole],
        out_specs=pl.BlockSpec((1, g, rows, cw), idx),
        out_shape=jax.ShapeDtypeStruct((b, g, nrows, cw), F32),
        scratch_shapes=[pltpu.VMEM((g * rows, LANES), F32), pltpu.VMEM((g * rows, LANES), F32),
                        pltpu.VMEM((g * rows, LANES), F32), pltpu.VMEM((2, g, LANES), F32)],
        compiler_params=_params("parallel", "arbitrary"),
        name="s5_scan_bwd" if reverse else "s5_scan_fwd",
    )(u_g, mats["m"], mats["w"], mats["ws"], mats["v"], mats["a"])


def _s5_matrices(lam_re, lam_im, log_step, b_re, b_im, c_re, c_im, reverse):
    hp = lax.Precision.HIGHEST
    t = S5_CHUNK
    g, p, c = B_GROUPS, B_STATE, B_GROUP
    lr, li = lam_re.astype(F32), lam_im.astype(F32)
    step = jnp.exp(log_step.astype(F32))[:, None]
    ar, ai = jnp.exp(lr * step) * jnp.cos(li * step), jnp.exp(lr * step) * jnp.sin(li * step)
    den = lr * lr + li * li
    nr, ni = ar - 1.0, ai
    fr, fi = (nr * lr + ni * li) / den, (ni * lr - nr * li) / den
    br, bi = b_re.astype(F32), b_im.astype(F32)
    bbr = fr[..., None] * br - fi[..., None] * bi
    bbi = fr[..., None] * bi + fi[..., None] * br
    cr, cim = c_re.astype(F32), c_im.astype(F32)
    taus = jnp.arange(t + 1, dtype=F32)[:, None, None]
    mag = jnp.exp(lr * step * taus)
    pr, pi = mag * jnp.cos(li * step * taus), mag * jnp.sin(li * step * taus)
    cpr = cr[None] * pr[:, :, None, :] - cim[None] * pi[:, :, None, :]
    cpi = cr[None] * pi[:, :, None, :] + cim[None] * pr[:, :, None, :]
    kern = (jnp.einsum("tgcp,gpd->tgcd", cpr, bbr, precision=hp)
            - jnp.einsum("tgcp,gpd->tgcd", cpi, bbi, precision=hp))
    s_idx = jnp.arange(t)[:, None]
    t_idx = jnp.arange(t)[None, :]
    lag = (s_idx - t_idx) if reverse else (t_idx - s_idx)
    kk = jnp.where((lag >= 0)[:, :, None, None, None], kern[jnp.clip(lag, 0, t)], 0.0)
    m = jnp.transpose(kk, (2, 0, 4, 1, 3)).reshape(g, t * c, t * c)
    e = jnp.arange(t) if reverse else (t - 1 - jnp.arange(t))
    pre, pie = pr[e], pi[e]
    wre = pre[..., None] * bbr[None] - pie[..., None] * bbi[None]
    wim = pre[..., None] * bbi[None] + pie[..., None] * bbr[None]
    wre = jnp.transpose(wre, (1, 0, 3, 2)).reshape(g, t * c, p)
    wim = jnp.transpose(wim, (1, 0, 3, 2)).reshape(g, t * c, p)
    w = jnp.concatenate([wre, wim], axis=-1)
    ws = jnp.concatenate([wim, wre], axis=-1)
    f = (t - jnp.arange(t)) if reverse else (jnp.arange(t) + 1)
    vre = jnp.transpose(cpr[f], (1, 3, 0, 2)).reshape(g, p, t * c)
    vim = jnp.transpose(-cpi[f], (1, 3, 0, 2)).reshape(g, p, t * c)
    v = jnp.concatenate([vre, vim], axis=1)
    atr, ati = pr[t], pi[t]
    a = jnp.stack([jnp.concatenate([atr, atr], -1), jnp.concatenate([-ati, ati], -1),
                   jnp.concatenate([ati, -ati], -1)])
    return {"m": m.astype(BF16), "w": w.astype(BF16), "ws": ws.astype(BF16), "v": v.astype(BF16), "a": a}


def _s5_glu_kernel(ys_ref, u_ref, d_ref, w_ref, b_ref, o_ref):
    y = ys_ref[0] + u_ref[0] * d_ref[...]
    z = jax.nn.gelu(y)
    gate = jax.nn.sigmoid(_dot(z.astype(BF16), w_ref[...]) + b_ref[...])
    o_ref[0] = (z * gate).astype(o_ref.dtype)


def _s5_glu(ys, u, d_skip, w_glu, b_glu):
    b, s, w = u.shape
    tm = min(512, s)
    full = lambda shape: pl.BlockSpec(shape, lambda bi, i: (0,) * len(shape))
    tok = pl.BlockSpec((1, tm, w), lambda bi, i: (bi, i, 0))
    return pl.pallas_call(
        _s5_glu_kernel,
        grid=(b, s // tm),
        in_specs=[tok, tok, full((1, w)), full((w, w)), full((1, w))],
        out_specs=tok,
        out_shape=jax.ShapeDtypeStruct((b, s, w), BF16),
        compiler_params=_params("parallel", "parallel"),
        name="s5_glu",
    )(ys, u, d_skip, w_glu, b_glu)


def _out_proj_kernel(a_ref, b_ref, wa_ref, wb_ref, x_ref, g_ref, o_ref):
    mix = _dot(a_ref[0], wa_ref[...]) + _dot(b_ref[0], wb_ref[...])
    o_ref[0] = x_ref[0] + g_ref[0] * mix


def _out_proj(a, bb, wa, wb, x, gate):
    b, s, d = x.shape
    k = a.shape[-1]
    tm = min(512, s)
    tn = 1024
    return pl.pallas_call(
        _out_proj_kernel,
        grid=(b, s // tm, d // tn),
        in_specs=[
            pl.BlockSpec((1, tm, k), lambda bi, i, j: (bi, i, 0)),
            pl.BlockSpec((1, tm, k), lambda bi, i, j: (bi, i, 0)),
            pl.BlockSpec((k, tn), lambda bi, i, j: (0, j)),
            pl.BlockSpec((k, tn), lambda bi, i, j: (0, j)),
            pl.BlockSpec((1, tm, tn), lambda bi, i, j: (bi, i, j)),
            pl.BlockSpec((1, 1, tn), lambda bi, i, j: (bi, 0, j)),
        ],
        out_specs=pl.BlockSpec((1, tm, tn), lambda bi, i, j: (bi, i, j)),
        out_shape=jax.ShapeDtypeStruct((b, s, d), F32),
        compiler_params=_params("parallel", "parallel", "parallel"),
        name="out_proj",
    )(a, bb, wa, wb, x, gate)


def _ffn_down_kernel(uv_ref, ug_ref, uvp_ref, uvn_ref, ugp_ref, ugn_ref, cwv_ref, cwg_ref, cbv_ref, cbg_ref,
                     wd_ref, x_ref, g_ref, fg_ref, o_ref, acc_scr, *, final):
    i = pl.program_id(1)
    j = pl.program_id(2)
    first = i == 0
    last = i == pl.num_programs(1) - 1

    @pl.when(j == 0)
    def _():
        acc_scr[...] = jnp.zeros_like(acc_scr)

    def conv(u_ref, up_ref, un_ref, cw_ref, cb_ref):
        u = u_ref[0]
        tm = u.shape[0]
        row = lax.broadcasted_iota(jnp.int32, (tm, 1), 0)
        prev_row = jnp.where(first, 0.0, up_ref[0][SUBLANES - 1:SUBLANES, :])
        next_row = jnp.where(last, 0.0, un_ref[0][0:1, :])
        u_prev = jnp.where(row == 0, prev_row, pltpu.roll(u, 1, 0))
        u_next = jnp.where(row == tm - 1, next_row, pltpu.roll(u, tm - 1, 0))
        cw = cw_ref[...]
        return u_prev * cw[0:1] + u * cw[1:2] + u_next * cw[2:3] + cb_ref[...]

    val = conv(uv_ref, uvp_ref, uvn_ref, cwv_ref, cbv_ref)
    gate = conv(ug_ref, ugp_ref, ugn_ref, cwg_ref, cbg_ref)
    act = (gate * jax.nn.sigmoid(gate)) * val
    acc_scr[...] += _dot(act.astype(BF16), wd_ref[...])

    @pl.when(j == pl.num_programs(2) - 1)
    def _():
        xn = x_ref[0] + g_ref[0] * acc_scr[...]
        if final:
            xn = xn * lax.rsqrt(jnp.mean(xn * xn, axis=-1, keepdims=True) + EPS) * fg_ref[...]
        o_ref[0] = xn


def _ffn_down(u, conv_w, conv_b, w_down, x, gate, final_g, final):
    b, s, d = x.shape
    f = FFN_HIDDEN
    tm = min(512, s)
    tf = 512
    nf = f // tf
    nh = tm // SUBLANES
    nrow8 = s // SUBLANES
    main = lambda off: pl.BlockSpec((1, tm, tf), lambda bi, i, j: (bi, i, j + off))
    prev = lambda off: pl.BlockSpec((1, SUBLANES, tf), lambda bi, i, j: (bi, jnp.maximum(i * nh - 1, 0), j + off))
    nxt = lambda off: pl.BlockSpec((1, SUBLANES, tf),
                                   lambda bi, i, j: (bi, jnp.minimum((i + 1) * nh, nrow8 - 1), j + off))
    cw = lambda off: pl.BlockSpec((3, tf), lambda bi, i, j: (0, j + off))
    cb = lambda off: pl.BlockSpec((1, tf), lambda bi, i, j: (0, j + off))
    return pl.pallas_call(
        functools.partial(_ffn_down_kernel, final=final),
        grid=(b, s // tm, nf),
        in_specs=[
            main(0), main(nf), prev(0), nxt(0), prev(nf), nxt(nf), cw(0), cw(nf), cb(0), cb(nf),
            pl.BlockSpec((tf, d), lambda bi, i, j: (j, 0)),
            pl.BlockSpec((1, tm, d), lambda bi, i, j: (bi, i, 0)),
            pl.BlockSpec((1, 1, d), lambda bi, i, j: (bi, 0, 0)),
            pl.BlockSpec((1, d), lambda bi, i, j: (0, 0)),
        ],
        out_specs=pl.BlockSpec((1, tm, d), lambda bi, i, j: (bi, i, 0)),
        out_shape=jax.ShapeDtypeStruct((b, s, d), F32),
        scratch_shapes=[pltpu.VMEM((tm, d), F32)],
        compiler_params=_params("parallel", "parallel", "arbitrary"),
        name="ffn_down",
    )(u, u, u, u, u, u, conv_w, conv_w, conv_b, conv_b, w_down, x, gate, final_g)


def _rms(x, g, eps):
    return x * lax.rsqrt(jnp.mean(x * x, axis=-1, keepdims=True) + eps) * g


def _odd_prep_kernel(cq_ref, ckv_ref, kr_ref, dq_ref, dk_ref, dv_ref, cos_ref, sin_ref, qg_ref, kvg_ref,
                     wqa_ref, wqb_ref, wk_ref, wv_ref, place_ref,
                     mq_out, mk_out, mv_out, q0_out, q1_out, dk_out, dv_out):
    cosq = cos_ref[...]
    sinq = sin_ref[...]
    qn = _rms(cq_ref[0], qg_ref[...], EPS).astype(BF16)
    qa = _dot(qn, wqa_ref[...])
    qb = _dot(qn, wqb_ref[...])
    mla_scale = (C_NOPE + C_ROPE) ** -0.5 * LOG2E
    for h in range(C_HEADS):
        sl = slice(h * LANES, (h + 1) * LANES)
        mq_out[0, :, sl] = ((qa[:, sl] * cosq + qb[:, sl] * sinq) * mla_scale).astype(BF16)
    kvn = _rms(ckv_ref[0], kvg_ref[...], EPS).astype(BF16)
    kr = kr_ref[0]
    cos_k = pltpu.roll(cosq, LANES - C_NOPE, 1)
    sin_k = pltpu.roll(sinq, LANES - C_NOPE, 1)
    partner = pltpu.roll(kr, LANES - C_ROPE, 1)
    lane = lax.broadcasted_iota(jnp.int32, (1, LANES), 1)
    kr_rope = jnp.where(lane < C_ROPE, kr * cos_k + partner * sin_k, 0.0)
    mk_out[0] = (_dot(kvn, wk_ref[...]) + _dot(kr_rope.astype(BF16), place_ref[...])).astype(BF16)
    mv_out[0] = _dot(kvn, wv_ref[...]).astype(BF16)
    dq = dq_ref[0] * (D_HEAD ** -0.5 * LOG2E)
    lane_w = lax.broadcasted_iota(jnp.int32, (1, dq.shape[1]), 1)
    first_map = (lane_w % LANES) < D_HEAD
    q0_out[0] = jnp.where(first_map, dq, 0.0).astype(BF16)
    q1_out[0] = jnp.where(first_map, 0.0, dq).astype(BF16)
    dk_out[0] = dk_ref[0].astype(BF16)
    dv_out[0] = dv_ref[0].astype(BF16)


ODD_DQ = 0
ODD_DK = ODD_DQ + D_HEADS * 2 * D_HEAD
ODD_DV = ODD_DK + D_HEADS * 2 * D_HEAD
ODD_CQ = ODD_DV + D_HEADS * D_V
ODD_CKV = ODD_CQ + C_Q_RANK
ODD_KR = ODD_CKV + C_KV_RANK
ODD_COLS = 4096


def _odd_prep(p, cos_t, sin_t, wts):
    b, s, _ = p.shape
    tm = min(256, s)
    hw = C_HEADS * LANES
    full = lambda shape: pl.BlockSpec(shape, lambda bi, i: (0,) * len(shape))
    col = lambda off, n: pl.BlockSpec((1, tm, n), lambda bi, i: (bi, i, off // n))
    tok = pl.BlockSpec((1, tm, hw), lambda bi, i: (bi, i, 0))
    tab = pl.BlockSpec((tm, LANES), lambda bi, i: (i, 0))
    outs = [jax.ShapeDtypeStruct((b, s, hw), BF16)] * 7
    return pl.pallas_call(
        _odd_prep_kernel,
        grid=(b, s // tm),
        in_specs=[
            col(ODD_CQ, C_Q_RANK), col(ODD_CKV, C_KV_RANK), col(ODD_KR, LANES),
            col(ODD_DQ, hw), col(ODD_DK, hw), col(ODD_DV, hw), tab, tab,
            full((1, C_Q_RANK)), full((1, C_KV_RANK)),
            full((C_Q_RANK, hw)), full((C_Q_RANK, hw)), full((C_KV_RANK, hw)), full((C_KV_RANK, hw)),
            full((LANES, hw)),
        ],
        out_specs=[tok] * 7,
        out_shape=outs,
        compiler_params=_params("parallel", "parallel"),
        name="odd_prep",
    )(p, p, p, p, p, p, cos_t, sin_t, wts["q_norm_g"], wts["kv_norm_g"], wts["wqa"], wts["wqb"], wts["wk"],
      wts["wv"], wts["place"])


def _online_softmax_step(s, v, m_scr, l_scr, acc_scr):
    m_prev = m_scr[...]
    m_new = jnp.maximum(m_prev, jnp.max(s, axis=-1, keepdims=True))
    alpha = jnp.exp2(m_prev - m_new)
    p = jnp.exp2(s - m_new)
    l_scr[...] = alpha * l_scr[...] + jnp.sum(p, axis=-1, keepdims=True)
    acc_scr[...] = alpha * acc_scr[...] + _dot(p.astype(BF16), v)
    m_scr[...] = m_new


def _mla_flash_kernel(q_ref, k_ref, v_ref, o_ref, m_scr, l_scr, acc_scr):
    kj = pl.program_id(3)

    @pl.when(kj == 0)
    def _():
        m_scr[...] = jnp.full_like(m_scr, -jnp.inf)
        l_scr[...] = jnp.zeros_like(l_scr)
        acc_scr[...] = jnp.zeros_like(acc_scr)

    s = _dot(q_ref[0], k_ref[0], NT)
    _online_softmax_step(s, v_ref[0], m_scr, l_scr, acc_scr)

    @pl.when(kj == pl.num_programs(3) - 1)
    def _():
        o_ref[0] = (acc_scr[...] / l_scr[...]).astype(o_ref.dtype)


def _mla_flash(q, k, v):
    b, s, hw = q.shape
    nh = hw // LANES
    tq = min(512, s)
    tk = min(1024, s)
    qspec = pl.BlockSpec((1, tq, LANES), lambda bi, h, i, j: (bi, i, h))
    kspec = pl.BlockSpec((1, tk, LANES), lambda bi, h, i, j: (bi, j, h))
    return pl.pallas_call(
        _mla_flash_kernel,
        grid=(b, nh, s // tq, s // tk),
        in_specs=[qspec, kspec, kspec],
        out_specs=qspec,
        out_shape=jax.ShapeDtypeStruct((b, s, hw), BF16),
        scratch_shapes=[pltpu.VMEM((tq, 1), F32), pltpu.VMEM((tq, 1), F32), pltpu.VMEM((tq, LANES), F32)],
        compiler_params=_params("parallel", "parallel", "parallel", "arbitrary"),
        name="mla_flash",
    )(q, k, v)


BIAS_HALF = 256


def _diff_flash_kernel(q0_ref, q1_ref, k_ref, v_ref, tab_ref, lq1_ref, lk1_ref, lq2_ref, lk2_ref, sg_ref, o_ref,
                       m_scr, l_scr, acc_scr, bias_scr, *, tile, lambda_init):
    qi = pl.program_id(2)
    kj = pl.program_id(3)
    nsub = tile // LANES

    @pl.when(kj == 0)
    def _():
        m_scr[...] = jnp.full_like(m_scr, -jnp.inf)
        l_scr[...] = jnp.zeros_like(l_scr)
        acc_scr[...] = jnp.zeros_like(acc_scr)

    tab = tab_ref[0]
    k = k_ref[0]
    v = v_ref[0]

    def update(bias):
        for m, q_ref in enumerate((q0_ref, q1_ref)):
            s = _dot(q_ref[0], k, NT) + bias
            _online_softmax_step(s, v, m_scr.at[m], l_scr.at[m], acc_scr.at[m])

    def toeplitz(d):
        r = tab[:, d + LANES:d + 3 * LANES]
        rows = jnp.broadcast_to(r, (LANES, 2 * LANES))
        return pltpu.roll(rows, LANES, 1, stride=1, stride_axis=0)[:, :LANES]

    far_left = tab[:, 0:1]
    far_right = tab[:, 2 * BIAS_HALF - 1:2 * BIAS_HALF]

    for delta in (-1, 0, 1):
        @pl.when(kj - qi == delta)
        def _(delta=delta):
            blocks = {}
            for ri in range(nsub):
                for cj in range(nsub):
                    d = delta * tile + (cj - ri) * LANES
                    sl = (slice(ri * LANES, (ri + 1) * LANES), slice(cj * LANES, (cj + 1) * LANES))
                    if d <= -BIAS_HALF:
                        bias_scr[sl] = jnp.broadcast_to(far_left, (LANES, LANES))
                    elif d >= BIAS_HALF:
                        bias_scr[sl] = jnp.broadcast_to(far_right, (LANES, LANES))
                    else:
                        if d not in blocks:
                            blocks[d] = toeplitz(d)
                        bias_scr[sl] = blocks[d]
            update(bias_scr[...])

    @pl.when(kj - qi < -1)
    def _():
        update(far_left)

    @pl.when(kj - qi > 1)
    def _():
        update(far_right)

    @pl.when(kj == pl.num_programs(3) - 1)
    def _():
        lam = (jnp.exp(jnp.sum(lq1_ref[...] * lk1_ref[...], axis=-1, keepdims=True))
               - jnp.exp(jnp.sum(lq2_ref[...] * lk2_ref[...], axis=-1, keepdims=True)) + lambda_init)
        o = acc_scr[0] / l_scr[0] - lam * (acc_scr[1] / l_scr[1])
        o = _rms(o, sg_ref[...], SUBLN_EPS) * (1.0 - lambda_init)
        o_ref[0] = o.astype(o_ref.dtype)


def _diff_flash(q0, q1, k, v, tab, lq1, lk1, lq2, lk2, subln_g, lambda_init):
    b, s, hw = q0.shape
    nh = hw // LANES
    tile = min(512, s)
    qspec = pl.BlockSpec((1, tile, LANES), lambda bi, h, i, j: (bi, i, h))
    kspec = pl.BlockSpec((1, tile, LANES), lambda bi, h, i, j: (bi, j, h))
    vec = lambda n: pl.BlockSpec((1, n), lambda bi, h, i, j: (0, 0))
    return pl.pallas_call(
        functools.partial(_diff_flash_kernel, tile=tile, lambda_init=lambda_init),
        grid=(b, nh, s // tile, s // tile),
        in_specs=[qspec, qspec, kspec, kspec,
                  pl.BlockSpec((1, 1, 2 * BIAS_HALF), lambda bi, h, i, j: (h, 0, 0)),
                  vec(D_HEAD), vec(D_HEAD), vec(D_HEAD), vec(D_HEAD), vec(D_V)],
        out_specs=qspec,
        out_shape=jax.ShapeDtypeStruct((b, s, hw), BF16),
        scratch_shapes=[pltpu.VMEM((2, tile, 1), F32), pltpu.VMEM((2, tile, 1), F32),
                        pltpu.VMEM((2, tile, LANES), F32), pltpu.VMEM((tile, tile), F32)],
        compiler_params=_params("parallel", "parallel", "parallel", "arbitrary"),
        name="diff_flash",
    )(q0, q1, k, v, tab, lq1, lk1, lq2, lk2, subln_g)


def _t5_bucket(rel):
    half = N_BUCKETS // 2
    max_exact = half // 2
    n = jnp.abs(rel)
    large = max_exact + (jnp.log(jnp.maximum(n, 1).astype(jnp.float32) / max_exact)
                         / math.log(MAX_DISTANCE / max_exact) * (half - max_exact)).astype(jnp.int32)
    large = jnp.minimum(large, half - 1)
    return jnp.where(rel > 0, half, 0) + jnp.where(n < max_exact, n, large)


def _rope_tables(s):
    inv = 1.0 / (ROPE_THETA ** (jnp.arange(0, C_ROPE, 2, dtype=F32) / C_ROPE))
    ang = jnp.arange(s, dtype=F32)[:, None] * inv[None, :]
    cos, sin = jnp.cos(ang), jnp.sin(ang)
    pad = LANES - C_NOPE - C_ROPE
    cos_t = jnp.concatenate([jnp.ones((s, C_NOPE), F32), cos, cos, jnp.zeros((s, pad), F32)], axis=-1)
    sin_t = jnp.concatenate([jnp.zeros((s, C_NOPE), F32), sin, sin, jnp.zeros((s, pad), F32)], axis=-1)
    return cos_t, sin_t


def _rot_half_cols(w):
    h = w.shape[-1] // 2
    return jnp.concatenate([-w[..., h:], w[..., :h]], axis=-1)


def _pack_even(j, even_w_in, even_w_out, rwkv_mu, rwkv_w0, rwkv_w_up, rwkv_a0, rwkv_a_up, rwkv_g_up, rwkv_k_k,
               rwkv_k_a, rwkv_r_k, rwkv_lnx_g, rwkv_lnx_b):
    w = A_WIDTH
    z = jnp.zeros((A_LORA, w), F32)
    blockdiag = lambda m: jnp.concatenate(
        [jnp.concatenate([m[0], z], axis=1), jnp.concatenate([z, m[1]], axis=1)], axis=0)
    head = jnp.arange(w) // A_HEAD
    return {
        "w_in_a": even_w_in[j][:, :A_COLS].astype(BF16),
        "w_in_b": even_w_in[j][:, A_COLS:].astype(BF16),
        "w_out_a": even_w_out[j][:w].astype(BF16),
        "w_out_b": even_w_out[j][w:].astype(BF16),
        "mu": rwkv_mu[j].reshape(1, A_COLS),
        "w0": rwkv_w0[j].reshape(1, 2 * w),
        "wup": blockdiag(rwkv_w_up[j]).astype(BF16),
        "a0": rwkv_a0[j].reshape(1, 2 * w),
        "aup": blockdiag(rwkv_a_up[j]).astype(BF16),
        "gup": rwkv_g_up[j].astype(BF16),
        "k_k": rwkv_k_k[j].reshape(1, w),
        "k_a": rwkv_k_a[j].reshape(1, w),
        "r_k": rwkv_r_k[j].reshape(1, w),
        "lnx_g": rwkv_lnx_g[j].reshape(1, w),
        "lnx_b": rwkv_lnx_b[j].reshape(1, w),
        "ones_bd": (head[:, None] == head[None, :]).astype(BF16),
    }


def _pack_odd(j, odd_w_in, odd_w_out, mla_q_norm_g, mla_kv_norm_g, mla_w_uq, mla_w_ukv):
    d = D_MODEL
    w_in = odd_w_in[j]
    o_cq, o_ckv = 0, C_Q_RANK
    o_kr = o_ckv + C_KV_RANK
    o_dq = o_kr + C_ROPE
    n_d = D_HEADS * 2 * D_HEAD
    w_kr = w_in[:, o_kr:o_kr + C_ROPE]
    packed = jnp.concatenate([
        w_in[:, o_dq:o_dq + 3 * n_d], w_in[:, o_cq:o_kr], w_kr, _rot_half_cols(w_kr),
        jnp.zeros((d, ODD_COLS - (C_Q_RANK + C_KV_RANK + 3 * n_d + 2 * C_ROPE)), F32)], axis=1)
    pad = LANES - C_NOPE - C_ROPE
    wq = mla_w_uq[j].reshape(C_Q_RANK, C_HEADS, C_NOPE + C_ROPE)
    zq = jnp.zeros((C_Q_RANK, C_HEADS, pad), F32)
    wqa = jnp.concatenate([wq, zq], axis=-1).reshape(C_Q_RANK, C_HEADS * LANES)
    wqb = jnp.concatenate([jnp.zeros((C_Q_RANK, C_HEADS, C_NOPE), F32), _rot_half_cols(wq[..., C_NOPE:]), zq],
                          axis=-1).reshape(C_Q_RANK, C_HEADS * LANES)
    wkv = mla_w_ukv[j].reshape(C_KV_RANK, C_HEADS, C_NOPE + C_V)
    wk = jnp.concatenate([wkv[..., :C_NOPE], jnp.zeros((C_KV_RANK, C_HEADS, LANES - C_NOPE), F32)],
                         axis=-1).reshape(C_KV_RANK, C_HEADS * LANES)
    wv = wkv[..., C_NOPE:].reshape(C_KV_RANK, C_HEADS * C_V)
    src = jnp.arange(LANES)[:, None]
    dst = jnp.arange(C_HEADS * LANES)[None, :] % LANES
    place = ((dst >= C_NOPE) & (dst < C_NOPE + C_ROPE) & (dst - C_NOPE == src)).astype(BF16)
    hv = C_HEADS * C_V
    return {
        "w_in": packed.astype(BF16),
        "w_out_a": odd_w_out[j][:hv].astype(BF16),
        "w_out_b": odd_w_out[j][hv:].astype(BF16),
        "q_norm_g": mla_q_norm_g[j].reshape(1, C_Q_RANK),
        "kv_norm_g": mla_kv_norm_g[j].reshape(1, C_KV_RANK),
        "wqa": wqa.astype(BF16), "wqb": wqb.astype(BF16), "wk": wk.astype(BF16), "wv": wv.astype(BF16),
        "place": place,
    }


def _even_mixers(x, g1n, sc1, sh1, ew, s5m, s5_d, s5_w_glu, s5_b_glu):
    b, s, _ = x.shape
    pa = _normmod_mm(x, g1n, sc1, sh1, ew["w_in_a"], tn=A_COLS // 3)
    u = _normmod_mm(x, g1n, sc1, sh1, ew["w_in_b"], tn=B_WIDTH)
    r, v, kk, lw, kd, bd, bv, g = _rwkv_prep(pa, ew)
    y2 = _rwkv_scan(r, v, kk, lw, kd, bd)
    ya = _rwkv_post(y2, bv, g, ew["lnx_g"], ew["lnx_b"], ew["ones_bd"])
    nrows = s // S5_CHUNK
    u_g = jnp.transpose(u.astype(BF16).reshape(b, nrows, S5_CHUNK, B_GROUPS, B_GROUP), (0, 3, 1, 2, 4))
    u_g = u_g.reshape(b, B_GROUPS, nrows, S5_CHUNK * B_GROUP)
    ys_g = _s5_scan(u_g, s5m[0], False) + _s5_scan(u_g, s5m[1], True)
    ys = jnp.transpose(ys_g.reshape(b, B_GROUPS, nrows, S5_CHUNK, B_GROUP), (0, 2, 3, 1, 4)).reshape(b, s, B_WIDTH)
    yb = _s5_glu(ys, u, s5_d, s5_w_glu, s5_b_glu)
    return ya, yb


def _odd_mixers(x, g1n, sc1, sh1, ow, tabs, diff_w, lambda_init):
    s = x.shape[1]
    p = _normmod_mm(x, g1n, sc1, sh1, ow["w_in"], tn=1024)
    mq, mk, mv, q0, q1, dk, dv = _odd_prep(p, tabs["cos"][:s], tabs["sin"][:s], ow)
    yc = _mla_flash(mq, mk, mv)
    yd = _diff_flash(q0, q1, dk, dv, tabs["bias"], diff_w["lq1"], diff_w["lk1"], diff_w["lq2"], diff_w["lk2"],
                     diff_w["subln_g"], lambda_init)
    return yc, yd


def kernel(x_prompt, x_sample, c_prompt, c_sample, ada_w, ada_b, norm1_g, norm2_g, even_w_in, even_w_out, rwkv_mu, rwkv_w0, rwkv_w_up, rwkv_a0, rwkv_a_up, rwkv_g_up, rwkv_k_k, rwkv_k_a, rwkv_r_k, rwkv_lnx_g, rwkv_lnx_b, s5_lam_re, s5_lam_im, s5_log_step, s5_b_re, s5_b_im, s5_c_re, s5_c_im, s5_d, s5_w_glu, s5_b_glu, odd_w_in, odd_w_out, mla_q_norm_g, mla_kv_norm_g, mla_w_uq, mla_w_ukv, diff_lq1, diff_lk1, diff_lq2, diff_lk2, diff_subln_g, rel_bias, ffn_w_up, ffn_conv_w, ffn_conv_b, ffn_w_down, final_g):
    d = D_MODEL
    groups = [(x_prompt, c_prompt), (x_sample, c_sample)]
    nb = [g[0].shape[0] for g in groups]
    c_all = jnp.concatenate([g[1] for g in groups] + [jnp.zeros((SUBLANES - sum(nb), d), F32)], axis=0)
    mod = _ada_mod(c_all, ada_w, ada_b)

    max_s = max(g[0].shape[1] for g in groups)
    cos_t, sin_t = _rope_tables(max_s)
    rel = jnp.arange(-BIAS_HALF, BIAS_HALF, dtype=jnp.int32)
    bias_tab = (jnp.transpose(rel_bias.astype(F32)[_t5_bucket(rel)]) * LOG2E).reshape(D_HEADS, 1, 2 * BIAS_HALF)
    tabs = {"cos": cos_t, "sin": sin_t, "bias": bias_tab}

    xs = [g[0] for g in groups]
    for i in range(DEPTH):
        j = i // 2
        if i % 2 == 0:
            ew = _pack_even(j, even_w_in, even_w_out, rwkv_mu, rwkv_w0, rwkv_w_up, rwkv_a0, rwkv_a_up, rwkv_g_up,
                            rwkv_k_k, rwkv_k_a, rwkv_r_k, rwkv_lnx_g, rwkv_lnx_b)
            s5m = [_s5_matrices(s5_lam_re[j, dr], s5_lam_im[j, dr], s5_log_step[j, dr], s5_b_re[j, dr],
                                s5_b_im[j, dr], s5_c_re[j, dr], s5_c_im[j, dr], dr == 1) for dr in range(2)]
        else:
            ow = _pack_odd(j, odd_w_in, odd_w_out, mla_q_norm_g, mla_kv_norm_g, mla_w_uq, mla_w_ukv)
            diff_w = {"lq1": diff_lq1[j].reshape(1, D_HEAD), "lk1": diff_lk1[j].reshape(1, D_HEAD),
                      "lq2": diff_lq2[j].reshape(1, D_HEAD), "lk2": diff_lk2[j].reshape(1, D_HEAD),
                      "subln_g": diff_subln_g[j].reshape(1, D_V)}
        w_up = ffn_w_up[i].astype(BF16)
        w_down = ffn_w_down[i].astype(BF16)
        row0 = 0
        for gi in range(len(groups)):
            x = xs[gi]
            m = mod[i, row0:row0 + nb[gi]]
            row0 += nb[gi]
            sh1, sc1, g1, sh2, sc2, g2 = [m[:, None, k * d:(k + 1) * d] for k in range(N_MOD)]
            if i % 2 == 0:
                ya, yb = _even_mixers(x, norm1_g[i], sc1, sh1, ew, s5m, s5_d[j].reshape(1, B_WIDTH),
                                      s5_w_glu[j].astype(BF16), s5_b_glu[j].reshape(1, B_WIDTH))
                x = _out_proj(ya, yb, ew["w_out_a"], ew["w_out_b"], x, g1)
            else:
                yc, yd = _odd_mixers(x, norm1_g[i], sc1, sh1, ow, tabs, diff_w, 0.8 - 0.6 * math.exp(-0.3 * i))
                x = _out_proj(yc, yd, ow["w_out_a"], ow["w_out_b"], x, g1)
            u = _normmod_mm(x, norm2_g[i], sc2, sh2, w_up, tn=1024)
            x = _ffn_down(u, ffn_conv_w[i], ffn_conv_b[i].reshape(1, 2 * FFN_HIDDEN), w_down, x, g2,
                          final_g.reshape(1, d), final=(i == DEPTH - 1))
            xs[gi] = x
    return (xs[0], xs[1])
```

```python
import functools
import math

import jax
import jax.numpy as jnp
from jax import lax
from jax.experimental import pallas as pl
from jax.experimental.pallas import tpu as pltpu

F32 = jnp.float32
BF16 = jnp.bfloat16

D_MODEL = 2048
DEPTH = 2
EPS = 1e-6
A_WIDTH = 1024
A_HEAD = 64
A_HEADS = 16
A_LORA = 64
A_GATE_LORA = 128
A_COLS = 3 * A_WIDTH + 2 * A_LORA + 2 * A_LORA + A_GATE_LORA
RWKV_GN_EPS = 64e-5
DECAY_SCALE = math.exp(-0.5)
B_WIDTH = 1024
B_GROUP = 16
B_GROUPS = 64
B_STATE = 64
C_HEADS = 8
C_NOPE = 64
C_ROPE = 32
C_V = 128
C_Q_RANK = 512
C_KV_RANK = 256
ROPE_THETA = 10000.0
D_HEADS = 8
D_HEAD = 64
D_V = 128
SUBLN_EPS = 1e-5
N_BUCKETS = 32
MAX_DISTANCE = 128
FFN_HIDDEN = 5632
N_MOD = 6

LANES = 128
SUBLANES = 8
VMEM_LIMIT_BYTES = 56 * 1024 * 1024

RWKV_CHUNK = 64
S5_CHUNK = 16
LOG2E = 1.4426950408889634

NN = (((1,), (0,)), ((), ()))
NT = (((1,), (1,)), ((), ()))
BATCH_NN = (((2,), (1,)), ((0,), (0,)))


def _params(*sem):
    return pltpu.CompilerParams(dimension_semantics=sem, vmem_limit_bytes=VMEM_LIMIT_BYTES)


def _dot(a, b, dims=NN):
    return lax.dot_general(a, b, dims, preferred_element_type=F32)


def _split2(x):
    hi = x.astype(BF16)
    lo = (x - hi.astype(F32)).astype(BF16)
    return hi, lo


def _dot3(a, b, dims=NN):
    ah, al = _split2(a)
    bh, bl = _split2(b)
    return _dot(ah, bh, dims) + _dot(ah, bl, dims) + _dot(al, bh, dims)


def _dot_exact_rhs(a, b):
    ah, al = _split2(a)
    return _dot(ah, b) + _dot(al, b)


def _ada_kernel(c_ref, w_ref, b_ref, o_ref):
    c = c_ref[...]
    cs = c * jax.nn.sigmoid(c)
    o_ref[0] = _dot(cs.astype(BF16), w_ref[0].astype(BF16)) + b_ref[0]


def _ada_mod(c_all, ada_w, ada_b):
    n = N_MOD * D_MODEL
    tn = 1024
    return pl.pallas_call(
        _ada_kernel,
        grid=(DEPTH, n // tn),
        in_specs=[
            pl.BlockSpec((SUBLANES, D_MODEL), lambda l, j: (0, 0)),
            pl.BlockSpec((1, D_MODEL, tn), lambda l, j: (l, 0, j)),
            pl.BlockSpec((1, 1, tn), lambda l, j: (l, 0, j)),
        ],
        out_specs=pl.BlockSpec((1, SUBLANES, tn), lambda l, j: (l, 0, j)),
        out_shape=jax.ShapeDtypeStruct((DEPTH, SUBLANES, n), F32),
        compiler_params=_params("parallel", "parallel"),
        name="ada_mod",
    )(c_all, ada_w, ada_b.reshape(DEPTH, 1, n))


def _modnorm(x, g, sc, sh):
    y = x * lax.rsqrt(jnp.mean(x * x, axis=-1, keepdims=True) + EPS)
    return (y * g) * (1.0 + sc) + sh


def _normmod_mm_kernel(x_ref, g_ref, sc_ref, sh_ref, w_ref, o_ref, h_scr):
    @pl.when(pl.program_id(2) == 0)
    def _():
        h_scr[...] = _modnorm(x_ref[0], g_ref[...], sc_ref[0], sh_ref[0]).astype(BF16)

    o_ref[0] = _dot(h_scr[...], w_ref[...]).astype(o_ref.dtype)


def _normmod_mm(x, g, sc, sh, w, tn, out_dtype=F32):
    b, s, d = x.shape
    n = w.shape[1]
    tm = min(512, s)
    return pl.pallas_call(
        _normmod_mm_kernel,
        grid=(b, s // tm, n // tn),
        in_specs=[
            pl.BlockSpec((1, tm, d), lambda bi, i, j: (bi, i, 0)),
            pl.BlockSpec((1, d), lambda bi, i, j: (0, 0)),
            pl.BlockSpec((1, 1, d), lambda bi, i, j: (bi, 0, 0)),
            pl.BlockSpec((1, 1, d), lambda bi, i, j: (bi, 0, 0)),
            pl.BlockSpec((d, tn), lambda bi, i, j: (0, j)),
        ],
        out_specs=pl.BlockSpec((1, tm, tn), lambda bi, i, j: (bi, i, j)),
        out_shape=jax.ShapeDtypeStruct((b, s, n), out_dtype),
        scratch_shapes=[pltpu.VMEM((tm, d), BF16)],
        compiler_params=_params("parallel", "parallel", "arbitrary"),
        name="normmod_mm",
    )(x, g.reshape(1, d), sc, sh, w)


def _rwkv_prep_kernel(p_ref, pp_ref, pn_ref, mu_ref, w0_ref, wup_ref, a0_ref, aup_ref, gup_ref, kk_ref, ka_ref,
                      rk_ref, ones_ref, r_out, v_out, kkn_out, lw_out, kd_out, bd_out, bv_out, g_out):
    i = pl.program_id(1)
    last = pl.num_programs(1) - 1
    pa = p_ref[0]
    tm = pa.shape[0]
    row = lax.broadcasted_iota(jnp.int32, (tm, 1), 0)
    prev_row = jnp.where(i == 0, 0.0, pp_ref[0][SUBLANES - 1:SUBLANES, :])
    next_row = jnp.where(i == last, 0.0, pn_ref[0][0:1, :])
    p_prev = jnp.where(row == 0, prev_row, pltpu.roll(pa, 1, 0))
    p_next = jnp.where(row == tm - 1, next_row, pltpu.roll(pa, tm - 1, 0))
    pa = pa + mu_ref[...] * (0.5 * (p_prev + p_next) - pa)

    w = A_WIDTH
    r = pa[:, 0:w]
    k = pa[:, w:2 * w]
    v = pa[:, 2 * w:3 * w]
    dw = pa[:, 3 * w:3 * w + 2 * A_LORA]
    da = pa[:, 3 * w + 2 * A_LORA:3 * w + 4 * A_LORA]
    dg = pa[:, 3 * w + 4 * A_LORA:A_COLS]

    lw = -DECAY_SCALE * jax.nn.sigmoid(w0_ref[...] + _dot(jnp.tanh(dw).astype(BF16), wup_ref[...]))
    icl = jax.nn.sigmoid(a0_ref[...] + _dot(da.astype(BF16), aup_ref[...]))
    g = _dot(jax.nn.sigmoid(dg).astype(BF16), gup_ref[...])

    ones_bd = ones_ref[...]
    kkr = k * kk_ref[...]
    ss = _dot_exact_rhs(kkr * kkr, ones_bd)
    kkn = kkr / jnp.maximum(jnp.sqrt(ss), 1e-12)

    r_out[0] = r
    v_out[0] = v
    kkn_out[0] = kkn
    lw_out[0] = lw
    g_out[0] = g
    bonus = jnp.zeros_like(r)
    for d in range(2):
        icl_d = icl[:, d * w:(d + 1) * w]
        k_d = k * (1.0 + (icl_d - 1.0) * ka_ref[...])
        kd_out[0, :, d * w:(d + 1) * w] = k_d
        bd_out[0, :, d * w:(d + 1) * w] = icl_d * kkn
        bonus = bonus + _dot_exact_rhs(r * k_d * rk_ref[...], ones_bd)
    bv_out[0] = bonus * v


def _rwkv_prep(pa, wts):
    b, s, _ = pa.shape
    tm = min(256, s)
    nh = tm // SUBLANES
    w = A_WIDTH
    full = lambda shape: pl.BlockSpec(shape, lambda bi, i: (0,) * len(shape))
    tok = lambda n: pl.BlockSpec((1, tm, n), lambda bi, i: (bi, i, 0))
    out_shapes = [jax.ShapeDtypeStruct((b, s, n), F32) for n in (w, w, w, 2 * w, 2 * w, 2 * w, w, w)]
    return pl.pallas_call(
        _rwkv_prep_kernel,
        grid=(b, s // tm),
        in_specs=[
            tok(A_COLS),
            pl.BlockSpec((1, SUBLANES, A_COLS), lambda bi, i: (bi, jnp.maximum(i * nh - 1, 0), 0)),
            pl.BlockSpec((1, SUBLANES, A_COLS), lambda bi, i: (bi, jnp.minimum((i + 1) * nh, s // SUBLANES - 1), 0)),
            full((1, A_COLS)), full((1, 2 * w)), full((2 * A_LORA, 2 * w)), full((1, 2 * w)),
            full((2 * A_LORA, 2 * w)), full((A_GATE_LORA, w)), full((1, w)), full((1, w)), full((1, w)),
            full((w, w)),
        ],
        out_specs=[tok(sd.shape[-1]) for sd in out_shapes],
        out_shape=out_shapes,
        compiler_params=_params("parallel", "parallel"),
        name="rwkv_prep",
    )(pa, pa, pa, wts["mu"], wts["w0"], wts["wup"], wts["a0"], wts["aup"], wts["gup"], wts["k_k"], wts["k_a"],
      wts["r_k"], wts["ones_bd"])


def _rwkv_chunk(r, v, kk, lw, kd, bd, s2, reverse):
    t = RWKV_CHUNK
    sign = -1 if reverse else 1
    ri = lax.broadcasted_iota(jnp.int32, (t, t), 0)
    ci = lax.broadcasted_iota(jnp.int32, (t, t), 1)
    tri = jnp.where((ri - ci) * sign >= 0, 1.0, 0.0).astype(BF16)
    lw_hi = lw.astype(BF16)
    rem = lw - lw_hi.astype(F32)
    lw_mid = rem.astype(BF16)
    lw_lo = (rem - lw_mid.astype(F32)).astype(BF16)
    cum = _dot(tri, lw_hi) + _dot(tri, lw_mid) + _dot(tri, lw_lo)
    yield
    tot = jnp.sum(lw, axis=0, keepdims=True)
    gam = jnp.exp(cum)
    gam_ex = jnp.exp(cum - lw)
    ginv = jnp.exp(-cum)
    gend = jnp.exp(tot - cum)
    gtot = jnp.exp(tot)

    width = r.shape[1]
    heads = width // A_HEAD
    lane_head = lax.broadcasted_iota(jnp.int32, (1, width), 1) // A_HEAD

    def stack(x):
        return jnp.concatenate([jnp.where(lane_head == h, x, 0.0) for h in range(heads)], axis=0).astype(BF16)

    q_rk = jnp.concatenate([stack(r * gam), stack(kk * gam_ex)], axis=0)
    k_kb = jnp.concatenate([stack(kd * ginv), stack(bd * ginv)], axis=0)
    k_kb_end = jnp.concatenate([stack(kd * gend), stack(bd * gend)], axis=0)
    vs = stack(v)

    n = heads * t
    a = _dot(q_rk, k_kb, NT)
    yield
    ri2 = lax.broadcasted_iota(jnp.int32, (n, n), 0)
    ci2 = lax.broadcasted_iota(jnp.int32, (n, n), 1)
    order = (ri2 - ci2) * sign
    incl = order >= 0
    strict = order > 0
    a_rk = jnp.where(incl, a[:n, :n], 0.0)
    a_rb = jnp.where(incl, a[:n, n:], 0.0)
    a_kk = jnp.where(strict, a[n:, :n], 0.0)
    a_bk = jnp.where(strict, a[n:, n:], 0.0)

    qs = _dot(q_rk, s2.astype(BF16), NT)
    yield
    akv = _dot(jnp.concatenate([a_rk, a_kk], axis=0).astype(BF16), vs)
    yield

    nb = a_bk.astype(BF16)
    n2 = _dot(nb, nb)
    yield
    n2b = n2.astype(BF16)
    n4 = _dot(n2b, n2b)
    yield
    n3 = _dot(nb, n2b)
    yield
    n4b = n4.astype(BF16)
    n8 = _dot(n4b, n4b)
    yield
    eye = jnp.where(ri2 == ci2, 1.0, 0.0)
    f1 = eye - a_bk + n2 - n3
    n8b = n8.astype(BF16)
    n16 = _dot(n8b, n8b)
    yield
    n12 = _dot(n4b, n8b)
    yield
    f2 = n4 + n8 + n12
    f12 = f1 + _dot(f1.astype(BF16), f2.astype(BF16))
    yield
    n16b = n16.astype(BF16)
    n32 = _dot(n16b, n16b)
    yield
    n48 = _dot(n16b, n32.astype(BF16))
    yield
    f3 = n16 + n32 + n48
    inv = f12 + _dot(f12.astype(BF16), f3.astype(BF16))
    yield

    ps = _dot(inv.astype(BF16), (-qs[n:] - akv[n:]).astype(BF16)).astype(BF16)
    yield
    ys = qs[:n] + akv[:n] + _dot(a_rb.astype(BF16), ps)
    yield
    y = ys[0:t]
    for h in range(1, heads):
        y = y + ys[h * t:(h + 1) * t]
    s2_new = s2 * gtot + _dot(jnp.concatenate([vs, ps], axis=0).astype(F32).T.astype(BF16), k_kb_end)
    return y, s2_new


def _run_lockstep(gens):
    results = [None] * len(gens)
    active = list(range(len(gens)))
    while active:
        for i in list(active):
            try:
                next(gens[i])
            except StopIteration as e:
                results[i] = e.value
                active.remove(i)
    return results


RWKV_CHAIN_LANES = 256
RWKV_CHAINS_PER_STEP = 4


def _rwkv_scan_kernel(rf_ref, vf_ref, kkf_ref, lwf_ref, kdf_ref, bdf_ref,
                      rb_ref, vb_ref, kkb_ref, lwb_ref, kdb_ref, bdb_ref, yf_ref, yb_ref, s_scr):
    @pl.when(pl.program_id(2) == 0)
    def _():
        s_scr[...] = jnp.zeros_like(s_scr)

    dirs = ((rf_ref, vf_ref, kkf_ref, lwf_ref, kdf_ref, bdf_ref, yf_ref),
            (rb_ref, vb_ref, kkb_ref, lwb_ref, kdb_ref, bdb_ref, yb_ref))
    chains = []
    for d, (r_ref, v_ref, kk_ref, lw_ref, kd_ref, bd_ref, y_ref) in enumerate(dirs):
        for p in range(RWKV_CHAINS_PER_STEP):
            sl = slice(p * RWKV_CHAIN_LANES, (p + 1) * RWKV_CHAIN_LANES)
            chains.append((d, p, sl, y_ref, _rwkv_chunk(
                r_ref[0, :, sl], v_ref[0, :, sl], kk_ref[0, :, sl], lw_ref[0, :, sl], kd_ref[0, :, sl],
                bd_ref[0, :, sl], s_scr[d, p], reverse=(d == 1))))
    results = _run_lockstep([c[4] for c in chains])
    for (d, p, sl, y_ref, _), (y, s_new) in zip(chains, results):
        y_ref[0, :, sl] = y
        s_scr[d, p] = s_new


def _rwkv_scan(r, v, kk, lw, kd, bd):
    b, s, w = r.shape
    t = RWKV_CHUNK
    nc = s // t
    pw = RWKV_CHAINS_PER_STEP * RWKV_CHAIN_LANES
    ngrp = w // pw
    fwd = lambda off: pl.BlockSpec((1, t, pw), lambda bi, g, c: (bi, c, g + off))
    bwd = lambda off: pl.BlockSpec((1, t, pw), lambda bi, g, c: (bi, nc - 1 - c, g + off))
    return pl.pallas_call(
        _rwkv_scan_kernel,
        grid=(b, ngrp, nc),
        in_specs=[fwd(0)] * 6 + [bwd(0)] * 3 + [bwd(ngrp)] * 3,
        out_specs=[fwd(0), bwd(0)],
        out_shape=[jax.ShapeDtypeStruct((b, s, w), F32)] * 2,
        scratch_shapes=[pltpu.VMEM((2, RWKV_CHAINS_PER_STEP, RWKV_CHAIN_LANES, RWKV_CHAIN_LANES), F32)],
        compiler_params=_params("parallel", "parallel", "arbitrary"),
        name="rwkv_scan",
    )(r, v, kk, lw, kd, bd, r, v, kk, lw, kd, bd)


def _rwkv_post_kernel(yf_ref, yb_ref, bv_ref, g_ref, lng_ref, lnb_ref, ones_ref, o_ref):
    y = yf_ref[0] + yb_ref[0]
    ones_bd = ones_ref[...]
    mean = _dot_exact_rhs(y, ones_bd) * (1.0 / A_HEAD)
    yc = y - mean
    var = _dot_exact_rhs(yc * yc, ones_bd) * (1.0 / A_HEAD)
    yn = yc * lax.rsqrt(var + RWKV_GN_EPS) * lng_ref[...] + lnb_ref[...]
    o_ref[0] = ((yn + bv_ref[0]) * g_ref[0]).astype(o_ref.dtype)


def _rwkv_post(yf, yb, bv, g, lng, lnb, ones_bd):
    b, s, w = bv.shape
    tm = min(512, s)
    full = lambda shape: pl.BlockSpec(shape, lambda bi, i: (0,) * len(shape))
    tok = lambda n: pl.BlockSpec((1, tm, n), lambda bi, i: (bi, i, 0))
    return pl.pallas_call(
        _rwkv_post_kernel,
        grid=(b, s // tm),
        in_specs=[tok(w), tok(w), tok(w), tok(w), full((1, w)), full((1, w)), full((w, w))],
        out_specs=tok(w),
        out_shape=jax.ShapeDtypeStruct((b, s, w), BF16),
        compiler_params=_params("parallel", "parallel"),
        name="rwkv_post",
    )(yf, yb, bv, g, lng, lnb, ones_bd)


def _s5_kernel(u_ref, m_ref, w_ref, ws_ref, v_ref, a_ref, y_ref, x_scr, xs_scr, h_scr, carry_scr, *, rows, reverse):
    g = B_GROUPS

    @pl.when(pl.program_id(1) == 0)
    def _():
        carry_scr[...] = jnp.zeros_like(carry_scr)

    u = u_ref[0]
    x_scr[...] = lax.dot_general(u, w_ref[...], BATCH_NN, preferred_element_type=F32).reshape(g * rows, LANES)
    xs_scr[...] = lax.dot_general(u, ws_ref[...], BATCH_NN, preferred_element_type=F32).reshape(g * rows, LANES)
    a1 = a_ref[0]
    a2 = a_ref[1]
    a2s = a_ref[2]

    def step(i, carry):
        h, hs = carry
        r = (rows - 1 - i) if reverse else i
        idx = pl.ds(r, g, stride=rows)
        h_scr[idx, :] = h
        hn = a1 * h + a2 * hs + x_scr[idx, :]
        hsn = a1 * hs + a2s * h + xs_scr[idx, :]
        return hn, hsn

    h, hs = lax.fori_loop(0, rows, step, (carry_scr[0], carry_scr[1]))
    carry_scr[0] = h
    carry_scr[1] = hs
    hprev = h_scr[...].reshape(g, rows, LANES).astype(BF16)
    y_ref[0] = (lax.dot_general(u, m_ref[...], BATCH_NN, preferred_element_type=F32)
                + lax.dot_general(hprev, v_ref[...], BATCH_NN, preferred_element_type=F32))


def _s5_scan(u_g, mats, reverse):
    b, g, nrows, cw = u_g.shape
    rows = min(64, nrows)
    nsb = nrows // rows
    whole = pl.BlockSpec(memory_space=pltpu.VMEM)
    idx = (lambda bi, i: (bi, 0, nsb - 1 - i, 0)) if reverse else (lambda bi, i: (bi, 0, i, 0))
    return pl.pallas_call(
        functools.partial(_s5_kernel, rows=rows, reverse=reverse),
        grid=(b, nsb),
        in_specs=[pl.BlockSpec((1, g, rows, cw), idx), whole, whole, whole, whole, whole],
        out_specs=pl.BlockSpec((1, g, rows, cw), idx),
        out_shape=jax.ShapeDtypeStruct((b, g, nrows, cw), F32),
        scratch_shapes=[pltpu.VMEM((g * rows, LANES), F32), pltpu.VMEM((g * rows, LANES), F32),
                        pltpu.VMEM((g * rows, LANES), F32), pltpu.VMEM((2, g, LANES), F32)],
        compiler_params=_params("parallel", "arbitrary"),
        name="s5_scan_bwd" if reverse else "s5_scan_fwd",
    )(u_g, mats["m"], mats["w"], mats["ws"], mats["v"], mats["a"])


def _s5_matrices(lam_re, lam_im, log_step, b_re, b_im, c_re, c_im, reverse):
    hp = lax.Precision.HIGHEST
    t = S5_CHUNK
    g, p, c = B_GROUPS, B_STATE, B_GROUP
    lr, li = lam_re.astype(F32), lam_im.astype(F32)
    step = jnp.exp(log_step.astype(F32))[:, None]
    ar, ai = jnp.exp(lr * step) * jnp.cos(li * step), jnp.exp(lr * step) * jnp.sin(li * step)
    den = lr * lr + li * li
    nr, ni = ar - 1.0, ai
    fr, fi = (nr * lr + ni * li) / den, (ni * lr - nr * li) / den
    br, bi = b_re.astype(F32), b_im.astype(F32)
    bbr = fr[..., None] * br - fi[..., None] * bi
    bbi = fr[..., None] * bi + fi[..., None] * br
    cr, cim = c_re.astype(F32), c_im.astype(F32)
    taus = jnp.arange(t + 1, dtype=F32)[:, None, None]
    mag = jnp.exp(lr * step * taus)
    pr, pi = mag * jnp.cos(li * step * taus), mag * jnp.sin(li * step * taus)
    cpr = cr[None] * pr[:, :, None, :] - cim[None] * pi[:, :, None, :]
    cpi = cr[None] * pi[:, :, None, :] + cim[None] * pr[:, :, None, :]
    kern = (jnp.einsum("tgcp,gpd->tgcd", cpr, bbr, precision=hp)
            - jnp.einsum("tgcp,gpd->tgcd", cpi, bbi, precision=hp))
    s_idx = jnp.arange(t)[:, None]
    t_idx = jnp.arange(t)[None, :]
    lag = (s_idx - t_idx) if reverse else (t_idx - s_idx)
    kk = jnp.where((lag >= 0)[:, :, None, None, None], kern[jnp.clip(lag, 0, t)], 0.0)
    m = jnp.transpose(kk, (2, 0, 4, 1, 3)).reshape(g, t * c, t * c)
    e = jnp.arange(t) if reverse else (t - 1 - jnp.arange(t))
    pre, pie = pr[e], pi[e]
    wre = pre[..., None] * bbr[None] - pie[..., None] * bbi[None]
    wim = pre[..., None] * bbi[None] + pie[..., None] * bbr[None]
    wre = jnp.transpose(wre, (1, 0, 3, 2)).reshape(g, t * c, p)
    wim = jnp.transpose(wim, (1, 0, 3, 2)).reshape(g, t * c, p)
    w = jnp.concatenate([wre, wim], axis=-1)
    ws = jnp.concatenate([wim, wre], axis=-1)
    f = (t - jnp.arange(t)) if reverse else (jnp.arange(t) + 1)
    vre = jnp.transpose(cpr[f], (1, 3, 0, 2)).reshape(g, p, t * c)
    vim = jnp.transpose(-cpi[f], (1, 3, 0, 2)).reshape(g, p, t * c)
    v = jnp.concatenate([vre, vim], axis=1)
    atr, ati = pr[t], pi[t]
    a = jnp.stack([jnp.concatenate([atr, atr], -1), jnp.concatenate([-ati, ati], -1),
                   jnp.concatenate([ati, -ati], -1)])
    return {"m": m.astype(BF16), "w": w.astype(BF16), "ws": ws.astype(BF16), "v": v.astype(BF16), "a": a}


def _s5_glu_kernel(ys_ref, u_ref, d_ref, w_ref, b_ref, o_ref):
    y = ys_ref[0] + u_ref[0] * d_ref[...]
    z = jax.nn.gelu(y)
    gate = jax.nn.sigmoid(_dot(z.astype(BF16), w_ref[...]) + b_ref[...])
    o_ref[0] = (z * gate).astype(o_ref.dtype)


def _s5_glu(ys, u, d_skip, w_glu, b_glu):
    b, s, w = u.shape
    tm = min(512, s)
    full = lambda shape: pl.BlockSpec(shape, lambda bi, i: (0,) * len(shape))
    tok = pl.BlockSpec((1, tm, w), lambda bi, i: (bi, i, 0))
    return pl.pallas_call(
        _s5_glu_kernel,
        grid=(b, s // tm),
        in_specs=[tok, tok, full((1, w)), full((w, w)), full((1, w))],
        out_specs=tok,
        out_shape=jax.ShapeDtypeStruct((b, s, w), BF16),
        compiler_params=_params("parallel", "parallel"),
        name="s5_glu",
    )(ys, u, d_skip, w_glu, b_glu)


def _out_proj_kernel(a_ref, b_ref, wa_ref, wb_ref, x_ref, g_ref, o_ref):
    mix = _dot(a_ref[0], wa_ref[...]) + _dot(b_ref[0], wb_ref[...])
    o_ref[0] = x_ref[0] + g_ref[0] * mix


def _out_proj(a, bb, wa, wb, x, gate):
    b, s, d = x.shape
    k = a.shape[-1]
    tm = min(512, s)
    tn = 1024
    return pl.pallas_call(
        _out_proj_kernel,
        grid=(b, s // tm, d // tn),
        in_specs=[
            pl.BlockSpec((1, tm, k), lambda bi, i, j: (bi, i, 0)),
            pl.BlockSpec((1, tm, k), lambda bi, i, j: (bi, i, 0)),
            pl.BlockSpec((k, tn), lambda bi, i, j: (0, j)),
            pl.BlockSpec((k, tn), lambda bi, i, j: (0, j)),
            pl.BlockSpec((1, tm, tn), lambda bi, i, j: (bi, i, j)),
            pl.BlockSpec((1, 1, tn), lambda bi, i, j: (bi, 0, j)),
        ],
        out_specs=pl.BlockSpec((1, tm, tn), lambda bi, i, j: (bi, i, j)),
        out_shape=jax.ShapeDtypeStruct((b, s, d), F32),
        compiler_params=_params("parallel", "parallel", "parallel"),
        name="out_proj",
    )(a, bb, wa, wb, x, gate)


def _ffn_down_kernel(uv_ref, ug_ref, uvp_ref, uvn_ref, ugp_ref, ugn_ref, cwv_ref, cwg_ref, cbv_ref, cbg_ref,
                     wd_ref, x_ref, g_ref, fg_ref, o_ref, acc_scr, *, final):
    i = pl.program_id(1)
    j = pl.program_id(2)
    first = i == 0
    last = i == pl.num_programs(1) - 1

    @pl.when(j == 0)
    def _():
        acc_scr[...] = jnp.zeros_like(acc_scr)

    def conv(u_ref, up_ref, un_ref, cw_ref, cb_ref):
        u = u_ref[0]
        tm = u.shape[0]
        row = lax.broadcasted_iota(jnp.int32, (tm, 1), 0)
        prev_row = jnp.where(first, 0.0, up_ref[0][SUBLANES - 1:SUBLANES, :])
        next_row = jnp.where(last, 0.0, un_ref[0][0:1, :])
        u_prev = jnp.where(row == 0, prev_row, pltpu.roll(u, 1, 0))
        u_next = jnp.where(row == tm - 1, next_row, pltpu.roll(u, tm - 1, 0))
        cw = cw_ref[...]
        return u_prev * cw[0:1] + u * cw[1:2] + u_next * cw[2:3] + cb_ref[...]

    val = conv(uv_ref, uvp_ref, uvn_ref, cwv_ref, cbv_ref)
    gate = conv(ug_ref, ugp_ref, ugn_ref, cwg_ref, cbg_ref)
    act = (gate * jax.nn.sigmoid(gate)) * val
    acc_scr[...] += _dot(act.astype(BF16), wd_ref[...])

    @pl.when(j == pl.num_programs(2) - 1)
    def _():
        xn = x_ref[0] + g_ref[0] * acc_scr[...]
        if final:
            xn = xn * lax.rsqrt(jnp.mean(xn * xn, axis=-1, keepdims=True) + EPS) * fg_ref[...]
        o_ref[0] = xn


def _ffn_down(u, conv_w, conv_b, w_down, x, gate, final_g, final):
    b, s, d = x.shape
    f = FFN_HIDDEN
    tm = min(512, s)
    tf = 512
    nf = f // tf
    nh = tm // SUBLANES
    nrow8 = s // SUBLANES
    main = lambda off: pl.BlockSpec((1, tm, tf), lambda bi, i, j: (bi, i, j + off))
    prev = lambda off: pl.BlockSpec((1, SUBLANES, tf), lambda bi, i, j: (bi, jnp.maximum(i * nh - 1, 0), j + off))
    nxt = lambda off: pl.BlockSpec((1, SUBLANES, tf),
                                   lambda bi, i, j: (bi, jnp.minimum((i + 1) * nh, nrow8 - 1), j + off))
    cw = lambda off: pl.BlockSpec((3, tf), lambda bi, i, j: (0, j + off))
    cb = lambda off: pl.BlockSpec((1, tf), lambda bi, i, j: (0, j + off))
    return pl.pallas_call(
        functools.partial(_ffn_down_kernel, final=final),
        grid=(b, s // tm, nf),
        in_specs=[
            main(0), main(nf), prev(0), nxt(0), prev(nf), nxt(nf), cw(0), cw(nf), cb(0), cb(nf),
            pl.BlockSpec((tf, d), lambda bi, i, j: (j, 0)),
            pl.BlockSpec((1, tm, d), lambda bi, i, j: (bi, i, 0)),
            pl.BlockSpec((1, 1, d), lambda bi, i, j: (bi, 0, 0)),
            pl.BlockSpec((1, d), lambda bi, i, j: (0, 0)),
        ],
        out_specs=pl.BlockSpec((1, tm, d), lambda bi, i, j: (bi, i, 0)),
        out_shape=jax.ShapeDtypeStruct((b, s, d), F32),
        scratch_shapes=[pltpu.VMEM((tm, d), F32)],
        compiler_params=_params("parallel", "parallel", "arbitrary"),
        name="ffn_down",
    )(u, u, u, u, u, u, conv_w, conv_w, conv_b, conv_b, w_down, x, gate, final_g)


def _rms(x, g, eps):
    return x * lax.rsqrt(jnp.mean(x * x, axis=-1, keepdims=True) + eps) * g


def _odd_prep_kernel(cq_ref, ckv_ref, kr_ref, dq_ref, dk_ref, dv_ref, cos_ref, sin_ref, qg_ref, kvg_ref,
                     wqa_ref, wqb_ref, wk_ref, wv_ref, place_ref,
                     mq_out, mk_out, mv_out, q0_out, q1_out, dk_out, dv_out):
    cosq = cos_ref[...]
    sinq = sin_ref[...]
    qn = _rms(cq_ref[0], qg_ref[...], EPS).astype(BF16)
    qa = _dot(qn, wqa_ref[...])
    qb = _dot(qn, wqb_ref[...])
    mla_scale = (C_NOPE + C_ROPE) ** -0.5 * LOG2E
    for h in range(C_HEADS):
        sl = slice(h * LANES, (h + 1) * LANES)
        mq_out[0, :, sl] = ((qa[:, sl] * cosq + qb[:, sl] * sinq) * mla_scale).astype(BF16)
    kvn = _rms(ckv_ref[0], kvg_ref[...], EPS).astype(BF16)
    kr = kr_ref[0]
    cos_k = pltpu.roll(cosq, LANES - C_NOPE, 1)
    sin_k = pltpu.roll(sinq, LANES - C_NOPE, 1)
    partner = pltpu.roll(kr, LANES - C_ROPE, 1)
    lane = lax.broadcasted_iota(jnp.int32, (1, LANES), 1)
    kr_rope = jnp.where(lane < C_ROPE, kr * cos_k + partner * sin_k, 0.0)
    mk_out[0] = (_dot(kvn, wk_ref[...]) + _dot(kr_rope.astype(BF16), place_ref[...])).astype(BF16)
    mv_out[0] = _dot(kvn, wv_ref[...]).astype(BF16)
    dq = dq_ref[0] * (D_HEAD ** -0.5 * LOG2E)
    lane_w = lax.broadcasted_iota(jnp.int32, (1, dq.shape[1]), 1)
    first_map = (lane_w % LANES) < D_HEAD
    q0_out[0] = jnp.where(first_map, dq, 0.0).astype(BF16)
    q1_out[0] = jnp.where(first_map, 0.0, dq).astype(BF16)
    dk_out[0] = dk_ref[0].astype(BF16)
    dv_out[0] = dv_ref[0].astype(BF16)


ODD_DQ = 0
ODD_DK = ODD_DQ + D_HEADS * 2 * D_HEAD
ODD_DV = ODD_DK + D_HEADS * 2 * D_HEAD
ODD_CQ = ODD_DV + D_HEADS * D_V
ODD_CKV = ODD_CQ + C_Q_RANK
ODD_KR = ODD_CKV + C_KV_RANK
ODD_COLS = 4096


def _odd_prep(p, cos_t, sin_t, wts):
    b, s, _ = p.shape
    tm = min(256, s)
    hw = C_HEADS * LANES
    full = lambda shape: pl.BlockSpec(shape, lambda bi, i: (0,) * len(shape))
    col = lambda off, n: pl.BlockSpec((1, tm, n), lambda bi, i: (bi, i, off // n))
    tok = pl.BlockSpec((1, tm, hw), lambda bi, i: (bi, i, 0))
    tab = pl.BlockSpec((tm, LANES), lambda bi, i: (i, 0))
    outs = [jax.ShapeDtypeStruct((b, s, hw), BF16)] * 7
    return pl.pallas_call(
        _odd_prep_kernel,
        grid=(b, s // tm),
        in_specs=[
            col(ODD_CQ, C_Q_RANK), col(ODD_CKV, C_KV_RANK), col(ODD_KR, LANES),
            col(ODD_DQ, hw), col(ODD_DK, hw), col(ODD_DV, hw), tab, tab,
            full((1, C_Q_RANK)), full((1, C_KV_RANK)),
            full((C_Q_RANK, hw)), full((C_Q_RANK, hw)), full((C_KV_RANK, hw)), full((C_KV_RANK, hw)),
            full((LANES, hw)),
        ],
        out_specs=[tok] * 7,
        out_shape=outs,
        compiler_params=_params("parallel", "parallel"),
        name="odd_prep",
    )(p, p, p, p, p, p, cos_t, sin_t, wts["q_norm_g"], wts["kv_norm_g"], wts["wqa"], wts["wqb"], wts["wk"],
      wts["wv"], wts["place"])


def _with_ones(v):
    return jnp.concatenate([v, jnp.ones_like(v)], axis=1)


def _online_softmax_step(s, shift, v1, m_scr, acc_scr):
    m_prev = m_scr[...]
    m_new = jnp.maximum(m_prev, jnp.max(s, axis=-1, keepdims=True) + shift)
    alpha = jnp.exp2(m_prev - m_new)
    p = jnp.exp2(s - jnp.tile(m_new - shift, (1, s.shape[1] // LANES)))
    acc_scr[...] = jnp.tile(alpha, (1, 2)) * acc_scr[...] + _dot(p.astype(BF16), v1)
    m_scr[...] = m_new


def _mla_flash_kernel(q_ref, k_ref, v_ref, o_ref, m_scr, acc_scr):
    kj = pl.program_id(3)

    @pl.when(kj == 0)
    def _():
        m_scr[...] = jnp.full_like(m_scr, -jnp.inf)
        acc_scr[...] = jnp.zeros_like(acc_scr)

    s = _dot(q_ref[0], k_ref[0], NT)
    _online_softmax_step(s, 0.0, _with_ones(v_ref[0]), m_scr, acc_scr)

    @pl.when(kj == pl.num_programs(3) - 1)
    def _():
        acc = acc_scr[...]
        o_ref[0] = (acc[:, :LANES] / acc[:, LANES:]).astype(o_ref.dtype)


def _mla_flash(q, k, v):
    b, s, hw = q.shape
    nh = hw // LANES
    tq = min(512, s)
    tk = min(1024, s)
    qspec = pl.BlockSpec((1, tq, LANES), lambda bi, h, i, j: (bi, i, h))
    kspec = pl.BlockSpec((1, tk, LANES), lambda bi, h, i, j: (bi, j, h))
    return pl.pallas_call(
        _mla_flash_kernel,
        grid=(b, nh, s // tq, s // tk),
        in_specs=[qspec, kspec, kspec],
        out_specs=qspec,
        out_shape=jax.ShapeDtypeStruct((b, s, hw), BF16),
        scratch_shapes=[pltpu.VMEM((tq, LANES), F32), pltpu.VMEM((tq, 2 * LANES), F32)],
        compiler_params=_params("parallel", "parallel", "parallel", "arbitrary"),
        name="mla_flash",
    )(q, k, v)


BIAS_HALF = 256


def _diff_flash_kernel(q0_ref, q1_ref, k_ref, v_ref, tab_ref, lq1_ref, lk1_ref, lq2_ref, lk2_ref, sg_ref, o_ref,
                       m_scr, acc_scr, bias_scr, *, tq, tk, near, lambda_init):
    kj = pl.program_id(3)
    off = kj * tk - pl.program_id(2) * tq

    @pl.when(kj == 0)
    def _():
        m_scr[...] = jnp.full_like(m_scr, -jnp.inf)
        acc_scr[...] = jnp.zeros_like(acc_scr)

    tab = tab_ref[0]
    k = k_ref[0]
    v1 = _with_ones(v_ref[0])

    def update(bias, shift):
        for m, q_ref in enumerate((q0_ref, q1_ref)):
            s = _dot(q_ref[0], k, NT)
            if bias is not None:
                s = s + bias
            _online_softmax_step(s, shift, v1, m_scr.at[m], acc_scr.at[m])

    def toeplitz(d):
        r = tab[:, d + LANES:d + 3 * LANES]
        rows = jnp.broadcast_to(r, (LANES, 2 * LANES))
        return pltpu.roll(rows, LANES, 1, stride=1, stride_axis=0)[:, :LANES]

    far_left = tab[:, 0:1]
    far_right = tab[:, 2 * BIAS_HALF - 1:2 * BIAS_HALF]

    for d0 in near:
        @pl.when(off == d0)
        def _(d0=d0):
            blocks = {}
            for ri in range(tq // LANES):
                for cj in range(tk // LANES):
                    d = d0 + (cj - ri) * LANES
                    sl = (slice(ri * LANES, (ri + 1) * LANES), slice(cj * LANES, (cj + 1) * LANES))
                    if d <= -BIAS_HALF:
                        bias_scr[sl] = jnp.broadcast_to(far_left, (LANES, LANES))
                    elif d >= BIAS_HALF:
                        bias_scr[sl] = jnp.broadcast_to(far_right, (LANES, LANES))
                    else:
                        if d not in blocks:
                            blocks[d] = toeplitz(d)
                        bias_scr[sl] = blocks[d]
            update(bias_scr[...], 0.0)

    @pl.when(off < near[0])
    def _():
        update(None, far_left)

    @pl.when(off > near[-1])
    def _():
        update(None, far_right)

    @pl.when(kj == pl.num_programs(3) - 1)
    def _():
        lam = (jnp.exp(jnp.sum(lq1_ref[...] * lk1_ref[...], axis=-1, keepdims=True))
               - jnp.exp(jnp.sum(lq2_ref[...] * lk2_ref[...], axis=-1, keepdims=True)) + lambda_init)
        a0 = acc_scr[0]
        a1 = acc_scr[1]
        o = a0[:, :LANES] / a0[:, LANES:] - lam * (a1[:, :LANES] / a1[:, LANES:])
        o = _rms(o, sg_ref[...], SUBLN_EPS) * (1.0 - lambda_init)
        o_ref[0] = o.astype(o_ref.dtype)


def _diff_flash(q0, q1, k, v, tab, lq1, lk1, lq2, lk2, subln_g, lambda_init):
    b, s, hw = q0.shape
    nh = hw // LANES
    tq = min(512, s)
    tk = min(1024, s)
    offs = sorted({j * tk - i * tq for i in range(s // tq) for j in range(s // tk)})
    near = tuple(d for d in offs if d - (tq - 1) < BIAS_HALF and d + tk - 1 > -BIAS_HALF)
    qspec = pl.BlockSpec((1, tq, LANES), lambda bi, h, i, j: (bi, i, h))
    kspec = pl.BlockSpec((1, tk, LANES), lambda bi, h, i, j: (bi, j, h))
    vec = lambda n: pl.BlockSpec((1, n), lambda bi, h, i, j: (0, 0))
    return pl.pallas_call(
        functools.partial(_diff_flash_kernel, tq=tq, tk=tk, near=near, lambda_init=lambda_init),
        grid=(b, nh, s // tq, s // tk),
        in_specs=[qspec, qspec, kspec, kspec,
                  pl.BlockSpec((1, 1, 2 * BIAS_HALF), lambda bi, h, i, j: (h, 0, 0)),
                  vec(D_HEAD), vec(D_HEAD), vec(D_HEAD), vec(D_HEAD), vec(D_V)],
        out_specs=qspec,
        out_shape=jax.ShapeDtypeStruct((b, s, hw), BF16),
        scratch_shapes=[pltpu.VMEM((2, tq, LANES), F32), pltpu.VMEM((2, tq, 2 * LANES), F32),
                        pltpu.VMEM((tq, tk), F32)],
        compiler_params=_params("parallel", "parallel", "parallel", "arbitrary"),
        name="diff_flash",
    )(q0, q1, k, v, tab, lq1, lk1, lq2, lk2, subln_g)


def _t5_bucket(rel):
    half = N_BUCKETS // 2
    max_exact = half // 2
    n = jnp.abs(rel)
    large = max_exact + (jnp.log(jnp.maximum(n, 1).astype(jnp.float32) / max_exact)
                         / math.log(MAX_DISTANCE / max_exact) * (half - max_exact)).astype(jnp.int32)
    large = jnp.minimum(large, half - 1)
    return jnp.where(rel > 0, half, 0) + jnp.where(n < max_exact, n, large)


def _rope_tables(s):
    inv = 1.0 / (ROPE_THETA ** (jnp.arange(0, C_ROPE, 2, dtype=F32) / C_ROPE))
    ang = jnp.arange(s, dtype=F32)[:, None] * inv[None, :]
    cos, sin = jnp.cos(ang), jnp.sin(ang)
    pad = LANES - C_NOPE - C_ROPE
    cos_t = jnp.concatenate([jnp.ones((s, C_NOPE), F32), cos, cos, jnp.zeros((s, pad), F32)], axis=-1)
    sin_t = jnp.concatenate([jnp.zeros((s, C_NOPE), F32), sin, sin, jnp.zeros((s, pad), F32)], axis=-1)
    return cos_t, sin_t


def _rot_half_cols(w):
    h = w.shape[-1] // 2
    return jnp.concatenate([-w[..., h:], w[..., :h]], axis=-1)


def _pack_even(j, even_w_in, even_w_out, rwkv_mu, rwkv_w0, rwkv_w_up, rwkv_a0, rwkv_a_up, rwkv_g_up, rwkv_k_k,
               rwkv_k_a, rwkv_r_k, rwkv_lnx_g, rwkv_lnx_b):
    w = A_WIDTH
    z = jnp.zeros((A_LORA, w), F32)
    blockdiag = lambda m: jnp.concatenate(
        [jnp.concatenate([m[0], z], axis=1), jnp.concatenate([z, m[1]], axis=1)], axis=0)
    head = jnp.arange(w) // A_HEAD
    return {
        "w_in_a": even_w_in[j][:, :A_COLS].astype(BF16),
        "w_in_b": even_w_in[j][:, A_COLS:].astype(BF16),
        "w_out_a": even_w_out[j][:w].astype(BF16),
        "w_out_b": even_w_out[j][w:].astype(BF16),
        "mu": rwkv_mu[j].reshape(1, A_COLS),
        "w0": rwkv_w0[j].reshape(1, 2 * w),
        "wup": blockdiag(rwkv_w_up[j]).astype(BF16),
        "a0": rwkv_a0[j].reshape(1, 2 * w),
        "aup": blockdiag(rwkv_a_up[j]).astype(BF16),
        "gup": rwkv_g_up[j].astype(BF16),
        "k_k": rwkv_k_k[j].reshape(1, w),
        "k_a": rwkv_k_a[j].reshape(1, w),
        "r_k": rwkv_r_k[j].reshape(1, w),
        "lnx_g": rwkv_lnx_g[j].reshape(1, w),
        "lnx_b": rwkv_lnx_b[j].reshape(1, w),
        "ones_bd": (head[:, None] == head[None, :]).astype(BF16),
    }


def _pack_odd(j, odd_w_in, odd_w_out, mla_q_norm_g, mla_kv_norm_g, mla_w_uq, mla_w_ukv):
    d = D_MODEL
    w_in = odd_w_in[j]
    o_cq, o_ckv = 0, C_Q_RANK
    o_kr = o_ckv + C_KV_RANK
    o_dq = o_kr + C_ROPE
    n_d = D_HEADS * 2 * D_HEAD
    w_kr = w_in[:, o_kr:o_kr + C_ROPE]
    packed = jnp.concatenate([
        w_in[:, o_dq:o_dq + 3 * n_d], w_in[:, o_cq:o_kr], w_kr, _rot_half_cols(w_kr),
        jnp.zeros((d, ODD_COLS - (C_Q_RANK + C_KV_RANK + 3 * n_d + 2 * C_ROPE)), F32)], axis=1)
    pad = LANES - C_NOPE - C_ROPE
    wq = mla_w_uq[j].reshape(C_Q_RANK, C_HEADS, C_NOPE + C_ROPE)
    zq = jnp.zeros((C_Q_RANK, C_HEADS, pad), F32)
    wqa = jnp.concatenate([wq, zq], axis=-1).reshape(C_Q_RANK, C_HEADS * LANES)
    wqb = jnp.concatenate([jnp.zeros((C_Q_RANK, C_HEADS, C_NOPE), F32), _rot_half_cols(wq[..., C_NOPE:]), zq],
                          axis=-1).reshape(C_Q_RANK, C_HEADS * LANES)
    wkv = mla_w_ukv[j].reshape(C_KV_RANK, C_HEADS, C_NOPE + C_V)
    wk = jnp.concatenate([wkv[..., :C_NOPE], jnp.zeros((C_KV_RANK, C_HEADS, LANES - C_NOPE), F32)],
                         axis=-1).reshape(C_KV_RANK, C_HEADS * LANES)
    wv = wkv[..., C_NOPE:].reshape(C_KV_RANK, C_HEADS * C_V)
    src = jnp.arange(LANES)[:, None]
    dst = jnp.arange(C_HEADS * LANES)[None, :] % LANES
    place = ((dst >= C_NOPE) & (dst < C_NOPE + C_ROPE) & (dst - C_NOPE == src)).astype(BF16)
    hv = C_HEADS * C_V
    return {
        "w_in": packed.astype(BF16),
        "w_out_a": odd_w_out[j][:hv].astype(BF16),
        "w_out_b": odd_w_out[j][hv:].astype(BF16),
        "q_norm_g": mla_q_norm_g[j].reshape(1, C_Q_RANK),
        "kv_norm_g": mla_kv_norm_g[j].reshape(1, C_KV_RANK),
        "wqa": wqa.astype(BF16), "wqb": wqb.astype(BF16), "wk": wk.astype(BF16), "wv": wv.astype(BF16),
        "place": place,
    }


def _even_mixers(x, g1n, sc1, sh1, ew, s5m, s5_d, s5_w_glu, s5_b_glu):
    b, s, _ = x.shape
    pa = _normmod_mm(x, g1n, sc1, sh1, ew["w_in_a"], tn=A_COLS // 3)
    u = _normmod_mm(x, g1n, sc1, sh1, ew["w_in_b"], tn=B_WIDTH)
    r, v, kk, lw, kd, bd, bv, g = _rwkv_prep(pa, ew)
    yf, yr = _rwkv_scan(r, v, kk, lw, kd, bd)
    ya = _rwkv_post(yf, yr, bv, g, ew["lnx_g"], ew["lnx_b"], ew["ones_bd"])
    nrows = s // S5_CHUNK
    u_g = jnp.transpose(u.astype(BF16).reshape(b, nrows, S5_CHUNK, B_GROUPS, B_GROUP), (0, 3, 1, 2, 4))
    u_g = u_g.reshape(b, B_GROUPS, nrows, S5_CHUNK * B_GROUP)
    ys_g = _s5_scan(u_g, s5m[0], False) + _s5_scan(u_g, s5m[1], True)
    ys = jnp.transpose(ys_g.reshape(b, B_GROUPS, nrows, S5_CHUNK, B_GROUP), (0, 2, 3, 1, 4)).reshape(b, s, B_WIDTH)
    yb = _s5_glu(ys, u, s5_d, s5_w_glu, s5_b_glu)
    return ya, yb


def _odd_mixers(x, g1n, sc1, sh1, ow, tabs, diff_w, lambda_init):
    s = x.shape[1]
    p = _normmod_mm(x, g1n, sc1, sh1, ow["w_in"], tn=1024)
    mq, mk, mv, q0, q1, dk, dv = _odd_prep(p, tabs["cos"][:s], tabs["sin"][:s], ow)
    yc = _mla_flash(mq, mk, mv)
    yd = _diff_flash(q0, q1, dk, dv, tabs["bias"], diff_w["lq1"], diff_w["lk1"], diff_w["lq2"], diff_w["lk2"],
                     diff_w["subln_g"], lambda_init)
    return yc, yd


def kernel(x_prompt, x_sample, c_prompt, c_sample, ada_w, ada_b, norm1_g, norm2_g, even_w_in, even_w_out, rwkv_mu, rwkv_w0, rwkv_w_up, rwkv_a0, rwkv_a_up, rwkv_g_up, rwkv_k_k, rwkv_k_a, rwkv_r_k, rwkv_lnx_g, rwkv_lnx_b, s5_lam_re, s5_lam_im, s5_log_step, s5_b_re, s5_b_im, s5_c_re, s5_c_im, s5_d, s5_w_glu, s5_b_glu, odd_w_in, odd_w_out, mla_q_norm_g, mla_kv_norm_g, mla_w_uq, mla_w_ukv, diff_lq1, diff_lk1, diff_lq2, diff_lk2, diff_subln_g, rel_bias, ffn_w_up, ffn_conv_w, ffn_conv_b, ffn_w_down, final_g):
    d = D_MODEL
    groups = [(x_prompt, c_prompt), (x_sample, c_sample)]
    nb = [g[0].shape[0] for g in groups]
    c_all = jnp.concatenate([g[1] for g in groups] + [jnp.zeros((SUBLANES - sum(nb), d), F32)], axis=0)
    mod = _ada_mod(c_all, ada_w, ada_b)

    max_s = max(g[0].shape[1] for g in groups)
    cos_t, sin_t = _rope_tables(max_s)
    rel = jnp.arange(-BIAS_HALF, BIAS_HALF, dtype=jnp.int32)
    bias_tab = (jnp.transpose(rel_bias.astype(F32)[_t5_bucket(rel)]) * LOG2E).reshape(D_HEADS, 1, 2 * BIAS_HALF)
    tabs = {"cos": cos_t, "sin": sin_t, "bias": bias_tab}

    xs = [g[0] for g in groups]
    for i in range(DEPTH):
        j = i // 2
        if i % 2 == 0:
            ew = _pack_even(j, even_w_in, even_w_out, rwkv_mu, rwkv_w0, rwkv_w_up, rwkv_a0, rwkv_a_up, rwkv_g_up,
                            rwkv_k_k, rwkv_k_a, rwkv_r_k, rwkv_lnx_g, rwkv_lnx_b)
            s5m = [_s5_matrices(s5_lam_re[j, dr], s5_lam_im[j, dr], s5_log_step[j, dr], s5_b_re[j, dr],
                                s5_b_im[j, dr], s5_c_re[j, dr], s5_c_im[j, dr], dr == 1) for dr in range(2)]
        else:
            ow = _pack_odd(j, odd_w_in, odd_w_out, mla_q_norm_g, mla_kv_norm_g, mla_w_uq, mla_w_ukv)
            diff_w = {"lq1": diff_lq1[j].reshape(1, D_HEAD), "lk1": diff_lk1[j].reshape(1, D_HEAD),
                      "lq2": diff_lq2[j].reshape(1, D_HEAD), "lk2": diff_lk2[j].reshape(1, D_HEAD),
                      "subln_g": diff_subln_g[j].reshape(1, D_V)}
        w_up = ffn_w_up[i].astype(BF16)
        w_down = ffn_w_down[i].astype(BF16)
        row0 = 0
        for gi in range(len(groups)):
            x = xs[gi]
            m = mod[i, row0:row0 + nb[gi]]
            row0 += nb[gi]
            sh1, sc1, g1, sh2, sc2, g2 = [m[:, None, k * d:(k + 1) * d] for k in range(N_MOD)]
            if i % 2 == 0:
                ya, yb = _even_mixers(x, norm1_g[i], sc1, sh1, ew, s5m, s5_d[j].reshape(1, B_WIDTH),
                                      s5_w_glu[j].astype(BF16), s5_b_glu[j].reshape(1, B_WIDTH))
                x = _out_proj(ya, yb, ew["w_out_a"], ew["w_out_b"], x, g1)
            else:
                yc, yd = _odd_mixers(x, norm1_g[i], sc1, sh1, ow, tabs, diff_w, 0.8 - 0.6 * math.exp(-0.3 * i))
                x = _out_proj(yc, yd, ow["w_out_a"], ow["w_out_b"], x, g1)
            u = _normmod_mm(x, norm2_g[i], sc2, sh2, w_up, tn=1024)
            x = _ffn_down(u, ffn_conv_w[i], ffn_conv_b[i].reshape(1, 2 * FFN_HIDDEN), w_down, x, g2,
                          final_g.reshape(1, d), final=(i == DEPTH - 1))
            xs[gi] = x
    return (xs[0], xs[1])
```

```python
import functools
import math

import jax
import jax.numpy as jnp
from jax import lax
from jax.experimental import pallas as pl
from jax.experimental.pallas import tpu as pltpu

F32 = jnp.float32
BF16 = jnp.bfloat16

D_MODEL = 2048
DEPTH = 2
EPS = 1e-6
A_WIDTH = 1024
A_HEAD = 64
A_HEADS = 16
A_LORA = 64
A_GATE_LORA = 128
A_COLS = 3 * A_WIDTH + 2 * A_LORA + 2 * A_LORA + A_GATE_LORA
RWKV_GN_EPS = 64e-5
DECAY_SCALE = math.exp(-0.5)
B_WIDTH = 1024
B_GROUP = 16
B_GROUPS = 64
B_STATE = 64
C_HEADS = 8
C_NOPE = 64
C_ROPE = 32
C_V = 128
C_Q_RANK = 512
C_KV_RANK = 256
ROPE_THETA = 10000.0
D_HEADS = 8
D_HEAD = 64
D_V = 128
SUBLN_EPS = 1e-5
N_BUCKETS = 32
MAX_DISTANCE = 128
FFN_HIDDEN = 5632
N_MOD = 6

LANES = 128
SUBLANES = 8
VMEM_LIMIT_BYTES = 56 * 1024 * 1024

RWKV_CHUNK = 64
S5_CHUNK = 16
LOG2E = 1.4426950408889634

NN = (((1,), (0,)), ((), ()))
NT = (((1,), (1,)), ((), ()))
BATCH_NN = (((2,), (1,)), ((0,), (0,)))


def _params(*sem):
    return pltpu.CompilerParams(dimension_semantics=sem, vmem_limit_bytes=VMEM_LIMIT_BYTES)


def _dot(a, b, dims=NN):
    return lax.dot_general(a, b, dims, preferred_element_type=F32)


def _split2(x):
    hi = x.astype(BF16)
    lo = (x - hi.astype(F32)).astype(BF16)
    return hi, lo


def _dot3(a, b, dims=NN):
    ah, al = _split2(a)
    bh, bl = _split2(b)
    return _dot(ah, bh, dims) + _dot(ah, bl, dims) + _dot(al, bh, dims)


def _dot_exact_rhs(a, b):
    ah, al = _split2(a)
    return _dot(ah, b) + _dot(al, b)


def _ada_kernel(c_ref, w_ref, b_ref, o_ref):
    c = c_ref[...]
    cs = c * jax.nn.sigmoid(c)
    o_ref[0] = _dot(cs.astype(BF16), w_ref[0].astype(BF16)) + b_ref[0]


def _ada_mod(c_all, ada_w, ada_b):
    n = N_MOD * D_MODEL
    tn = 1024
    return pl.pallas_call(
        _ada_kernel,
        grid=(DEPTH, n // tn),
        in_specs=[
            pl.BlockSpec((SUBLANES, D_MODEL), lambda l, j: (0, 0)),
            pl.BlockSpec((1, D_MODEL, tn), lambda l, j: (l, 0, j)),
            pl.BlockSpec((1, 1, tn), lambda l, j: (l, 0, j)),
        ],
        out_specs=pl.BlockSpec((1, SUBLANES, tn), lambda l, j: (l, 0, j)),
        out_shape=jax.ShapeDtypeStruct((DEPTH, SUBLANES, n), F32),
        compiler_params=_params("parallel", "parallel"),
        name="ada_mod",
    )(c_all, ada_w, ada_b.reshape(DEPTH, 1, n))


def _modnorm(x, g, sc, sh):
    y = x * lax.rsqrt(jnp.mean(x * x, axis=-1, keepdims=True) + EPS)
    return (y * g) * (1.0 + sc) + sh


def _normmod_mm_kernel(x_ref, g_ref, sc_ref, sh_ref, w_ref, o_ref, h_scr):
    @pl.when(pl.program_id(2) == 0)
    def _():
        h_scr[...] = _modnorm(x_ref[0], g_ref[...], sc_ref[0], sh_ref[0]).astype(BF16)

    o_ref[0] = _dot(h_scr[...], w_ref[...]).astype(o_ref.dtype)


def _normmod_mm(x, g, sc, sh, w, tn, out_dtype=F32):
    b, s, d = x.shape
    n = w.shape[1]
    tm = min(512, s)
    return pl.pallas_call(
        _normmod_mm_kernel,
        grid=(b, s // tm, n // tn),
        in_specs=[
            pl.BlockSpec((1, tm, d), lambda bi, i, j: (bi, i, 0)),
            pl.BlockSpec((1, d), lambda bi, i, j: (0, 0)),
            pl.BlockSpec((1, 1, d), lambda bi, i, j: (bi, 0, 0)),
            pl.BlockSpec((1, 1, d), lambda bi, i, j: (bi, 0, 0)),
            pl.BlockSpec((d, tn), lambda bi, i, j: (0, j)),
        ],
        out_specs=pl.BlockSpec((1, tm, tn), lambda bi, i, j: (bi, i, j)),
        out_shape=jax.ShapeDtypeStruct((b, s, n), out_dtype),
        scratch_shapes=[pltpu.VMEM((tm, d), BF16)],
        compiler_params=_params("parallel", "parallel", "arbitrary"),
        name="normmod_mm",
    )(x, g.reshape(1, d), sc, sh, w)


def _rwkv_prep_kernel(p_ref, pp_ref, pn_ref, mu_ref, w0_ref, wup_ref, a0_ref, aup_ref, gup_ref, kk_ref, ka_ref,
                      rk_ref, ones_ref, r_out, v_out, kkn_out, lw_out, kd_out, bd_out, bv_out, g_out):
    i = pl.program_id(1)
    last = pl.num_programs(1) - 1
    pa = p_ref[0]
    tm = pa.shape[0]
    row = lax.broadcasted_iota(jnp.int32, (tm, 1), 0)
    prev_row = jnp.where(i == 0, 0.0, pp_ref[0][SUBLANES - 1:SUBLANES, :])
    next_row = jnp.where(i == last, 0.0, pn_ref[0][0:1, :])
    p_prev = jnp.where(row == 0, prev_row, pltpu.roll(pa, 1, 0))
    p_next = jnp.where(row == tm - 1, next_row, pltpu.roll(pa, tm - 1, 0))
    pa = pa + mu_ref[...] * (0.5 * (p_prev + p_next) - pa)

    w = A_WIDTH
    r = pa[:, 0:w]
    k = pa[:, w:2 * w]
    v = pa[:, 2 * w:3 * w]
    dw = pa[:, 3 * w:3 * w + 2 * A_LORA]
    da = pa[:, 3 * w + 2 * A_LORA:3 * w + 4 * A_LORA]
    dg = pa[:, 3 * w + 4 * A_LORA:A_COLS]

    lw = -DECAY_SCALE * jax.nn.sigmoid(w0_ref[...] + _dot(jnp.tanh(dw).astype(BF16), wup_ref[...]))
    icl = jax.nn.sigmoid(a0_ref[...] + _dot(da.astype(BF16), aup_ref[...]))
    g = _dot(jax.nn.sigmoid(dg).astype(BF16), gup_ref[...])

    ones_bd = ones_ref[...]
    kkr = k * kk_ref[...]
    ss = _dot_exact_rhs(kkr * kkr, ones_bd)
    kkn = kkr / jnp.maximum(jnp.sqrt(ss), 1e-12)

    r_out[0] = r
    v_out[0] = v
    kkn_out[0] = kkn
    lw_out[0] = lw
    g_out[0] = g
    bonus = jnp.zeros_like(r)
    for d in range(2):
        icl_d = icl[:, d * w:(d + 1) * w]
        k_d = k * (1.0 + (icl_d - 1.0) * ka_ref[...])
        kd_out[0, :, d * w:(d + 1) * w] = k_d
        bd_out[0, :, d * w:(d + 1) * w] = icl_d * kkn
        bonus = bonus + _dot_exact_rhs(r * k_d * rk_ref[...], ones_bd)
    bv_out[0] = bonus * v


def _rwkv_prep(pa, wts):
    b, s, _ = pa.shape
    tm = min(256, s)
    nh = tm // SUBLANES
    w = A_WIDTH
    full = lambda shape: pl.BlockSpec(shape, lambda bi, i: (0,) * len(shape))
    tok = lambda n: pl.BlockSpec((1, tm, n), lambda bi, i: (bi, i, 0))
    out_shapes = [jax.ShapeDtypeStruct((b, s, n), F32) for n in (w, w, w, 2 * w, 2 * w, 2 * w, w, w)]
    return pl.pallas_call(
        _rwkv_prep_kernel,
        grid=(b, s // tm),
        in_specs=[
            tok(A_COLS),
            pl.BlockSpec((1, SUBLANES, A_COLS), lambda bi, i: (bi, jnp.maximum(i * nh - 1, 0), 0)),
            pl.BlockSpec((1, SUBLANES, A_COLS), lambda bi, i: (bi, jnp.minimum((i + 1) * nh, s // SUBLANES - 1), 0)),
            full((1, A_COLS)), full((1, 2 * w)), full((2 * A_LORA, 2 * w)), full((1, 2 * w)),
            full((2 * A_LORA, 2 * w)), full((A_GATE_LORA, w)), full((1, w)), full((1, w)), full((1, w)),
            full((w, w)),
        ],
        out_specs=[tok(sd.shape[-1]) for sd in out_shapes],
        out_shape=out_shapes,
        compiler_params=_params("parallel", "parallel"),
        name="rwkv_prep",
    )(pa, pa, pa, wts["mu"], wts["w0"], wts["wup"], wts["a0"], wts["aup"], wts["gup"], wts["k_k"], wts["k_a"],
      wts["r_k"], wts["ones_bd"])


def _rwkv_chunk(r, v, kk, lw, kd, bd, s2, reverse):
    t = RWKV_CHUNK
    sign = -1 if reverse else 1
    ri = lax.broadcasted_iota(jnp.int32, (t, t), 0)
    ci = lax.broadcasted_iota(jnp.int32, (t, t), 1)
    tri = jnp.where((ri - ci) * sign >= 0, 1.0, 0.0).astype(BF16)
    lw_hi = lw.astype(BF16)
    rem = lw - lw_hi.astype(F32)
    lw_mid = rem.astype(BF16)
    lw_lo = (rem - lw_mid.astype(F32)).astype(BF16)
    cum = _dot(tri, lw_hi) + _dot(tri, lw_mid) + _dot(tri, lw_lo)
    yield
    tot = jnp.sum(lw, axis=0, keepdims=True)
    gam = jnp.exp(cum)
    gam_ex = jnp.exp(cum - lw)
    ginv = jnp.exp(-cum)
    gend = jnp.exp(tot - cum)
    gtot = jnp.exp(tot)

    width = r.shape[1]
    heads = width // A_HEAD
    lane_head = lax.broadcasted_iota(jnp.int32, (1, width), 1) // A_HEAD

    def stack(x):
        return jnp.concatenate([jnp.where(lane_head == h, x, 0.0) for h in range(heads)], axis=0).astype(BF16)

    q_rk = jnp.concatenate([stack(r * gam), stack(kk * gam_ex)], axis=0)
    k_kb = jnp.concatenate([stack(kd * ginv), stack(bd * ginv)], axis=0)
    k_kb_end = jnp.concatenate([stack(kd * gend), stack(bd * gend)], axis=0)
    vs = stack(v)

    n = heads * t
    a = _dot(q_rk, k_kb, NT)
    yield
    ri2 = lax.broadcasted_iota(jnp.int32, (n, n), 0)
    ci2 = lax.broadcasted_iota(jnp.int32, (n, n), 1)
    order = (ri2 - ci2) * sign
    incl = order >= 0
    strict = order > 0
    a_rk = jnp.where(incl, a[:n, :n], 0.0)
    a_rb = jnp.where(incl, a[:n, n:], 0.0)
    a_kk = jnp.where(strict, a[n:, :n], 0.0)
    a_bk = jnp.where(strict, a[n:, n:], 0.0)

    qs = _dot(q_rk, s2.astype(BF16), NT)
    yield
    akv = _dot(jnp.concatenate([a_rk, a_kk], axis=0).astype(BF16), vs)
    yield

    nb = a_bk.astype(BF16)
    n2 = _dot(nb, nb)
    yield
    n2b = n2.astype(BF16)
    n4 = _dot(n2b, n2b)
    yield
    n3 = _dot(nb, n2b)
    yield
    n4b = n4.astype(BF16)
    n8 = _dot(n4b, n4b)
    yield
    eye = jnp.where(ri2 == ci2, 1.0, 0.0)
    f1 = eye - a_bk + n2 - n3
    n8b = n8.astype(BF16)
    n16 = _dot(n8b, n8b)
    yield
    n12 = _dot(n4b, n8b)
    yield
    f2 = n4 + n8 + n12
    f12 = f1 + _dot(f1.astype(BF16), f2.astype(BF16))
    yield
    n16b = n16.astype(BF16)
    n32 = _dot(n16b, n16b)
    yield
    n48 = _dot(n16b, n32.astype(BF16))
    yield
    f3 = n16 + n32 + n48
    inv = f12 + _dot(f12.astype(BF16), f3.astype(BF16))
    yield

    ps = _dot(inv.astype(BF16), (-qs[n:] - akv[n:]).astype(BF16)).astype(BF16)
    yield
    ys = qs[:n] + akv[:n] + _dot(a_rb.astype(BF16), ps)
    yield
    y = ys[0:t]
    for h in range(1, heads):
        y = y + ys[h * t:(h + 1) * t]
    s2_new = s2 * gtot + _dot(jnp.concatenate([vs, ps], axis=0).astype(F32).T.astype(BF16), k_kb_end)
    return y, s2_new


def _run_lockstep(gens):
    results = [None] * len(gens)
    active = list(range(len(gens)))
    while active:
        for i in list(active):
            try:
                next(gens[i])
            except StopIteration as e:
                results[i] = e.value
                active.remove(i)
    return results


RWKV_CHAIN_LANES = 256
RWKV_CHAINS_PER_STEP = 4


def _rwkv_scan_kernel(rf_ref, vf_ref, kkf_ref, lwf_ref, kdf_ref, bdf_ref,
                      rb_ref, vb_ref, kkb_ref, lwb_ref, kdb_ref, bdb_ref, yf_ref, yb_ref, s_scr):
    @pl.when(pl.program_id(2) == 0)
    def _():
        s_scr[...] = jnp.zeros_like(s_scr)

    dirs = ((rf_ref, vf_ref, kkf_ref, lwf_ref, kdf_ref, bdf_ref, yf_ref),
            (rb_ref, vb_ref, kkb_ref, lwb_ref, kdb_ref, bdb_ref, yb_ref))
    chains = []
    for d, (r_ref, v_ref, kk_ref, lw_ref, kd_ref, bd_ref, y_ref) in enumerate(dirs):
        for p in range(RWKV_CHAINS_PER_STEP):
            sl = slice(p * RWKV_CHAIN_LANES, (p + 1) * RWKV_CHAIN_LANES)
            chains.append((d, p, sl, y_ref, _rwkv_chunk(
                r_ref[0, :, sl], v_ref[0, :, sl], kk_ref[0, :, sl], lw_ref[0, :, sl], kd_ref[0, :, sl],
                bd_ref[0, :, sl], s_scr[d, p], reverse=(d == 1))))
    results = _run_lockstep([c[4] for c in chains])
    for (d, p, sl, y_ref, _), (y, s_new) in zip(chains, results):
        y_ref[0, :, sl] = y
        s_scr[d, p] = s_new


def _rwkv_scan(r, v, kk, lw, kd, bd):
    b, s, w = r.shape
    t = RWKV_CHUNK
    nc = s // t
    pw = RWKV_CHAINS_PER_STEP * RWKV_CHAIN_LANES
    ngrp = w // pw
    fwd = lambda off: pl.BlockSpec((1, t, pw), lambda bi, g, c: (bi, c, g + off))
    bwd = lambda off: pl.BlockSpec((1, t, pw), lambda bi, g, c: (bi, nc - 1 - c, g + off))
    return pl.pallas_call(
        _rwkv_scan_kernel,
        grid=(b, ngrp, nc),
        in_specs=[fwd(0)] * 6 + [bwd(0)] * 3 + [bwd(ngrp)] * 3,
        out_specs=[fwd(0), bwd(0)],
        out_shape=[jax.ShapeDtypeStruct((b, s, w), F32)] * 2,
        scratch_shapes=[pltpu.VMEM((2, RWKV_CHAINS_PER_STEP, RWKV_CHAIN_LANES, RWKV_CHAIN_LANES), F32)],
        compiler_params=_params("parallel", "parallel", "arbitrary"),
        name="rwkv_scan",
    )(r, v, kk, lw, kd, bd, r, v, kk, lw, kd, bd)


def _rwkv_post_kernel(yf_ref, yb_ref, bv_ref, g_ref, lng_ref, lnb_ref, ones_ref, o_ref):
    y = yf_ref[0] + yb_ref[0]
    ones_bd = ones_ref[...]
    mean = _dot_exact_rhs(y, ones_bd) * (1.0 / A_HEAD)
    yc = y - mean
    var = _dot_exact_rhs(yc * yc, ones_bd) * (1.0 / A_HEAD)
    yn = yc * lax.rsqrt(var + RWKV_GN_EPS) * lng_ref[...] + lnb_ref[...]
    o_ref[0] = ((yn + bv_ref[0]) * g_ref[0]).astype(o_ref.dtype)


def _rwkv_post(yf, yb, bv, g, lng, lnb, ones_bd):
    b, s, w = bv.shape
    tm = min(512, s)
    full = lambda shape: pl.BlockSpec(shape, lambda bi, i: (0,) * len(shape))
    tok = lambda n: pl.BlockSpec((1, tm, n), lambda bi, i: (bi, i, 0))
    return pl.pallas_call(
        _rwkv_post_kernel,
        grid=(b, s // tm),
        in_specs=[tok(w), tok(w), tok(w), tok(w), full((1, w)), full((1, w)), full((w, w))],
        out_specs=tok(w),
        out_shape=jax.ShapeDtypeStruct((b, s, w), BF16),
        compiler_params=_params("parallel", "parallel"),
        name="rwkv_post",
    )(yf, yb, bv, g, lng, lnb, ones_bd)


def _s5_kernel(u_ref, m_ref, w_ref, ws_ref, v_ref, a_ref, y_ref, x_scr, xs_scr, h_scr, carry_scr, *, rows, reverse):
    g = B_GROUPS

    @pl.when(pl.program_id(1) == 0)
    def _():
        carry_scr[...] = jnp.zeros_like(carry_scr)

    u = u_ref[0]
    x_scr[...] = lax.dot_general(u, w_ref[...], BATCH_NN, preferred_element_type=F32).reshape(g * rows, LANES)
    xs_scr[...] = lax.dot_general(u, ws_ref[...], BATCH_NN, preferred_element_type=F32).reshape(g * rows, LANES)
    a1 = a_ref[0]
    a2 = a_ref[1]
    a2s = a_ref[2]

    def step(i, carry):
        h, hs = carry
        r = (rows - 1 - i) if reverse else i
        idx = pl.ds(r, g, stride=rows)
        h_scr[idx, :] = h
        hn = a1 * h + a2 * hs + x_scr[idx, :]
        hsn = a1 * hs + a2s * h + xs_scr[idx, :]
        return hn, hsn

    h, hs = lax.fori_loop(0, rows, step, (carry_scr[0], carry_scr[1]))
    carry_scr[0] = h
    carry_scr[1] = hs
    hprev = h_scr[...].reshape(g, rows, LANES).astype(BF16)
    y_ref[0] = (lax.dot_general(u, m_ref[...], BATCH_NN, preferred_element_type=F32)
                + lax.dot_general(hprev, v_ref[...], BATCH_NN, preferred_element_type=F32))


def _s5_scan(u_g, mats, reverse):
    b, g, nrows, cw = u_g.shape
    rows = min(64, nrows)
    nsb = nrows // rows
    whole = pl.BlockSpec(memory_space=pltpu.VMEM)
    idx = (lambda bi, i: (bi, 0, nsb - 1 - i, 0)) if reverse else (lambda bi, i: (bi, 0, i, 0))
    return pl.pallas_call(
        functools.partial(_s5_kernel, rows=rows, reverse=reverse),
        grid=(b, nsb),
        in_specs=[pl.BlockSpec((1, g, rows, cw), idx), whole, whole, whole, whole, whole],
        out_specs=pl.BlockSpec((1, g, rows, cw), idx),
        out_shape=jax.ShapeDtypeStruct((b, g, nrows, cw), F32),
        scratch_shapes=[pltpu.VMEM((g * rows, LANES), F32), pltpu.VMEM((g * rows, LANES), F32),
                        pltpu.VMEM((g * rows, LANES), F32), pltpu.VMEM((2, g, LANES), F32)],
        compiler_params=_params("parallel", "arbitrary"),
        name="s5_scan_bwd" if reverse else "s5_scan_fwd",
    )(u_g, mats["m"], mats["w"], mats["ws"], mats["v"], mats["a"])


def _s5_matrices(lam_re, lam_im, log_step, b_re, b_im, c_re, c_im, reverse):
    hp = lax.Precision.HIGHEST
    t = S5_CHUNK
    g, p, c = B_GROUPS, B_STATE, B_GROUP
    lr, li = lam_re.astype(F32), lam_im.astype(F32)
    step = jnp.exp(log_step.astype(F32))[:, None]
    ar, ai = jnp.exp(lr * step) * jnp.cos(li * step), jnp.exp(lr * step) * jnp.sin(li * step)
    den = lr * lr + li * li
    nr, ni = ar - 1.0, ai
    fr, fi = (nr * lr + ni * li) / den, (ni * lr - nr * li) / den
    br, bi = b_re.astype(F32), b_im.astype(F32)
    bbr = fr[..., None] * br - fi[..., None] * bi
    bbi = fr[..., None] * bi + fi[..., None] * br
    cr, cim = c_re.astype(F32), c_im.astype(F32)
    taus = jnp.arange(t + 1, dtype=F32)[:, None, None]
    mag = jnp.exp(lr * step * taus)
    pr, pi = mag * jnp.cos(li * step * taus), mag * jnp.sin(li * step * taus)
    cpr = cr[None] * pr[:, :, None, :] - cim[None] * pi[:, :, None, :]
    cpi = cr[None] * pi[:, :, None, :] + cim[None] * pr[:, :, None, :]
    kern = (jnp.einsum("tgcp,gpd->tgcd", cpr, bbr, precision=hp)
            - jnp.einsum("tgcp,gpd->tgcd", cpi, bbi, precision=hp))
    s_idx = jnp.arange(t)[:, None]
    t_idx = jnp.arange(t)[None, :]
    lag = (s_idx - t_idx) if reverse else (t_idx - s_idx)
    kk = jnp.where((lag >= 0)[:, :, None, None, None], kern[jnp.clip(lag, 0, t)], 0.0)
    m = jnp.transpose(kk, (2, 0, 4, 1, 3)).reshape(g, t * c, t * c)
    e = jnp.arange(t) if reverse else (t - 1 - jnp.arange(t))
    pre, pie = pr[e], pi[e]
    wre = pre[..., None] * bbr[None] - pie[..., None] * bbi[None]
    wim = pre[..., None] * bbi[None] + pie[..., None] * bbr[None]
    wre = jnp.transpose(wre, (1, 0, 3, 2)).reshape(g, t * c, p)
    wim = jnp.transpose(wim, (1, 0, 3, 2)).reshape(g, t * c, p)
    w = jnp.concatenate([wre, wim], axis=-1)
    ws = jnp.concatenate([wim, wre], axis=-1)
    f = (t - jnp.arange(t)) if reverse else (jnp.arange(t) + 1)
    vre = jnp.transpose(cpr[f], (1, 3, 0, 2)).reshape(g, p, t * c)
    vim = jnp.transpose(-cpi[f], (1, 3, 0, 2)).reshape(g, p, t * c)
    v = jnp.concatenate([vre, vim], axis=1)
    atr, ati = pr[t], pi[t]
    a = jnp.stack([jnp.concatenate([atr, atr], -1), jnp.concatenate([-ati, ati], -1),
                   jnp.concatenate([ati, -ati], -1)])
    return {"m": m.astype(BF16), "w": w.astype(BF16), "ws": ws.astype(BF16), "v": v.astype(BF16), "a": a}


def _s5_glu_kernel(ys_ref, u_ref, d_ref, w_ref, b_ref, o_ref):
    y = ys_ref[0] + u_ref[0] * d_ref[...]
    z = jax.nn.gelu(y)
    gate = jax.nn.sigmoid(_dot(z.astype(BF16), w_ref[...]) + b_ref[...])
    o_ref[0] = (z * gate).astype(o_ref.dtype)


def _s5_glu(ys, u, d_skip, w_glu, b_glu):
    b, s, w = u.shape
    tm = min(512, s)
    full = lambda shape: pl.BlockSpec(shape, lambda bi, i: (0,) * len(shape))
    tok = pl.BlockSpec((1, tm, w), lambda bi, i: (bi, i, 0))
    return pl.pallas_call(
        _s5_glu_kernel,
        grid=(b, s // tm),
        in_specs=[tok, tok, full((1, w)), full((w, w)), full((1, w))],
        out_specs=tok,
        out_shape=jax.ShapeDtypeStruct((b, s, w), BF16),
        compiler_params=_params("parallel", "parallel"),
        name="s5_glu",
    )(ys, u, d_skip, w_glu, b_glu)


def _out_proj_kernel(a_ref, b_ref, wa_ref, wb_ref, x_ref, g_ref, o_ref):
    mix = _dot(a_ref[0], wa_ref[...]) + _dot(b_ref[0], wb_ref[...])
    o_ref[0] = x_ref[0] + g_ref[0] * mix


def _out_proj(a, bb, wa, wb, x, gate):
    b, s, d = x.shape
    k = a.shape[-1]
    tm = min(512, s)
    tn = 1024
    return pl.pallas_call(
        _out_proj_kernel,
        grid=(b, s // tm, d // tn),
        in_specs=[
            pl.BlockSpec((1, tm, k), lambda bi, i, j: (bi, i, 0)),
            pl.BlockSpec((1, tm, k), lambda bi, i, j: (bi, i, 0)),
            pl.BlockSpec((k, tn), lambda bi, i, j: (0, j)),
            pl.BlockSpec((k, tn), lambda bi, i, j: (0, j)),
            pl.BlockSpec((1, tm, tn), lambda bi, i, j: (bi, i, j)),
            pl.BlockSpec((1, 1, tn), lambda bi, i, j: (bi, 0, j)),
        ],
        out_specs=pl.BlockSpec((1, tm, tn), lambda bi, i, j: (bi, i, j)),
        out_shape=jax.ShapeDtypeStruct((b, s, d), F32),
        compiler_params=_params("parallel", "parallel", "parallel"),
        name="out_proj",
    )(a, bb, wa, wb, x, gate)


def _ffn_down_kernel(uv_ref, ug_ref, uvp_ref, uvn_ref, ugp_ref, ugn_ref, cwv_ref, cwg_ref, cbv_ref, cbg_ref,
                     wd_ref, x_ref, g_ref, fg_ref, o_ref, acc_scr, *, final):
    i = pl.program_id(1)
    j = pl.program_id(2)
    first = i == 0
    last = i == pl.num_programs(1) - 1

    @pl.when(j == 0)
    def _():
        acc_scr[...] = jnp.zeros_like(acc_scr)

    def conv(u_ref, up_ref, un_ref, cw_ref, cb_ref):
        u = u_ref[0]
        tm = u.shape[0]
        row = lax.broadcasted_iota(jnp.int32, (tm, 1), 0)
        prev_row = jnp.where(first, 0.0, up_ref[0][SUBLANES - 1:SUBLANES, :])
        next_row = jnp.where(last, 0.0, un_ref[0][0:1, :])
        u_prev = jnp.where(row == 0, prev_row, pltpu.roll(u, 1, 0))
        u_next = jnp.where(row == tm - 1, next_row, pltpu.roll(u, tm - 1, 0))
        cw = cw_ref[...]
        return u_prev * cw[0:1] + u * cw[1:2] + u_next * cw[2:3] + cb_ref[...]

    val = conv(uv_ref, uvp_ref, uvn_ref, cwv_ref, cbv_ref)
    gate = conv(ug_ref, ugp_ref, ugn_ref, cwg_ref, cbg_ref)
    act = (gate * jax.nn.sigmoid(gate)) * val
    acc_scr[...] += _dot(act.astype(BF16), wd_ref[...])

    @pl.when(j == pl.num_programs(2) - 1)
    def _():
        xn = x_ref[0] + g_ref[0] * acc_scr[...]
        if final:
            xn = xn * lax.rsqrt(jnp.mean(xn * xn, axis=-1, keepdims=True) + EPS) * fg_ref[...]
        o_ref[0] = xn


def _ffn_down(u, conv_w, conv_b, w_down, x, gate, final_g, final):
    b, s, d = x.shape
    f = FFN_HIDDEN
    tm = min(512, s)
    tf = 512
    nf = f // tf
    nh = tm // SUBLANES
    nrow8 = s // SUBLANES
    main = lambda off: pl.BlockSpec((1, tm, tf), lambda bi, i, j: (bi, i, j + off))
    prev = lambda off: pl.BlockSpec((1, SUBLANES, tf), lambda bi, i, j: (bi, jnp.maximum(i * nh - 1, 0), j + off))
    nxt = lambda off: pl.BlockSpec((1, SUBLANES, tf),
                                   lambda bi, i, j: (bi, jnp.minimum((i + 1) * nh, nrow8 - 1), j + off))
    cw = lambda off: pl.BlockSpec((3, tf), lambda bi, i, j: (0, j + off))
    cb = lambda off: pl.BlockSpec((1, tf), lambda bi, i, j: (0, j + off))
    return pl.pallas_call(
        functools.partial(_ffn_down_kernel, final=final),
        grid=(b, s // tm, nf),
        in_specs=[
            main(0), main(nf), prev(0), nxt(0), prev(nf), nxt(nf), cw(0), cw(nf), cb(0), cb(nf),
            pl.BlockSpec((tf, d), lambda bi, i, j: (j, 0)),
            pl.BlockSpec((1, tm, d), lambda bi, i, j: (bi, i, 0)),
            pl.BlockSpec((1, 1, d), lambda bi, i, j: (bi, 0, 0)),
            pl.BlockSpec((1, d), lambda bi, i, j: (0, 0)),
        ],
        out_specs=pl.BlockSpec((1, tm, d), lambda bi, i, j: (bi, i, 0)),
        out_shape=jax.ShapeDtypeStruct((b, s, d), F32),
        scratch_shapes=[pltpu.VMEM((tm, d), F32)],
        compiler_params=_params("parallel", "parallel", "arbitrary"),
        name="ffn_down",
    )(u, u, u, u, u, u, conv_w, conv_w, conv_b, conv_b, w_down, x, gate, final_g)


def _rms(x, g, eps):
    return x * lax.rsqrt(jnp.mean(x * x, axis=-1, keepdims=True) + eps) * g


def _odd_prep_kernel(cq_ref, ckv_ref, kr_ref, dq_ref, dk_ref, dv_ref, cos_ref, sin_ref, qg_ref, kvg_ref,
                     wqa_ref, wqb_ref, wk_ref, wv_ref, place_ref,
                     mq_out, mk_out, mv_out, q0_out, q1_out, dk_out, dv_out):
    cosq = cos_ref[...]
    sinq = sin_ref[...]
    qn = _rms(cq_ref[0], qg_ref[...], EPS).astype(BF16)
    qa = _dot(qn, wqa_ref[...])
    qb = _dot(qn, wqb_ref[...])
    mla_scale = (C_NOPE + C_ROPE) ** -0.5 * LOG2E
    for h in range(C_HEADS):
        sl = slice(h * LANES, (h + 1) * LANES)
        mq_out[0, :, sl] = ((qa[:, sl] * cosq + qb[:, sl] * sinq) * mla_scale).astype(BF16)
    kvn = _rms(ckv_ref[0], kvg_ref[...], EPS).astype(BF16)
    kr = kr_ref[0]
    cos_k = pltpu.roll(cosq, LANES - C_NOPE, 1)
    sin_k = pltpu.roll(sinq, LANES - C_NOPE, 1)
    partner = pltpu.roll(kr, LANES - C_ROPE, 1)
    lane = lax.broadcasted_iota(jnp.int32, (1, LANES), 1)
    kr_rope = jnp.where(lane < C_ROPE, kr * cos_k + partner * sin_k, 0.0)
    mk_out[0] = (_dot(kvn, wk_ref[...]) + _dot(kr_rope.astype(BF16), place_ref[...])).astype(BF16)
    mv_out[0] = _dot(kvn, wv_ref[...]).astype(BF16)
    dq = dq_ref[0] * (D_HEAD ** -0.5 * LOG2E)
    lane_w = lax.broadcasted_iota(jnp.int32, (1, dq.shape[1]), 1)
    first_map = (lane_w % LANES) < D_HEAD
    q0_out[0] = jnp.where(first_map, dq, 0.0).astype(BF16)
    q1_out[0] = jnp.where(first_map, 0.0, dq).astype(BF16)
    dk_out[0] = dk_ref[0].astype(BF16)
    dv_out[0] = dv_ref[0].astype(BF16)


ODD_DQ = 0
ODD_DK = ODD_DQ + D_HEADS * 2 * D_HEAD
ODD_DV = ODD_DK + D_HEADS * 2 * D_HEAD
ODD_CQ = ODD_DV + D_HEADS * D_V
ODD_CKV = ODD_CQ + C_Q_RANK
ODD_KR = ODD_CKV + C_KV_RANK
ODD_COLS = 4096


def _odd_prep(p, cos_t, sin_t, wts):
    b, s, _ = p.shape
    tm = min(256, s)
    hw = C_HEADS * LANES
    full = lambda shape: pl.BlockSpec(shape, lambda bi, i: (0,) * len(shape))
    col = lambda off, n: pl.BlockSpec((1, tm, n), lambda bi, i: (bi, i, off // n))
    tok = pl.BlockSpec((1, tm, hw), lambda bi, i: (bi, i, 0))
    tab = pl.BlockSpec((tm, LANES), lambda bi, i: (i, 0))
    outs = [jax.ShapeDtypeStruct((b, s, hw), BF16)] * 7
    return pl.pallas_call(
        _odd_prep_kernel,
        grid=(b, s // tm),
        in_specs=[
            col(ODD_CQ, C_Q_RANK), col(ODD_CKV, C_KV_RANK), col(ODD_KR, LANES),
            col(ODD_DQ, hw), col(ODD_DK, hw), col(ODD_DV, hw), tab, tab,
            full((1, C_Q_RANK)), full((1, C_KV_RANK)),
            full((C_Q_RANK, hw)), full((C_Q_RANK, hw)), full((C_KV_RANK, hw)), full((C_KV_RANK, hw)),
            full((LANES, hw)),
        ],
        out_specs=[tok] * 7,
        out_shape=outs,
        compiler_params=_params("parallel", "parallel"),
        name="odd_prep",
    )(p, p, p, p, p, p, cos_t, sin_t, wts["q_norm_g"], wts["kv_norm_g"], wts["wqa"], wts["wqb"], wts["wk"],
      wts["wv"], wts["place"])


def _with_ones(v):
    return jnp.concatenate([v, jnp.ones_like(v)], axis=1)


def _flash_chain(q, k, v1, bias, shift, m_scr, acc_scr):
    s = _dot(q, k, NT)
    yield
    if bias is not None:
        s = s + bias
    m_prev = m_scr[...]
    m_new = jnp.maximum(m_prev, jnp.max(s, axis=-1, keepdims=True) + shift)
    alpha = jnp.exp2(m_prev - m_new)
    p = jnp.exp2(s - jnp.tile(m_new - shift, (1, s.shape[1] // LANES))).astype(BF16)
    yield
    acc_scr[...] = jnp.tile(alpha, (1, 2)) * acc_scr[...] + _dot(p, v1)
    m_scr[...] = m_new


MLA_HEADS_PER_STEP = 4


def _mla_flash_kernel(q_ref, k_ref, v_ref, o_ref, m_scr, acc_scr):
    kj = pl.program_id(3)

    @pl.when(kj == 0)
    def _():
        m_scr[...] = jnp.full_like(m_scr, -jnp.inf)
        acc_scr[...] = jnp.zeros_like(acc_scr)

    heads = [slice(h * LANES, (h + 1) * LANES) for h in range(MLA_HEADS_PER_STEP)]
    _run_lockstep([_flash_chain(q_ref[0, :, sl], k_ref[0, :, sl], _with_ones(v_ref[0, :, sl]), None, 0.0,
                                m_scr.at[h], acc_scr.at[h]) for h, sl in enumerate(heads)])

    @pl.when(kj == pl.num_programs(3) - 1)
    def _():
        for h, sl in enumerate(heads):
            acc = acc_scr[h]
            o_ref[0, :, sl] = (acc[:, :LANES] / acc[:, LANES:]).astype(o_ref.dtype)


def _mla_flash(q, k, v):
    b, s, hw = q.shape
    gw = MLA_HEADS_PER_STEP * LANES
    tq = min(512, s)
    tk = min(1024, s)
    qspec = pl.BlockSpec((1, tq, gw), lambda bi, h, i, j: (bi, i, h))
    kspec = pl.BlockSpec((1, tk, gw), lambda bi, h, i, j: (bi, j, h))
    return pl.pallas_call(
        _mla_flash_kernel,
        grid=(b, hw // gw, s // tq, s // tk),
        in_specs=[qspec, kspec, kspec],
        out_specs=qspec,
        out_shape=jax.ShapeDtypeStruct((b, s, hw), BF16),
        scratch_shapes=[pltpu.VMEM((MLA_HEADS_PER_STEP, tq, LANES), F32),
                        pltpu.VMEM((MLA_HEADS_PER_STEP, tq, 2 * LANES), F32)],
        compiler_params=_params("parallel", "parallel", "parallel", "arbitrary"),
        name="mla_flash",
    )(q, k, v)


BIAS_HALF = 256


DIFF_HEADS_PER_STEP = 2


def _diff_flash_kernel(q0_ref, q1_ref, k_ref, v_ref, tab_ref, lq1_ref, lk1_ref, lq2_ref, lk2_ref, sg_ref, o_ref,
                       m_scr, acc_scr, bias_scr, *, tq, tk, near, lambda_init):
    kj = pl.program_id(3)
    off = kj * tk - pl.program_id(2) * tq
    heads = [slice(h * LANES, (h + 1) * LANES) for h in range(DIFF_HEADS_PER_STEP)]

    @pl.when(kj == 0)
    def _():
        m_scr[...] = jnp.full_like(m_scr, -jnp.inf)
        acc_scr[...] = jnp.zeros_like(acc_scr)

    tabs = [tab_ref[h] for h in range(DIFF_HEADS_PER_STEP)]
    far_left = [tab[:, 0:1] for tab in tabs]
    far_right = [tab[:, 2 * BIAS_HALF - 1:2 * BIAS_HALF] for tab in tabs]

    def update(biases, shifts):
        chains = []
        for h, sl in enumerate(heads):
            k = k_ref[0, :, sl]
            v1 = _with_ones(v_ref[0, :, sl])
            for m, q_ref in enumerate((q0_ref, q1_ref)):
                chains.append(_flash_chain(q_ref[0, :, sl], k, v1, biases[h], shifts[h], m_scr.at[h, m],
                                           acc_scr.at[h, m]))
        _run_lockstep(chains)

    def toeplitz(tab, d):
        r = tab[:, d + LANES:d + 3 * LANES]
        rows = jnp.broadcast_to(r, (LANES, 2 * LANES))
        return pltpu.roll(rows, LANES, 1, stride=1, stride_axis=0)[:, :LANES]

    for d0 in near:
        @pl.when(off == d0)
        def _(d0=d0):
            for h in range(DIFF_HEADS_PER_STEP):
                blocks = {}
                for ri in range(tq // LANES):
                    for cj in range(tk // LANES):
                        d = d0 + (cj - ri) * LANES
                        sl = (h, slice(ri * LANES, (ri + 1) * LANES), slice(cj * LANES, (cj + 1) * LANES))
                        if d <= -BIAS_HALF:
                            bias_scr[sl] = jnp.broadcast_to(far_left[h], (LANES, LANES))
                        elif d >= BIAS_HALF:
                            bias_scr[sl] = jnp.broadcast_to(far_right[h], (LANES, LANES))
                        else:
                            if d not in blocks:
                                blocks[d] = toeplitz(tabs[h], d)
                            bias_scr[sl] = blocks[d]
            update([bias_scr[h] for h in range(DIFF_HEADS_PER_STEP)], [0.0] * DIFF_HEADS_PER_STEP)

    @pl.when(off < near[0])
    def _():
        update([None] * DIFF_HEADS_PER_STEP, far_left)

    @pl.when(off > near[-1])
    def _():
        update([None] * DIFF_HEADS_PER_STEP, far_right)

    @pl.when(kj == pl.num_programs(3) - 1)
    def _():
        lam = (jnp.exp(jnp.sum(lq1_ref[...] * lk1_ref[...], axis=-1, keepdims=True))
               - jnp.exp(jnp.sum(lq2_ref[...] * lk2_ref[...], axis=-1, keepdims=True)) + lambda_init)
        for h, sl in enumerate(heads):
            a0 = acc_scr[h, 0]
            a1 = acc_scr[h, 1]
            o = a0[:, :LANES] / a0[:, LANES:] - lam * (a1[:, :LANES] / a1[:, LANES:])
            o = _rms(o, sg_ref[...], SUBLN_EPS) * (1.0 - lambda_init)
            o_ref[0, :, sl] = o.astype(o_ref.dtype)


def _diff_flash(q0, q1, k, v, tab, lq1, lk1, lq2, lk2, subln_g, lambda_init):
    b, s, hw = q0.shape
    hps = DIFF_HEADS_PER_STEP
    gw = hps * LANES
    tq = min(512, s)
    tk = min(1024, s)
    offs = sorted({j * tk - i * tq for i in range(s // tq) for j in range(s // tk)})
    near = tuple(d for d in offs if d - (tq - 1) < BIAS_HALF and d + tk - 1 > -BIAS_HALF)
    qspec = pl.BlockSpec((1, tq, gw), lambda bi, h, i, j: (bi, i, h))
    kspec = pl.BlockSpec((1, tk, gw), lambda bi, h, i, j: (bi, j, h))
    vec = lambda n: pl.BlockSpec((1, n), lambda bi, h, i, j: (0, 0))
    return pl.pallas_call(
        functools.partial(_diff_flash_kernel, tq=tq, tk=tk, near=near, lambda_init=lambda_init),
        grid=(b, hw // gw, s // tq, s // tk),
        in_specs=[qspec, qspec, kspec, kspec,
                  pl.BlockSpec((hps, 1, 2 * BIAS_HALF), lambda bi, h, i, j: (h, 0, 0)),
                  vec(D_HEAD), vec(D_HEAD), vec(D_HEAD), vec(D_HEAD), vec(D_V)],
        out_specs=qspec,
        out_shape=jax.ShapeDtypeStruct((b, s, hw), BF16),
        scratch_shapes=[pltpu.VMEM((hps, 2, tq, LANES), F32), pltpu.VMEM((hps, 2, tq, 2 * LANES), F32),
                        pltpu.VMEM((hps, tq, tk), F32)],
        compiler_params=_params("parallel", "parallel", "parallel", "arbitrary"),
        name="diff_flash",
    )(q0, q1, k, v, tab, lq1, lk1, lq2, lk2, subln_g)


def _t5_bucket(rel):
    half = N_BUCKETS // 2
    max_exact = half // 2
    n = jnp.abs(rel)
    large = max_exact + (jnp.log(jnp.maximum(n, 1).astype(jnp.float32) / max_exact)
                         / math.log(MAX_DISTANCE / max_exact) * (half - max_exact)).astype(jnp.int32)
    large = jnp.minimum(large, half - 1)
    return jnp.where(rel > 0, half, 0) + jnp.where(n < max_exact, n, large)


def _rope_tables(s):
    inv = 1.0 / (ROPE_THETA ** (jnp.arange(0, C_ROPE, 2, dtype=F32) / C_ROPE))
    ang = jnp.arange(s, dtype=F32)[:, None] * inv[None, :]
    cos, sin = jnp.cos(ang), jnp.sin(ang)
    pad = LANES - C_NOPE - C_ROPE
    cos_t = jnp.concatenate([jnp.ones((s, C_NOPE), F32), cos, cos, jnp.zeros((s, pad), F32)], axis=-1)
    sin_t = jnp.concatenate([jnp.zeros((s, C_NOPE), F32), sin, sin, jnp.zeros((s, pad), F32)], axis=-1)
    return cos_t, sin_t


def _rot_half_cols(w):
    h = w.shape[-1] // 2
    return jnp.concatenate([-w[..., h:], w[..., :h]], axis=-1)


def _pack_even(j, even_w_in, even_w_out, rwkv_mu, rwkv_w0, rwkv_w_up, rwkv_a0, rwkv_a_up, rwkv_g_up, rwkv_k_k,
               rwkv_k_a, rwkv_r_k, rwkv_lnx_g, rwkv_lnx_b):
    w = A_WIDTH
    z = jnp.zeros((A_LORA, w), F32)
    blockdiag = lambda m: jnp.concatenate(
        [jnp.concatenate([m[0], z], axis=1), jnp.concatenate([z, m[1]], axis=1)], axis=0)
    head = jnp.arange(w) // A_HEAD
    return {
        "w_in_a": even_w_in[j][:, :A_COLS].astype(BF16),
        "w_in_b": even_w_in[j][:, A_COLS:].astype(BF16),
        "w_out_a": even_w_out[j][:w].astype(BF16),
        "w_out_b": even_w_out[j][w:].astype(BF16),
        "mu": rwkv_mu[j].reshape(1, A_COLS),
        "w0": rwkv_w0[j].reshape(1, 2 * w),
        "wup": blockdiag(rwkv_w_up[j]).astype(BF16),
        "a0": rwkv_a0[j].reshape(1, 2 * w),
        "aup": blockdiag(rwkv_a_up[j]).astype(BF16),
        "gup": rwkv_g_up[j].astype(BF16),
        "k_k": rwkv_k_k[j].reshape(1, w),
        "k_a": rwkv_k_a[j].reshape(1, w),
        "r_k": rwkv_r_k[j].reshape(1, w),
        "lnx_g": rwkv_lnx_g[j].reshape(1, w),
        "lnx_b": rwkv_lnx_b[j].reshape(1, w),
        "ones_bd": (head[:, None] == head[None, :]).astype(BF16),
    }


def _pack_odd(j, odd_w_in, odd_w_out, mla_q_norm_g, mla_kv_norm_g, mla_w_uq, mla_w_ukv):
    d = D_MODEL
    w_in = odd_w_in[j]
    o_cq, o_ckv = 0, C_Q_RANK
    o_kr = o_ckv + C_KV_RANK
    o_dq = o_kr + C_ROPE
    n_d = D_HEADS * 2 * D_HEAD
    w_kr = w_in[:, o_kr:o_kr + C_ROPE]
    packed = jnp.concatenate([
        w_in[:, o_dq:o_dq + 3 * n_d], w_in[:, o_cq:o_kr], w_kr, _rot_half_cols(w_kr),
        jnp.zeros((d, ODD_COLS - (C_Q_RANK + C_KV_RANK + 3 * n_d + 2 * C_ROPE)), F32)], axis=1)
    pad = LANES - C_NOPE - C_ROPE
    wq = mla_w_uq[j].reshape(C_Q_RANK, C_HEADS, C_NOPE + C_ROPE)
    zq = jnp.zeros((C_Q_RANK, C_HEADS, pad), F32)
    wqa = jnp.concatenate([wq, zq], axis=-1).reshape(C_Q_RANK, C_HEADS * LANES)
    wqb = jnp.concatenate([jnp.zeros((C_Q_RANK, C_HEADS, C_NOPE), F32), _rot_half_cols(wq[..., C_NOPE:]), zq],
                          axis=-1).reshape(C_Q_RANK, C_HEADS * LANES)
    wkv = mla_w_ukv[j].reshape(C_KV_RANK, C_HEADS, C_NOPE + C_V)
    wk = jnp.concatenate([wkv[..., :C_NOPE], jnp.zeros((C_KV_RANK, C_HEADS, LANES - C_NOPE), F32)],
                         axis=-1).reshape(C_KV_RANK, C_HEADS * LANES)
    wv = wkv[..., C_NOPE:].reshape(C_KV_RANK, C_HEADS * C_V)
    src = jnp.arange(LANES)[:, None]
    dst = jnp.arange(C_HEADS * LANES)[None, :] % LANES
    place = ((dst >= C_NOPE) & (dst < C_NOPE + C_ROPE) & (dst - C_NOPE == src)).astype(BF16)
    hv = C_HEADS * C_V
    return {
        "w_in": packed.astype(BF16),
        "w_out_a": odd_w_out[j][:hv].astype(BF16),
        "w_out_b": odd_w_out[j][hv:].astype(BF16),
        "q_norm_g": mla_q_norm_g[j].reshape(1, C_Q_RANK),
        "kv_norm_g": mla_kv_norm_g[j].reshape(1, C_KV_RANK),
        "wqa": wqa.astype(BF16), "wqb": wqb.astype(BF16), "wk": wk.astype(BF16), "wv": wv.astype(BF16),
        "place": place,
    }


def _even_mixers(x, g1n, sc1, sh1, ew, s5m, s5_d, s5_w_glu, s5_b_glu):
    b, s, _ = x.shape
    pa = _normmod_mm(x, g1n, sc1, sh1, ew["w_in_a"], tn=A_COLS // 3)
    u = _normmod_mm(x, g1n, sc1, sh1, ew["w_in_b"], tn=B_WIDTH)
    r, v, kk, lw, kd, bd, bv, g = _rwkv_prep(pa, ew)
    yf, yr = _rwkv_scan(r, v, kk, lw, kd, bd)
    ya = _rwkv_post(yf, yr, bv, g, ew["lnx_g"], ew["lnx_b"], ew["ones_bd"])
    nrows = s // S5_CHUNK
    u_g = jnp.transpose(u.astype(BF16).reshape(b, nrows, S5_CHUNK, B_GROUPS, B_GROUP), (0, 3, 1, 2, 4))
    u_g = u_g.reshape(b, B_GROUPS, nrows, S5_CHUNK * B_GROUP)
    ys_g = _s5_scan(u_g, s5m[0], False) + _s5_scan(u_g, s5m[1], True)
    ys = jnp.transpose(ys_g.reshape(b, B_GROUPS, nrows, S5_CHUNK, B_GROUP), (0, 2, 3, 1, 4)).reshape(b, s, B_WIDTH)
    yb = _s5_glu(ys, u, s5_d, s5_w_glu, s5_b_glu)
    return ya, yb


def _odd_mixers(x, g1n, sc1, sh1, ow, tabs, diff_w, lambda_init):
    s = x.shape[1]
    p = _normmod_mm(x, g1n, sc1, sh1, ow["w_in"], tn=1024)
    mq, mk, mv, q0, q1, dk, dv = _odd_prep(p, tabs["cos"][:s], tabs["sin"][:s], ow)
    yc = _mla_flash(mq, mk, mv)
    yd = _diff_flash(q0, q1, dk, dv, tabs["bias"], diff_w["lq1"], diff_w["lk1"], diff_w["lq2"], diff_w["lk2"],
                     diff_w["subln_g"], lambda_init)
    return yc, yd


def kernel(x_prompt, x_sample, c_prompt, c_sample, ada_w, ada_b, norm1_g, norm2_g, even_w_in, even_w_out, rwkv_mu, rwkv_w0, rwkv_w_up, rwkv_a0, rwkv_a_up, rwkv_g_up, rwkv_k_k, rwkv_k_a, rwkv_r_k, rwkv_lnx_g, rwkv_lnx_b, s5_lam_re, s5_lam_im, s5_log_step, s5_b_re, s5_b_im, s5_c_re, s5_c_im, s5_d, s5_w_glu, s5_b_glu, odd_w_in, odd_w_out, mla_q_norm_g, mla_kv_norm_g, mla_w_uq, mla_w_ukv, diff_lq1, diff_lk1, diff_lq2, diff_lk2, diff_subln_g, rel_bias, ffn_w_up, ffn_conv_w, ffn_conv_b, ffn_w_down, final_g):
    d = D_MODEL
    groups = [(x_prompt, c_prompt), (x_sample, c_sample)]
    nb = [g[0].shape[0] for g in groups]
    c_all = jnp.concatenate([g[1] for g in groups] + [jnp.zeros((SUBLANES - sum(nb), d), F32)], axis=0)
    mod = _ada_mod(c_all, ada_w, ada_b)

    max_s = max(g[0].shape[1] for g in groups)
    cos_t, sin_t = _rope_tables(max_s)
    rel = jnp.arange(-BIAS_HALF, BIAS_HALF, dtype=jnp.int32)
    bias_tab = (jnp.transpose(rel_bias.astype(F32)[_t5_bucket(rel)]) * LOG2E).reshape(D_HEADS, 1, 2 * BIAS_HALF)
    tabs = {"cos": cos_t, "sin": sin_t, "bias": bias_tab}

    xs = [g[0] for g in groups]
    for i in range(DEPTH):
        j = i // 2
        if i % 2 == 0:
            ew = _pack_even(j, even_w_in, even_w_out, rwkv_mu, rwkv_w0, rwkv_w_up, rwkv_a0, rwkv_a_up, rwkv_g_up,
                            rwkv_k_k, rwkv_k_a, rwkv_r_k, rwkv_lnx_g, rwkv_lnx_b)
            s5m = [_s5_matrices(s5_lam_re[j, dr], s5_lam_im[j, dr], s5_log_step[j, dr], s5_b_re[j, dr],
                                s5_b_im[j, dr], s5_c_re[j, dr], s5_c_im[j, dr], dr == 1) for dr in range(2)]
        else:
            ow = _pack_odd(j, odd_w_in, odd_w_out, mla_q_norm_g, mla_kv_norm_g, mla_w_uq, mla_w_ukv)
            diff_w = {"lq1": diff_lq1[j].reshape(1, D_HEAD), "lk1": diff_lk1[j].reshape(1, D_HEAD),
                      "lq2": diff_lq2[j].reshape(1, D_HEAD), "lk2": diff_lk2[j].reshape(1, D_HEAD),
                      "subln_g": diff_subln_g[j].reshape(1, D_V)}
        w_up = ffn_w_up[i].astype(BF16)
        w_down = ffn_w_down[i].astype(BF16)
        row0 = 0
        for gi in range(len(groups)):
            x = xs[gi]
            m = mod[i, row0:row0 + nb[gi]]
            row0 += nb[gi]
            sh1, sc1, g1, sh2, sc2, g2 = [m[:, None, k * d:(k + 1) * d] for k in range(N_MOD)]
            if i % 2 == 0:
                ya, yb = _even_mixers(x, norm1_g[i], sc1, sh1, ew, s5m, s5_d[j].reshape(1, B_WIDTH),
                                      s5_w_glu[j].astype(BF16), s5_b_glu[j].reshape(1, B_WIDTH))
                x = _out_proj(ya, yb, ew["w_out_a"], ew["w_out_b"], x, g1)
            else:
                yc, yd = _odd_mixers(x, norm1_g[i], sc1, sh1, ow, tabs, diff_w, 0.8 - 0.6 * math.exp(-0.3 * i))
                x = _out_proj(yc, yd, ow["w_out_a"], ow["w_out_b"], x, g1)
            u = _normmod_mm(x, norm2_g[i], sc2, sh2, w_up, tn=1024)
            x = _ffn_down(u, ffn_conv_w[i], ffn_conv_b[i].reshape(1, 2 * FFN_HIDDEN), w_down, x, g2,
                          final_g.reshape(1, d), final=(i == DEPTH - 1))
            xs[gi] = x
    return (xs[0], xs[1])
```

```python
import functools
import math

import jax
import jax.numpy as jnp
from jax import lax
from jax.experimental import pallas as pl
from jax.experimental.pallas import tpu as pltpu

F32 = jnp.float32
BF16 = jnp.bfloat16

D_MODEL = 2048
DEPTH = 2
EPS = 1e-6
A_WIDTH = 1024
A_HEAD = 64
A_HEADS = 16
A_LORA = 64
A_GATE_LORA = 128
A_COLS = 3 * A_WIDTH + 2 * A_LORA + 2 * A_LORA + A_GATE_LORA
RWKV_GN_EPS = 64e-5
DECAY_SCALE = math.exp(-0.5)
B_WIDTH = 1024
B_GROUP = 16
B_GROUPS = 64
B_STATE = 64
C_HEADS = 8
C_NOPE = 64
C_ROPE = 32
C_V = 128
C_Q_RANK = 512
C_KV_RANK = 256
ROPE_THETA = 10000.0
D_HEADS = 8
D_HEAD = 64
D_V = 128
SUBLN_EPS = 1e-5
N_BUCKETS = 32
MAX_DISTANCE = 128
FFN_HIDDEN = 5632
N_MOD = 6

LANES = 128
SUBLANES = 8
VMEM_LIMIT_BYTES = 56 * 1024 * 1024

RWKV_CHUNK = 64
S5_CHUNK = 16
LOG2E = 1.4426950408889634

NN = (((1,), (0,)), ((), ()))
NT = (((1,), (1,)), ((), ()))
BATCH_NN = (((2,), (1,)), ((0,), (0,)))


def _params(*sem):
    return pltpu.CompilerParams(dimension_semantics=sem, vmem_limit_bytes=VMEM_LIMIT_BYTES)


def _dot(a, b, dims=NN):
    return lax.dot_general(a, b, dims, preferred_element_type=F32)


def _split2(x):
    hi = x.astype(BF16)
    lo = (x - hi.astype(F32)).astype(BF16)
    return hi, lo


def _dot3(a, b, dims=NN):
    ah, al = _split2(a)
    bh, bl = _split2(b)
    return _dot(ah, bh, dims) + _dot(ah, bl, dims) + _dot(al, bh, dims)


def _dot_exact_rhs(a, b):
    ah, al = _split2(a)
    return _dot(ah, b) + _dot(al, b)


def _ada_kernel(c_ref, w_ref, b_ref, o_ref):
    c = c_ref[...]
    cs = c * jax.nn.sigmoid(c)
    o_ref[0] = _dot(cs.astype(BF16), w_ref[0].astype(BF16)) + b_ref[0]


def _ada_mod(c_all, ada_w, ada_b):
    n = N_MOD * D_MODEL
    tn = 1024
    return pl.pallas_call(
        _ada_kernel,
        grid=(DEPTH, n // tn),
        in_specs=[
            pl.BlockSpec((SUBLANES, D_MODEL), lambda l, j: (0, 0)),
            pl.BlockSpec((1, D_MODEL, tn), lambda l, j: (l, 0, j)),
            pl.BlockSpec((1, 1, tn), lambda l, j: (l, 0, j)),
        ],
        out_specs=pl.BlockSpec((1, SUBLANES, tn), lambda l, j: (l, 0, j)),
        out_shape=jax.ShapeDtypeStruct((DEPTH, SUBLANES, n), F32),
        compiler_params=_params("parallel", "parallel"),
        name="ada_mod",
    )(c_all, ada_w, ada_b.reshape(DEPTH, 1, n))


def _modnorm(x, g, sc, sh):
    y = x * lax.rsqrt(jnp.mean(x * x, axis=-1, keepdims=True) + EPS)
    return (y * g) * (1.0 + sc) + sh


def _normmod_mm_kernel(x_ref, g_ref, sc_ref, sh_ref, w_ref, o_ref, h_scr):
    @pl.when(pl.program_id(2) == 0)
    def _():
        h_scr[...] = _modnorm(x_ref[0], g_ref[...], sc_ref[0], sh_ref[0]).astype(BF16)

    o_ref[0] = _dot(h_scr[...], w_ref[...]).astype(o_ref.dtype)


def _normmod_mm(x, g, sc, sh, w, tn, out_dtype=F32):
    b, s, d = x.shape
    n = w.shape[1]
    tm = min(512, s)
    return pl.pallas_call(
        _normmod_mm_kernel,
        grid=(b, s // tm, n // tn),
        in_specs=[
            pl.BlockSpec((1, tm, d), lambda bi, i, j: (bi, i, 0)),
            pl.BlockSpec((1, d), lambda bi, i, j: (0, 0)),
            pl.BlockSpec((1, 1, d), lambda bi, i, j: (bi, 0, 0)),
            pl.BlockSpec((1, 1, d), lambda bi, i, j: (bi, 0, 0)),
            pl.BlockSpec((d, tn), lambda bi, i, j: (0, j)),
        ],
        out_specs=pl.BlockSpec((1, tm, tn), lambda bi, i, j: (bi, i, j)),
        out_shape=jax.ShapeDtypeStruct((b, s, n), out_dtype),
        scratch_shapes=[pltpu.VMEM((tm, d), BF16)],
        compiler_params=_params("parallel", "parallel", "arbitrary"),
        name="normmod_mm",
    )(x, g.reshape(1, d), sc, sh, w)


def _rwkv_prep_kernel(p_ref, pp_ref, pn_ref, mu_ref, w0_ref, wup_ref, a0_ref, aup_ref, gup_ref, kk_ref, ka_ref,
                      rk_ref, ones_ref, r_out, v_out, kkn_out, lw_out, kd_out, bd_out, bv_out, g_out):
    i = pl.program_id(1)
    last = pl.num_programs(1) - 1
    pa = p_ref[0]
    tm = pa.shape[0]
    row = lax.broadcasted_iota(jnp.int32, (tm, 1), 0)
    prev_row = jnp.where(i == 0, 0.0, pp_ref[0][SUBLANES - 1:SUBLANES, :])
    next_row = jnp.where(i == last, 0.0, pn_ref[0][0:1, :])
    p_prev = jnp.where(row == 0, prev_row, pltpu.roll(pa, 1, 0))
    p_next = jnp.where(row == tm - 1, next_row, pltpu.roll(pa, tm - 1, 0))
    pa = pa + mu_ref[...] * (0.5 * (p_prev + p_next) - pa)

    w = A_WIDTH
    r = pa[:, 0:w]
    k = pa[:, w:2 * w]
    v = pa[:, 2 * w:3 * w]
    dw = pa[:, 3 * w:3 * w + 2 * A_LORA]
    da = pa[:, 3 * w + 2 * A_LORA:3 * w + 4 * A_LORA]
    dg = pa[:, 3 * w + 4 * A_LORA:A_COLS]

    lw = -DECAY_SCALE * jax.nn.sigmoid(w0_ref[...] + _dot(jnp.tanh(dw).astype(BF16), wup_ref[...]))
    icl = jax.nn.sigmoid(a0_ref[...] + _dot(da.astype(BF16), aup_ref[...]))
    g = _dot(jax.nn.sigmoid(dg).astype(BF16), gup_ref[...])

    ones_bd = ones_ref[...]
    kkr = k * kk_ref[...]
    ss = _dot_exact_rhs(kkr * kkr, ones_bd)
    kkn = kkr / jnp.maximum(jnp.sqrt(ss), 1e-12)

    r_out[0] = r
    v_out[0] = v
    kkn_out[0] = kkn
    lw_out[0] = lw
    g_out[0] = g
    bonus = jnp.zeros_like(r)
    for d in range(2):
        icl_d = icl[:, d * w:(d + 1) * w]
        k_d = k * (1.0 + (icl_d - 1.0) * ka_ref[...])
        kd_out[0, :, d * w:(d + 1) * w] = k_d
        bd_out[0, :, d * w:(d + 1) * w] = icl_d * kkn
        bonus = bonus + _dot_exact_rhs(r * k_d * rk_ref[...], ones_bd)
    bv_out[0] = bonus * v


def _rwkv_prep(pa, wts):
    b, s, _ = pa.shape
    tm = min(256, s)
    nh = tm // SUBLANES
    w = A_WIDTH
    full = lambda shape: pl.BlockSpec(shape, lambda bi, i: (0,) * len(shape))
    tok = lambda n: pl.BlockSpec((1, tm, n), lambda bi, i: (bi, i, 0))
    out_shapes = [jax.ShapeDtypeStruct((b, s, n), F32) for n in (w, w, w, 2 * w, 2 * w, 2 * w, w, w)]
    return pl.pallas_call(
        _rwkv_prep_kernel,
        grid=(b, s // tm),
        in_specs=[
            tok(A_COLS),
            pl.BlockSpec((1, SUBLANES, A_COLS), lambda bi, i: (bi, jnp.maximum(i * nh - 1, 0), 0)),
            pl.BlockSpec((1, SUBLANES, A_COLS), lambda bi, i: (bi, jnp.minimum((i + 1) * nh, s // SUBLANES - 1), 0)),
            full((1, A_COLS)), full((1, 2 * w)), full((2 * A_LORA, 2 * w)), full((1, 2 * w)),
            full((2 * A_LORA, 2 * w)), full((A_GATE_LORA, w)), full((1, w)), full((1, w)), full((1, w)),
            full((w, w)),
        ],
        out_specs=[tok(sd.shape[-1]) for sd in out_shapes],
        out_shape=out_shapes,
        compiler_params=_params("parallel", "parallel"),
        name="rwkv_prep",
    )(pa, pa, pa, wts["mu"], wts["w0"], wts["wup"], wts["a0"], wts["aup"], wts["gup"], wts["k_k"], wts["k_a"],
      wts["r_k"], wts["ones_bd"])


def _rwkv_chunk(r, v, kk, lw, kd, bd, s2, reverse):
    t = RWKV_CHUNK
    sign = -1 if reverse else 1
    ri = lax.broadcasted_iota(jnp.int32, (t, t), 0)
    ci = lax.broadcasted_iota(jnp.int32, (t, t), 1)
    tri = jnp.where((ri - ci) * sign >= 0, 1.0, 0.0).astype(BF16)
    lw_hi = lw.astype(BF16)
    rem = lw - lw_hi.astype(F32)
    lw_mid = rem.astype(BF16)
    lw_lo = (rem - lw_mid.astype(F32)).astype(BF16)
    cum = _dot(tri, lw_hi) + _dot(tri, lw_mid) + _dot(tri, lw_lo)
    yield
    tot = jnp.sum(lw, axis=0, keepdims=True)
    gam = jnp.exp(cum)
    gam_ex = jnp.exp(cum - lw)
    ginv = jnp.exp(-cum)
    gend = jnp.exp(tot - cum)
    gtot = jnp.exp(tot)

    width = r.shape[1]
    heads = width // A_HEAD
    lane_head = lax.broadcasted_iota(jnp.int32, (1, width), 1) // A_HEAD

    def stack(x):
        return jnp.concatenate([jnp.where(lane_head == h, x, 0.0) for h in range(heads)], axis=0).astype(BF16)

    q_rk = jnp.concatenate([stack(r * gam), stack(kk * gam_ex)], axis=0)
    k_kb = jnp.concatenate([stack(kd * ginv), stack(bd * ginv)], axis=0)
    k_kb_end = jnp.concatenate([stack(kd * gend), stack(bd * gend)], axis=0)
    vs = stack(v)

    n = heads * t
    a = _dot(q_rk, k_kb, NT)
    yield
    ri2 = lax.broadcasted_iota(jnp.int32, (n, n), 0)
    ci2 = lax.broadcasted_iota(jnp.int32, (n, n), 1)
    order = (ri2 - ci2) * sign
    incl = order >= 0
    strict = order > 0
    a_rk = jnp.where(incl, a[:n, :n], 0.0)
    a_rb = jnp.where(incl, a[:n, n:], 0.0)
    a_kk = jnp.where(strict, a[n:, :n], 0.0)
    a_bk = jnp.where(strict, a[n:, n:], 0.0)

    qs = _dot(q_rk, s2.astype(BF16), NT)
    yield
    akv = _dot(jnp.concatenate([a_rk, a_kk], axis=0).astype(BF16), vs)
    yield

    nb = a_bk.astype(BF16)
    n2 = _dot(nb, nb)
    yield
    n2b = n2.astype(BF16)
    n4 = _dot(n2b, n2b)
    yield
    n3 = _dot(nb, n2b)
    yield
    n4b = n4.astype(BF16)
    n8 = _dot(n4b, n4b)
    yield
    eye = jnp.where(ri2 == ci2, 1.0, 0.0)
    f1 = eye - a_bk + n2 - n3
    n8b = n8.astype(BF16)
    n16 = _dot(n8b, n8b)
    yield
    n12 = _dot(n4b, n8b)
    yield
    f2 = n4 + n8 + n12
    f12 = f1 + _dot(f1.astype(BF16), f2.astype(BF16))
    yield
    n16b = n16.astype(BF16)
    n32 = _dot(n16b, n16b)
    yield
    n48 = _dot(n16b, n32.astype(BF16))
    yield
    f3 = n16 + n32 + n48
    inv = f12 + _dot(f12.astype(BF16), f3.astype(BF16))
    yield

    ps = _dot(inv.astype(BF16), (-qs[n:] - akv[n:]).astype(BF16)).astype(BF16)
    yield
    ys = qs[:n] + akv[:n] + _dot(a_rb.astype(BF16), ps)
    yield
    y = ys[0:t]
    for h in range(1, heads):
        y = y + ys[h * t:(h + 1) * t]
    s2_new = s2 * gtot + _dot(jnp.concatenate([vs, ps], axis=0).astype(F32).T.astype(BF16), k_kb_end)
    return y, s2_new


def _run_lockstep(gens):
    results = [None] * len(gens)
    active = list(range(len(gens)))
    while active:
        for i in list(active):
            try:
                next(gens[i])
            except StopIteration as e:
                results[i] = e.value
                active.remove(i)
    return results


RWKV_CHAIN_LANES = 256
RWKV_CHAINS_PER_STEP = 4


def _rwkv_scan_kernel(rf_ref, vf_ref, kkf_ref, lwf_ref, kdf_ref, bdf_ref,
                      rb_ref, vb_ref, kkb_ref, lwb_ref, kdb_ref, bdb_ref, yf_ref, yb_ref, s_scr):
    @pl.when(pl.program_id(2) == 0)
    def _():
        s_scr[...] = jnp.zeros_like(s_scr)

    dirs = ((rf_ref, vf_ref, kkf_ref, lwf_ref, kdf_ref, bdf_ref, yf_ref),
            (rb_ref, vb_ref, kkb_ref, lwb_ref, kdb_ref, bdb_ref, yb_ref))
    chains = []
    for d, (r_ref, v_ref, kk_ref, lw_ref, kd_ref, bd_ref, y_ref) in enumerate(dirs):
        for p in range(RWKV_CHAINS_PER_STEP):
            sl = slice(p * RWKV_CHAIN_LANES, (p + 1) * RWKV_CHAIN_LANES)
            chains.append((d, p, sl, y_ref, _rwkv_chunk(
                r_ref[0, :, sl], v_ref[0, :, sl], kk_ref[0, :, sl], lw_ref[0, :, sl], kd_ref[0, :, sl],
                bd_ref[0, :, sl], s_scr[d, p], reverse=(d == 1))))
    results = _run_lockstep([c[4] for c in chains])
    for (d, p, sl, y_ref, _), (y, s_new) in zip(chains, results):
        y_ref[0, :, sl] = y
        s_scr[d, p] = s_new


def _rwkv_scan(r, v, kk, lw, kd, bd):
    b, s, w = r.shape
    t = RWKV_CHUNK
    nc = s // t
    pw = RWKV_CHAINS_PER_STEP * RWKV_CHAIN_LANES
    ngrp = w // pw
    fwd = lambda off: pl.BlockSpec((1, t, pw), lambda bi, g, c: (bi, c, g + off))
    bwd = lambda off: pl.BlockSpec((1, t, pw), lambda bi, g, c: (bi, nc - 1 - c, g + off))
    return pl.pallas_call(
        _rwkv_scan_kernel,
        grid=(b, ngrp, nc),
        in_specs=[fwd(0)] * 6 + [bwd(0)] * 3 + [bwd(ngrp)] * 3,
        out_specs=[fwd(0), bwd(0)],
        out_shape=[jax.ShapeDtypeStruct((b, s, w), F32)] * 2,
        scratch_shapes=[pltpu.VMEM((2, RWKV_CHAINS_PER_STEP, RWKV_CHAIN_LANES, RWKV_CHAIN_LANES), F32)],
        compiler_params=_params("parallel", "parallel", "arbitrary"),
        name="rwkv_scan",
    )(r, v, kk, lw, kd, bd, r, v, kk, lw, kd, bd)


def _rwkv_post_kernel(yf_ref, yb_ref, bv_ref, g_ref, lng_ref, lnb_ref, ones_ref, o_ref):
    y = yf_ref[0] + yb_ref[0]
    ones_bd = ones_ref[...]
    mean = _dot_exact_rhs(y, ones_bd) * (1.0 / A_HEAD)
    yc = y - mean
    var = _dot_exact_rhs(yc * yc, ones_bd) * (1.0 / A_HEAD)
    yn = yc * lax.rsqrt(var + RWKV_GN_EPS) * lng_ref[...] + lnb_ref[...]
    o_ref[0] = ((yn + bv_ref[0]) * g_ref[0]).astype(o_ref.dtype)


def _rwkv_post(yf, yb, bv, g, lng, lnb, ones_bd):
    b, s, w = bv.shape
    tm = min(512, s)
    full = lambda shape: pl.BlockSpec(shape, lambda bi, i: (0,) * len(shape))
    tok = lambda n: pl.BlockSpec((1, tm, n), lambda bi, i: (bi, i, 0))
    return pl.pallas_call(
        _rwkv_post_kernel,
        grid=(b, s // tm),
        in_specs=[tok(w), tok(w), tok(w), tok(w), full((1, w)), full((1, w)), full((w, w))],
        out_specs=tok(w),
        out_shape=jax.ShapeDtypeStruct((b, s, w), BF16),
        compiler_params=_params("parallel", "parallel"),
        name="rwkv_post",
    )(yf, yb, bv, g, lng, lnb, ones_bd)


def _s5_kernel(u_ref, m_ref, w_ref, ws_ref, v_ref, a_ref, y_ref, x_scr, xs_scr, h_scr, carry_scr, *, rows, reverse):
    g = B_GROUPS

    @pl.when(pl.program_id(1) == 0)
    def _():
        carry_scr[...] = jnp.zeros_like(carry_scr)

    u = u_ref[0]
    x_scr[...] = lax.dot_general(u, w_ref[...], BATCH_NN, preferred_element_type=F32).reshape(g * rows, LANES)
    xs_scr[...] = lax.dot_general(u, ws_ref[...], BATCH_NN, preferred_element_type=F32).reshape(g * rows, LANES)
    a1 = a_ref[0]
    a2 = a_ref[1]
    a2s = a_ref[2]

    def step(i, carry):
        h, hs = carry
        r = (rows - 1 - i) if reverse else i
        idx = pl.ds(r, g, stride=rows)
        h_scr[idx, :] = h
        hn = a1 * h + a2 * hs + x_scr[idx, :]
        hsn = a1 * hs + a2s * h + xs_scr[idx, :]
        return hn, hsn

    h, hs = lax.fori_loop(0, rows, step, (carry_scr[0], carry_scr[1]))
    carry_scr[0] = h
    carry_scr[1] = hs
    hprev = h_scr[...].reshape(g, rows, LANES).astype(BF16)
    y_ref[0] = (lax.dot_general(u, m_ref[...], BATCH_NN, preferred_element_type=F32)
                + lax.dot_general(hprev, v_ref[...], BATCH_NN, preferred_element_type=F32))


def _s5_scan(u_g, mats, reverse):
    b, g, nrows, cw = u_g.shape
    rows = min(64, nrows)
    nsb = nrows // rows
    whole = pl.BlockSpec(memory_space=pltpu.VMEM)
    idx = (lambda bi, i: (bi, 0, nsb - 1 - i, 0)) if reverse else (lambda bi, i: (bi, 0, i, 0))
    return pl.pallas_call(
        functools.partial(_s5_kernel, rows=rows, reverse=reverse),
        grid=(b, nsb),
        in_specs=[pl.BlockSpec((1, g, rows, cw), idx), whole, whole, whole, whole, whole],
        out_specs=pl.BlockSpec((1, g, rows, cw), idx),
        out_shape=jax.ShapeDtypeStruct((b, g, nrows, cw), F32),
        scratch_shapes=[pltpu.VMEM((g * rows, LANES), F32), pltpu.VMEM((g * rows, LANES), F32),
                        pltpu.VMEM((g * rows, LANES), F32), pltpu.VMEM((2, g, LANES), F32)],
        compiler_params=_params("parallel", "arbitrary"),
        name="s5_scan_bwd" if reverse else "s5_scan_fwd",
    )(u_g, mats["m"], mats["w"], mats["ws"], mats["v"], mats["a"])


def _s5_matrices(lam_re, lam_im, log_step, b_re, b_im, c_re, c_im, reverse):
    hp = lax.Precision.HIGHEST
    t = S5_CHUNK
    g, p, c = B_GROUPS, B_STATE, B_GROUP
    lr, li = lam_re.astype(F32), lam_im.astype(F32)
    step = jnp.exp(log_step.astype(F32))[:, None]
    ar, ai = jnp.exp(lr * step) * jnp.cos(li * step), jnp.exp(lr * step) * jnp.sin(li * step)
    den = lr * lr + li * li
    nr, ni = ar - 1.0, ai
    fr, fi = (nr * lr + ni * li) / den, (ni * lr - nr * li) / den
    br, bi = b_re.astype(F32), b_im.astype(F32)
    bbr = fr[..., None] * br - fi[..., None] * bi
    bbi = fr[..., None] * bi + fi[..., None] * br
    cr, cim = c_re.astype(F32), c_im.astype(F32)
    taus = jnp.arange(t + 1, dtype=F32)[:, None, None]
    mag = jnp.exp(lr * step * taus)
    pr, pi = mag * jnp.cos(li * step * taus), mag * jnp.sin(li * step * taus)
    cpr = cr[None] * pr[:, :, None, :] - cim[None] * pi[:, :, None, :]
    cpi = cr[None] * pi[:, :, None, :] + cim[None] * pr[:, :, None, :]
    kern = (jnp.einsum("tgcp,gpd->tgcd", cpr, bbr, precision=hp)
            - jnp.einsum("tgcp,gpd->tgcd", cpi, bbi, precision=hp))
    s_idx = jnp.arange(t)[:, None]
    t_idx = jnp.arange(t)[None, :]
    lag = (s_idx - t_idx) if reverse else (t_idx - s_idx)
    kk = jnp.where((lag >= 0)[:, :, None, None, None], kern[jnp.clip(lag, 0, t)], 0.0)
    m = jnp.transpose(kk, (2, 0, 4, 1, 3)).reshape(g, t * c, t * c)
    e = jnp.arange(t) if reverse else (t - 1 - jnp.arange(t))
    pre, pie = pr[e], pi[e]
    wre = pre[..., None] * bbr[None] - pie[..., None] * bbi[None]
    wim = pre[..., None] * bbi[None] + pie[..., None] * bbr[None]
    wre = jnp.transpose(wre, (1, 0, 3, 2)).reshape(g, t * c, p)
    wim = jnp.transpose(wim, (1, 0, 3, 2)).reshape(g, t * c, p)
    w = jnp.concatenate([wre, wim], axis=-1)
    ws = jnp.concatenate([wim, wre], axis=-1)
    f = (t - jnp.arange(t)) if reverse else (jnp.arange(t) + 1)
    vre = jnp.transpose(cpr[f], (1, 3, 0, 2)).reshape(g, p, t * c)
    vim = jnp.transpose(-cpi[f], (1, 3, 0, 2)).reshape(g, p, t * c)
    v = jnp.concatenate([vre, vim], axis=1)
    atr, ati = pr[t], pi[t]
    a = jnp.stack([jnp.concatenate([atr, atr], -1), jnp.concatenate([-ati, ati], -1),
                   jnp.concatenate([ati, -ati], -1)])
    return {"m": m.astype(BF16), "w": w.astype(BF16), "ws": ws.astype(BF16), "v": v.astype(BF16), "a": a}


def _s5_glu_kernel(ys_ref, u_ref, d_ref, w_ref, b_ref, o_ref):
    y = ys_ref[0] + u_ref[0] * d_ref[...]
    z = jax.nn.gelu(y)
    gate = jax.nn.sigmoid(_dot(z.astype(BF16), w_ref[...]) + b_ref[...])
    o_ref[0] = (z * gate).astype(o_ref.dtype)


def _s5_glu(ys, u, d_skip, w_glu, b_glu):
    b, s, w = u.shape
    tm = min(512, s)
    full = lambda shape: pl.BlockSpec(shape, lambda bi, i: (0,) * len(shape))
    tok = pl.BlockSpec((1, tm, w), lambda bi, i: (bi, i, 0))
    return pl.pallas_call(
        _s5_glu_kernel,
        grid=(b, s // tm),
        in_specs=[tok, tok, full((1, w)), full((w, w)), full((1, w))],
        out_specs=tok,
        out_shape=jax.ShapeDtypeStruct((b, s, w), BF16),
        compiler_params=_params("parallel", "parallel"),
        name="s5_glu",
    )(ys, u, d_skip, w_glu, b_glu)


def _out_proj_kernel(a_ref, b_ref, wa_ref, wb_ref, x_ref, g_ref, o_ref):
    mix = _dot(a_ref[0], wa_ref[...]) + _dot(b_ref[0], wb_ref[...])
    o_ref[0] = x_ref[0] + g_ref[0] * mix


def _out_proj(a, bb, wa, wb, x, gate):
    b, s, d = x.shape
    k = a.shape[-1]
    tm = min(512, s)
    tn = 1024
    return pl.pallas_call(
        _out_proj_kernel,
        grid=(b, s // tm, d // tn),
        in_specs=[
            pl.BlockSpec((1, tm, k), lambda bi, i, j: (bi, i, 0)),
            pl.BlockSpec((1, tm, k), lambda bi, i, j: (bi, i, 0)),
            pl.BlockSpec((k, tn), lambda bi, i, j: (0, j)),
            pl.BlockSpec((k, tn), lambda bi, i, j: (0, j)),
            pl.BlockSpec((1, tm, tn), lambda bi, i, j: (bi, i, j)),
            pl.BlockSpec((1, 1, tn), lambda bi, i, j: (bi, 0, j)),
        ],
        out_specs=pl.BlockSpec((1, tm, tn), lambda bi, i, j: (bi, i, j)),
        out_shape=jax.ShapeDtypeStruct((b, s, d), F32),
        compiler_params=_params("parallel", "parallel", "parallel"),
        name="out_proj",
    )(a, bb, wa, wb, x, gate)


FFN_HALO = 2 * SUBLANES


def _ffn_kernel(x_ref, xp_ref, xn_ref, ng_ref, sc_ref, sh_ref, wv_ref, wg_ref, cwv_ref, cwg_ref, cbv_ref, cbg_ref,
                wd_ref, g_ref, fg_ref, o_ref, h_scr, acc_scr, *, final):
    i = pl.program_id(1)
    j = pl.program_id(2)
    tm = x_ref.shape[1]
    hl = FFN_HALO

    @pl.when(j == 0)
    def _():
        g, sc, sh = ng_ref[...], sc_ref[0], sh_ref[0]
        h_scr[hl:hl + tm] = _modnorm(x_ref[0], g, sc, sh).astype(BF16)
        before = jnp.where(i == 0, 0.0, _modnorm(xp_ref[0], g, sc, sh))
        after = jnp.where(i == pl.num_programs(1) - 1, 0.0, _modnorm(xn_ref[0], g, sc, sh))
        h_scr[0:hl] = before.astype(BF16)
        h_scr[hl + tm:2 * hl + tm] = after.astype(BF16)
        acc_scr[...] = jnp.zeros_like(acc_scr)

    h = h_scr[...]
    rows = tm + 2 * hl

    def conv(w_ref, cw_ref, cb_ref):
        u = _dot(h, w_ref[...])
        cw = cw_ref[...]
        u_prev = pltpu.roll(u, 1, 0)[hl:hl + tm]
        u_next = pltpu.roll(u, rows - 1, 0)[hl:hl + tm]
        return u_prev * cw[0:1] + u[hl:hl + tm] * cw[1:2] + u_next * cw[2:3] + cb_ref[...]

    val = conv(wv_ref, cwv_ref, cbv_ref)
    gate = conv(wg_ref, cwg_ref, cbg_ref)
    act = (gate * jax.nn.sigmoid(gate)) * val
    acc_scr[...] += _dot(act.astype(BF16), wd_ref[...])

    @pl.when(j == pl.num_programs(2) - 1)
    def _():
        xn = x_ref[0] + g_ref[0] * acc_scr[...]
        if final:
            xn = xn * lax.rsqrt(jnp.mean(xn * xn, axis=-1, keepdims=True) + EPS) * fg_ref[...]
        o_ref[0] = xn


def _ffn(x, norm_g, sc, sh, w_up, conv_w, conv_b, w_down, gate, final_g, final):
    b, s, d = x.shape
    f = FFN_HIDDEN
    tm = min(512, s)
    tf = 512
    nf = f // tf
    nh = tm // FFN_HALO
    nhalo = s // FFN_HALO
    const = lambda shape: pl.BlockSpec(shape, lambda bi, i, j: (0,) * len(shape))
    per_b = pl.BlockSpec((1, 1, d), lambda bi, i, j: (bi, 0, 0))
    up = lambda off: pl.BlockSpec((d, tf), lambda bi, i, j: (0, j + off))
    cw = lambda off: pl.BlockSpec((3, tf), lambda bi, i, j: (0, j + off))
    cb = lambda off: pl.BlockSpec((1, tf), lambda bi, i, j: (0, j + off))
    return pl.pallas_call(
        functools.partial(_ffn_kernel, final=final),
        grid=(b, s // tm, nf),
        in_specs=[
            pl.BlockSpec((1, tm, d), lambda bi, i, j: (bi, i, 0)),
            pl.BlockSpec((1, FFN_HALO, d), lambda bi, i, j: (bi, jnp.maximum(i * nh - 1, 0), 0)),
            pl.BlockSpec((1, FFN_HALO, d), lambda bi, i, j: (bi, jnp.minimum((i + 1) * nh, nhalo - 1), 0)),
            const((1, d)), per_b, per_b,
            up(0), up(nf), cw(0), cw(nf), cb(0), cb(nf),
            pl.BlockSpec((tf, d), lambda bi, i, j: (j, 0)),
            per_b, const((1, d)),
        ],
        out_specs=pl.BlockSpec((1, tm, d), lambda bi, i, j: (bi, i, 0)),
        out_shape=jax.ShapeDtypeStruct((b, s, d), F32),
        scratch_shapes=[pltpu.VMEM((tm + 2 * FFN_HALO, d), BF16), pltpu.VMEM((tm, d), F32)],
        compiler_params=_params("parallel", "parallel", "arbitrary"),
        name="ffn",
    )(x, x, x, norm_g.reshape(1, d), sc, sh, w_up, w_up, conv_w, conv_w, conv_b, conv_b, w_down, gate, final_g)


def _rms(x, g, eps):
    return x * lax.rsqrt(jnp.mean(x * x, axis=-1, keepdims=True) + eps) * g


def _odd_prep_kernel(cq_ref, ckv_ref, kr_ref, dq_ref, dk_ref, dv_ref, cos_ref, sin_ref, qg_ref, kvg_ref,
                     wqa_ref, wqb_ref, wk_ref, wv_ref, place_ref,
                     mq_out, mk_out, mv_out, q0_out, q1_out, dk_out, dv_out):
    cosq = cos_ref[...]
    sinq = sin_ref[...]
    qn = _rms(cq_ref[0], qg_ref[...], EPS).astype(BF16)
    qa = _dot(qn, wqa_ref[...])
    qb = _dot(qn, wqb_ref[...])
    mla_scale = (C_NOPE + C_ROPE) ** -0.5 * LOG2E
    for h in range(C_HEADS):
        sl = slice(h * LANES, (h + 1) * LANES)
        mq_out[0, :, sl] = ((qa[:, sl] * cosq + qb[:, sl] * sinq) * mla_scale).astype(BF16)
    kvn = _rms(ckv_ref[0], kvg_ref[...], EPS).astype(BF16)
    kr = kr_ref[0]
    cos_k = pltpu.roll(cosq, LANES - C_NOPE, 1)
    sin_k = pltpu.roll(sinq, LANES - C_NOPE, 1)
    partner = pltpu.roll(kr, LANES - C_ROPE, 1)
    lane = lax.broadcasted_iota(jnp.int32, (1, LANES), 1)
    kr_rope = jnp.where(lane < C_ROPE, kr * cos_k + partner * sin_k, 0.0)
    mk_out[0] = (_dot(kvn, wk_ref[...]) + _dot(kr_rope.astype(BF16), place_ref[...])).astype(BF16)
    mv_out[0] = _dot(kvn, wv_ref[...]).astype(BF16)
    dq = dq_ref[0] * (D_HEAD ** -0.5 * LOG2E)
    lane_w = lax.broadcasted_iota(jnp.int32, (1, dq.shape[1]), 1)
    first_map = (lane_w % LANES) < D_HEAD
    q0_out[0] = jnp.where(first_map, dq, 0.0).astype(BF16)
    q1_out[0] = jnp.where(first_map, 0.0, dq).astype(BF16)
    dk_out[0] = dk_ref[0].astype(BF16)
    dv_out[0] = dv_ref[0].astype(BF16)


ODD_DQ = 0
ODD_DK = ODD_DQ + D_HEADS * 2 * D_HEAD
ODD_DV = ODD_DK + D_HEADS * 2 * D_HEAD
ODD_CQ = ODD_DV + D_HEADS * D_V
ODD_CKV = ODD_CQ + C_Q_RANK
ODD_KR = ODD_CKV + C_KV_RANK
ODD_COLS = 4096


def _odd_prep(p, cos_t, sin_t, wts):
    b, s, _ = p.shape
    tm = min(256, s)
    hw = C_HEADS * LANES
    full = lambda shape: pl.BlockSpec(shape, lambda bi, i: (0,) * len(shape))
    col = lambda off, n: pl.BlockSpec((1, tm, n), lambda bi, i: (bi, i, off // n))
    tok = pl.BlockSpec((1, tm, hw), lambda bi, i: (bi, i, 0))
    tab = pl.BlockSpec((tm, LANES), lambda bi, i: (i, 0))
    outs = [jax.ShapeDtypeStruct((b, s, hw), BF16)] * 7
    return pl.pallas_call(
        _odd_prep_kernel,
        grid=(b, s // tm),
        in_specs=[
            col(ODD_CQ, C_Q_RANK), col(ODD_CKV, C_KV_RANK), col(ODD_KR, LANES),
            col(ODD_DQ, hw), col(ODD_DK, hw), col(ODD_DV, hw), tab, tab,
            full((1, C_Q_RANK)), full((1, C_KV_RANK)),
            full((C_Q_RANK, hw)), full((C_Q_RANK, hw)), full((C_KV_RANK, hw)), full((C_KV_RANK, hw)),
            full((LANES, hw)),
        ],
        out_specs=[tok] * 7,
        out_shape=outs,
        compiler_params=_params("parallel", "parallel"),
        name="odd_prep",
    )(p, p, p, p, p, p, cos_t, sin_t, wts["q_norm_g"], wts["kv_norm_g"], wts["wqa"], wts["wqb"], wts["wk"],
      wts["wv"], wts["place"])


def _with_ones(v):
    return jnp.concatenate([v, jnp.ones_like(v)], axis=1)


def _flash_chain(q, k, v1, bias, shift, m_scr, acc_scr):
    s = _dot(q, k, NT)
    yield
    if bias is not None:
        s = s + bias
    m_prev = m_scr[...]
    m_new = jnp.maximum(m_prev, jnp.max(s, axis=-1, keepdims=True) + shift)
    alpha = jnp.exp2(m_prev - m_new)
    p = jnp.exp2(s - jnp.tile(m_new - shift, (1, s.shape[1] // LANES))).astype(BF16)
    yield
    acc_scr[...] = jnp.tile(alpha, (1, 2)) * acc_scr[...] + _dot(p, v1)
    m_scr[...] = m_new


MLA_HEADS_PER_STEP = 4


def _mla_flash_kernel(q_ref, k_ref, v_ref, o_ref, m_scr, acc_scr):
    kj = pl.program_id(3)

    @pl.when(kj == 0)
    def _():
        m_scr[...] = jnp.full_like(m_scr, -jnp.inf)
        acc_scr[...] = jnp.zeros_like(acc_scr)

    heads = [slice(h * LANES, (h + 1) * LANES) for h in range(MLA_HEADS_PER_STEP)]
    _run_lockstep([_flash_chain(q_ref[0, :, sl], k_ref[0, :, sl], _with_ones(v_ref[0, :, sl]), None, 0.0,
                                m_scr.at[h], acc_scr.at[h]) for h, sl in enumerate(heads)])

    @pl.when(kj == pl.num_programs(3) - 1)
    def _():
        for h, sl in enumerate(heads):
            acc = acc_scr[h]
            o_ref[0, :, sl] = (acc[:, :LANES] / acc[:, LANES:]).astype(o_ref.dtype)


def _mla_flash(q, k, v):
    b, s, hw = q.shape
    gw = MLA_HEADS_PER_STEP * LANES
    tq = min(512, s)
    tk = min(1024, s)
    qspec = pl.BlockSpec((1, tq, gw), lambda bi, h, i, j: (bi, i, h))
    kspec = pl.BlockSpec((1, tk, gw), lambda bi, h, i, j: (bi, j, h))
    return pl.pallas_call(
        _mla_flash_kernel,
        grid=(b, hw // gw, s // tq, s // tk),
        in_specs=[qspec, kspec, kspec],
        out_specs=qspec,
        out_shape=jax.ShapeDtypeStruct((b, s, hw), BF16),
        scratch_shapes=[pltpu.VMEM((MLA_HEADS_PER_STEP, tq, LANES), F32),
                        pltpu.VMEM((MLA_HEADS_PER_STEP, tq, 2 * LANES), F32)],
        compiler_params=_params("parallel", "parallel", "parallel", "arbitrary"),
        name="mla_flash",
    )(q, k, v)


BIAS_HALF = 256


DIFF_HEADS_PER_STEP = 2


def _diff_flash_kernel(q0_ref, q1_ref, k_ref, v_ref, tab_ref, lq1_ref, lk1_ref, lq2_ref, lk2_ref, sg_ref, o_ref,
                       m_scr, acc_scr, bias_scr, *, tq, tk, near, lambda_init):
    kj = pl.program_id(3)
    off = kj * tk - pl.program_id(2) * tq
    heads = [slice(h * LANES, (h + 1) * LANES) for h in range(DIFF_HEADS_PER_STEP)]

    @pl.when(kj == 0)
    def _():
        m_scr[...] = jnp.full_like(m_scr, -jnp.inf)
        acc_scr[...] = jnp.zeros_like(acc_scr)

    tabs = [tab_ref[h] for h in range(DIFF_HEADS_PER_STEP)]
    far_left = [tab[:, 0:1] for tab in tabs]
    far_right = [tab[:, 2 * BIAS_HALF - 1:2 * BIAS_HALF] for tab in tabs]

    def update(biases, shifts):
        chains = []
        for h, sl in enumerate(heads):
            k = k_ref[0, :, sl]
            v1 = _with_ones(v_ref[0, :, sl])
            for m, q_ref in enumerate((q0_ref, q1_ref)):
                chains.append(_flash_chain(q_ref[0, :, sl], k, v1, biases[h], shifts[h], m_scr.at[h, m],
                                           acc_scr.at[h, m]))
        _run_lockstep(chains)

    def toeplitz(tab, d):
        r = tab[:, d + LANES:d + 3 * LANES]
        rows = jnp.broadcast_to(r, (LANES, 2 * LANES))
        return pltpu.roll(rows, LANES, 1, stride=1, stride_axis=0)[:, :LANES]

    for d0 in near:
        @pl.when(off == d0)
        def _(d0=d0):
            for h in range(DIFF_HEADS_PER_STEP):
                blocks = {}
                for ri in range(tq // LANES):
                    for cj in range(tk // LANES):
                        d = d0 + (cj - ri) * LANES
                        sl = (h, slice(ri * LANES, (ri + 1) * LANES), slice(cj * LANES, (cj + 1) * LANES))
                        if d <= -BIAS_HALF:
                            bias_scr[sl] = jnp.broadcast_to(far_left[h], (LANES, LANES))
                        elif d >= BIAS_HALF:
                            bias_scr[sl] = jnp.broadcast_to(far_right[h], (LANES, LANES))
                        else:
                            if d not in blocks:
                                blocks[d] = toeplitz(tabs[h], d)
                            bias_scr[sl] = blocks[d]
            update([bias_scr[h] for h in range(DIFF_HEADS_PER_STEP)], [0.0] * DIFF_HEADS_PER_STEP)

    @pl.when(off < near[0])
    def _():
        update([None] * DIFF_HEADS_PER_STEP, far_left)

    @pl.when(off > near[-1])
    def _():
        update([None] * DIFF_HEADS_PER_STEP, far_right)

    @pl.when(kj == pl.num_programs(3) - 1)
    def _():
        lam = (jnp.exp(jnp.sum(lq1_ref[...] * lk1_ref[...], axis=-1, keepdims=True))
               - jnp.exp(jnp.sum(lq2_ref[...] * lk2_ref[...], axis=-1, keepdims=True)) + lambda_init)
        for h, sl in enumerate(heads):
            a0 = acc_scr[h, 0]
            a1 = acc_scr[h, 1]
            o = a0[:, :LANES] / a0[:, LANES:] - lam * (a1[:, :LANES] / a1[:, LANES:])
            o = _rms(o, sg_ref[...], SUBLN_EPS) * (1.0 - lambda_init)
            o_ref[0, :, sl] = o.astype(o_ref.dtype)


def _diff_flash(q0, q1, k, v, tab, lq1, lk1, lq2, lk2, subln_g, lambda_init):
    b, s, hw = q0.shape
    hps = DIFF_HEADS_PER_STEP
    gw = hps * LANES
    tq = min(512, s)
    tk = min(1024, s)
    offs = sorted({j * tk - i * tq for i in range(s // tq) for j in range(s // tk)})
    near = tuple(d for d in offs if d - (tq - 1) < BIAS_HALF and d + tk - 1 > -BIAS_HALF)
    qspec = pl.BlockSpec((1, tq, gw), lambda bi, h, i, j: (bi, i, h))
    kspec = pl.BlockSpec((1, tk, gw), lambda bi, h, i, j: (bi, j, h))
    vec = lambda n: pl.BlockSpec((1, n), lambda bi, h, i, j: (0, 0))
    return pl.pallas_call(
        functools.partial(_diff_flash_kernel, tq=tq, tk=tk, near=near, lambda_init=lambda_init),
        grid=(b, hw // gw, s // tq, s // tk),
        in_specs=[qspec, qspec, kspec, kspec,
                  pl.BlockSpec((hps, 1, 2 * BIAS_HALF), lambda bi, h, i, j: (h, 0, 0)),
                  vec(D_HEAD), vec(D_HEAD), vec(D_HEAD), vec(D_HEAD), vec(D_V)],
        out_specs=qspec,
        out_shape=jax.ShapeDtypeStruct((b, s, hw), BF16),
        scratch_shapes=[pltpu.VMEM((hps, 2, tq, LANES), F32), pltpu.VMEM((hps, 2, tq, 2 * LANES), F32),
                        pltpu.VMEM((hps, tq, tk), F32)],
        compiler_params=_params("parallel", "parallel", "parallel", "arbitrary"),
        name="diff_flash",
    )(q0, q1, k, v, tab, lq1, lk1, lq2, lk2, subln_g)


def _t5_bucket(rel):
    half = N_BUCKETS // 2
    max_exact = half // 2
    n = jnp.abs(rel)
    large = max_exact + (jnp.log(jnp.maximum(n, 1).astype(jnp.float32) / max_exact)
                         / math.log(MAX_DISTANCE / max_exact) * (half - max_exact)).astype(jnp.int32)
    large = jnp.minimum(large, half - 1)
    return jnp.where(rel > 0, half, 0) + jnp.where(n < max_exact, n, large)


def _rope_tables(s):
    inv = 1.0 / (ROPE_THETA ** (jnp.arange(0, C_ROPE, 2, dtype=F32) / C_ROPE))
    ang = jnp.arange(s, dtype=F32)[:, None] * inv[None, :]
    cos, sin = jnp.cos(ang), jnp.sin(ang)
    pad = LANES - C_NOPE - C_ROPE
    cos_t = jnp.concatenate([jnp.ones((s, C_NOPE), F32), cos, cos, jnp.zeros((s, pad), F32)], axis=-1)
    sin_t = jnp.concatenate([jnp.zeros((s, C_NOPE), F32), sin, sin, jnp.zeros((s, pad), F32)], axis=-1)
    return cos_t, sin_t


def _rot_half_cols(w):
    h = w.shape[-1] // 2
    return jnp.concatenate([-w[..., h:], w[..., :h]], axis=-1)


def _pack_even(j, even_w_in, even_w_out, rwkv_mu, rwkv_w0, rwkv_w_up, rwkv_a0, rwkv_a_up, rwkv_g_up, rwkv_k_k,
               rwkv_k_a, rwkv_r_k, rwkv_lnx_g, rwkv_lnx_b):
    w = A_WIDTH
    z = jnp.zeros((A_LORA, w), F32)
    blockdiag = lambda m: jnp.concatenate(
        [jnp.concatenate([m[0], z], axis=1), jnp.concatenate([z, m[1]], axis=1)], axis=0)
    head = jnp.arange(w) // A_HEAD
    return {
        "w_in_a": even_w_in[j][:, :A_COLS].astype(BF16),
        "w_in_b": even_w_in[j][:, A_COLS:].astype(BF16),
        "w_out_a": even_w_out[j][:w].astype(BF16),
        "w_out_b": even_w_out[j][w:].astype(BF16),
        "mu": rwkv_mu[j].reshape(1, A_COLS),
        "w0": rwkv_w0[j].reshape(1, 2 * w),
        "wup": blockdiag(rwkv_w_up[j]).astype(BF16),
        "a0": rwkv_a0[j].reshape(1, 2 * w),
        "aup": blockdiag(rwkv_a_up[j]).astype(BF16),
        "gup": rwkv_g_up[j].astype(BF16),
        "k_k": rwkv_k_k[j].reshape(1, w),
        "k_a": rwkv_k_a[j].reshape(1, w),
        "r_k": rwkv_r_k[j].reshape(1, w),
        "lnx_g": rwkv_lnx_g[j].reshape(1, w),
        "lnx_b": rwkv_lnx_b[j].reshape(1, w),
        "ones_bd": (head[:, None] == head[None, :]).astype(BF16),
    }


def _pack_odd(j, odd_w_in, odd_w_out, mla_q_norm_g, mla_kv_norm_g, mla_w_uq, mla_w_ukv):
    d = D_MODEL
    w_in = odd_w_in[j]
    o_cq, o_ckv = 0, C_Q_RANK
    o_kr = o_ckv + C_KV_RANK
    o_dq = o_kr + C_ROPE
    n_d = D_HEADS * 2 * D_HEAD
    w_kr = w_in[:, o_kr:o_kr + C_ROPE]
    packed = jnp.concatenate([
        w_in[:, o_dq:o_dq + 3 * n_d], w_in[:, o_cq:o_kr], w_kr, _rot_half_cols(w_kr),
        jnp.zeros((d, ODD_COLS - (C_Q_RANK + C_KV_RANK + 3 * n_d + 2 * C_ROPE)), F32)], axis=1)
    pad = LANES - C_NOPE - C_ROPE
    wq = mla_w_uq[j].reshape(C_Q_RANK, C_HEADS, C_NOPE + C_ROPE)
    zq = jnp.zeros((C_Q_RANK, C_HEADS, pad), F32)
    wqa = jnp.concatenate([wq, zq], axis=-1).reshape(C_Q_RANK, C_HEADS * LANES)
    wqb = jnp.concatenate([jnp.zeros((C_Q_RANK, C_HEADS, C_NOPE), F32), _rot_half_cols(wq[..., C_NOPE:]), zq],
                          axis=-1).reshape(C_Q_RANK, C_HEADS * LANES)
    wkv = mla_w_ukv[j].reshape(C_KV_RANK, C_HEADS, C_NOPE + C_V)
    wk = jnp.concatenate([wkv[..., :C_NOPE], jnp.zeros((C_KV_RANK, C_HEADS, LANES - C_NOPE), F32)],
                         axis=-1).reshape(C_KV_RANK, C_HEADS * LANES)
    wv = wkv[..., C_NOPE:].reshape(C_KV_RANK, C_HEADS * C_V)
    src = jnp.arange(LANES)[:, None]
    dst = jnp.arange(C_HEADS * LANES)[None, :] % LANES
    place = ((dst >= C_NOPE) & (dst < C_NOPE + C_ROPE) & (dst - C_NOPE == src)).astype(BF16)
    hv = C_HEADS * C_V
    return {
        "w_in": packed.astype(BF16),
        "w_out_a": odd_w_out[j][:hv].astype(BF16),
        "w_out_b": odd_w_out[j][hv:].astype(BF16),
        "q_norm_g": mla_q_norm_g[j].reshape(1, C_Q_RANK),
        "kv_norm_g": mla_kv_norm_g[j].reshape(1, C_KV_RANK),
        "wqa": wqa.astype(BF16), "wqb": wqb.astype(BF16), "wk": wk.astype(BF16), "wv": wv.astype(BF16),
        "place": place,
    }


def _even_mixers(x, g1n, sc1, sh1, ew, s5m, s5_d, s5_w_glu, s5_b_glu):
    b, s, _ = x.shape
    pa = _normmod_mm(x, g1n, sc1, sh1, ew["w_in_a"], tn=A_COLS // 3)
    u = _normmod_mm(x, g1n, sc1, sh1, ew["w_in_b"], tn=B_WIDTH)
    r, v, kk, lw, kd, bd, bv, g = _rwkv_prep(pa, ew)
    yf, yr = _rwkv_scan(r, v, kk, lw, kd, bd)
    ya = _rwkv_post(yf, yr, bv, g, ew["lnx_g"], ew["lnx_b"], ew["ones_bd"])
    nrows = s // S5_CHUNK
    u_g = jnp.transpose(u.astype(BF16).reshape(b, nrows, S5_CHUNK, B_GROUPS, B_GROUP), (0, 3, 1, 2, 4))
    u_g = u_g.reshape(b, B_GROUPS, nrows, S5_CHUNK * B_GROUP)
    ys_g = _s5_scan(u_g, s5m[0], False) + _s5_scan(u_g, s5m[1], True)
    ys = jnp.transpose(ys_g.reshape(b, B_GROUPS, nrows, S5_CHUNK, B_GROUP), (0, 2, 3, 1, 4)).reshape(b, s, B_WIDTH)
    yb = _s5_glu(ys, u, s5_d, s5_w_glu, s5_b_glu)
    return ya, yb


def _odd_mixers(x, g1n, sc1, sh1, ow, tabs, diff_w, lambda_init):
    s = x.shape[1]
    p = _normmod_mm(x, g1n, sc1, sh1, ow["w_in"], tn=1024)
    mq, mk, mv, q0, q1, dk, dv = _odd_prep(p, tabs["cos"][:s], tabs["sin"][:s], ow)
    yc = _mla_flash(mq, mk, mv)
    yd = _diff_flash(q0, q1, dk, dv, tabs["bias"], diff_w["lq1"], diff_w["lk1"], diff_w["lq2"], diff_w["lk2"],
                     diff_w["subln_g"], lambda_init)
    return yc, yd


def kernel(x_prompt, x_sample, c_prompt, c_sample, ada_w, ada_b, norm1_g, norm2_g, even_w_in, even_w_out, rwkv_mu, rwkv_w0, rwkv_w_up, rwkv_a0, rwkv_a_up, rwkv_g_up, rwkv_k_k, rwkv_k_a, rwkv_r_k, rwkv_lnx_g, rwkv_lnx_b, s5_lam_re, s5_lam_im, s5_log_step, s5_b_re, s5_b_im, s5_c_re, s5_c_im, s5_d, s5_w_glu, s5_b_glu, odd_w_in, odd_w_out, mla_q_norm_g, mla_kv_norm_g, mla_w_uq, mla_w_ukv, diff_lq1, diff_lk1, diff_lq2, diff_lk2, diff_subln_g, rel_bias, ffn_w_up, ffn_conv_w, ffn_conv_b, ffn_w_down, final_g):
    d = D_MODEL
    groups = [(x_prompt, c_prompt), (x_sample, c_sample)]
    nb = [g[0].shape[0] for g in groups]
    c_all = jnp.concatenate([g[1] for g in groups] + [jnp.zeros((SUBLANES - sum(nb), d), F32)], axis=0)
    mod = _ada_mod(c_all, ada_w, ada_b)

    max_s = max(g[0].shape[1] for g in groups)
    cos_t, sin_t = _rope_tables(max_s)
    rel = jnp.arange(-BIAS_HALF, BIAS_HALF, dtype=jnp.int32)
    bias_tab = (jnp.transpose(rel_bias.astype(F32)[_t5_bucket(rel)]) * LOG2E).reshape(D_HEADS, 1, 2 * BIAS_HALF)
    tabs = {"cos": cos_t, "sin": sin_t, "bias": bias_tab}

    xs = [g[0] for g in groups]
    for i in range(DEPTH):
        j = i // 2
        if i % 2 == 0:
            ew = _pack_even(j, even_w_in, even_w_out, rwkv_mu, rwkv_w0, rwkv_w_up, rwkv_a0, rwkv_a_up, rwkv_g_up,
                            rwkv_k_k, rwkv_k_a, rwkv_r_k, rwkv_lnx_g, rwkv_lnx_b)
            s5m = [_s5_matrices(s5_lam_re[j, dr], s5_lam_im[j, dr], s5_log_step[j, dr], s5_b_re[j, dr],
                                s5_b_im[j, dr], s5_c_re[j, dr], s5_c_im[j, dr], dr == 1) for dr in range(2)]
        else:
            ow = _pack_odd(j, odd_w_in, odd_w_out, mla_q_norm_g, mla_kv_norm_g, mla_w_uq, mla_w_ukv)
            diff_w = {"lq1": diff_lq1[j].reshape(1, D_HEAD), "lk1": diff_lk1[j].reshape(1, D_HEAD),
                      "lq2": diff_lq2[j].reshape(1, D_HEAD), "lk2": diff_lk2[j].reshape(1, D_HEAD),
                      "subln_g": diff_subln_g[j].reshape(1, D_V)}
        w_up = ffn_w_up[i].astype(BF16)
        w_down = ffn_w_down[i].astype(BF16)
        row0 = 0
        for gi in range(len(groups)):
            x = xs[gi]
            m = mod[i, row0:row0 + nb[gi]]
            row0 += nb[gi]
            sh1, sc1, g1, sh2, sc2, g2 = [m[:, None, k * d:(k + 1) * d] for k in range(N_MOD)]
            if i % 2 == 0:
                ya, yb = _even_mixers(x, norm1_g[i], sc1, sh1, ew, s5m, s5_d[j].reshape(1, B_WIDTH),
                                      s5_w_glu[j].astype(BF16), s5_b_glu[j].reshape(1, B_WIDTH))
                x = _out_proj(ya, yb, ew["w_out_a"], ew["w_out_b"], x, g1)
            else:
                yc, yd = _odd_mixers(x, norm1_g[i], sc1, sh1, ow, tabs, diff_w, 0.8 - 0.6 * math.exp(-0.3 * i))
                x = _out_proj(yc, yd, ow["w_out_a"], ow["w_out_b"], x, g1)
            x = _ffn(x, norm2_g[i], sc2, sh2, w_up, ffn_conv_w[i], ffn_conv_b[i].reshape(1, 2 * FFN_HIDDEN), w_down,
                     g2, final_g.reshape(1, d), final=(i == DEPTH - 1))
            xs[gi] = x
    return (xs[0], xs[1])
```

```python
import functools
import math

import jax
import jax.numpy as jnp
from jax import lax
from jax.experimental import pallas as pl
from jax.experimental.pallas import tpu as pltpu

F32 = jnp.float32
BF16 = jnp.bfloat16

D_MODEL = 2048
DEPTH = 2
EPS = 1e-6
A_WIDTH = 1024
A_HEAD = 64
A_HEADS = 16
A_LORA = 64
A_GATE_LORA = 128
A_COLS = 3 * A_WIDTH + 2 * A_LORA + 2 * A_LORA + A_GATE_LORA
RWKV_GN_EPS = 64e-5
DECAY_SCALE = math.exp(-0.5)
B_WIDTH = 1024
B_GROUP = 16
B_GROUPS = 64
B_STATE = 64
C_HEADS = 8
C_NOPE = 64
C_ROPE = 32
C_V = 128
C_Q_RANK = 512
C_KV_RANK = 256
ROPE_THETA = 10000.0
D_HEADS = 8
D_HEAD = 64
D_V = 128
SUBLN_EPS = 1e-5
N_BUCKETS = 32
MAX_DISTANCE = 128
FFN_HIDDEN = 5632
N_MOD = 6

LANES = 128
SUBLANES = 8
VMEM_LIMIT_BYTES = 56 * 1024 * 1024

MM_TOKENS = 1024
FFN_TOKENS = 1024
ATT_TQ = 1024
ATT_TK = 1024

RWKV_CHUNK = 64
S5_CHUNK = 16
LOG2E = 1.4426950408889634

NN = (((1,), (0,)), ((), ()))
NT = (((1,), (1,)), ((), ()))
BATCH_NN = (((2,), (1,)), ((0,), (0,)))


def _params(*sem):
    return pltpu.CompilerParams(dimension_semantics=sem, vmem_limit_bytes=VMEM_LIMIT_BYTES)


def _dot(a, b, dims=NN):
    return lax.dot_general(a, b, dims, preferred_element_type=F32)


def _split2(x):
    hi = x.astype(BF16)
    lo = (x - hi.astype(F32)).astype(BF16)
    return hi, lo


def _dot3(a, b, dims=NN):
    ah, al = _split2(a)
    bh, bl = _split2(b)
    return _dot(ah, bh, dims) + _dot(ah, bl, dims) + _dot(al, bh, dims)


def _dot_exact_rhs(a, b):
    ah, al = _split2(a)
    return _dot(ah, b) + _dot(al, b)


def _ada_kernel(c_ref, w_ref, b_ref, o_ref):
    c = c_ref[...]
    cs = c * jax.nn.sigmoid(c)
    o_ref[0] = _dot(cs.astype(BF16), w_ref[0].astype(BF16)) + b_ref[0]


def _ada_mod(c_all, ada_w, ada_b):
    n = N_MOD * D_MODEL
    tn = 1024
    return pl.pallas_call(
        _ada_kernel,
        grid=(DEPTH, n // tn),
        in_specs=[
            pl.BlockSpec((SUBLANES, D_MODEL), lambda l, j: (0, 0)),
            pl.BlockSpec((1, D_MODEL, tn), lambda l, j: (l, 0, j)),
            pl.BlockSpec((1, 1, tn), lambda l, j: (l, 0, j)),
        ],
        out_specs=pl.BlockSpec((1, SUBLANES, tn), lambda l, j: (l, 0, j)),
        out_shape=jax.ShapeDtypeStruct((DEPTH, SUBLANES, n), F32),
        compiler_params=_params("parallel", "parallel"),
        name="ada_mod",
    )(c_all, ada_w, ada_b.reshape(DEPTH, 1, n))


def _modnorm(x, g, sc, sh):
    y = x * lax.rsqrt(jnp.mean(x * x, axis=-1, keepdims=True) + EPS)
    return (y * g) * (1.0 + sc) + sh


def _normmod_mm_kernel(x_ref, g_ref, sc_ref, sh_ref, w_ref, o_ref, h_scr):
    @pl.when(pl.program_id(2) == 0)
    def _():
        h_scr[...] = _modnorm(x_ref[0], g_ref[...], sc_ref[0], sh_ref[0]).astype(BF16)

    o_ref[0] = _dot(h_scr[...], w_ref[...]).astype(o_ref.dtype)


def _normmod_mm(x, g, sc, sh, w, tn, out_dtype=F32):
    b, s, d = x.shape
    n = w.shape[1]
    tm = min(MM_TOKENS, s)
    return pl.pallas_call(
        _normmod_mm_kernel,
        grid=(b, s // tm, n // tn),
        in_specs=[
            pl.BlockSpec((1, tm, d), lambda bi, i, j: (bi, i, 0)),
            pl.BlockSpec((1, d), lambda bi, i, j: (0, 0)),
            pl.BlockSpec((1, 1, d), lambda bi, i, j: (bi, 0, 0)),
            pl.BlockSpec((1, 1, d), lambda bi, i, j: (bi, 0, 0)),
            pl.BlockSpec((d, tn), lambda bi, i, j: (0, j)),
        ],
        out_specs=pl.BlockSpec((1, tm, tn), lambda bi, i, j: (bi, i, j)),
        out_shape=jax.ShapeDtypeStruct((b, s, n), out_dtype),
        scratch_shapes=[pltpu.VMEM((tm, d), BF16)],
        compiler_params=_params("parallel", "parallel", "arbitrary"),
        name="normmod_mm",
    )(x, g.reshape(1, d), sc, sh, w)


def _rwkv_prep_kernel(p_ref, pp_ref, pn_ref, mu_ref, w0_ref, wup_ref, a0_ref, aup_ref, gup_ref, kk_ref, ka_ref,
                      rk_ref, ones_ref, r_out, v_out, kkn_out, lw_out, kd_out, bd_out, bv_out, g_out):
    i = pl.program_id(1)
    last = pl.num_programs(1) - 1
    pa = p_ref[0]
    tm = pa.shape[0]
    row = lax.broadcasted_iota(jnp.int32, (tm, 1), 0)
    prev_row = jnp.where(i == 0, 0.0, pp_ref[0][SUBLANES - 1:SUBLANES, :])
    next_row = jnp.where(i == last, 0.0, pn_ref[0][0:1, :])
    p_prev = jnp.where(row == 0, prev_row, pltpu.roll(pa, 1, 0))
    p_next = jnp.where(row == tm - 1, next_row, pltpu.roll(pa, tm - 1, 0))
    pa = pa + mu_ref[...] * (0.5 * (p_prev + p_next) - pa)

    w = A_WIDTH
    r = pa[:, 0:w]
    k = pa[:, w:2 * w]
    v = pa[:, 2 * w:3 * w]
    dw = pa[:, 3 * w:3 * w + 2 * A_LORA]
    da = pa[:, 3 * w + 2 * A_LORA:3 * w + 4 * A_LORA]
    dg = pa[:, 3 * w + 4 * A_LORA:A_COLS]

    lw = -DECAY_SCALE * jax.nn.sigmoid(w0_ref[...] + _dot(jnp.tanh(dw).astype(BF16), wup_ref[...]))
    icl = jax.nn.sigmoid(a0_ref[...] + _dot(da.astype(BF16), aup_ref[...]))
    g = _dot(jax.nn.sigmoid(dg).astype(BF16), gup_ref[...])

    ones_bd = ones_ref[...]
    kkr = k * kk_ref[...]
    ss = _dot_exact_rhs(kkr * kkr, ones_bd)
    kkn = kkr / jnp.maximum(jnp.sqrt(ss), 1e-12)

    r_out[0] = r
    v_out[0] = v
    kkn_out[0] = kkn
    lw_out[0] = lw
    g_out[0] = g
    bonus = jnp.zeros_like(r)
    for d in range(2):
        icl_d = icl[:, d * w:(d + 1) * w]
        k_d = k * (1.0 + (icl_d - 1.0) * ka_ref[...])
        kd_out[0, :, d * w:(d + 1) * w] = k_d
        bd_out[0, :, d * w:(d + 1) * w] = icl_d * kkn
        bonus = bonus + _dot_exact_rhs(r * k_d * rk_ref[...], ones_bd)
    bv_out[0] = bonus * v


def _rwkv_prep(pa, wts):
    b, s, _ = pa.shape
    tm = min(256, s)
    nh = tm // SUBLANES
    w = A_WIDTH
    full = lambda shape: pl.BlockSpec(shape, lambda bi, i: (0,) * len(shape))
    tok = lambda n: pl.BlockSpec((1, tm, n), lambda bi, i: (bi, i, 0))
    out_shapes = [jax.ShapeDtypeStruct((b, s, n), F32) for n in (w, w, w, 2 * w, 2 * w, 2 * w, w, w)]
    return pl.pallas_call(
        _rwkv_prep_kernel,
        grid=(b, s // tm),
        in_specs=[
            tok(A_COLS),
            pl.BlockSpec((1, SUBLANES, A_COLS), lambda bi, i: (bi, jnp.maximum(i * nh - 1, 0), 0)),
            pl.BlockSpec((1, SUBLANES, A_COLS), lambda bi, i: (bi, jnp.minimum((i + 1) * nh, s // SUBLANES - 1), 0)),
            full((1, A_COLS)), full((1, 2 * w)), full((2 * A_LORA, 2 * w)), full((1, 2 * w)),
            full((2 * A_LORA, 2 * w)), full((A_GATE_LORA, w)), full((1, w)), full((1, w)), full((1, w)),
            full((w, w)),
        ],
        out_specs=[tok(sd.shape[-1]) for sd in out_shapes],
        out_shape=out_shapes,
        compiler_params=_params("parallel", "parallel"),
        name="rwkv_prep",
    )(pa, pa, pa, wts["mu"], wts["w0"], wts["wup"], wts["a0"], wts["aup"], wts["gup"], wts["k_k"], wts["k_a"],
      wts["r_k"], wts["ones_bd"])


def _rwkv_chunk(r, v, kk, lw, kd, bd, s2, reverse):
    t = RWKV_CHUNK
    sign = -1 if reverse else 1
    ri = lax.broadcasted_iota(jnp.int32, (t, t), 0)
    ci = lax.broadcasted_iota(jnp.int32, (t, t), 1)
    tri = jnp.where((ri - ci) * sign >= 0, 1.0, 0.0).astype(BF16)
    lw_hi = lw.astype(BF16)
    rem = lw - lw_hi.astype(F32)
    lw_mid = rem.astype(BF16)
    lw_lo = (rem - lw_mid.astype(F32)).astype(BF16)
    cum = _dot(tri, lw_hi) + _dot(tri, lw_mid) + _dot(tri, lw_lo)
    yield
    tot = jnp.sum(lw, axis=0, keepdims=True)
    gam = jnp.exp(cum)
    gam_ex = jnp.exp(cum - lw)
    ginv = jnp.exp(-cum)
    gend = jnp.exp(tot - cum)
    gtot = jnp.exp(tot)

    width = r.shape[1]
    heads = width // A_HEAD
    lane_head = lax.broadcasted_iota(jnp.int32, (1, width), 1) // A_HEAD

    def stack(x):
        return jnp.concatenate([jnp.where(lane_head == h, x, 0.0) for h in range(heads)], axis=0).astype(BF16)

    q_rk = jnp.concatenate([stack(r * gam), stack(kk * gam_ex)], axis=0)
    k_kb = jnp.concatenate([stack(kd * ginv), stack(bd * ginv)], axis=0)
    k_kb_end = jnp.concatenate([stack(kd * gend), stack(bd * gend)], axis=0)
    vs = stack(v)

    n = heads * t
    a = _dot(q_rk, k_kb, NT)
    yield
    ri2 = lax.broadcasted_iota(jnp.int32, (n, n), 0)
    ci2 = lax.broadcasted_iota(jnp.int32, (n, n), 1)
    order = (ri2 - ci2) * sign
    incl = order >= 0
    strict = order > 0
    a_rk = jnp.where(incl, a[:n, :n], 0.0)
    a_rb = jnp.where(incl, a[:n, n:], 0.0)
    a_kk = jnp.where(strict, a[n:, :n], 0.0)
    a_bk = jnp.where(strict, a[n:, n:], 0.0)

    qs = _dot(q_rk, s2.astype(BF16), NT)
    yield
    akv = _dot(jnp.concatenate([a_rk, a_kk], axis=0).astype(BF16), vs)
    yield

    nb = a_bk.astype(BF16)
    n2 = _dot(nb, nb)
    yield
    n2b = n2.astype(BF16)
    n4 = _dot(n2b, n2b)
    yield
    n3 = _dot(nb, n2b)
    yield
    n4b = n4.astype(BF16)
    n8 = _dot(n4b, n4b)
    yield
    eye = jnp.where(ri2 == ci2, 1.0, 0.0)
    f1 = eye - a_bk + n2 - n3
    n8b = n8.astype(BF16)
    n16 = _dot(n8b, n8b)
    yield
    n12 = _dot(n4b, n8b)
    yield
    f2 = n4 + n8 + n12
    f12 = f1 + _dot(f1.astype(BF16), f2.astype(BF16))
    yield
    n16b = n16.astype(BF16)
    n32 = _dot(n16b, n16b)
    yield
    n48 = _dot(n16b, n32.astype(BF16))
    yield
    f3 = n16 + n32 + n48
    inv = f12 + _dot(f12.astype(BF16), f3.astype(BF16))
    yield

    ps = _dot(inv.astype(BF16), (-qs[n:] - akv[n:]).astype(BF16)).astype(BF16)
    yield
    ys = qs[:n] + akv[:n] + _dot(a_rb.astype(BF16), ps)
    yield
    y = ys[0:t]
    for h in range(1, heads):
        y = y + ys[h * t:(h + 1) * t]
    s2_new = s2 * gtot + _dot(jnp.concatenate([vs, ps], axis=0).astype(F32).T.astype(BF16), k_kb_end)
    return y, s2_new


def _run_lockstep(gens):
    results = [None] * len(gens)
    active = list(range(len(gens)))
    while active:
        for i in list(active):
            try:
                next(gens[i])
            except StopIteration as e:
                results[i] = e.value
                active.remove(i)
    return results


RWKV_CHAIN_LANES = 256
RWKV_CHAINS_PER_STEP = 4


def _rwkv_scan_kernel(rf_ref, vf_ref, kkf_ref, lwf_ref, kdf_ref, bdf_ref,
                      rb_ref, vb_ref, kkb_ref, lwb_ref, kdb_ref, bdb_ref, yf_ref, yb_ref, s_scr):
    @pl.when(pl.program_id(2) == 0)
    def _():
        s_scr[...] = jnp.zeros_like(s_scr)

    dirs = ((rf_ref, vf_ref, kkf_ref, lwf_ref, kdf_ref, bdf_ref, yf_ref),
            (rb_ref, vb_ref, kkb_ref, lwb_ref, kdb_ref, bdb_ref, yb_ref))
    chains = []
    for d, (r_ref, v_ref, kk_ref, lw_ref, kd_ref, bd_ref, y_ref) in enumerate(dirs):
        for p in range(RWKV_CHAINS_PER_STEP):
            sl = slice(p * RWKV_CHAIN_LANES, (p + 1) * RWKV_CHAIN_LANES)
            chains.append((d, p, sl, y_ref, _rwkv_chunk(
                r_ref[0, :, sl], v_ref[0, :, sl], kk_ref[0, :, sl], lw_ref[0, :, sl], kd_ref[0, :, sl],
                bd_ref[0, :, sl], s_scr[d, p], reverse=(d == 1))))
    results = _run_lockstep([c[4] for c in chains])
    for (d, p, sl, y_ref, _), (y, s_new) in zip(chains, results):
        y_ref[0, :, sl] = y
        s_scr[d, p] = s_new


def _rwkv_scan(r, v, kk, lw, kd, bd):
    b, s, w = r.shape
    t = RWKV_CHUNK
    nc = s // t
    pw = RWKV_CHAINS_PER_STEP * RWKV_CHAIN_LANES
    ngrp = w // pw
    fwd = lambda off: pl.BlockSpec((1, t, pw), lambda bi, g, c: (bi, c, g + off))
    bwd = lambda off: pl.BlockSpec((1, t, pw), lambda bi, g, c: (bi, nc - 1 - c, g + off))
    return pl.pallas_call(
        _rwkv_scan_kernel,
        grid=(b, ngrp, nc),
        in_specs=[fwd(0)] * 6 + [bwd(0)] * 3 + [bwd(ngrp)] * 3,
        out_specs=[fwd(0), bwd(0)],
        out_shape=[jax.ShapeDtypeStruct((b, s, w), F32)] * 2,
        scratch_shapes=[pltpu.VMEM((2, RWKV_CHAINS_PER_STEP, RWKV_CHAIN_LANES, RWKV_CHAIN_LANES), F32)],
        compiler_params=_params("parallel", "parallel", "arbitrary"),
        name="rwkv_scan",
    )(r, v, kk, lw, kd, bd, r, v, kk, lw, kd, bd)


def _rwkv_post_kernel(yf_ref, yb_ref, bv_ref, g_ref, lng_ref, lnb_ref, ones_ref, o_ref):
    y = yf_ref[0] + yb_ref[0]
    ones_bd = ones_ref[...]
    mean = _dot_exact_rhs(y, ones_bd) * (1.0 / A_HEAD)
    yc = y - mean
    var = _dot_exact_rhs(yc * yc, ones_bd) * (1.0 / A_HEAD)
    yn = yc * lax.rsqrt(var + RWKV_GN_EPS) * lng_ref[...] + lnb_ref[...]
    o_ref[0] = ((yn + bv_ref[0]) * g_ref[0]).astype(o_ref.dtype)


def _rwkv_post(yf, yb, bv, g, lng, lnb, ones_bd):
    b, s, w = bv.shape
    tm = min(512, s)
    full = lambda shape: pl.BlockSpec(shape, lambda bi, i: (0,) * len(shape))
    tok = lambda n: pl.BlockSpec((1, tm, n), lambda bi, i: (bi, i, 0))
    return pl.pallas_call(
        _rwkv_post_kernel,
        grid=(b, s // tm),
        in_specs=[tok(w), tok(w), tok(w), tok(w), full((1, w)), full((1, w)), full((w, w))],
        out_specs=tok(w),
        out_shape=jax.ShapeDtypeStruct((b, s, w), BF16),
        compiler_params=_params("parallel", "parallel"),
        name="rwkv_post",
    )(yf, yb, bv, g, lng, lnb, ones_bd)


def _s5_kernel(u_ref, m_ref, w_ref, ws_ref, v_ref, a_ref, y_ref, x_scr, xs_scr, h_scr, carry_scr, *, rows, reverse):
    g = B_GROUPS

    @pl.when(pl.program_id(1) == 0)
    def _():
        carry_scr[...] = jnp.zeros_like(carry_scr)

    u = u_ref[0]
    x_scr[...] = lax.dot_general(u, w_ref[...], BATCH_NN, preferred_element_type=F32).reshape(g * rows, LANES)
    xs_scr[...] = lax.dot_general(u, ws_ref[...], BATCH_NN, preferred_element_type=F32).reshape(g * rows, LANES)
    a1 = a_ref[0]
    a2 = a_ref[1]
    a2s = a_ref[2]

    def step(i, carry):
        h, hs = carry
        r = (rows - 1 - i) if reverse else i
        idx = pl.ds(r, g, stride=rows)
        h_scr[idx, :] = h
        hn = a1 * h + a2 * hs + x_scr[idx, :]
        hsn = a1 * hs + a2s * h + xs_scr[idx, :]
        return hn, hsn

    h, hs = lax.fori_loop(0, rows, step, (carry_scr[0], carry_scr[1]))
    carry_scr[0] = h
    carry_scr[1] = hs
    hprev = h_scr[...].reshape(g, rows, LANES).astype(BF16)
    y_ref[0] = (lax.dot_general(u, m_ref[...], BATCH_NN, preferred_element_type=F32)
                + lax.dot_general(hprev, v_ref[...], BATCH_NN, preferred_element_type=F32))


def _s5_scan(u_g, mats, reverse):
    b, g, nrows, cw = u_g.shape
    rows = min(64, nrows)
    nsb = nrows // rows
    whole = pl.BlockSpec(memory_space=pltpu.VMEM)
    idx = (lambda bi, i: (bi, 0, nsb - 1 - i, 0)) if reverse else (lambda bi, i: (bi, 0, i, 0))
    return pl.pallas_call(
        functools.partial(_s5_kernel, rows=rows, reverse=reverse),
        grid=(b, nsb),
        in_specs=[pl.BlockSpec((1, g, rows, cw), idx), whole, whole, whole, whole, whole],
        out_specs=pl.BlockSpec((1, g, rows, cw), idx),
        out_shape=jax.ShapeDtypeStruct((b, g, nrows, cw), F32),
        scratch_shapes=[pltpu.VMEM((g * rows, LANES), F32), pltpu.VMEM((g * rows, LANES), F32),
                        pltpu.VMEM((g * rows, LANES), F32), pltpu.VMEM((2, g, LANES), F32)],
        compiler_params=_params("parallel", "arbitrary"),
        name="s5_scan_bwd" if reverse else "s5_scan_fwd",
    )(u_g, mats["m"], mats["w"], mats["ws"], mats["v"], mats["a"])


def _s5_matrices(lam_re, lam_im, log_step, b_re, b_im, c_re, c_im, reverse):
    hp = lax.Precision.HIGHEST
    t = S5_CHUNK
    g, p, c = B_GROUPS, B_STATE, B_GROUP
    lr, li = lam_re.astype(F32), lam_im.astype(F32)
    step = jnp.exp(log_step.astype(F32))[:, None]
    ar, ai = jnp.exp(lr * step) * jnp.cos(li * step), jnp.exp(lr * step) * jnp.sin(li * step)
    den = lr * lr + li * li
    nr, ni = ar - 1.0, ai
    fr, fi = (nr * lr + ni * li) / den, (ni * lr - nr * li) / den
    br, bi = b_re.astype(F32), b_im.astype(F32)
    bbr = fr[..., None] * br - fi[..., None] * bi
    bbi = fr[..., None] * bi + fi[..., None] * br
    cr, cim = c_re.astype(F32), c_im.astype(F32)
    taus = jnp.arange(t + 1, dtype=F32)[:, None, None]
    mag = jnp.exp(lr * step * taus)
    pr, pi = mag * jnp.cos(li * step * taus), mag * jnp.sin(li * step * taus)
    cpr = cr[None] * pr[:, :, None, :] - cim[None] * pi[:, :, None, :]
    cpi = cr[None] * pi[:, :, None, :] + cim[None] * pr[:, :, None, :]
    kern = (jnp.einsum("tgcp,gpd->tgcd", cpr, bbr, precision=hp)
            - jnp.einsum("tgcp,gpd->tgcd", cpi, bbi, precision=hp))
    s_idx = jnp.arange(t)[:, None]
    t_idx = jnp.arange(t)[None, :]
    lag = (s_idx - t_idx) if reverse else (t_idx - s_idx)
    kk = jnp.where((lag >= 0)[:, :, None, None, None], kern[jnp.clip(lag, 0, t)], 0.0)
    m = jnp.transpose(kk, (2, 0, 4, 1, 3)).reshape(g, t * c, t * c)
    e = jnp.arange(t) if reverse else (t - 1 - jnp.arange(t))
    pre, pie = pr[e], pi[e]
    wre = pre[..., None] * bbr[None] - pie[..., None] * bbi[None]
    wim = pre[..., None] * bbi[None] + pie[..., None] * bbr[None]
    wre = jnp.transpose(wre, (1, 0, 3, 2)).reshape(g, t * c, p)
    wim = jnp.transpose(wim, (1, 0, 3, 2)).reshape(g, t * c, p)
    w = jnp.concatenate([wre, wim], axis=-1)
    ws = jnp.concatenate([wim, wre], axis=-1)
    f = (t - jnp.arange(t)) if reverse else (jnp.arange(t) + 1)
    vre = jnp.transpose(cpr[f], (1, 3, 0, 2)).reshape(g, p, t * c)
    vim = jnp.transpose(-cpi[f], (1, 3, 0, 2)).reshape(g, p, t * c)
    v = jnp.concatenate([vre, vim], axis=1)
    atr, ati = pr[t], pi[t]
    a = jnp.stack([jnp.concatenate([atr, atr], -1), jnp.concatenate([-ati, ati], -1),
                   jnp.concatenate([ati, -ati], -1)])
    return {"m": m.astype(BF16), "w": w.astype(BF16), "ws": ws.astype(BF16), "v": v.astype(BF16), "a": a}


def _s5_glu_kernel(ys_ref, u_ref, d_ref, w_ref, b_ref, o_ref):
    y = ys_ref[0] + u_ref[0] * d_ref[...]
    z = jax.nn.gelu(y)
    gate = jax.nn.sigmoid(_dot(z.astype(BF16), w_ref[...]) + b_ref[...])
    o_ref[0] = (z * gate).astype(o_ref.dtype)


def _s5_glu(ys, u, d_skip, w_glu, b_glu):
    b, s, w = u.shape
    tm = min(512, s)
    full = lambda shape: pl.BlockSpec(shape, lambda bi, i: (0,) * len(shape))
    tok = pl.BlockSpec((1, tm, w), lambda bi, i: (bi, i, 0))
    return pl.pallas_call(
        _s5_glu_kernel,
        grid=(b, s // tm),
        in_specs=[tok, tok, full((1, w)), full((w, w)), full((1, w))],
        out_specs=tok,
        out_shape=jax.ShapeDtypeStruct((b, s, w), BF16),
        compiler_params=_params("parallel", "parallel"),
        name="s5_glu",
    )(ys, u, d_skip, w_glu, b_glu)


def _out_proj_kernel(a_ref, b_ref, wa_ref, wb_ref, x_ref, g_ref, o_ref):
    mix = _dot(a_ref[0], wa_ref[...]) + _dot(b_ref[0], wb_ref[...])
    o_ref[0] = x_ref[0] + g_ref[0] * mix


def _out_proj(a, bb, wa, wb, x, gate):
    b, s, d = x.shape
    k = a.shape[-1]
    tm = min(MM_TOKENS, s)
    tn = 1024
    return pl.pallas_call(
        _out_proj_kernel,
        grid=(b, s // tm, d // tn),
        in_specs=[
            pl.BlockSpec((1, tm, k), lambda bi, i, j: (bi, i, 0)),
            pl.BlockSpec((1, tm, k), lambda bi, i, j: (bi, i, 0)),
            pl.BlockSpec((k, tn), lambda bi, i, j: (0, j)),
            pl.BlockSpec((k, tn), lambda bi, i, j: (0, j)),
            pl.BlockSpec((1, tm, tn), lambda bi, i, j: (bi, i, j)),
            pl.BlockSpec((1, 1, tn), lambda bi, i, j: (bi, 0, j)),
        ],
        out_specs=pl.BlockSpec((1, tm, tn), lambda bi, i, j: (bi, i, j)),
        out_shape=jax.ShapeDtypeStruct((b, s, d), F32),
        compiler_params=_params("parallel", "parallel", "parallel"),
        name="out_proj",
    )(a, bb, wa, wb, x, gate)


FFN_HALO = 2 * SUBLANES


def _ffn_kernel(x_ref, xp_ref, xn_ref, ng_ref, sc_ref, sh_ref, wv_ref, wg_ref, cwv_ref, cwg_ref, cbv_ref, cbg_ref,
                wd_ref, g_ref, fg_ref, o_ref, h_scr, *, final):
    i = pl.program_id(1)
    j = pl.program_id(2)
    tm = x_ref.shape[1]
    hl = FFN_HALO

    @pl.when(j == 0)
    def _():
        g, sc, sh = ng_ref[...], sc_ref[0], sh_ref[0]
        h_scr[hl:hl + tm] = _modnorm(x_ref[0], g, sc, sh).astype(BF16)
        before = jnp.where(i == 0, 0.0, _modnorm(xp_ref[0], g, sc, sh))
        after = jnp.where(i == pl.num_programs(1) - 1, 0.0, _modnorm(xn_ref[0], g, sc, sh))
        h_scr[0:hl] = before.astype(BF16)
        h_scr[hl + tm:2 * hl + tm] = after.astype(BF16)
        o_ref[...] = jnp.zeros_like(o_ref)

    h = h_scr[...]
    rows = tm + 2 * hl

    def conv(w_ref, cw_ref, cb_ref):
        u = _dot(h, w_ref[...])
        cw = cw_ref[...]
        u_prev = pltpu.roll(u, 1, 0)[hl:hl + tm]
        u_next = pltpu.roll(u, rows - 1, 0)[hl:hl + tm]
        return u_prev * cw[0:1] + u[hl:hl + tm] * cw[1:2] + u_next * cw[2:3] + cb_ref[...]

    val = conv(wv_ref, cwv_ref, cbv_ref)
    gate = conv(wg_ref, cwg_ref, cbg_ref)
    act = (gate * jax.nn.sigmoid(gate)) * val
    o_ref[0] += _dot(act.astype(BF16), wd_ref[...])

    @pl.when(j == pl.num_programs(2) - 1)
    def _():
        xn = x_ref[0] + g_ref[0] * o_ref[0]
        if final:
            xn = xn * lax.rsqrt(jnp.mean(xn * xn, axis=-1, keepdims=True) + EPS) * fg_ref[...]
        o_ref[0] = xn


def _ffn(x, norm_g, sc, sh, w_up, conv_w, conv_b, w_down, gate, final_g, final):
    b, s, d = x.shape
    f = FFN_HIDDEN
    tm = min(FFN_TOKENS, s)
    tf = 512
    nf = f // tf
    nh = tm // FFN_HALO
    nhalo = s // FFN_HALO
    const = lambda shape: pl.BlockSpec(shape, lambda bi, i, j: (0,) * len(shape))
    per_b = pl.BlockSpec((1, 1, d), lambda bi, i, j: (bi, 0, 0))
    up = lambda off: pl.BlockSpec((d, tf), lambda bi, i, j: (0, j + off))
    cw = lambda off: pl.BlockSpec((3, tf), lambda bi, i, j: (0, j + off))
    cb = lambda off: pl.BlockSpec((1, tf), lambda bi, i, j: (0, j + off))
    once = pl.Buffered(1)
    return pl.pallas_call(
        functools.partial(_ffn_kernel, final=final),
        grid=(b, s // tm, nf),
        in_specs=[
            pl.BlockSpec((1, tm, d), lambda bi, i, j: (bi, i, 0), pipeline_mode=once),
            pl.BlockSpec((1, FFN_HALO, d), lambda bi, i, j: (bi, jnp.maximum(i * nh - 1, 0), 0)),
            pl.BlockSpec((1, FFN_HALO, d), lambda bi, i, j: (bi, jnp.minimum((i + 1) * nh, nhalo - 1), 0)),
            const((1, d)), per_b, per_b,
            up(0), up(nf), cw(0), cw(nf), cb(0), cb(nf),
            pl.BlockSpec((tf, d), lambda bi, i, j: (j, 0)),
            per_b, const((1, d)),
        ],
        out_specs=pl.BlockSpec((1, tm, d), lambda bi, i, j: (bi, i, 0), pipeline_mode=once),
        out_shape=jax.ShapeDtypeStruct((b, s, d), F32),
        scratch_shapes=[pltpu.VMEM((tm + 2 * FFN_HALO, d), BF16)],
        compiler_params=_params("parallel", "parallel", "arbitrary"),
        name="ffn",
    )(x, x, x, norm_g.reshape(1, d), sc, sh, w_up, w_up, conv_w, conv_w, conv_b, conv_b, w_down, gate, final_g)


def _rms(x, g, eps):
    return x * lax.rsqrt(jnp.mean(x * x, axis=-1, keepdims=True) + eps) * g


def _odd_prep_kernel(cq_ref, ckv_ref, kr_ref, dq_ref, dk_ref, dv_ref, cos_ref, sin_ref, qg_ref, kvg_ref,
                     wqa_ref, wqb_ref, wk_ref, wv_ref, place_ref,
                     mq_out, mk_out, mv_out, q0_out, q1_out, dk_out, dv_out):
    cosq = cos_ref[...]
    sinq = sin_ref[...]
    qn = _rms(cq_ref[0], qg_ref[...], EPS).astype(BF16)
    qa = _dot(qn, wqa_ref[...])
    qb = _dot(qn, wqb_ref[...])
    mla_scale = (C_NOPE + C_ROPE) ** -0.5 * LOG2E
    for h in range(C_HEADS):
        sl = slice(h * LANES, (h + 1) * LANES)
        mq_out[0, :, sl] = ((qa[:, sl] * cosq + qb[:, sl] * sinq) * mla_scale).astype(BF16)
    kvn = _rms(ckv_ref[0], kvg_ref[...], EPS).astype(BF16)
    kr = kr_ref[0]
    cos_k = pltpu.roll(cosq, LANES - C_NOPE, 1)
    sin_k = pltpu.roll(sinq, LANES - C_NOPE, 1)
    partner = pltpu.roll(kr, LANES - C_ROPE, 1)
    lane = lax.broadcasted_iota(jnp.int32, (1, LANES), 1)
    kr_rope = jnp.where(lane < C_ROPE, kr * cos_k + partner * sin_k, 0.0)
    mk_out[0] = (_dot(kvn, wk_ref[...]) + _dot(kr_rope.astype(BF16), place_ref[...])).astype(BF16)
    mv_out[0] = _dot(kvn, wv_ref[...]).astype(BF16)
    dq = dq_ref[0] * (D_HEAD ** -0.5 * LOG2E)
    lane_w = lax.broadcasted_iota(jnp.int32, (1, dq.shape[1]), 1)
    first_map = (lane_w % LANES) < D_HEAD
    q0_out[0] = jnp.where(first_map, dq, 0.0).astype(BF16)
    q1_out[0] = jnp.where(first_map, 0.0, dq).astype(BF16)
    dk_out[0] = dk_ref[0].astype(BF16)
    dv_out[0] = dv_ref[0].astype(BF16)


ODD_DQ = 0
ODD_DK = ODD_DQ + D_HEADS * 2 * D_HEAD
ODD_DV = ODD_DK + D_HEADS * 2 * D_HEAD
ODD_CQ = ODD_DV + D_HEADS * D_V
ODD_CKV = ODD_CQ + C_Q_RANK
ODD_KR = ODD_CKV + C_KV_RANK
ODD_COLS = 4096


def _odd_prep(p, cos_t, sin_t, wts):
    b, s, _ = p.shape
    tm = min(256, s)
    hw = C_HEADS * LANES
    full = lambda shape: pl.BlockSpec(shape, lambda bi, i: (0,) * len(shape))
    col = lambda off, n: pl.BlockSpec((1, tm, n), lambda bi, i: (bi, i, off // n))
    tok = pl.BlockSpec((1, tm, hw), lambda bi, i: (bi, i, 0))
    tab = pl.BlockSpec((tm, LANES), lambda bi, i: (i, 0))
    outs = [jax.ShapeDtypeStruct((b, s, hw), BF16)] * 7
    return pl.pallas_call(
        _odd_prep_kernel,
        grid=(b, s // tm),
        in_specs=[
            col(ODD_CQ, C_Q_RANK), col(ODD_CKV, C_KV_RANK), col(ODD_KR, LANES),
            col(ODD_DQ, hw), col(ODD_DK, hw), col(ODD_DV, hw), tab, tab,
            full((1, C_Q_RANK)), full((1, C_KV_RANK)),
            full((C_Q_RANK, hw)), full((C_Q_RANK, hw)), full((C_KV_RANK, hw)), full((C_KV_RANK, hw)),
            full((LANES, hw)),
        ],
        out_specs=[tok] * 7,
        out_shape=outs,
        compiler_params=_params("parallel", "parallel"),
        name="odd_prep",
    )(p, p, p, p, p, p, cos_t, sin_t, wts["q_norm_g"], wts["kv_norm_g"], wts["wqa"], wts["wqb"], wts["wk"],
      wts["wv"], wts["place"])


def _with_ones(v):
    return jnp.concatenate([v, jnp.ones_like(v)], axis=1)


def _flash_chain(q, k, v1, bias, shift, m_scr, acc_scr):
    s = _dot(q, k, NT)
    yield
    if bias is not None:
        s = s + bias
    m_prev = m_scr[...]
    m_new = jnp.maximum(m_prev, jnp.max(s, axis=-1, keepdims=True) + shift)
    alpha = jnp.exp2(m_prev - m_new)
    p = jnp.exp2(s - jnp.tile(m_new - shift, (1, s.shape[1] // LANES))).astype(BF16)
    yield
    acc_scr[...] = jnp.tile(alpha, (1, 2)) * acc_scr[...] + _dot(p, v1)
    m_scr[...] = m_new


MLA_HEADS_PER_STEP = 4


def _mla_flash_kernel(q_ref, k_ref, v_ref, o_ref, m_scr, acc_scr):
    kj = pl.program_id(3)

    @pl.when(kj == 0)
    def _():
        m_scr[...] = jnp.full_like(m_scr, -jnp.inf)
        acc_scr[...] = jnp.zeros_like(acc_scr)

    heads = [slice(h * LANES, (h + 1) * LANES) for h in range(MLA_HEADS_PER_STEP)]
    _run_lockstep([_flash_chain(q_ref[0, :, sl], k_ref[0, :, sl], _with_ones(v_ref[0, :, sl]), None, 0.0,
                                m_scr.at[h], acc_scr.at[h]) for h, sl in enumerate(heads)])

    @pl.when(kj == pl.num_programs(3) - 1)
    def _():
        for h, sl in enumerate(heads):
            acc = acc_scr[h]
            o_ref[0, :, sl] = (acc[:, :LANES] / acc[:, LANES:]).astype(o_ref.dtype)


def _mla_flash(q, k, v):
    b, s, hw = q.shape
    gw = MLA_HEADS_PER_STEP * LANES
    tq = min(ATT_TQ, s)
    tk = min(ATT_TK, s)
    qspec = pl.BlockSpec((1, tq, gw), lambda bi, h, i, j: (bi, i, h))
    kspec = pl.BlockSpec((1, tk, gw), lambda bi, h, i, j: (bi, j, h))
    return pl.pallas_call(
        _mla_flash_kernel,
        grid=(b, hw // gw, s // tq, s // tk),
        in_specs=[qspec, kspec, kspec],
        out_specs=qspec,
        out_shape=jax.ShapeDtypeStruct((b, s, hw), BF16),
        scratch_shapes=[pltpu.VMEM((MLA_HEADS_PER_STEP, tq, LANES), F32),
                        pltpu.VMEM((MLA_HEADS_PER_STEP, tq, 2 * LANES), F32)],
        compiler_params=_params("parallel", "parallel", "parallel", "arbitrary"),
        name="mla_flash",
    )(q, k, v)


BIAS_HALF = 256


DIFF_HEADS_PER_STEP = 2


def _diff_flash_kernel(q0_ref, q1_ref, k_ref, v_ref, tab_ref, lq1_ref, lk1_ref, lq2_ref, lk2_ref, sg_ref, o_ref,
                       m_scr, acc_scr, bias_scr, *, tq, tk, near, lambda_init):
    kj = pl.program_id(3)
    off = kj * tk - pl.program_id(2) * tq
    heads = [slice(h * LANES, (h + 1) * LANES) for h in range(DIFF_HEADS_PER_STEP)]

    @pl.when(kj == 0)
    def _():
        m_scr[...] = jnp.full_like(m_scr, -jnp.inf)
        acc_scr[...] = jnp.zeros_like(acc_scr)

    tabs = [tab_ref[h] for h in range(DIFF_HEADS_PER_STEP)]
    far_left = [tab[:, 0:1] for tab in tabs]
    far_right = [tab[:, 2 * BIAS_HALF - 1:2 * BIAS_HALF] for tab in tabs]

    def update(biases, shifts):
        chains = []
        for h, sl in enumerate(heads):
            k = k_ref[0, :, sl]
            v1 = _with_ones(v_ref[0, :, sl])
            for m, q_ref in enumerate((q0_ref, q1_ref)):
                chains.append(_flash_chain(q_ref[0, :, sl], k, v1, biases[h], shifts[h], m_scr.at[h, m],
                                           acc_scr.at[h, m]))
        _run_lockstep(chains)

    def toeplitz(tab, d):
        r = tab[:, d + LANES:d + 3 * LANES]
        rows = jnp.broadcast_to(r, (LANES, 2 * LANES))
        return pltpu.roll(rows, LANES, 1, stride=1, stride_axis=0)[:, :LANES]

    for d0 in near:
        @pl.when(off == d0)
        def _(d0=d0):
            for h in range(DIFF_HEADS_PER_STEP):
                blocks = {}
                for ri in range(tq // LANES):
                    for cj in range(tk // LANES):
                        d = d0 + (cj - ri) * LANES
                        sl = (h, slice(ri * LANES, (ri + 1) * LANES), slice(cj * LANES, (cj + 1) * LANES))
                        if d <= -BIAS_HALF:
                            bias_scr[sl] = jnp.broadcast_to(far_left[h], (LANES, LANES))
                        elif d >= BIAS_HALF:
                            bias_scr[sl] = jnp.broadcast_to(far_right[h], (LANES, LANES))
                        else:
                            if d not in blocks:
                                blocks[d] = toeplitz(tabs[h], d)
                            bias_scr[sl] = blocks[d]
            update([bias_scr[h] for h in range(DIFF_HEADS_PER_STEP)], [0.0] * DIFF_HEADS_PER_STEP)

    @pl.when(off < near[0])
    def _():
        update([None] * DIFF_HEADS_PER_STEP, far_left)

    @pl.when(off > near[-1])
    def _():
        update([None] * DIFF_HEADS_PER_STEP, far_right)

    @pl.when(kj == pl.num_programs(3) - 1)
    def _():
        lam = (jnp.exp(jnp.sum(lq1_ref[...] * lk1_ref[...], axis=-1, keepdims=True))
               - jnp.exp(jnp.sum(lq2_ref[...] * lk2_ref[...], axis=-1, keepdims=True)) + lambda_init)
        for h, sl in enumerate(heads):
            a0 = acc_scr[h, 0]
            a1 = acc_scr[h, 1]
            o = a0[:, :LANES] / a0[:, LANES:] - lam * (a1[:, :LANES] / a1[:, LANES:])
            o = _rms(o, sg_ref[...], SUBLN_EPS) * (1.0 - lambda_init)
            o_ref[0, :, sl] = o.astype(o_ref.dtype)


def _diff_flash(q0, q1, k, v, tab, lq1, lk1, lq2, lk2, subln_g, lambda_init):
    b, s, hw = q0.shape
    hps = DIFF_HEADS_PER_STEP
    gw = hps * LANES
    tq = min(ATT_TQ, s)
    tk = min(ATT_TK, s)
    offs = sorted({j * tk - i * tq for i in range(s // tq) for j in range(s // tk)})
    near = tuple(d for d in offs if d - (tq - 1) < BIAS_HALF and d + tk - 1 > -BIAS_HALF)
    qspec = pl.BlockSpec((1, tq, gw), lambda bi, h, i, j: (bi, i, h))
    kspec = pl.BlockSpec((1, tk, gw), lambda bi, h, i, j: (bi, j, h))
    vec = lambda n: pl.BlockSpec((1, n), lambda bi, h, i, j: (0, 0))
    return pl.pallas_call(
        functools.partial(_diff_flash_kernel, tq=tq, tk=tk, near=near, lambda_init=lambda_init),
        grid=(b, hw // gw, s // tq, s // tk),
        in_specs=[qspec, qspec, kspec, kspec,
                  pl.BlockSpec((hps, 1, 2 * BIAS_HALF), lambda bi, h, i, j: (h, 0, 0)),
                  vec(D_HEAD), vec(D_HEAD), vec(D_HEAD), vec(D_HEAD), vec(D_V)],
        out_specs=qspec,
        out_shape=jax.ShapeDtypeStruct((b, s, hw), BF16),
        scratch_shapes=[pltpu.VMEM((hps, 2, tq, LANES), F32), pltpu.VMEM((hps, 2, tq, 2 * LANES), F32),
                        pltpu.VMEM((hps, tq, tk), F32)],
        compiler_params=_params("parallel", "parallel", "parallel", "arbitrary"),
        name="diff_flash",
    )(q0, q1, k, v, tab, lq1, lk1, lq2, lk2, subln_g)


def _t5_bucket(rel):
    half = N_BUCKETS // 2
    max_exact = half // 2
    n = jnp.abs(rel)
    large = max_exact + (jnp.log(jnp.maximum(n, 1).astype(jnp.float32) / max_exact)
                         / math.log(MAX_DISTANCE / max_exact) * (half - max_exact)).astype(jnp.int32)
    large = jnp.minimum(large, half - 1)
    return jnp.where(rel > 0, half, 0) + jnp.where(n < max_exact, n, large)


def _rope_tables(s):
    inv = 1.0 / (ROPE_THETA ** (jnp.arange(0, C_ROPE, 2, dtype=F32) / C_ROPE))
    ang = jnp.arange(s, dtype=F32)[:, None] * inv[None, :]
    cos, sin = jnp.cos(ang), jnp.sin(ang)
    pad = LANES - C_NOPE - C_ROPE
    cos_t = jnp.concatenate([jnp.ones((s, C_NOPE), F32), cos, cos, jnp.zeros((s, pad), F32)], axis=-1)
    sin_t = jnp.concatenate([jnp.zeros((s, C_NOPE), F32), sin, sin, jnp.zeros((s, pad), F32)], axis=-1)
    return cos_t, sin_t


def _rot_half_cols(w):
    h = w.shape[-1] // 2
    return jnp.concatenate([-w[..., h:], w[..., :h]], axis=-1)


def _pack_even(j, even_w_in, even_w_out, rwkv_mu, rwkv_w0, rwkv_w_up, rwkv_a0, rwkv_a_up, rwkv_g_up, rwkv_k_k,
               rwkv_k_a, rwkv_r_k, rwkv_lnx_g, rwkv_lnx_b):
    w = A_WIDTH
    z = jnp.zeros((A_LORA, w), F32)
    blockdiag = lambda m: jnp.concatenate(
        [jnp.concatenate([m[0], z], axis=1), jnp.concatenate([z, m[1]], axis=1)], axis=0)
    head = jnp.arange(w) // A_HEAD
    return {
        "w_in_a": even_w_in[j][:, :A_COLS].astype(BF16),
        "w_in_b": even_w_in[j][:, A_COLS:].astype(BF16),
        "w_out_a": even_w_out[j][:w].astype(BF16),
        "w_out_b": even_w_out[j][w:].astype(BF16),
        "mu": rwkv_mu[j].reshape(1, A_COLS),
        "w0": rwkv_w0[j].reshape(1, 2 * w),
        "wup": blockdiag(rwkv_w_up[j]).astype(BF16),
        "a0": rwkv_a0[j].reshape(1, 2 * w),
        "aup": blockdiag(rwkv_a_up[j]).astype(BF16),
        "gup": rwkv_g_up[j].astype(BF16),
        "k_k": rwkv_k_k[j].reshape(1, w),
        "k_a": rwkv_k_a[j].reshape(1, w),
        "r_k": rwkv_r_k[j].reshape(1, w),
        "lnx_g": rwkv_lnx_g[j].reshape(1, w),
        "lnx_b": rwkv_lnx_b[j].reshape(1, w),
        "ones_bd": (head[:, None] == head[None, :]).astype(BF16),
    }


def _pack_odd(j, odd_w_in, odd_w_out, mla_q_norm_g, mla_kv_norm_g, mla_w_uq, mla_w_ukv):
    d = D_MODEL
    w_in = odd_w_in[j]
    o_cq, o_ckv = 0, C_Q_RANK
    o_kr = o_ckv + C_KV_RANK
    o_dq = o_kr + C_ROPE
    n_d = D_HEADS * 2 * D_HEAD
    w_kr = w_in[:, o_kr:o_kr + C_ROPE]
    packed = jnp.concatenate([
        w_in[:, o_dq:o_dq + 3 * n_d], w_in[:, o_cq:o_kr], w_kr, _rot_half_cols(w_kr),
        jnp.zeros((d, ODD_COLS - (C_Q_RANK + C_KV_RANK + 3 * n_d + 2 * C_ROPE)), F32)], axis=1)
    pad = LANES - C_NOPE - C_ROPE
    wq = mla_w_uq[j].reshape(C_Q_RANK, C_HEADS, C_NOPE + C_ROPE)
    zq = jnp.zeros((C_Q_RANK, C_HEADS, pad), F32)
    wqa = jnp.concatenate([wq, zq], axis=-1).reshape(C_Q_RANK, C_HEADS * LANES)
    wqb = jnp.concatenate([jnp.zeros((C_Q_RANK, C_HEADS, C_NOPE), F32), _rot_half_cols(wq[..., C_NOPE:]), zq],
                          axis=-1).reshape(C_Q_RANK, C_HEADS * LANES)
    wkv = mla_w_ukv[j].reshape(C_KV_RANK, C_HEADS, C_NOPE + C_V)
    wk = jnp.concatenate([wkv[..., :C_NOPE], jnp.zeros((C_KV_RANK, C_HEADS, LANES - C_NOPE), F32)],
                         axis=-1).reshape(C_KV_RANK, C_HEADS * LANES)
    wv = wkv[..., C_NOPE:].reshape(C_KV_RANK, C_HEADS * C_V)
    src = jnp.arange(LANES)[:, None]
    dst = jnp.arange(C_HEADS * LANES)[None, :] % LANES
    place = ((dst >= C_NOPE) & (dst < C_NOPE + C_ROPE) & (dst - C_NOPE == src)).astype(BF16)
    hv = C_HEADS * C_V
    return {
        "w_in": packed.astype(BF16),
        "w_out_a": odd_w_out[j][:hv].astype(BF16),
        "w_out_b": odd_w_out[j][hv:].astype(BF16),
        "q_norm_g": mla_q_norm_g[j].reshape(1, C_Q_RANK),
        "kv_norm_g": mla_kv_norm_g[j].reshape(1, C_KV_RANK),
        "wqa": wqa.astype(BF16), "wqb": wqb.astype(BF16), "wk": wk.astype(BF16), "wv": wv.astype(BF16),
        "place": place,
    }


def _even_mixers(x, g1n, sc1, sh1, ew, s5m, s5_d, s5_w_glu, s5_b_glu):
    b, s, _ = x.shape
    pa = _normmod_mm(x, g1n, sc1, sh1, ew["w_in_a"], tn=A_COLS // 3)
    u = _normmod_mm(x, g1n, sc1, sh1, ew["w_in_b"], tn=B_WIDTH)
    r, v, kk, lw, kd, bd, bv, g = _rwkv_prep(pa, ew)
    yf, yr = _rwkv_scan(r, v, kk, lw, kd, bd)
    ya = _rwkv_post(yf, yr, bv, g, ew["lnx_g"], ew["lnx_b"], ew["ones_bd"])
    nrows = s // S5_CHUNK
    u_g = jnp.transpose(u.astype(BF16).reshape(b, nrows, S5_CHUNK, B_GROUPS, B_GROUP), (0, 3, 1, 2, 4))
    u_g = u_g.reshape(b, B_GROUPS, nrows, S5_CHUNK * B_GROUP)
    ys_g = _s5_scan(u_g, s5m[0], False) + _s5_scan(u_g, s5m[1], True)
    ys = jnp.transpose(ys_g.reshape(b, B_GROUPS, nrows, S5_CHUNK, B_GROUP), (0, 2, 3, 1, 4)).reshape(b, s, B_WIDTH)
    yb = _s5_glu(ys, u, s5_d, s5_w_glu, s5_b_glu)
    return ya, yb


def _odd_mixers(x, g1n, sc1, sh1, ow, tabs, diff_w, lambda_init):
    s = x.shape[1]
    p = _normmod_mm(x, g1n, sc1, sh1, ow["w_in"], tn=1024)
    mq, mk, mv, q0, q1, dk, dv = _odd_prep(p, tabs["cos"][:s], tabs["sin"][:s], ow)
    yc = _mla_flash(mq, mk, mv)
    yd = _diff_flash(q0, q1, dk, dv, tabs["bias"], diff_w["lq1"], diff_w["lk1"], diff_w["lq2"], diff_w["lk2"],
                     diff_w["subln_g"], lambda_init)
    return yc, yd


def kernel(x_prompt, x_sample, c_prompt, c_sample, ada_w, ada_b, norm1_g, norm2_g, even_w_in, even_w_out, rwkv_mu, rwkv_w0, rwkv_w_up, rwkv_a0, rwkv_a_up, rwkv_g_up, rwkv_k_k, rwkv_k_a, rwkv_r_k, rwkv_lnx_g, rwkv_lnx_b, s5_lam_re, s5_lam_im, s5_log_step, s5_b_re, s5_b_im, s5_c_re, s5_c_im, s5_d, s5_w_glu, s5_b_glu, odd_w_in, odd_w_out, mla_q_norm_g, mla_kv_norm_g, mla_w_uq, mla_w_ukv, diff_lq1, diff_lk1, diff_lq2, diff_lk2, diff_subln_g, rel_bias, ffn_w_up, ffn_conv_w, ffn_conv_b, ffn_w_down, final_g):
    d = D_MODEL
    groups = [(x_prompt, c_prompt), (x_sample, c_sample)]
    nb = [g[0].shape[0] for g in groups]
    c_all = jnp.concatenate([g[1] for g in groups] + [jnp.zeros((SUBLANES - sum(nb), d), F32)], axis=0)
    mod = _ada_mod(c_all, ada_w, ada_b)

    max_s = max(g[0].shape[1] for g in groups)
    cos_t, sin_t = _rope_tables(max_s)
    rel = jnp.arange(-BIAS_HALF, BIAS_HALF, dtype=jnp.int32)
    bias_tab = (jnp.transpose(rel_bias.astype(F32)[_t5_bucket(rel)]) * LOG2E).reshape(D_HEADS, 1, 2 * BIAS_HALF)
    tabs = {"cos": cos_t, "sin": sin_t, "bias": bias_tab}

    xs = [g[0] for g in groups]
    for i in range(DEPTH):
        j = i // 2
        if i % 2 == 0:
            ew = _pack_even(j, even_w_in, even_w_out, rwkv_mu, rwkv_w0, rwkv_w_up, rwkv_a0, rwkv_a_up, rwkv_g_up,
                            rwkv_k_k, rwkv_k_a, rwkv_r_k, rwkv_lnx_g, rwkv_lnx_b)
            s5m = [_s5_matrices(s5_lam_re[j, dr], s5_lam_im[j, dr], s5_log_step[j, dr], s5_b_re[j, dr],
                                s5_b_im[j, dr], s5_c_re[j, dr], s5_c_im[j, dr], dr == 1) for dr in range(2)]
        else:
            ow = _pack_odd(j, odd_w_in, odd_w_out, mla_q_norm_g, mla_kv_norm_g, mla_w_uq, mla_w_ukv)
            diff_w = {"lq1": diff_lq1[j].reshape(1, D_HEAD), "lk1": diff_lk1[j].reshape(1, D_HEAD),
                      "lq2": diff_lq2[j].reshape(1, D_HEAD), "lk2": diff_lk2[j].reshape(1, D_HEAD),
                      "subln_g": diff_subln_g[j].reshape(1, D_V)}
        w_up = ffn_w_up[i].astype(BF16)
        w_down = ffn_w_down[i].astype(BF16)
        row0 = 0
        for gi in range(len(groups)):
            x = xs[gi]
            m = mod[i, row0:row0 + nb[gi]]
            row0 += nb[gi]
            sh1, sc1, g1, sh2, sc2, g2 = [m[:, None, k * d:(k + 1) * d] for k in range(N_MOD)]
            if i % 2 == 0:
                ya, yb = _even_mixers(x, norm1_g[i], sc1, sh1, ew, s5m, s5_d[j].reshape(1, B_WIDTH),
                                      s5_w_glu[j].astype(BF16), s5_b_glu[j].reshape(1, B_WIDTH))
                x = _out_proj(ya, yb, ew["w_out_a"], ew["w_out_b"], x, g1)
            else:
                yc, yd = _odd_mixers(x, norm1_g[i], sc1, sh1, ow, tabs, diff_w, 0.8 - 0.6 * math.exp(-0.3 * i))
                x = _out_proj(yc, yd, ow["w_out_a"], ow["w_out_b"], x, g1)
            x = _ffn(x, norm2_g[i], sc2, sh2, w_up, ffn_conv_w[i], ffn_conv_b[i].reshape(1, 2 * FFN_HIDDEN), w_down,
                     g2, final_g.reshape(1, d), final=(i == DEPTH - 1))
            xs[gi] = x
    return (xs[0], xs[1])
```

```python
import functools
import math

import jax
import jax.numpy as jnp
from jax import lax
from jax.experimental import pallas as pl
from jax.experimental.pallas import tpu as pltpu

F32 = jnp.float32
BF16 = jnp.bfloat16

D_MODEL = 2048
DEPTH = 2
EPS = 1e-6
A_WIDTH = 1024
A_HEAD = 64
A_HEADS = 16
A_LORA = 64
A_GATE_LORA = 128
A_COLS = 3 * A_WIDTH + 2 * A_LORA + 2 * A_LORA + A_GATE_LORA
RWKV_GN_EPS = 64e-5
DECAY_SCALE = math.exp(-0.5)
B_WIDTH = 1024
B_GROUP = 16
B_GROUPS = 64
B_STATE = 64
C_HEADS = 8
C_NOPE = 64
C_ROPE = 32
C_V = 128
C_Q_RANK = 512
C_KV_RANK = 256
ROPE_THETA = 10000.0
D_HEADS = 8
D_HEAD = 64
D_V = 128
SUBLN_EPS = 1e-5
N_BUCKETS = 32
MAX_DISTANCE = 128
FFN_HIDDEN = 5632
N_MOD = 6

LANES = 128
SUBLANES = 8
VMEM_LIMIT_BYTES = 56 * 1024 * 1024

MM_TOKENS = 1024
FFN_TOKENS = 1024
ATT_TQ = 1024
ATT_TK = 1024

RWKV_CHUNK = 64
S5_CHUNK = 16
LOG2E = 1.4426950408889634

NN = (((1,), (0,)), ((), ()))
NT = (((1,), (1,)), ((), ()))
BATCH_NN = (((2,), (1,)), ((0,), (0,)))


def _params(*sem):
    return pltpu.CompilerParams(dimension_semantics=sem, vmem_limit_bytes=VMEM_LIMIT_BYTES)


def _dot(a, b, dims=NN):
    return lax.dot_general(a, b, dims, preferred_element_type=F32)


def _split2(x):
    hi = x.astype(BF16)
    lo = (x - hi.astype(F32)).astype(BF16)
    return hi, lo


def _dot3(a, b, dims=NN):
    ah, al = _split2(a)
    bh, bl = _split2(b)
    return _dot(ah, bh, dims) + _dot(ah, bl, dims) + _dot(al, bh, dims)


def _dot_exact_rhs(a, b):
    ah, al = _split2(a)
    return _dot(ah, b) + _dot(al, b)


def _ada_kernel(c_ref, w_ref, b_ref, o_ref):
    c = c_ref[...]
    cs = c * jax.nn.sigmoid(c)
    o_ref[0] = _dot(cs.astype(BF16), w_ref[0].astype(BF16)) + b_ref[0]


def _ada_mod(c_all, ada_w, ada_b):
    n = N_MOD * D_MODEL
    tn = 1024
    return pl.pallas_call(
        _ada_kernel,
        grid=(DEPTH, n // tn),
        in_specs=[
            pl.BlockSpec((SUBLANES, D_MODEL), lambda l, j: (0, 0)),
            pl.BlockSpec((1, D_MODEL, tn), lambda l, j: (l, 0, j)),
            pl.BlockSpec((1, 1, tn), lambda l, j: (l, 0, j)),
        ],
        out_specs=pl.BlockSpec((1, SUBLANES, tn), lambda l, j: (l, 0, j)),
        out_shape=jax.ShapeDtypeStruct((DEPTH, SUBLANES, n), F32),
        compiler_params=_params("parallel", "parallel"),
        name="ada_mod",
    )(c_all, ada_w, ada_b.reshape(DEPTH, 1, n))


def _modnorm(x, g, sc, sh):
    y = x * lax.rsqrt(jnp.mean(x * x, axis=-1, keepdims=True) + EPS)
    return (y * g) * (1.0 + sc) + sh


def _normmod_mm_kernel(x_ref, g_ref, sc_ref, sh_ref, w_ref, o_ref, h_scr):
    @pl.when(pl.program_id(2) == 0)
    def _():
        h_scr[...] = _modnorm(x_ref[0], g_ref[...], sc_ref[0], sh_ref[0]).astype(BF16)

    o_ref[0] = _dot(h_scr[...], w_ref[...]).astype(o_ref.dtype)


def _normmod_mm(x, g, sc, sh, w, tn, out_dtype=F32):
    b, s, d = x.shape
    n = w.shape[1]
    tm = min(MM_TOKENS, s)
    return pl.pallas_call(
        _normmod_mm_kernel,
        grid=(b, s // tm, n // tn),
        in_specs=[
            pl.BlockSpec((1, tm, d), lambda bi, i, j: (bi, i, 0)),
            pl.BlockSpec((1, d), lambda bi, i, j: (0, 0)),
            pl.BlockSpec((1, 1, d), lambda bi, i, j: (bi, 0, 0)),
            pl.BlockSpec((1, 1, d), lambda bi, i, j: (bi, 0, 0)),
            pl.BlockSpec((d, tn), lambda bi, i, j: (0, j)),
        ],
        out_specs=pl.BlockSpec((1, tm, tn), lambda bi, i, j: (bi, i, j)),
        out_shape=jax.ShapeDtypeStruct((b, s, n), out_dtype),
        scratch_shapes=[pltpu.VMEM((tm, d), BF16)],
        compiler_params=_params("parallel", "parallel", "arbitrary"),
        name="normmod_mm",
    )(x, g.reshape(1, d), sc, sh, w)


def _rwkv_prep_kernel(p_ref, pp_ref, pn_ref, mu_ref, w0_ref, wup_ref, a0_ref, aup_ref, gup_ref, kk_ref, ka_ref,
                      rk_ref, ones_ref, r_out, v_out, kkn_out, lw_out, kd_out, bd_out, bv_out, g_out):
    i = pl.program_id(1)
    last = pl.num_programs(1) - 1
    pa = p_ref[0]
    tm = pa.shape[0]
    row = lax.broadcasted_iota(jnp.int32, (tm, 1), 0)
    prev_row = jnp.where(i == 0, 0.0, pp_ref[0][SUBLANES - 1:SUBLANES, :])
    next_row = jnp.where(i == last, 0.0, pn_ref[0][0:1, :])
    p_prev = jnp.where(row == 0, prev_row, pltpu.roll(pa, 1, 0))
    p_next = jnp.where(row == tm - 1, next_row, pltpu.roll(pa, tm - 1, 0))
    pa = pa + mu_ref[...] * (0.5 * (p_prev + p_next) - pa)

    w = A_WIDTH
    r = pa[:, 0:w]
    k = pa[:, w:2 * w]
    v = pa[:, 2 * w:3 * w]
    dw = pa[:, 3 * w:3 * w + 2 * A_LORA]
    da = pa[:, 3 * w + 2 * A_LORA:3 * w + 4 * A_LORA]
    dg = pa[:, 3 * w + 4 * A_LORA:A_COLS]

    lw = -DECAY_SCALE * jax.nn.sigmoid(w0_ref[...] + _dot(jnp.tanh(dw).astype(BF16), wup_ref[...]))
    icl = jax.nn.sigmoid(a0_ref[...] + _dot(da.astype(BF16), aup_ref[...]))
    g = _dot(jax.nn.sigmoid(dg).astype(BF16), gup_ref[...])

    ones_bd = ones_ref[...]
    kkr = k * kk_ref[...]
    ss = _dot_exact_rhs(kkr * kkr, ones_bd)
    kkn = kkr / jnp.maximum(jnp.sqrt(ss), 1e-12)

    r_out[0] = r
    v_out[0] = v
    kkn_out[0] = kkn
    lw_out[0] = lw
    g_out[0] = g
    bonus = jnp.zeros_like(r)
    for d in range(2):
        icl_d = icl[:, d * w:(d + 1) * w]
        k_d = k * (1.0 + (icl_d - 1.0) * ka_ref[...])
        kd_out[0, :, d * w:(d + 1) * w] = k_d
        bd_out[0, :, d * w:(d + 1) * w] = icl_d * kkn
        bonus = bonus + _dot_exact_rhs(r * k_d * rk_ref[...], ones_bd)
    bv_out[0] = bonus * v


def _rwkv_prep(pa, wts):
    b, s, _ = pa.shape
    tm = min(256, s)
    nh = tm // SUBLANES
    w = A_WIDTH
    full = lambda shape: pl.BlockSpec(shape, lambda bi, i: (0,) * len(shape))
    tok = lambda n: pl.BlockSpec((1, tm, n), lambda bi, i: (bi, i, 0))
    out_shapes = [jax.ShapeDtypeStruct((b, s, n), F32) for n in (w, w, w, 2 * w, 2 * w, 2 * w, w, w)]
    return pl.pallas_call(
        _rwkv_prep_kernel,
        grid=(b, s // tm),
        in_specs=[
            tok(A_COLS),
            pl.BlockSpec((1, SUBLANES, A_COLS), lambda bi, i: (bi, jnp.maximum(i * nh - 1, 0), 0)),
            pl.BlockSpec((1, SUBLANES, A_COLS), lambda bi, i: (bi, jnp.minimum((i + 1) * nh, s // SUBLANES - 1), 0)),
            full((1, A_COLS)), full((1, 2 * w)), full((2 * A_LORA, 2 * w)), full((1, 2 * w)),
            full((2 * A_LORA, 2 * w)), full((A_GATE_LORA, w)), full((1, w)), full((1, w)), full((1, w)),
            full((w, w)),
        ],
        out_specs=[tok(sd.shape[-1]) for sd in out_shapes],
        out_shape=out_shapes,
        compiler_params=_params("parallel", "parallel"),
        name="rwkv_prep",
    )(pa, pa, pa, wts["mu"], wts["w0"], wts["wup"], wts["a0"], wts["aup"], wts["gup"], wts["k_k"], wts["k_a"],
      wts["r_k"], wts["ones_bd"])


def _rwkv_chunk(r, v, kk, lw, kd, bd, s2, reverse):
    t = RWKV_CHUNK
    assert t == A_HEAD
    sign = -1 if reverse else 1
    ri = lax.broadcasted_iota(jnp.int32, (t, t), 0)
    ci = lax.broadcasted_iota(jnp.int32, (t, t), 1)
    tri = jnp.where((ri - ci) * sign >= 0, 1.0, 0.0).astype(BF16)
    lw_hi = lw.astype(BF16)
    rem = lw - lw_hi.astype(F32)
    lw_mid = rem.astype(BF16)
    lw_lo = (rem - lw_mid.astype(F32)).astype(BF16)
    cum = _dot(tri, lw_hi) + _dot(tri, lw_mid) + _dot(tri, lw_lo)
    yield
    tot = jnp.sum(lw, axis=0, keepdims=True)
    gam = jnp.exp(cum)
    gam_ex = jnp.exp(cum - lw)
    ginv = jnp.exp(-cum)
    gend = jnp.exp(tot - cum)
    gtot = jnp.exp(tot)

    width = r.shape[1]
    heads = width // A_HEAD
    lane_head = lax.broadcasted_iota(jnp.int32, (1, width), 1) // A_HEAD

    def stack(x):
        return jnp.concatenate([jnp.where(lane_head == h, x, 0.0) for h in range(heads)], axis=0).astype(BF16)

    def mm(x_wide, y_stack):
        return _dot(x_wide.astype(BF16), y_stack)

    ti = lax.broadcasted_iota(jnp.int32, (t, width), 0)
    si = lax.broadcasted_iota(jnp.int32, (t, width), 1) % t
    order = (ti - si) * sign
    incl = order >= 0
    strict = order > 0

    q_rk = jnp.concatenate([r * gam, kk * gam_ex], axis=0).astype(BF16)
    a = _dot(q_rk, jnp.concatenate([stack(kd * ginv), stack(bd * ginv)], axis=0), NT)
    yield
    n = heads * t
    a_rk = jnp.where(incl, a[:t, :n], 0.0)
    a_rb = jnp.where(incl, a[:t, n:], 0.0)
    a_kk = jnp.where(strict, a[t:, :n], 0.0)
    a_bk = jnp.where(strict, a[t:, n:], 0.0)

    qs = _dot(q_rk, s2.astype(BF16), NT)
    yield
    akv = mm(jnp.concatenate([a_rk, a_kk], axis=0), stack(v))
    yield

    s_n = stack(a_bk)
    n2 = mm(a_bk, s_n)
    yield
    n3 = mm(n2, s_n)
    yield
    n4 = mm(n2, stack(n2))
    yield
    s_n4 = stack(n4)
    n8 = mm(n4, s_n4)
    yield
    n12 = mm(n8, s_n4)
    yield
    f1 = jnp.where(ti == si, 1.0, 0.0) - a_bk + n2 - n3
    n16 = mm(n8, stack(n8))
    yield
    f12 = f1 + mm(f1, stack(n4 + n8 + n12))
    yield
    s_n16 = stack(n16)
    n32 = mm(n16, s_n16)
    yield
    n48 = mm(n32, s_n16)
    yield
    inv = f12 + mm(f12, stack(n16 + n32 + n48))
    yield

    ps = mm(inv, stack(-qs[t:] - akv[t:]))
    yield
    y = qs[:t] + akv[:t] + mm(a_rb, stack(ps))
    yield
    upd = _dot(jnp.concatenate([v, ps], axis=0).T.astype(BF16),
               jnp.concatenate([kd * gend, bd * gend], axis=0).astype(BF16))
    row_head = lax.broadcasted_iota(jnp.int32, (width, 1), 0) // A_HEAD
    s2_new = s2 * gtot + jnp.where(row_head == lane_head, upd, 0.0)
    return y, s2_new


def _run_lockstep(gens):
    results = [None] * len(gens)
    active = list(range(len(gens)))
    while active:
        for i in list(active):
            try:
                next(gens[i])
            except StopIteration as e:
                results[i] = e.value
                active.remove(i)
    return results


RWKV_CHAIN_LANES = 256
RWKV_CHAINS_PER_STEP = 4


def _rwkv_scan_kernel(rf_ref, vf_ref, kkf_ref, lwf_ref, kdf_ref, bdf_ref,
                      rb_ref, vb_ref, kkb_ref, lwb_ref, kdb_ref, bdb_ref, yf_ref, yb_ref, s_scr):
    @pl.when(pl.program_id(2) == 0)
    def _():
        s_scr[...] = jnp.zeros_like(s_scr)

    dirs = ((rf_ref, vf_ref, kkf_ref, lwf_ref, kdf_ref, bdf_ref, yf_ref),
            (rb_ref, vb_ref, kkb_ref, lwb_ref, kdb_ref, bdb_ref, yb_ref))
    chains = []
    for d, (r_ref, v_ref, kk_ref, lw_ref, kd_ref, bd_ref, y_ref) in enumerate(dirs):
        for p in range(RWKV_CHAINS_PER_STEP):
            sl = slice(p * RWKV_CHAIN_LANES, (p + 1) * RWKV_CHAIN_LANES)
            chains.append((d, p, sl, y_ref, _rwkv_chunk(
                r_ref[0, :, sl], v_ref[0, :, sl], kk_ref[0, :, sl], lw_ref[0, :, sl], kd_ref[0, :, sl],
                bd_ref[0, :, sl], s_scr[d, p], reverse=(d == 1))))
    results = _run_lockstep([c[4] for c in chains])
    for (d, p, sl, y_ref, _), (y, s_new) in zip(chains, results):
        y_ref[0, :, sl] = y
        s_scr[d, p] = s_new


def _rwkv_scan(r, v, kk, lw, kd, bd):
    b, s, w = r.shape
    t = RWKV_CHUNK
    nc = s // t
    pw = RWKV_CHAINS_PER_STEP * RWKV_CHAIN_LANES
    ngrp = w // pw
    fwd = lambda off: pl.BlockSpec((1, t, pw), lambda bi, g, c: (bi, c, g + off))
    bwd = lambda off: pl.BlockSpec((1, t, pw), lambda bi, g, c: (bi, nc - 1 - c, g + off))
    return pl.pallas_call(
        _rwkv_scan_kernel,
        grid=(b, ngrp, nc),
        in_specs=[fwd(0)] * 6 + [bwd(0)] * 3 + [bwd(ngrp)] * 3,
        out_specs=[fwd(0), bwd(0)],
        out_shape=[jax.ShapeDtypeStruct((b, s, w), F32)] * 2,
        scratch_shapes=[pltpu.VMEM((2, RWKV_CHAINS_PER_STEP, RWKV_CHAIN_LANES, RWKV_CHAIN_LANES), F32)],
        compiler_params=_params("parallel", "parallel", "arbitrary"),
        name="rwkv_scan",
    )(r, v, kk, lw, kd, bd, r, v, kk, lw, kd, bd)


def _rwkv_post_kernel(yf_ref, yb_ref, bv_ref, g_ref, lng_ref, lnb_ref, ones_ref, o_ref):
    y = yf_ref[0] + yb_ref[0]
    ones_bd = ones_ref[...]
    mean = _dot_exact_rhs(y, ones_bd) * (1.0 / A_HEAD)
    yc = y - mean
    var = _dot_exact_rhs(yc * yc, ones_bd) * (1.0 / A_HEAD)
    yn = yc * lax.rsqrt(var + RWKV_GN_EPS) * lng_ref[...] + lnb_ref[...]
    o_ref[0] = ((yn + bv_ref[0]) * g_ref[0]).astype(o_ref.dtype)


def _rwkv_post(yf, yb, bv, g, lng, lnb, ones_bd):
    b, s, w = bv.shape
    tm = min(512, s)
    full = lambda shape: pl.BlockSpec(shape, lambda bi, i: (0,) * len(shape))
    tok = lambda n: pl.BlockSpec((1, tm, n), lambda bi, i: (bi, i, 0))
    return pl.pallas_call(
        _rwkv_post_kernel,
        grid=(b, s // tm),
        in_specs=[tok(w), tok(w), tok(w), tok(w), full((1, w)), full((1, w)), full((w, w))],
        out_specs=tok(w),
        out_shape=jax.ShapeDtypeStruct((b, s, w), BF16),
        compiler_params=_params("parallel", "parallel"),
        name="rwkv_post",
    )(yf, yb, bv, g, lng, lnb, ones_bd)


def _s5_kernel(u_ref, m_ref, w_ref, ws_ref, v_ref, a_ref, y_ref, x_scr, xs_scr, h_scr, carry_scr, *, rows, reverse):
    g = B_GROUPS

    @pl.when(pl.program_id(1) == 0)
    def _():
        carry_scr[...] = jnp.zeros_like(carry_scr)

    u = u_ref[0]
    x_scr[...] = lax.dot_general(u, w_ref[...], BATCH_NN, preferred_element_type=F32).reshape(g * rows, LANES)
    xs_scr[...] = lax.dot_general(u, ws_ref[...], BATCH_NN, preferred_element_type=F32).reshape(g * rows, LANES)
    a1 = a_ref[0]
    a2 = a_ref[1]
    a2s = a_ref[2]

    def step(i, carry):
        h, hs = carry
        r = (rows - 1 - i) if reverse else i
        idx = pl.ds(r, g, stride=rows)
        h_scr[idx, :] = h
        hn = a1 * h + a2 * hs + x_scr[idx, :]
        hsn = a1 * hs + a2s * h + xs_scr[idx, :]
        return hn, hsn

    h, hs = lax.fori_loop(0, rows, step, (carry_scr[0], carry_scr[1]))
    carry_scr[0] = h
    carry_scr[1] = hs
    hprev = h_scr[...].reshape(g, rows, LANES).astype(BF16)
    y_ref[0] = (lax.dot_general(u, m_ref[...], BATCH_NN, preferred_element_type=F32)
                + lax.dot_general(hprev, v_ref[...], BATCH_NN, preferred_element_type=F32))


def _s5_scan(u_g, mats, reverse):
    b, g, nrows, cw = u_g.shape
    rows = min(64, nrows)
    nsb = nrows // rows
    whole = pl.BlockSpec(memory_space=pltpu.VMEM)
    idx = (lambda bi, i: (bi, 0, nsb - 1 - i, 0)) if reverse else (lambda bi, i: (bi, 0, i, 0))
    return pl.pallas_call(
        functools.partial(_s5_kernel, rows=rows, reverse=reverse),
        grid=(b, nsb),
        in_specs=[pl.BlockSpec((1, g, rows, cw), idx), whole, whole, whole, whole, whole],
        out_specs=pl.BlockSpec((1, g, rows, cw), idx),
        out_shape=jax.ShapeDtypeStruct((b, g, nrows, cw), F32),
        scratch_shapes=[pltpu.VMEM((g * rows, LANES), F32), pltpu.VMEM((g * rows, LANES), F32),
                        pltpu.VMEM((g * rows, LANES), F32), pltpu.VMEM((2, g, LANES), F32)],
        compiler_params=_params("parallel", "arbitrary"),
        name="s5_scan_bwd" if reverse else "s5_scan_fwd",
    )(u_g, mats["m"], mats["w"], mats["ws"], mats["v"], mats["a"])


def _s5_matrices(lam_re, lam_im, log_step, b_re, b_im, c_re, c_im, reverse):
    hp = lax.Precision.HIGHEST
    t = S5_CHUNK
    g, p, c = B_GROUPS, B_STATE, B_GROUP
    lr, li = lam_re.astype(F32), lam_im.astype(F32)
    step = jnp.exp(log_step.astype(F32))[:, None]
    ar, ai = jnp.exp(lr * step) * jnp.cos(li * step), jnp.exp(lr * step) * jnp.sin(li * step)
    den = lr * lr + li * li
    nr, ni = ar - 1.0, ai
    fr, fi = (nr * lr + ni * li) / den, (ni * lr - nr * li) / den
    br, bi = b_re.astype(F32), b_im.astype(F32)
    bbr = fr[..., None] * br - fi[..., None] * bi
    bbi = fr[..., None] * bi + fi[..., None] * br
    cr, cim = c_re.astype(F32), c_im.astype(F32)
    taus = jnp.arange(t + 1, dtype=F32)[:, None, None]
    mag = jnp.exp(lr * step * taus)
    pr, pi = mag * jnp.cos(li * step * taus), mag * jnp.sin(li * step * taus)
    cpr = cr[None] * pr[:, :, None, :] - cim[None] * pi[:, :, None, :]
    cpi = cr[None] * pi[:, :, None, :] + cim[None] * pr[:, :, None, :]
    kern = (jnp.einsum("tgcp,gpd->tgcd", cpr, bbr, precision=hp)
            - jnp.einsum("tgcp,gpd->tgcd", cpi, bbi, precision=hp))
    s_idx = jnp.arange(t)[:, None]
    t_idx = jnp.arange(t)[None, :]
    lag = (s_idx - t_idx) if reverse else (t_idx - s_idx)
    kk = jnp.where((lag >= 0)[:, :, None, None, None], kern[jnp.clip(lag, 0, t)], 0.0)
    m = jnp.transpose(kk, (2, 0, 4, 1, 3)).reshape(g, t * c, t * c)
    e = jnp.arange(t) if reverse else (t - 1 - jnp.arange(t))
    pre, pie = pr[e], pi[e]
    wre = pre[..., None] * bbr[None] - pie[..., None] * bbi[None]
    wim = pre[..., None] * bbi[None] + pie[..., None] * bbr[None]
    wre = jnp.transpose(wre, (1, 0, 3, 2)).reshape(g, t * c, p)
    wim = jnp.transpose(wim, (1, 0, 3, 2)).reshape(g, t * c, p)
    w = jnp.concatenate([wre, wim], axis=-1)
    ws = jnp.concatenate([wim, wre], axis=-1)
    f = (t - jnp.arange(t)) if reverse else (jnp.arange(t) + 1)
    vre = jnp.transpose(cpr[f], (1, 3, 0, 2)).reshape(g, p, t * c)
    vim = jnp.transpose(-cpi[f], (1, 3, 0, 2)).reshape(g, p, t * c)
    v = jnp.concatenate([vre, vim], axis=1)
    atr, ati = pr[t], pi[t]
    a = jnp.stack([jnp.concatenate([atr, atr], -1), jnp.concatenate([-ati, ati], -1),
                   jnp.concatenate([ati, -ati], -1)])
    return {"m": m.astype(BF16), "w": w.astype(BF16), "ws": ws.astype(BF16), "v": v.astype(BF16), "a": a}


def _s5_glu_kernel(ys_ref, u_ref, d_ref, w_ref, b_ref, o_ref):
    y = ys_ref[0] + u_ref[0] * d_ref[...]
    z = jax.nn.gelu(y)
    gate = jax.nn.sigmoid(_dot(z.astype(BF16), w_ref[...]) + b_ref[...])
    o_ref[0] = (z * gate).astype(o_ref.dtype)


def _s5_glu(ys, u, d_skip, w_glu, b_glu):
    b, s, w = u.shape
    tm = min(512, s)
    full = lambda shape: pl.BlockSpec(shape, lambda bi, i: (0,) * len(shape))
    tok = pl.BlockSpec((1, tm, w), lambda bi, i: (bi, i, 0))
    return pl.pallas_call(
        _s5_glu_kernel,
        grid=(b, s // tm),
        in_specs=[tok, tok, full((1, w)), full((w, w)), full((1, w))],
        out_specs=tok,
        out_shape=jax.ShapeDtypeStruct((b, s, w), BF16),
        compiler_params=_params("parallel", "parallel"),
        name="s5_glu",
    )(ys, u, d_skip, w_glu, b_glu)


def _out_proj_kernel(a_ref, b_ref, wa_ref, wb_ref, x_ref, g_ref, o_ref):
    mix = _dot(a_ref[0], wa_ref[...]) + _dot(b_ref[0], wb_ref[...])
    o_ref[0] = x_ref[0] + g_ref[0] * mix


def _out_proj(a, bb, wa, wb, x, gate):
    b, s, d = x.shape
    k = a.shape[-1]
    tm = min(MM_TOKENS, s)
    tn = 1024
    return pl.pallas_call(
        _out_proj_kernel,
        grid=(b, s // tm, d // tn),
        in_specs=[
            pl.BlockSpec((1, tm, k), lambda bi, i, j: (bi, i, 0)),
            pl.BlockSpec((1, tm, k), lambda bi, i, j: (bi, i, 0)),
            pl.BlockSpec((k, tn), lambda bi, i, j: (0, j)),
            pl.BlockSpec((k, tn), lambda bi, i, j: (0, j)),
            pl.BlockSpec((1, tm, tn), lambda bi, i, j: (bi, i, j)),
            pl.BlockSpec((1, 1, tn), lambda bi, i, j: (bi, 0, j)),
        ],
        out_specs=pl.BlockSpec((1, tm, tn), lambda bi, i, j: (bi, i, j)),
        out_shape=jax.ShapeDtypeStruct((b, s, d), F32),
        compiler_params=_params("parallel", "parallel", "parallel"),
        name="out_proj",
    )(a, bb, wa, wb, x, gate)


FFN_HALO = 2 * SUBLANES


def _ffn_kernel(x_ref, xp_ref, xn_ref, ng_ref, sc_ref, sh_ref, wv_ref, wg_ref, cwv_ref, cwg_ref, cbv_ref, cbg_ref,
                wd_ref, g_ref, fg_ref, o_ref, h_scr, *, final):
    i = pl.program_id(1)
    j = pl.program_id(2)
    tm = x_ref.shape[1]
    hl = FFN_HALO

    @pl.when(j == 0)
    def _():
        g, sc, sh = ng_ref[...], sc_ref[0], sh_ref[0]
        h_scr[hl:hl + tm] = _modnorm(x_ref[0], g, sc, sh).astype(BF16)
        before = jnp.where(i == 0, 0.0, _modnorm(xp_ref[0], g, sc, sh))
        after = jnp.where(i == pl.num_programs(1) - 1, 0.0, _modnorm(xn_ref[0], g, sc, sh))
        h_scr[0:hl] = before.astype(BF16)
        h_scr[hl + tm:2 * hl + tm] = after.astype(BF16)
        o_ref[...] = jnp.zeros_like(o_ref)

    h = h_scr[...]
    rows = tm + 2 * hl

    def conv(w_ref, cw_ref, cb_ref):
        u = _dot(h, w_ref[...])
        cw = cw_ref[...]
        u_prev = pltpu.roll(u, 1, 0)[hl:hl + tm]
        u_next = pltpu.roll(u, rows - 1, 0)[hl:hl + tm]
        return u_prev * cw[0:1] + u[hl:hl + tm] * cw[1:2] + u_next * cw[2:3] + cb_ref[...]

    val = conv(wv_ref, cwv_ref, cbv_ref)
    gate = conv(wg_ref, cwg_ref, cbg_ref)
    act = (gate * jax.nn.sigmoid(gate)) * val
    o_ref[0] += _dot(act.astype(BF16), wd_ref[...])

    @pl.when(j == pl.num_programs(2) - 1)
    def _():
        xn = x_ref[0] + g_ref[0] * o_ref[0]
        if final:
            xn = xn * lax.rsqrt(jnp.mean(xn * xn, axis=-1, keepdims=True) + EPS) * fg_ref[...]
        o_ref[0] = xn


def _ffn(x, norm_g, sc, sh, w_up, conv_w, conv_b, w_down, gate, final_g, final):
    b, s, d = x.shape
    f = FFN_HIDDEN
    tm = min(FFN_TOKENS, s)
    tf = 512
    nf = f // tf
    nh = tm // FFN_HALO
    nhalo = s // FFN_HALO
    const = lambda shape: pl.BlockSpec(shape, lambda bi, i, j: (0,) * len(shape))
    per_b = pl.BlockSpec((1, 1, d), lambda bi, i, j: (bi, 0, 0))
    up = lambda off: pl.BlockSpec((d, tf), lambda bi, i, j: (0, j + off))
    cw = lambda off: pl.BlockSpec((3, tf), lambda bi, i, j: (0, j + off))
    cb = lambda off: pl.BlockSpec((1, tf), lambda bi, i, j: (0, j + off))
    once = pl.Buffered(1)
    return pl.pallas_call(
        functools.partial(_ffn_kernel, final=final),
        grid=(b, s // tm, nf),
        in_specs=[
            pl.BlockSpec((1, tm, d), lambda bi, i, j: (bi, i, 0), pipeline_mode=once),
            pl.BlockSpec((1, FFN_HALO, d), lambda bi, i, j: (bi, jnp.maximum(i * nh - 1, 0), 0)),
            pl.BlockSpec((1, FFN_HALO, d), lambda bi, i, j: (bi, jnp.minimum((i + 1) * nh, nhalo - 1), 0)),
            const((1, d)), per_b, per_b,
            up(0), up(nf), cw(0), cw(nf), cb(0), cb(nf),
            pl.BlockSpec((tf, d), lambda bi, i, j: (j, 0)),
            per_b, const((1, d)),
        ],
        out_specs=pl.BlockSpec((1, tm, d), lambda bi, i, j: (bi, i, 0), pipeline_mode=once),
        out_shape=jax.ShapeDtypeStruct((b, s, d), F32),
        scratch_shapes=[pltpu.VMEM((tm + 2 * FFN_HALO, d), BF16)],
        compiler_params=_params("parallel", "parallel", "arbitrary"),
        name="ffn",
    )(x, x, x, norm_g.reshape(1, d), sc, sh, w_up, w_up, conv_w, conv_w, conv_b, conv_b, w_down, gate, final_g)


def _rms(x, g, eps):
    return x * lax.rsqrt(jnp.mean(x * x, axis=-1, keepdims=True) + eps) * g


def _odd_prep_kernel(cq_ref, ckv_ref, kr_ref, dq_ref, dk_ref, dv_ref, cos_ref, sin_ref, qg_ref, kvg_ref,
                     wqa_ref, wqb_ref, wk_ref, wv_ref, place_ref,
                     mq_out, mk_out, mv_out, q0_out, q1_out, dk_out, dv_out):
    cosq = cos_ref[...]
    sinq = sin_ref[...]
    qn = _rms(cq_ref[0], qg_ref[...], EPS).astype(BF16)
    qa = _dot(qn, wqa_ref[...])
    qb = _dot(qn, wqb_ref[...])
    mla_scale = (C_NOPE + C_ROPE) ** -0.5 * LOG2E
    for h in range(C_HEADS):
        sl = slice(h * LANES, (h + 1) * LANES)
        mq_out[0, :, sl] = ((qa[:, sl] * cosq + qb[:, sl] * sinq) * mla_scale).astype(BF16)
    kvn = _rms(ckv_ref[0], kvg_ref[...], EPS).astype(BF16)
    kr = kr_ref[0]
    cos_k = pltpu.roll(cosq, LANES - C_NOPE, 1)
    sin_k = pltpu.roll(sinq, LANES - C_NOPE, 1)
    partner = pltpu.roll(kr, LANES - C_ROPE, 1)
    lane = lax.broadcasted_iota(jnp.int32, (1, LANES), 1)
    kr_rope = jnp.where(lane < C_ROPE, kr * cos_k + partner * sin_k, 0.0)
    mk_out[0] = (_dot(kvn, wk_ref[...]) + _dot(kr_rope.astype(BF16), place_ref[...])).astype(BF16)
    mv_out[0] = _dot(kvn, wv_ref[...]).astype(BF16)
    dq = dq_ref[0] * (D_HEAD ** -0.5 * LOG2E)
    lane_w = lax.broadcasted_iota(jnp.int32, (1, dq.shape[1]), 1)
    first_map = (lane_w % LANES) < D_HEAD
    q0_out[0] = jnp.where(first_map, dq, 0.0).astype(BF16)
    q1_out[0] = jnp.where(first_map, 0.0, dq).astype(BF16)
    dk_out[0] = dk_ref[0].astype(BF16)
    dv_out[0] = dv_ref[0].astype(BF16)


ODD_DQ = 0
ODD_DK = ODD_DQ + D_HEADS * 2 * D_HEAD
ODD_DV = ODD_DK + D_HEADS * 2 * D_HEAD
ODD_CQ = ODD_DV + D_HEADS * D_V
ODD_CKV = ODD_CQ + C_Q_RANK
ODD_KR = ODD_CKV + C_KV_RANK
ODD_COLS = 4096


def _odd_prep(p, cos_t, sin_t, wts):
    b, s, _ = p.shape
    tm = min(256, s)
    hw = C_HEADS * LANES
    full = lambda shape: pl.BlockSpec(shape, lambda bi, i: (0,) * len(shape))
    col = lambda off, n: pl.BlockSpec((1, tm, n), lambda bi, i: (bi, i, off // n))
    tok = pl.BlockSpec((1, tm, hw), lambda bi, i: (bi, i, 0))
    tab = pl.BlockSpec((tm, LANES), lambda bi, i: (i, 0))
    outs = [jax.ShapeDtypeStruct((b, s, hw), BF16)] * 7
    return pl.pallas_call(
        _odd_prep_kernel,
        grid=(b, s // tm),
        in_specs=[
            col(ODD_CQ, C_Q_RANK), col(ODD_CKV, C_KV_RANK), col(ODD_KR, LANES),
            col(ODD_DQ, hw), col(ODD_DK, hw), col(ODD_DV, hw), tab, tab,
            full((1, C_Q_RANK)), full((1, C_KV_RANK)),
            full((C_Q_RANK, hw)), full((C_Q_RANK, hw)), full((C_KV_RANK, hw)), full((C_KV_RANK, hw)),
            full((LANES, hw)),
        ],
        out_specs=[tok] * 7,
        out_shape=outs,
        compiler_params=_params("parallel", "parallel"),
        name="odd_prep",
    )(p, p, p, p, p, p, cos_t, sin_t, wts["q_norm_g"], wts["kv_norm_g"], wts["wqa"], wts["wqb"], wts["wk"],
      wts["wv"], wts["place"])


def _with_ones(v):
    return jnp.concatenate([v, jnp.ones_like(v)], axis=1)


def _flash_chain(q, k, v1, bias, shift, m_scr, acc_scr):
    s = _dot(q, k, NT)
    yield
    if bias is not None:
        s = s + bias
    m_prev = m_scr[...]
    m_new = jnp.maximum(m_prev, jnp.max(s, axis=-1, keepdims=True) + shift)
    alpha = jnp.exp2(m_prev - m_new)
    p = jnp.exp2(s - jnp.tile(m_new - shift, (1, s.shape[1] // LANES))).astype(BF16)
    yield
    acc_scr[...] = jnp.tile(alpha, (1, 2)) * acc_scr[...] + _dot(p, v1)
    m_scr[...] = m_new


MLA_HEADS_PER_STEP = 4


def _mla_flash_kernel(q_ref, k_ref, v_ref, o_ref, m_scr, acc_scr):
    kj = pl.program_id(3)

    @pl.when(kj == 0)
    def _():
        m_scr[...] = jnp.full_like(m_scr, -jnp.inf)
        acc_scr[...] = jnp.zeros_like(acc_scr)

    heads = [slice(h * LANES, (h + 1) * LANES) for h in range(MLA_HEADS_PER_STEP)]
    _run_lockstep([_flash_chain(q_ref[0, :, sl], k_ref[0, :, sl], _with_ones(v_ref[0, :, sl]), None, 0.0,
                                m_scr.at[h], acc_scr.at[h]) for h, sl in enumerate(heads)])

    @pl.when(kj == pl.num_programs(3) - 1)
    def _():
        for h, sl in enumerate(heads):
            acc = acc_scr[h]
            o_ref[0, :, sl] = (acc[:, :LANES] / acc[:, LANES:]).astype(o_ref.dtype)


def _mla_flash(q, k, v):
    b, s, hw = q.shape
    gw = MLA_HEADS_PER_STEP * LANES
    tq = min(ATT_TQ, s)
    tk = min(ATT_TK, s)
    qspec = pl.BlockSpec((1, tq, gw), lambda bi, h, i, j: (bi, i, h))
    kspec = pl.BlockSpec((1, tk, gw), lambda bi, h, i, j: (bi, j, h))
    return pl.pallas_call(
        _mla_flash_kernel,
        grid=(b, hw // gw, s // tq, s // tk),
        in_specs=[qspec, kspec, kspec],
        out_specs=qspec,
        out_shape=jax.ShapeDtypeStruct((b, s, hw), BF16),
        scratch_shapes=[pltpu.VMEM((MLA_HEADS_PER_STEP, tq, LANES), F32),
                        pltpu.VMEM((MLA_HEADS_PER_STEP, tq, 2 * LANES), F32)],
        compiler_params=_params("parallel", "parallel", "parallel", "arbitrary"),
        name="mla_flash",
    )(q, k, v)


BIAS_HALF = 256


DIFF_HEADS_PER_STEP = 2


def _diff_flash_kernel(q0_ref, q1_ref, k_ref, v_ref, tab_ref, lq1_ref, lk1_ref, lq2_ref, lk2_ref, sg_ref, o_ref,
                       m_scr, acc_scr, bias_scr, *, tq, tk, near, lambda_init):
    kj = pl.program_id(3)
    off = kj * tk - pl.program_id(2) * tq
    heads = [slice(h * LANES, (h + 1) * LANES) for h in range(DIFF_HEADS_PER_STEP)]

    @pl.when(kj == 0)
    def _():
        m_scr[...] = jnp.full_like(m_scr, -jnp.inf)
        acc_scr[...] = jnp.zeros_like(acc_scr)

    tabs = [tab_ref[h] for h in range(DIFF_HEADS_PER_STEP)]
    far_left = [tab[:, 0:1] for tab in tabs]
    far_right = [tab[:, 2 * BIAS_HALF - 1:2 * BIAS_HALF] for tab in tabs]

    def update(biases, shifts):
        chains = []
        for h, sl in enumerate(heads):
            k = k_ref[0, :, sl]
            v1 = _with_ones(v_ref[0, :, sl])
            for m, q_ref in enumerate((q0_ref, q1_ref)):
                chains.append(_flash_chain(q_ref[0, :, sl], k, v1, biases[h], shifts[h], m_scr.at[h, m],
                                           acc_scr.at[h, m]))
        _run_lockstep(chains)

    def toeplitz(tab, d):
        r = tab[:, d + LANES:d + 3 * LANES]
        rows = jnp.broadcast_to(r, (LANES, 2 * LANES))
        return pltpu.roll(rows, LANES, 1, stride=1, stride_axis=0)[:, :LANES]

    for d0 in near:
        @pl.when(off == d0)
        def _(d0=d0):
            for h in range(DIFF_HEADS_PER_STEP):
                blocks = {}
                for ri in range(tq // LANES):
                    for cj in range(tk // LANES):
                        d = d0 + (cj - ri) * LANES
                        sl = (h, slice(ri * LANES, (ri + 1) * LANES), slice(cj * LANES, (cj + 1) * LANES))
                        if d <= -BIAS_HALF:
                            bias_scr[sl] = jnp.broadcast_to(far_left[h], (LANES, LANES))
                        elif d >= BIAS_HALF:
                            bias_scr[sl] = jnp.broadcast_to(far_right[h], (LANES, LANES))
                        else:
                            if d not in blocks:
                                blocks[d] = toeplitz(tabs[h], d)
                            bias_scr[sl] = blocks[d]
            update([bias_scr[h] for h in range(DIFF_HEADS_PER_STEP)], [0.0] * DIFF_HEADS_PER_STEP)

    @pl.when(off < near[0])
    def _():
        update([None] * DIFF_HEADS_PER_STEP, far_left)

    @pl.when(off > near[-1])
    def _():
        update([None] * DIFF_HEADS_PER_STEP, far_right)

    @pl.when(kj == pl.num_programs(3) - 1)
    def _():
        lam = (jnp.exp(jnp.sum(lq1_ref[...] * lk1_ref[...], axis=-1, keepdims=True))
               - jnp.exp(jnp.sum(lq2_ref[...] * lk2_ref[...], axis=-1, keepdims=True)) + lambda_init)
        for h, sl in enumerate(heads):
            a0 = acc_scr[h, 0]
            a1 = acc_scr[h, 1]
            o = a0[:, :LANES] / a0[:, LANES:] - lam * (a1[:, :LANES] / a1[:, LANES:])
            o = _rms(o, sg_ref[...], SUBLN_EPS) * (1.0 - lambda_init)
            o_ref[0, :, sl] = o.astype(o_ref.dtype)


def _diff_flash(q0, q1, k, v, tab, lq1, lk1, lq2, lk2, subln_g, lambda_init):
    b, s, hw = q0.shape
    hps = DIFF_HEADS_PER_STEP
    gw = hps * LANES
    tq = min(ATT_TQ, s)
    tk = min(ATT_TK, s)
    offs = sorted({j * tk - i * tq for i in range(s // tq) for j in range(s // tk)})
    near = tuple(d for d in offs if d - (tq - 1) < BIAS_HALF and d + tk - 1 > -BIAS_HALF)
    qspec = pl.BlockSpec((1, tq, gw), lambda bi, h, i, j: (bi, i, h))
    kspec = pl.BlockSpec((1, tk, gw), lambda bi, h, i, j: (bi, j, h))
    vec = lambda n: pl.BlockSpec((1, n), lambda bi, h, i, j: (0, 0))
    return pl.pallas_call(
        functools.partial(_diff_flash_kernel, tq=tq, tk=tk, near=near, lambda_init=lambda_init),
        grid=(b, hw // gw, s // tq, s // tk),
        in_specs=[qspec, qspec, kspec, kspec,
                  pl.BlockSpec((hps, 1, 2 * BIAS_HALF), lambda bi, h, i, j: (h, 0, 0)),
                  vec(D_HEAD), vec(D_HEAD), vec(D_HEAD), vec(D_HEAD), vec(D_V)],
        out_specs=qspec,
        out_shape=jax.ShapeDtypeStruct((b, s, hw), BF16),
        scratch_shapes=[pltpu.VMEM((hps, 2, tq, LANES), F32), pltpu.VMEM((hps, 2, tq, 2 * LANES), F32),
                        pltpu.VMEM((hps, tq, tk), F32)],
        compiler_params=_params("parallel", "parallel", "parallel", "arbitrary"),
        name="diff_flash",
    )(q0, q1, k, v, tab, lq1, lk1, lq2, lk2, subln_g)


def _t5_bucket(rel):
    half = N_BUCKETS // 2
    max_exact = half // 2
    n = jnp.abs(rel)
    large = max_exact + (jnp.log(jnp.maximum(n, 1).astype(jnp.float32) / max_exact)
                         / math.log(MAX_DISTANCE / max_exact) * (half - max_exact)).astype(jnp.int32)
    large = jnp.minimum(large, half - 1)
    return jnp.where(rel > 0, half, 0) + jnp.where(n < max_exact, n, large)


def _rope_tables(s):
    inv = 1.0 / (ROPE_THETA ** (jnp.arange(0, C_ROPE, 2, dtype=F32) / C_ROPE))
    ang = jnp.arange(s, dtype=F32)[:, None] * inv[None, :]
    cos, sin = jnp.cos(ang), jnp.sin(ang)
    pad = LANES - C_NOPE - C_ROPE
    cos_t = jnp.concatenate([jnp.ones((s, C_NOPE), F32), cos, cos, jnp.zeros((s, pad), F32)], axis=-1)
    sin_t = jnp.concatenate([jnp.zeros((s, C_NOPE), F32), sin, sin, jnp.zeros((s, pad), F32)], axis=-1)
    return cos_t, sin_t


def _rot_half_cols(w):
    h = w.shape[-1] // 2
    return jnp.concatenate([-w[..., h:], w[..., :h]], axis=-1)


def _pack_even(j, even_w_in, even_w_out, rwkv_mu, rwkv_w0, rwkv_w_up, rwkv_a0, rwkv_a_up, rwkv_g_up, rwkv_k_k,
               rwkv_k_a, rwkv_r_k, rwkv_lnx_g, rwkv_lnx_b):
    w = A_WIDTH
    z = jnp.zeros((A_LORA, w), F32)
    blockdiag = lambda m: jnp.concatenate(
        [jnp.concatenate([m[0], z], axis=1), jnp.concatenate([z, m[1]], axis=1)], axis=0)
    head = jnp.arange(w) // A_HEAD
    return {
        "w_in_a": even_w_in[j][:, :A_COLS].astype(BF16),
        "w_in_b": even_w_in[j][:, A_COLS:].astype(BF16),
        "w_out_a": even_w_out[j][:w].astype(BF16),
        "w_out_b": even_w_out[j][w:].astype(BF16),
        "mu": rwkv_mu[j].reshape(1, A_COLS),
        "w0": rwkv_w0[j].reshape(1, 2 * w),
        "wup": blockdiag(rwkv_w_up[j]).astype(BF16),
        "a0": rwkv_a0[j].reshape(1, 2 * w),
        "aup": blockdiag(rwkv_a_up[j]).astype(BF16),
        "gup": rwkv_g_up[j].astype(BF16),
        "k_k": rwkv_k_k[j].reshape(1, w),
        "k_a": rwkv_k_a[j].reshape(1, w),
        "r_k": rwkv_r_k[j].reshape(1, w),
        "lnx_g": rwkv_lnx_g[j].reshape(1, w),
        "lnx_b": rwkv_lnx_b[j].reshape(1, w),
        "ones_bd": (head[:, None] == head[None, :]).astype(BF16),
    }


def _pack_odd(j, odd_w_in, odd_w_out, mla_q_norm_g, mla_kv_norm_g, mla_w_uq, mla_w_ukv):
    d = D_MODEL
    w_in = odd_w_in[j]
    o_cq, o_ckv = 0, C_Q_RANK
    o_kr = o_ckv + C_KV_RANK
    o_dq = o_kr + C_ROPE
    n_d = D_HEADS * 2 * D_HEAD
    w_kr = w_in[:, o_kr:o_kr + C_ROPE]
    packed = jnp.concatenate([
        w_in[:, o_dq:o_dq + 3 * n_d], w_in[:, o_cq:o_kr], w_kr, _rot_half_cols(w_kr),
        jnp.zeros((d, ODD_COLS - (C_Q_RANK + C_KV_RANK + 3 * n_d + 2 * C_ROPE)), F32)], axis=1)
    pad = LANES - C_NOPE - C_ROPE
    wq = mla_w_uq[j].reshape(C_Q_RANK, C_HEADS, C_NOPE + C_ROPE)
    zq = jnp.zeros((C_Q_RANK, C_HEADS, pad), F32)
    wqa = jnp.concatenate([wq, zq], axis=-1).reshape(C_Q_RANK, C_HEADS * LANES)
    wqb = jnp.concatenate([jnp.zeros((C_Q_RANK, C_HEADS, C_NOPE), F32), _rot_half_cols(wq[..., C_NOPE:]), zq],
                          axis=-1).reshape(C_Q_RANK, C_HEADS * LANES)
    wkv = mla_w_ukv[j].reshape(C_KV_RANK, C_HEADS, C_NOPE + C_V)
    wk = jnp.concatenate([wkv[..., :C_NOPE], jnp.zeros((C_KV_RANK, C_HEADS, LANES - C_NOPE), F32)],
                         axis=-1).reshape(C_KV_RANK, C_HEADS * LANES)
    wv = wkv[..., C_NOPE:].reshape(C_KV_RANK, C_HEADS * C_V)
    src = jnp.arange(LANES)[:, None]
    dst = jnp.arange(C_HEADS * LANES)[None, :] % LANES
    place = ((dst >= C_NOPE) & (dst < C_NOPE + C_ROPE) & (dst - C_NOPE == src)).astype(BF16)
    hv = C_HEADS * C_V
    return {
        "w_in": packed.astype(BF16),
        "w_out_a": odd_w_out[j][:hv].astype(BF16),
        "w_out_b": odd_w_out[j][hv:].astype(BF16),
        "q_norm_g": mla_q_norm_g[j].reshape(1, C_Q_RANK),
        "kv_norm_g": mla_kv_norm_g[j].reshape(1, C_KV_RANK),
        "wqa": wqa.astype(BF16), "wqb": wqb.astype(BF16), "wk": wk.astype(BF16), "wv": wv.astype(BF16),
        "place": place,
    }


def _even_mixers(x, g1n, sc1, sh1, ew, s5m, s5_d, s5_w_glu, s5_b_glu):
    b, s, _ = x.shape
    pa = _normmod_mm(x, g1n, sc1, sh1, ew["w_in_a"], tn=A_COLS // 3)
    u = _normmod_mm(x, g1n, sc1, sh1, ew["w_in_b"], tn=B_WIDTH)
    r, v, kk, lw, kd, bd, bv, g = _rwkv_prep(pa, ew)
    yf, yr = _rwkv_scan(r, v, kk, lw, kd, bd)
    ya = _rwkv_post(yf, yr, bv, g, ew["lnx_g"], ew["lnx_b"], ew["ones_bd"])
    nrows = s // S5_CHUNK
    u_g = jnp.transpose(u.astype(BF16).reshape(b, nrows, S5_CHUNK, B_GROUPS, B_GROUP), (0, 3, 1, 2, 4))
    u_g = u_g.reshape(b, B_GROUPS, nrows, S5_CHUNK * B_GROUP)
    ys_g = _s5_scan(u_g, s5m[0], False) + _s5_scan(u_g, s5m[1], True)
    ys = jnp.transpose(ys_g.reshape(b, B_GROUPS, nrows, S5_CHUNK, B_GROUP), (0, 2, 3, 1, 4)).reshape(b, s, B_WIDTH)
    yb = _s5_glu(ys, u, s5_d, s5_w_glu, s5_b_glu)
    return ya, yb


def _odd_mixers(x, g1n, sc1, sh1, ow, tabs, diff_w, lambda_init):
    s = x.shape[1]
    p = _normmod_mm(x, g1n, sc1, sh1, ow["w_in"], tn=1024)
    mq, mk, mv, q0, q1, dk, dv = _odd_prep(p, tabs["cos"][:s], tabs["sin"][:s], ow)
    yc = _mla_flash(mq, mk, mv)
    yd = _diff_flash(q0, q1, dk, dv, tabs["bias"], diff_w["lq1"], diff_w["lk1"], diff_w["lq2"], diff_w["lk2"],
                     diff_w["subln_g"], lambda_init)
    return yc, yd


def kernel(x_prompt, x_sample, c_prompt, c_sample, ada_w, ada_b, norm1_g, norm2_g, even_w_in, even_w_out, rwkv_mu, rwkv_w0, rwkv_w_up, rwkv_a0, rwkv_a_up, rwkv_g_up, rwkv_k_k, rwkv_k_a, rwkv_r_k, rwkv_lnx_g, rwkv_lnx_b, s5_lam_re, s5_lam_im, s5_log_step, s5_b_re, s5_b_im, s5_c_re, s5_c_im, s5_d, s5_w_glu, s5_b_glu, odd_w_in, odd_w_out, mla_q_norm_g, mla_kv_norm_g, mla_w_uq, mla_w_ukv, diff_lq1, diff_lk1, diff_lq2, diff_lk2, diff_subln_g, rel_bias, ffn_w_up, ffn_conv_w, ffn_conv_b, ffn_w_down, final_g):
    d = D_MODEL
    groups = [(x_prompt, c_prompt), (x_sample, c_sample)]
    nb = [g[0].shape[0] for g in groups]
    c_all = jnp.concatenate([g[1] for g in groups] + [jnp.zeros((SUBLANES - sum(nb), d), F32)], axis=0)
    mod = _ada_mod(c_all, ada_w, ada_b)

    max_s = max(g[0].shape[1] for g in groups)
    cos_t, sin_t = _rope_tables(max_s)
    rel = jnp.arange(-BIAS_HALF, BIAS_HALF, dtype=jnp.int32)
    bias_tab = (jnp.transpose(rel_bias.astype(F32)[_t5_bucket(rel)]) * LOG2E).reshape(D_HEADS, 1, 2 * BIAS_HALF)
    tabs = {"cos": cos_t, "sin": sin_t, "bias": bias_tab}

    xs = [g[0] for g in groups]
    for i in range(DEPTH):
        j = i // 2
        if i % 2 == 0:
            ew = _pack_even(j, even_w_in, even_w_out, rwkv_mu, rwkv_w0, rwkv_w_up, rwkv_a0, rwkv_a_up, rwkv_g_up,
                            rwkv_k_k, rwkv_k_a, rwkv_r_k, rwkv_lnx_g, rwkv_lnx_b)
            s5m = [_s5_matrices(s5_lam_re[j, dr], s5_lam_im[j, dr], s5_log_step[j, dr], s5_b_re[j, dr],
                                s5_b_im[j, dr], s5_c_re[j, dr], s5_c_im[j, dr], dr == 1) for dr in range(2)]
        else:
            ow = _pack_odd(j, odd_w_in, odd_w_out, mla_q_norm_g, mla_kv_norm_g, mla_w_uq, mla_w_ukv)
            diff_w = {"lq1": diff_lq1[j].reshape(1, D_HEAD), "lk1": diff_lk1[j].reshape(1, D_HEAD),
                      "lq2": diff_lq2[j].reshape(1, D_HEAD), "lk2": diff_lk2[j].reshape(1, D_HEAD),
                      "subln_g": diff_subln_g[j].reshape(1, D_V)}
        w_up = ffn_w_up[i].astype(BF16)
        w_down = ffn_w_down[i].astype(BF16)
        row0 = 0
        for gi in range(len(groups)):
            x = xs[gi]
            m = mod[i, row0:row0 + nb[gi]]
            row0 += nb[gi]
            sh1, sc1, g1, sh2, sc2, g2 = [m[:, None, k * d:(k + 1) * d] for k in range(N_MOD)]
            if i % 2 == 0:
                ya, yb = _even_mixers(x, norm1_g[i], sc1, sh1, ew, s5m, s5_d[j].reshape(1, B_WIDTH),
                                      s5_w_glu[j].astype(BF16), s5_b_glu[j].reshape(1, B_WIDTH))
                x = _out_proj(ya, yb, ew["w_out_a"], ew["w_out_b"], x, g1)
            else:
                yc, yd = _odd_mixers(x, norm1_g[i], sc1, sh1, ow, tabs, diff_w, 0.8 - 0.6 * math.exp(-0.3 * i))
                x = _out_proj(yc, yd, ow["w_out_a"], ow["w_out_b"], x, g1)
            x = _ffn(x, norm2_g[i], sc2, sh2, w_up, ffn_conv_w[i], ffn_conv_b[i].reshape(1, 2 * FFN_HIDDEN), w_down,
                     g2, final_g.reshape(1, d), final=(i == DEPTH - 1))
            xs[gi] = x
    return (xs[0], xs[1])
```

```python
import functools
import math

import jax
import jax.numpy as jnp
from jax import lax
from jax.experimental import pallas as pl
from jax.experimental.pallas import tpu as pltpu

F32 = jnp.float32
BF16 = jnp.bfloat16

D_MODEL = 2048
DEPTH = 2
EPS = 1e-6
A_WIDTH = 1024
A_HEAD = 64
A_HEADS = 16
A_LORA = 64
A_GATE_LORA = 128
A_COLS = 3 * A_WIDTH + 2 * A_LORA + 2 * A_LORA + A_GATE_LORA
RWKV_GN_EPS = 64e-5
DECAY_SCALE = math.exp(-0.5)
B_WIDTH = 1024
B_GROUP = 16
B_GROUPS = 64
B_STATE = 64
C_HEADS = 8
C_NOPE = 64
C_ROPE = 32
C_V = 128
C_Q_RANK = 512
C_KV_RANK = 256
ROPE_THETA = 10000.0
D_HEADS = 8
D_HEAD = 64
D_V = 128
SUBLN_EPS = 1e-5
N_BUCKETS = 32
MAX_DISTANCE = 128
FFN_HIDDEN = 5632
N_MOD = 6

LANES = 128
SUBLANES = 8
VMEM_LIMIT_BYTES = 56 * 1024 * 1024

MM_TOKENS = 1024
FFN_TOKENS = 1024
ATT_TQ = 1024
ATT_TK = 1024

RWKV_CHUNK = 64
S5_CHUNK = 16
LOG2E = 1.4426950408889634

NN = (((1,), (0,)), ((), ()))
NT = (((1,), (1,)), ((), ()))
BATCH_NN = (((2,), (1,)), ((0,), (0,)))


def _params(*sem):
    return pltpu.CompilerParams(dimension_semantics=sem, vmem_limit_bytes=VMEM_LIMIT_BYTES)


def _dot(a, b, dims=NN):
    return lax.dot_general(a, b, dims, preferred_element_type=F32)


def _split2(x):
    hi = x.astype(BF16)
    lo = (x - hi.astype(F32)).astype(BF16)
    return hi, lo


def _dot3(a, b, dims=NN):
    ah, al = _split2(a)
    bh, bl = _split2(b)
    return _dot(ah, bh, dims) + _dot(ah, bl, dims) + _dot(al, bh, dims)


def _dot_exact_rhs(a, b):
    ah, al = _split2(a)
    return _dot(ah, b) + _dot(al, b)


def _ada_kernel(c_ref, w_ref, b_ref, o_ref):
    c = c_ref[...]
    cs = c * jax.nn.sigmoid(c)
    o_ref[0] = _dot(cs.astype(BF16), w_ref[0].astype(BF16)) + b_ref[0]


def _ada_mod(c_all, ada_w, ada_b):
    n = N_MOD * D_MODEL
    tn = 1024
    return pl.pallas_call(
        _ada_kernel,
        grid=(DEPTH, n // tn),
        in_specs=[
            pl.BlockSpec((SUBLANES, D_MODEL), lambda l, j: (0, 0)),
            pl.BlockSpec((1, D_MODEL, tn), lambda l, j: (l, 0, j)),
            pl.BlockSpec((1, 1, tn), lambda l, j: (l, 0, j)),
        ],
        out_specs=pl.BlockSpec((1, SUBLANES, tn), lambda l, j: (l, 0, j)),
        out_shape=jax.ShapeDtypeStruct((DEPTH, SUBLANES, n), F32),
        compiler_params=_params("parallel", "parallel"),
        name="ada_mod",
    )(c_all, ada_w, ada_b.reshape(DEPTH, 1, n))


def _modnorm(x, g, sc, sh):
    y = x * lax.rsqrt(jnp.mean(x * x, axis=-1, keepdims=True) + EPS)
    return (y * g) * (1.0 + sc) + sh


def _normmod_mm_kernel(x_ref, g_ref, sc_ref, sh_ref, w_ref, o_ref, h_scr):
    @pl.when(pl.program_id(2) == 0)
    def _():
        h_scr[...] = _modnorm(x_ref[0], g_ref[...], sc_ref[0], sh_ref[0]).astype(BF16)

    o_ref[0] = _dot(h_scr[...], w_ref[...]).astype(o_ref.dtype)


def _normmod_mm(x, g, sc, sh, w, tn, out_dtype=F32):
    b, s, d = x.shape
    n = w.shape[1]
    tm = min(MM_TOKENS, s)
    return pl.pallas_call(
        _normmod_mm_kernel,
        grid=(b, s // tm, n // tn),
        in_specs=[
            pl.BlockSpec((1, tm, d), lambda bi, i, j: (bi, i, 0)),
            pl.BlockSpec((1, d), lambda bi, i, j: (0, 0)),
            pl.BlockSpec((1, 1, d), lambda bi, i, j: (bi, 0, 0)),
            pl.BlockSpec((1, 1, d), lambda bi, i, j: (bi, 0, 0)),
            pl.BlockSpec((d, tn), lambda bi, i, j: (0, j)),
        ],
        out_specs=pl.BlockSpec((1, tm, tn), lambda bi, i, j: (bi, i, j)),
        out_shape=jax.ShapeDtypeStruct((b, s, n), out_dtype),
        scratch_shapes=[pltpu.VMEM((tm, d), BF16)],
        compiler_params=_params("parallel", "parallel", "arbitrary"),
        name="normmod_mm",
    )(x, g.reshape(1, d), sc, sh, w)


def _rwkv_prep_kernel(p_ref, pp_ref, pn_ref, mu_ref, w0_ref, wup_ref, a0_ref, aup_ref, gup_ref, kk_ref, ka_ref,
                      rk_ref, ones_ref, r_out, v_out, kkn_out, lw_out, kd_out, bd_out, bv_out, g_out):
    i = pl.program_id(1)
    last = pl.num_programs(1) - 1
    pa = p_ref[0]
    tm = pa.shape[0]
    row = lax.broadcasted_iota(jnp.int32, (tm, 1), 0)
    prev_row = jnp.where(i == 0, 0.0, pp_ref[0][SUBLANES - 1:SUBLANES, :])
    next_row = jnp.where(i == last, 0.0, pn_ref[0][0:1, :])
    p_prev = jnp.where(row == 0, prev_row, pltpu.roll(pa, 1, 0))
    p_next = jnp.where(row == tm - 1, next_row, pltpu.roll(pa, tm - 1, 0))
    pa = pa + mu_ref[...] * (0.5 * (p_prev + p_next) - pa)

    w = A_WIDTH
    r = pa[:, 0:w]
    k = pa[:, w:2 * w]
    v = pa[:, 2 * w:3 * w]
    dw = pa[:, 3 * w:3 * w + 2 * A_LORA]
    da = pa[:, 3 * w + 2 * A_LORA:3 * w + 4 * A_LORA]
    dg = pa[:, 3 * w + 4 * A_LORA:A_COLS]

    lw = -DECAY_SCALE * jax.nn.sigmoid(w0_ref[...] + _dot(jnp.tanh(dw).astype(BF16), wup_ref[...]))
    icl = jax.nn.sigmoid(a0_ref[...] + _dot(da.astype(BF16), aup_ref[...]))
    g = _dot(jax.nn.sigmoid(dg).astype(BF16), gup_ref[...])

    ones_bd = ones_ref[...]
    kkr = k * kk_ref[...]
    ss = _dot_exact_rhs(kkr * kkr, ones_bd)
    kkn = kkr / jnp.maximum(jnp.sqrt(ss), 1e-12)

    r_out[0] = r
    v_out[0] = v
    kkn_out[0] = kkn
    lw_out[0] = lw
    g_out[0] = g
    bonus = jnp.zeros_like(r)
    for d in range(2):
        icl_d = icl[:, d * w:(d + 1) * w]
        k_d = k * (1.0 + (icl_d - 1.0) * ka_ref[...])
        kd_out[0, :, d * w:(d + 1) * w] = k_d
        bd_out[0, :, d * w:(d + 1) * w] = icl_d * kkn
        bonus = bonus + _dot_exact_rhs(r * k_d * rk_ref[...], ones_bd)
    bv_out[0] = bonus * v


def _rwkv_prep(pa, wts):
    b, s, _ = pa.shape
    tm = min(256, s)
    nh = tm // SUBLANES
    w = A_WIDTH
    full = lambda shape: pl.BlockSpec(shape, lambda bi, i: (0,) * len(shape))
    tok = lambda n: pl.BlockSpec((1, tm, n), lambda bi, i: (bi, i, 0))
    out_shapes = [jax.ShapeDtypeStruct((b, s, n), F32) for n in (w, w, w, 2 * w, 2 * w, 2 * w, w, w)]
    return pl.pallas_call(
        _rwkv_prep_kernel,
        grid=(b, s // tm),
        in_specs=[
            tok(A_COLS),
            pl.BlockSpec((1, SUBLANES, A_COLS), lambda bi, i: (bi, jnp.maximum(i * nh - 1, 0), 0)),
            pl.BlockSpec((1, SUBLANES, A_COLS), lambda bi, i: (bi, jnp.minimum((i + 1) * nh, s // SUBLANES - 1), 0)),
            full((1, A_COLS)), full((1, 2 * w)), full((2 * A_LORA, 2 * w)), full((1, 2 * w)),
            full((2 * A_LORA, 2 * w)), full((A_GATE_LORA, w)), full((1, w)), full((1, w)), full((1, w)),
            full((w, w)),
        ],
        out_specs=[tok(sd.shape[-1]) for sd in out_shapes],
        out_shape=out_shapes,
        compiler_params=_params("parallel", "parallel"),
        name="rwkv_prep",
    )(pa, pa, pa, wts["mu"], wts["w0"], wts["wup"], wts["a0"], wts["aup"], wts["gup"], wts["k_k"], wts["k_a"],
      wts["r_k"], wts["ones_bd"])


def _rwkv_chunk(r, v, kk, lw, kd, bd, s2, reverse):
    t = RWKV_CHUNK
    assert t == A_HEAD
    sign = -1 if reverse else 1
    ri = lax.broadcasted_iota(jnp.int32, (t, t), 0)
    ci = lax.broadcasted_iota(jnp.int32, (t, t), 1)
    tri = jnp.where((ri - ci) * sign >= 0, 1.0, 0.0).astype(BF16)
    lw_hi = lw.astype(BF16)
    rem = lw - lw_hi.astype(F32)
    lw_mid = rem.astype(BF16)
    lw_lo = (rem - lw_mid.astype(F32)).astype(BF16)
    cum = _dot(tri, lw_hi) + _dot(tri, lw_mid) + _dot(tri, lw_lo)
    yield
    tot = jnp.sum(lw, axis=0, keepdims=True)
    gam = jnp.exp(cum)
    gam_ex = jnp.exp(cum - lw)
    ginv = jnp.exp(-cum)
    gend = jnp.exp(tot - cum)
    gtot = jnp.exp(tot)

    width = r.shape[1]
    heads = width // A_HEAD
    lane_head = lax.broadcasted_iota(jnp.int32, (1, width), 1) // A_HEAD

    def stack(x):
        return jnp.concatenate([jnp.where(lane_head == h, x, 0.0) for h in range(heads)], axis=0).astype(BF16)

    def mm(x_wide, y_stack):
        return _dot(x_wide.astype(BF16), y_stack)

    ti = lax.broadcasted_iota(jnp.int32, (t, width), 0)
    si = lax.broadcasted_iota(jnp.int32, (t, width), 1) % t
    order = (ti - si) * sign
    incl = order >= 0
    strict = order > 0

    q_rk = jnp.concatenate([r * gam, kk * gam_ex], axis=0).astype(BF16)
    a = _dot(q_rk, jnp.concatenate([stack(kd * ginv), stack(bd * ginv)], axis=0), NT)
    yield
    n = heads * t
    a_rk = jnp.where(incl, a[:t, :n], 0.0)
    a_rb = jnp.where(incl, a[:t, n:], 0.0)
    a_kk = jnp.where(strict, a[t:, :n], 0.0)
    a_bk = jnp.where(strict, a[t:, n:], 0.0)

    qs = _dot(q_rk, s2.astype(BF16), NT)
    yield
    akv = mm(jnp.concatenate([a_rk, a_kk], axis=0), stack(v))
    yield

    s_n = stack(a_bk)
    n2 = mm(a_bk, s_n)
    yield
    n3 = mm(n2, s_n)
    yield
    n4 = mm(n2, stack(n2))
    yield
    s_n4 = stack(n4)
    n8 = mm(n4, s_n4)
    yield
    n12 = mm(n8, s_n4)
    yield
    f1 = jnp.where(ti == si, 1.0, 0.0) - a_bk + n2 - n3
    n16 = mm(n8, stack(n8))
    yield
    f12 = f1 + mm(f1, stack(n4 + n8 + n12))
    yield
    s_n16 = stack(n16)
    n32 = mm(n16, s_n16)
    yield
    n48 = mm(n32, s_n16)
    yield
    inv = f12 + mm(f12, stack(n16 + n32 + n48))
    yield

    ps = mm(inv, stack(-qs[t:] - akv[t:]))
    yield
    y = qs[:t] + akv[:t] + mm(a_rb, stack(ps))
    yield
    upd = _dot(jnp.concatenate([v, ps], axis=0).T.astype(BF16),
               jnp.concatenate([kd * gend, bd * gend], axis=0).astype(BF16))
    row_head = lax.broadcasted_iota(jnp.int32, (width, 1), 0) // A_HEAD
    s2_new = s2 * gtot + jnp.where(row_head == lane_head, upd, 0.0)
    return y, s2_new


def _run_lockstep(gens):
    results = [None] * len(gens)
    active = list(range(len(gens)))
    while active:
        for i in list(active):
            try:
                next(gens[i])
            except StopIteration as e:
                results[i] = e.value
                active.remove(i)
    return results


RWKV_CHAIN_LANES = 256
RWKV_CHAINS_PER_STEP = 4


def _rwkv_scan_kernel(rf_ref, vf_ref, kkf_ref, lwf_ref, kdf_ref, bdf_ref,
                      rb_ref, vb_ref, kkb_ref, lwb_ref, kdb_ref, bdb_ref, yf_ref, yb_ref, s_scr):
    @pl.when(pl.program_id(2) == 0)
    def _():
        s_scr[...] = jnp.zeros_like(s_scr)

    dirs = ((rf_ref, vf_ref, kkf_ref, lwf_ref, kdf_ref, bdf_ref, yf_ref),
            (rb_ref, vb_ref, kkb_ref, lwb_ref, kdb_ref, bdb_ref, yb_ref))
    chains = []
    for d, (r_ref, v_ref, kk_ref, lw_ref, kd_ref, bd_ref, y_ref) in enumerate(dirs):
        for p in range(RWKV_CHAINS_PER_STEP):
            sl = slice(p * RWKV_CHAIN_LANES, (p + 1) * RWKV_CHAIN_LANES)
            chains.append((d, p, sl, y_ref, _rwkv_chunk(
                r_ref[0, :, sl], v_ref[0, :, sl], kk_ref[0, :, sl], lw_ref[0, :, sl], kd_ref[0, :, sl],
                bd_ref[0, :, sl], s_scr[d, p], reverse=(d == 1))))
    results = _run_lockstep([c[4] for c in chains])
    for (d, p, sl, y_ref, _), (y, s_new) in zip(chains, results):
        y_ref[0, :, sl] = y
        s_scr[d, p] = s_new


def _rwkv_scan(r, v, kk, lw, kd, bd):
    b, s, w = r.shape
    t = RWKV_CHUNK
    nc = s // t
    pw = RWKV_CHAINS_PER_STEP * RWKV_CHAIN_LANES
    ngrp = w // pw
    fwd = lambda off: pl.BlockSpec((1, t, pw), lambda bi, g, c: (bi, c, g + off))
    bwd = lambda off: pl.BlockSpec((1, t, pw), lambda bi, g, c: (bi, nc - 1 - c, g + off))
    return pl.pallas_call(
        _rwkv_scan_kernel,
        grid=(b, ngrp, nc),
        in_specs=[fwd(0)] * 6 + [bwd(0)] * 3 + [bwd(ngrp)] * 3,
        out_specs=[fwd(0), bwd(0)],
        out_shape=[jax.ShapeDtypeStruct((b, s, w), F32)] * 2,
        scratch_shapes=[pltpu.VMEM((2, RWKV_CHAINS_PER_STEP, RWKV_CHAIN_LANES, RWKV_CHAIN_LANES), F32)],
        compiler_params=_params("parallel", "parallel", "arbitrary"),
        name="rwkv_scan",
    )(r, v, kk, lw, kd, bd, r, v, kk, lw, kd, bd)


def _rwkv_post_kernel(yf_ref, yb_ref, bv_ref, g_ref, lng_ref, lnb_ref, ones_ref, o_ref):
    y = yf_ref[0] + yb_ref[0]
    ones_bd = ones_ref[...]
    mean = _dot_exact_rhs(y, ones_bd) * (1.0 / A_HEAD)
    yc = y - mean
    var = _dot_exact_rhs(yc * yc, ones_bd) * (1.0 / A_HEAD)
    yn = yc * lax.rsqrt(var + RWKV_GN_EPS) * lng_ref[...] + lnb_ref[...]
    o_ref[0] = ((yn + bv_ref[0]) * g_ref[0]).astype(o_ref.dtype)


def _rwkv_post(yf, yb, bv, g, lng, lnb, ones_bd):
    b, s, w = bv.shape
    tm = min(512, s)
    full = lambda shape: pl.BlockSpec(shape, lambda bi, i: (0,) * len(shape))
    tok = lambda n: pl.BlockSpec((1, tm, n), lambda bi, i: (bi, i, 0))
    return pl.pallas_call(
        _rwkv_post_kernel,
        grid=(b, s // tm),
        in_specs=[tok(w), tok(w), tok(w), tok(w), full((1, w)), full((1, w)), full((w, w))],
        out_specs=tok(w),
        out_shape=jax.ShapeDtypeStruct((b, s, w), BF16),
        compiler_params=_params("parallel", "parallel"),
        name="rwkv_post",
    )(yf, yb, bv, g, lng, lnb, ones_bd)


def _s5_kernel(u_ref, m_ref, w_ref, ws_ref, v_ref, a_ref, y_ref, x_scr, xs_scr, h_scr, carry_scr, *, rows, reverse):
    g = B_GROUPS

    @pl.when(pl.program_id(1) == 0)
    def _():
        carry_scr[...] = jnp.zeros_like(carry_scr)

    u = u_ref[0]
    x_scr[...] = lax.dot_general(u, w_ref[...], BATCH_NN, preferred_element_type=F32).reshape(g * rows, LANES)
    xs_scr[...] = lax.dot_general(u, ws_ref[...], BATCH_NN, preferred_element_type=F32).reshape(g * rows, LANES)
    a1 = a_ref[0]
    a2 = a_ref[1]
    a2s = a_ref[2]

    def step(i, carry):
        h, hs = carry
        r = (rows - 1 - i) if reverse else i
        idx = pl.ds(r, g, stride=rows)
        h_scr[idx, :] = h
        hn = a1 * h + a2 * hs + x_scr[idx, :]
        hsn = a1 * hs + a2s * h + xs_scr[idx, :]
        return hn, hsn

    h, hs = lax.fori_loop(0, rows, step, (carry_scr[0], carry_scr[1]))
    carry_scr[0] = h
    carry_scr[1] = hs
    hprev = h_scr[...].reshape(g, rows, LANES).astype(BF16)
    y_ref[0] = (lax.dot_general(u, m_ref[...], BATCH_NN, preferred_element_type=F32)
                + lax.dot_general(hprev, v_ref[...], BATCH_NN, preferred_element_type=F32))


def _s5_scan(u_g, mats, reverse):
    b, g, nrows, cw = u_g.shape
    rows = min(64, nrows)
    nsb = nrows // rows
    whole = pl.BlockSpec(memory_space=pltpu.VMEM)
    idx = (lambda bi, i: (bi, 0, nsb - 1 - i, 0)) if reverse else (lambda bi, i: (bi, 0, i, 0))
    return pl.pallas_call(
        functools.partial(_s5_kernel, rows=rows, reverse=reverse),
        grid=(b, nsb),
        in_specs=[pl.BlockSpec((1, g, rows, cw), idx), whole, whole, whole, whole, whole],
        out_specs=pl.BlockSpec((1, g, rows, cw), idx),
        out_shape=jax.ShapeDtypeStruct((b, g, nrows, cw), F32),
        scratch_shapes=[pltpu.VMEM((g * rows, LANES), F32), pltpu.VMEM((g * rows, LANES), F32),
                        pltpu.VMEM((g * rows, LANES), F32), pltpu.VMEM((2, g, LANES), F32)],
        compiler_params=_params("parallel", "arbitrary"),
        name="s5_scan_bwd" if reverse else "s5_scan_fwd",
    )(u_g, mats["m"], mats["w"], mats["ws"], mats["v"], mats["a"])


def _s5_matrices(lam_re, lam_im, log_step, b_re, b_im, c_re, c_im, reverse):
    hp = lax.Precision.HIGHEST
    t = S5_CHUNK
    g, p, c = B_GROUPS, B_STATE, B_GROUP
    lr, li = lam_re.astype(F32), lam_im.astype(F32)
    step = jnp.exp(log_step.astype(F32))[:, None]
    ar, ai = jnp.exp(lr * step) * jnp.cos(li * step), jnp.exp(lr * step) * jnp.sin(li * step)
    den = lr * lr + li * li
    nr, ni = ar - 1.0, ai
    fr, fi = (nr * lr + ni * li) / den, (ni * lr - nr * li) / den
    br, bi = b_re.astype(F32), b_im.astype(F32)
    bbr = fr[..., None] * br - fi[..., None] * bi
    bbi = fr[..., None] * bi + fi[..., None] * br
    cr, cim = c_re.astype(F32), c_im.astype(F32)
    taus = jnp.arange(t + 1, dtype=F32)[:, None, None]
    mag = jnp.exp(lr * step * taus)
    pr, pi = mag * jnp.cos(li * step * taus), mag * jnp.sin(li * step * taus)
    cpr = cr[None] * pr[:, :, None, :] - cim[None] * pi[:, :, None, :]
    cpi = cr[None] * pi[:, :, None, :] + cim[None] * pr[:, :, None, :]
    kern = (jnp.einsum("tgcp,gpd->tgcd", cpr, bbr, precision=hp)
            - jnp.einsum("tgcp,gpd->tgcd", cpi, bbi, precision=hp))
    s_idx = jnp.arange(t)[:, None]
    t_idx = jnp.arange(t)[None, :]
    lag = (s_idx - t_idx) if reverse else (t_idx - s_idx)
    kk = jnp.where((lag >= 0)[:, :, None, None, None], kern[jnp.clip(lag, 0, t)], 0.0)
    m = jnp.transpose(kk, (2, 0, 4, 1, 3)).reshape(g, t * c, t * c)
    e = jnp.arange(t) if reverse else (t - 1 - jnp.arange(t))
    pre, pie = pr[e], pi[e]
    wre = pre[..., None] * bbr[None] - pie[..., None] * bbi[None]
    wim = pre[..., None] * bbi[None] + pie[..., None] * bbr[None]
    wre = jnp.transpose(wre, (1, 0, 3, 2)).reshape(g, t * c, p)
    wim = jnp.transpose(wim, (1, 0, 3, 2)).reshape(g, t * c, p)
    w = jnp.concatenate([wre, wim], axis=-1)
    ws = jnp.concatenate([wim, wre], axis=-1)
    f = (t - jnp.arange(t)) if reverse else (jnp.arange(t) + 1)
    vre = jnp.transpose(cpr[f], (1, 3, 0, 2)).reshape(g, p, t * c)
    vim = jnp.transpose(-cpi[f], (1, 3, 0, 2)).reshape(g, p, t * c)
    v = jnp.concatenate([vre, vim], axis=1)
    atr, ati = pr[t], pi[t]
    a = jnp.stack([jnp.concatenate([atr, atr], -1), jnp.concatenate([-ati, ati], -1),
                   jnp.concatenate([ati, -ati], -1)])
    return {"m": m.astype(BF16), "w": w.astype(BF16), "ws": ws.astype(BF16), "v": v.astype(BF16), "a": a}


def _s5_glu_kernel(ys_ref, u_ref, d_ref, w_ref, b_ref, o_ref):
    y = ys_ref[0] + u_ref[0] * d_ref[...]
    z = jax.nn.gelu(y)
    gate = jax.nn.sigmoid(_dot(z.astype(BF16), w_ref[...]) + b_ref[...])
    o_ref[0] = (z * gate).astype(o_ref.dtype)


def _s5_glu(ys, u, d_skip, w_glu, b_glu):
    b, s, w = u.shape
    tm = min(512, s)
    full = lambda shape: pl.BlockSpec(shape, lambda bi, i: (0,) * len(shape))
    tok = pl.BlockSpec((1, tm, w), lambda bi, i: (bi, i, 0))
    return pl.pallas_call(
        _s5_glu_kernel,
        grid=(b, s // tm),
        in_specs=[tok, tok, full((1, w)), full((w, w)), full((1, w))],
        out_specs=tok,
        out_shape=jax.ShapeDtypeStruct((b, s, w), BF16),
        compiler_params=_params("parallel", "parallel"),
        name="s5_glu",
    )(ys, u, d_skip, w_glu, b_glu)


def _out_proj_kernel(a_ref, b_ref, wa_ref, wb_ref, x_ref, g_ref, o_ref):
    mix = _dot(a_ref[0], wa_ref[...]) + _dot(b_ref[0], wb_ref[...])
    o_ref[0] = x_ref[0] + g_ref[0] * mix


def _out_proj(a, bb, wa, wb, x, gate):
    b, s, d = x.shape
    k = a.shape[-1]
    tm = min(MM_TOKENS, s)
    tn = 1024
    return pl.pallas_call(
        _out_proj_kernel,
        grid=(b, s // tm, d // tn),
        in_specs=[
            pl.BlockSpec((1, tm, k), lambda bi, i, j: (bi, i, 0)),
            pl.BlockSpec((1, tm, k), lambda bi, i, j: (bi, i, 0)),
            pl.BlockSpec((k, tn), lambda bi, i, j: (0, j)),
            pl.BlockSpec((k, tn), lambda bi, i, j: (0, j)),
            pl.BlockSpec((1, tm, tn), lambda bi, i, j: (bi, i, j)),
            pl.BlockSpec((1, 1, tn), lambda bi, i, j: (bi, 0, j)),
        ],
        out_specs=pl.BlockSpec((1, tm, tn), lambda bi, i, j: (bi, i, j)),
        out_shape=jax.ShapeDtypeStruct((b, s, d), F32),
        compiler_params=_params("parallel", "parallel", "parallel"),
        name="out_proj",
    )(a, bb, wa, wb, x, gate)


FFN_HALO = 2 * SUBLANES


def _ffn_kernel(x_ref, xp_ref, xn_ref, ng_ref, sc_ref, sh_ref, wv_ref, wg_ref, cwv_ref, cwg_ref, cbv_ref, cbg_ref,
                wd_ref, g_ref, fg_ref, o_ref, h_scr, *, final):
    i = pl.program_id(1)
    j = pl.program_id(2)
    tm = x_ref.shape[1]
    hl = FFN_HALO

    @pl.when(j == 0)
    def _():
        g, sc, sh = ng_ref[...], sc_ref[0], sh_ref[0]
        h_scr[hl:hl + tm] = _modnorm(x_ref[0], g, sc, sh).astype(BF16)
        before = jnp.where(i == 0, 0.0, _modnorm(xp_ref[0], g, sc, sh))
        after = jnp.where(i == pl.num_programs(1) - 1, 0.0, _modnorm(xn_ref[0], g, sc, sh))
        h_scr[0:hl] = before.astype(BF16)
        h_scr[hl + tm:2 * hl + tm] = after.astype(BF16)
        o_ref[...] = jnp.zeros_like(o_ref)

    h = h_scr[...]
    rows = tm + 2 * hl

    def conv(w_ref, cw_ref, cb_ref):
        u = _dot(h, w_ref[...])
        cw = cw_ref[...]
        u_prev = pltpu.roll(u, 1, 0)[hl:hl + tm]
        u_next = pltpu.roll(u, rows - 1, 0)[hl:hl + tm]
        return u_prev * cw[0:1] + u[hl:hl + tm] * cw[1:2] + u_next * cw[2:3] + cb_ref[...]

    val = conv(wv_ref, cwv_ref, cbv_ref)
    gate = conv(wg_ref, cwg_ref, cbg_ref)
    act = (gate * jax.nn.sigmoid(gate)) * val
    o_ref[0] += _dot(act.astype(BF16), wd_ref[...])

    @pl.when(j == pl.num_programs(2) - 1)
    def _():
        xn = x_ref[0] + g_ref[0] * o_ref[0]
        if final:
            xn = xn * lax.rsqrt(jnp.mean(xn * xn, axis=-1, keepdims=True) + EPS) * fg_ref[...]
        o_ref[0] = xn


def _ffn(x, norm_g, sc, sh, w_up, conv_w, conv_b, w_down, gate, final_g, final):
    b, s, d = x.shape
    f = FFN_HIDDEN
    tm = min(FFN_TOKENS, s)
    tf = 512
    nf = f // tf
    nh = tm // FFN_HALO
    nhalo = s // FFN_HALO
    const = lambda shape: pl.BlockSpec(shape, lambda bi, i, j: (0,) * len(shape))
    per_b = pl.BlockSpec((1, 1, d), lambda bi, i, j: (bi, 0, 0))
    up = lambda off: pl.BlockSpec((d, tf), lambda bi, i, j: (0, j + off))
    cw = lambda off: pl.BlockSpec((3, tf), lambda bi, i, j: (0, j + off))
    cb = lambda off: pl.BlockSpec((1, tf), lambda bi, i, j: (0, j + off))
    once = pl.Buffered(1)
    return pl.pallas_call(
        functools.partial(_ffn_kernel, final=final),
        grid=(b, s // tm, nf),
        in_specs=[
            pl.BlockSpec((1, tm, d), lambda bi, i, j: (bi, i, 0)),
            pl.BlockSpec((1, FFN_HALO, d), lambda bi, i, j: (bi, jnp.maximum(i * nh - 1, 0), 0)),
            pl.BlockSpec((1, FFN_HALO, d), lambda bi, i, j: (bi, jnp.minimum((i + 1) * nh, nhalo - 1), 0)),
            const((1, d)), per_b, per_b,
            up(0), up(nf), cw(0), cw(nf), cb(0), cb(nf),
            pl.BlockSpec((tf, d), lambda bi, i, j: (j, 0)),
            per_b, const((1, d)),
        ],
        out_specs=pl.BlockSpec((1, tm, d), lambda bi, i, j: (bi, i, 0), pipeline_mode=once),
        out_shape=jax.ShapeDtypeStruct((b, s, d), F32),
        scratch_shapes=[pltpu.VMEM((tm + 2 * FFN_HALO, d), BF16)],
        compiler_params=_params("parallel", "parallel", "arbitrary"),
        name="ffn",
    )(x, x, x, norm_g.reshape(1, d), sc, sh, w_up, w_up, conv_w, conv_w, conv_b, conv_b, w_down, gate, final_g)


def _rms(x, g, eps):
    return x * lax.rsqrt(jnp.mean(x * x, axis=-1, keepdims=True) + eps) * g


def _odd_prep_kernel(cq_ref, ckv_ref, kr_ref, dq_ref, dk_ref, dv_ref, cos_ref, sin_ref, qg_ref, kvg_ref,
                     wqa_ref, wqb_ref, wk_ref, wv_ref, place_ref,
                     mq_out, mk_out, mv_out, q0_out, q1_out, dk_out, dv_out):
    cosq = cos_ref[...]
    sinq = sin_ref[...]
    qn = _rms(cq_ref[0], qg_ref[...], EPS).astype(BF16)
    qa = _dot(qn, wqa_ref[...])
    qb = _dot(qn, wqb_ref[...])
    mla_scale = (C_NOPE + C_ROPE) ** -0.5 * LOG2E
    for h in range(C_HEADS):
        sl = slice(h * LANES, (h + 1) * LANES)
        mq_out[0, :, sl] = ((qa[:, sl] * cosq + qb[:, sl] * sinq) * mla_scale).astype(BF16)
    kvn = _rms(ckv_ref[0], kvg_ref[...], EPS).astype(BF16)
    kr = kr_ref[0]
    cos_k = pltpu.roll(cosq, LANES - C_NOPE, 1)
    sin_k = pltpu.roll(sinq, LANES - C_NOPE, 1)
    partner = pltpu.roll(kr, LANES - C_ROPE, 1)
    lane = lax.broadcasted_iota(jnp.int32, (1, LANES), 1)
    kr_rope = jnp.where(lane < C_ROPE, kr * cos_k + partner * sin_k, 0.0)
    mk_out[0] = (_dot(kvn, wk_ref[...]) + _dot(kr_rope.astype(BF16), place_ref[...])).astype(BF16)
    mv_out[0] = _dot(kvn, wv_ref[...]).astype(BF16)
    dq = dq_ref[0] * (D_HEAD ** -0.5 * LOG2E)
    lane_w = lax.broadcasted_iota(jnp.int32, (1, dq.shape[1]), 1)
    first_map = (lane_w % LANES) < D_HEAD
    q0_out[0] = jnp.where(first_map, dq, 0.0).astype(BF16)
    q1_out[0] = jnp.where(first_map, 0.0, dq).astype(BF16)
    dk_out[0] = dk_ref[0].astype(BF16)
    dv_out[0] = dv_ref[0].astype(BF16)


ODD_DQ = 0
ODD_DK = ODD_DQ + D_HEADS * 2 * D_HEAD
ODD_DV = ODD_DK + D_HEADS * 2 * D_HEAD
ODD_CQ = ODD_DV + D_HEADS * D_V
ODD_CKV = ODD_CQ + C_Q_RANK
ODD_KR = ODD_CKV + C_KV_RANK
ODD_COLS = 4096


def _odd_prep(p, cos_t, sin_t, wts):
    b, s, _ = p.shape
    tm = min(256, s)
    hw = C_HEADS * LANES
    full = lambda shape: pl.BlockSpec(shape, lambda bi, i: (0,) * len(shape))
    col = lambda off, n: pl.BlockSpec((1, tm, n), lambda bi, i: (bi, i, off // n))
    tok = pl.BlockSpec((1, tm, hw), lambda bi, i: (bi, i, 0))
    tab = pl.BlockSpec((tm, LANES), lambda bi, i: (i, 0))
    outs = [jax.ShapeDtypeStruct((b, s, hw), BF16)] * 7
    return pl.pallas_call(
        _odd_prep_kernel,
        grid=(b, s // tm),
        in_specs=[
            col(ODD_CQ, C_Q_RANK), col(ODD_CKV, C_KV_RANK), col(ODD_KR, LANES),
            col(ODD_DQ, hw), col(ODD_DK, hw), col(ODD_DV, hw), tab, tab,
            full((1, C_Q_RANK)), full((1, C_KV_RANK)),
            full((C_Q_RANK, hw)), full((C_Q_RANK, hw)), full((C_KV_RANK, hw)), full((C_KV_RANK, hw)),
            full((LANES, hw)),
        ],
        out_specs=[tok] * 7,
        out_shape=outs,
        compiler_params=_params("parallel", "parallel"),
        name="odd_prep",
    )(p, p, p, p, p, p, cos_t, sin_t, wts["q_norm_g"], wts["kv_norm_g"], wts["wqa"], wts["wqb"], wts["wk"],
      wts["wv"], wts["place"])


def _with_ones(v):
    return jnp.concatenate([v, jnp.ones_like(v)], axis=1)


def _flash_chain(q, k, v1, bias, shift, m_scr, acc_scr):
    s = _dot(q, k, NT)
    yield
    if bias is not None:
        s = s + bias
    m_prev = m_scr[...]
    m_new = jnp.maximum(m_prev, jnp.max(s, axis=-1, keepdims=True) + shift)
    alpha = jnp.exp2(m_prev - m_new)
    p = jnp.exp2(s - jnp.tile(m_new - shift, (1, s.shape[1] // LANES))).astype(BF16)
    yield
    acc_scr[...] = jnp.tile(alpha, (1, 2)) * acc_scr[...] + _dot(p, v1)
    m_scr[...] = m_new


MLA_HEADS_PER_STEP = 4


def _mla_flash_kernel(q_ref, k_ref, v_ref, o_ref, m_scr, acc_scr):
    kj = pl.program_id(3)

    @pl.when(kj == 0)
    def _():
        m_scr[...] = jnp.full_like(m_scr, -jnp.inf)
        acc_scr[...] = jnp.zeros_like(acc_scr)

    heads = [slice(h * LANES, (h + 1) * LANES) for h in range(MLA_HEADS_PER_STEP)]
    _run_lockstep([_flash_chain(q_ref[0, :, sl], k_ref[0, :, sl], _with_ones(v_ref[0, :, sl]), None, 0.0,
                                m_scr.at[h], acc_scr.at[h]) for h, sl in enumerate(heads)])

    @pl.when(kj == pl.num_programs(3) - 1)
    def _():
        for h, sl in enumerate(heads):
            acc = acc_scr[h]
            o_ref[0, :, sl] = (acc[:, :LANES] / acc[:, LANES:]).astype(o_ref.dtype)


def _mla_flash(q, k, v):
    b, s, hw = q.shape
    gw = MLA_HEADS_PER_STEP * LANES
    tq = min(ATT_TQ, s)
    tk = min(ATT_TK, s)
    qspec = pl.BlockSpec((1, tq, gw), lambda bi, h, i, j: (bi, i, h))
    kspec = pl.BlockSpec((1, tk, gw), lambda bi, h, i, j: (bi, j, h))
    return pl.pallas_call(
        _mla_flash_kernel,
        grid=(b, hw // gw, s // tq, s // tk),
        in_specs=[qspec, kspec, kspec],
        out_specs=qspec,
        out_shape=jax.ShapeDtypeStruct((b, s, hw), BF16),
        scratch_shapes=[pltpu.VMEM((MLA_HEADS_PER_STEP, tq, LANES), F32),
                        pltpu.VMEM((MLA_HEADS_PER_STEP, tq, 2 * LANES), F32)],
        compiler_params=_params("parallel", "parallel", "parallel", "arbitrary"),
        name="mla_flash",
    )(q, k, v)


BIAS_HALF = 256


DIFF_HEADS_PER_STEP = 2


def _diff_flash_kernel(q0_ref, q1_ref, k_ref, v_ref, tab_ref, lq1_ref, lk1_ref, lq2_ref, lk2_ref, sg_ref, o_ref,
                       m_scr, acc_scr, bias_scr, *, tq, tk, near, lambda_init):
    kj = pl.program_id(3)
    off = kj * tk - pl.program_id(2) * tq
    heads = [slice(h * LANES, (h + 1) * LANES) for h in range(DIFF_HEADS_PER_STEP)]

    @pl.when(kj == 0)
    def _():
        m_scr[...] = jnp.full_like(m_scr, -jnp.inf)
        acc_scr[...] = jnp.zeros_like(acc_scr)

    tabs = [tab_ref[h] for h in range(DIFF_HEADS_PER_STEP)]
    far_left = [tab[:, 0:1] for tab in tabs]
    far_right = [tab[:, 2 * BIAS_HALF - 1:2 * BIAS_HALF] for tab in tabs]

    def update(biases, shifts):
        chains = []
        for h, sl in enumerate(heads):
            k = k_ref[0, :, sl]
            v1 = _with_ones(v_ref[0, :, sl])
            for m, q_ref in enumerate((q0_ref, q1_ref)):
                chains.append(_flash_chain(q_ref[0, :, sl], k, v1, biases[h], shifts[h], m_scr.at[h, m],
                                           acc_scr.at[h, m]))
        _run_lockstep(chains)

    def toeplitz(tab, d):
        r = tab[:, d + LANES:d + 3 * LANES]
        rows = jnp.broadcast_to(r, (LANES, 2 * LANES))
        return pltpu.roll(rows, LANES, 1, stride=1, stride_axis=0)[:, :LANES]

    for d0 in near:
        @pl.when(off == d0)
        def _(d0=d0):
            for h in range(DIFF_HEADS_PER_STEP):
                blocks = {}
                for ri in range(tq // LANES):
                    for cj in range(tk // LANES):
                        d = d0 + (cj - ri) * LANES
                        sl = (h, slice(ri * LANES, (ri + 1) * LANES), slice(cj * LANES, (cj + 1) * LANES))
                        if d <= -BIAS_HALF:
                            bias_scr[sl] = jnp.broadcast_to(far_left[h], (LANES, LANES))
                        elif d >= BIAS_HALF:
                            bias_scr[sl] = jnp.broadcast_to(far_right[h], (LANES, LANES))
                        else:
                            if d not in blocks:
                                blocks[d] = toeplitz(tabs[h], d)
                            bias_scr[sl] = blocks[d]

    is_near = jnp.logical_and(off >= near[0], off <= near[-1])

    @pl.when(is_near)
    def _():
        update([bias_scr[h] for h in range(DIFF_HEADS_PER_STEP)], [0.0] * DIFF_HEADS_PER_STEP)

    @pl.when(jnp.logical_not(is_near))
    def _():
        update([None] * DIFF_HEADS_PER_STEP,
               [jnp.where(off < near[0], lo, hi) for lo, hi in zip(far_left, far_right)])

    @pl.when(kj == pl.num_programs(3) - 1)
    def _():
        lam = (jnp.exp(jnp.sum(lq1_ref[...] * lk1_ref[...], axis=-1, keepdims=True))
               - jnp.exp(jnp.sum(lq2_ref[...] * lk2_ref[...], axis=-1, keepdims=True)) + lambda_init)
        for h, sl in enumerate(heads):
            a0 = acc_scr[h, 0]
            a1 = acc_scr[h, 1]
            o = a0[:, :LANES] / a0[:, LANES:] - lam * (a1[:, :LANES] / a1[:, LANES:])
            o = _rms(o, sg_ref[...], SUBLN_EPS) * (1.0 - lambda_init)
            o_ref[0, :, sl] = o.astype(o_ref.dtype)


def _diff_flash(q0, q1, k, v, tab, lq1, lk1, lq2, lk2, subln_g, lambda_init):
    b, s, hw = q0.shape
    hps = DIFF_HEADS_PER_STEP
    gw = hps * LANES
    tq = min(ATT_TQ, s)
    tk = min(ATT_TK, s)
    offs = sorted({j * tk - i * tq for i in range(s // tq) for j in range(s // tk)})
    near = tuple(d for d in offs if d - (tq - 1) < BIAS_HALF and d + tk - 1 > -BIAS_HALF)
    assert near == tuple(d for d in offs if near[0] <= d <= near[-1])
    qspec = pl.BlockSpec((1, tq, gw), lambda bi, h, i, j: (bi, i, h))
    kspec = pl.BlockSpec((1, tk, gw), lambda bi, h, i, j: (bi, j, h))
    vec = lambda n: pl.BlockSpec((1, n), lambda bi, h, i, j: (0, 0))
    return pl.pallas_call(
        functools.partial(_diff_flash_kernel, tq=tq, tk=tk, near=near, lambda_init=lambda_init),
        grid=(b, hw // gw, s // tq, s // tk),
        in_specs=[qspec, qspec, kspec, kspec,
                  pl.BlockSpec((hps, 1, 2 * BIAS_HALF), lambda bi, h, i, j: (h, 0, 0)),
                  vec(D_HEAD), vec(D_HEAD), vec(D_HEAD), vec(D_HEAD), vec(D_V)],
        out_specs=qspec,
        out_shape=jax.ShapeDtypeStruct((b, s, hw), BF16),
        scratch_shapes=[pltpu.VMEM((hps, 2, tq, LANES), F32), pltpu.VMEM((hps, 2, tq, 2 * LANES), F32),
                        pltpu.VMEM((hps, tq, tk), F32)],
        compiler_params=_params("parallel", "parallel", "parallel", "arbitrary"),
        name="diff_flash",
    )(q0, q1, k, v, tab, lq1, lk1, lq2, lk2, subln_g)


def _t5_bucket(rel):
    half = N_BUCKETS // 2
    max_exact = half // 2
    n = jnp.abs(rel)
    large = max_exact + (jnp.log(jnp.maximum(n, 1).astype(jnp.float32) / max_exact)
                         / math.log(MAX_DISTANCE / max_exact) * (half - max_exact)).astype(jnp.int32)
    large = jnp.minimum(large, half - 1)
    return jnp.where(rel > 0, half, 0) + jnp.where(n < max_exact, n, large)


def _rope_tables(s):
    inv = 1.0 / (ROPE_THETA ** (jnp.arange(0, C_ROPE, 2, dtype=F32) / C_ROPE))
    ang = jnp.arange(s, dtype=F32)[:, None] * inv[None, :]
    cos, sin = jnp.cos(ang), jnp.sin(ang)
    pad = LANES - C_NOPE - C_ROPE
    cos_t = jnp.concatenate([jnp.ones((s, C_NOPE), F32), cos, cos, jnp.zeros((s, pad), F32)], axis=-1)
    sin_t = jnp.concatenate([jnp.zeros((s, C_NOPE), F32), sin, sin, jnp.zeros((s, pad), F32)], axis=-1)
    return cos_t, sin_t


def _rot_half_cols(w):
    h = w.shape[-1] // 2
    return jnp.concatenate([-w[..., h:], w[..., :h]], axis=-1)


def _pack_even(j, even_w_in, even_w_out, rwkv_mu, rwkv_w0, rwkv_w_up, rwkv_a0, rwkv_a_up, rwkv_g_up, rwkv_k_k,
               rwkv_k_a, rwkv_r_k, rwkv_lnx_g, rwkv_lnx_b):
    w = A_WIDTH
    z = jnp.zeros((A_LORA, w), F32)
    blockdiag = lambda m: jnp.concatenate(
        [jnp.concatenate([m[0], z], axis=1), jnp.concatenate([z, m[1]], axis=1)], axis=0)
    head = jnp.arange(w) // A_HEAD
    return {
        "w_in_a": even_w_in[j][:, :A_COLS].astype(BF16),
        "w_in_b": even_w_in[j][:, A_COLS:].astype(BF16),
        "w_out_a": even_w_out[j][:w].astype(BF16),
        "w_out_b": even_w_out[j][w:].astype(BF16),
        "mu": rwkv_mu[j].reshape(1, A_COLS),
        "w0": rwkv_w0[j].reshape(1, 2 * w),
        "wup": blockdiag(rwkv_w_up[j]).astype(BF16),
        "a0": rwkv_a0[j].reshape(1, 2 * w),
        "aup": blockdiag(rwkv_a_up[j]).astype(BF16),
        "gup": rwkv_g_up[j].astype(BF16),
        "k_k": rwkv_k_k[j].reshape(1, w),
        "k_a": rwkv_k_a[j].reshape(1, w),
        "r_k": rwkv_r_k[j].reshape(1, w),
        "lnx_g": rwkv_lnx_g[j].reshape(1, w),
        "lnx_b": rwkv_lnx_b[j].reshape(1, w),
        "ones_bd": (head[:, None] == head[None, :]).astype(BF16),
    }


def _pack_odd(j, odd_w_in, odd_w_out, mla_q_norm_g, mla_kv_norm_g, mla_w_uq, mla_w_ukv):
    d = D_MODEL
    w_in = odd_w_in[j]
    o_cq, o_ckv = 0, C_Q_RANK
    o_kr = o_ckv + C_KV_RANK
    o_dq = o_kr + C_ROPE
    n_d = D_HEADS * 2 * D_HEAD
    w_kr = w_in[:, o_kr:o_kr + C_ROPE]
    packed = jnp.concatenate([
        w_in[:, o_dq:o_dq + 3 * n_d], w_in[:, o_cq:o_kr], w_kr, _rot_half_cols(w_kr),
        jnp.zeros((d, ODD_COLS - (C_Q_RANK + C_KV_RANK + 3 * n_d + 2 * C_ROPE)), F32)], axis=1)
    pad = LANES - C_NOPE - C_ROPE
    wq = mla_w_uq[j].reshape(C_Q_RANK, C_HEADS, C_NOPE + C_ROPE)
    zq = jnp.zeros((C_Q_RANK, C_HEADS, pad), F32)
    wqa = jnp.concatenate([wq, zq], axis=-1).reshape(C_Q_RANK, C_HEADS * LANES)
    wqb = jnp.concatenate([jnp.zeros((C_Q_RANK, C_HEADS, C_NOPE), F32), _rot_half_cols(wq[..., C_NOPE:]), zq],
                          axis=-1).reshape(C_Q_RANK, C_HEADS * LANES)
    wkv = mla_w_ukv[j].reshape(C_KV_RANK, C_HEADS, C_NOPE + C_V)
    wk = jnp.concatenate([wkv[..., :C_NOPE], jnp.zeros((C_KV_RANK, C_HEADS, LANES - C_NOPE), F32)],
                         axis=-1).reshape(C_KV_RANK, C_HEADS * LANES)
    wv = wkv[..., C_NOPE:].reshape(C_KV_RANK, C_HEADS * C_V)
    src = jnp.arange(LANES)[:, None]
    dst = jnp.arange(C_HEADS * LANES)[None, :] % LANES
    place = ((dst >= C_NOPE) & (dst < C_NOPE + C_ROPE) & (dst - C_NOPE == src)).astype(BF16)
    hv = C_HEADS * C_V
    return {
        "w_in": packed.astype(BF16),
        "w_out_a": odd_w_out[j][:hv].astype(BF16),
        "w_out_b": odd_w_out[j][hv:].astype(BF16),
        "q_norm_g": mla_q_norm_g[j].reshape(1, C_Q_RANK),
        "kv_norm_g": mla_kv_norm_g[j].reshape(1, C_KV_RANK),
        "wqa": wqa.astype(BF16), "wqb": wqb.astype(BF16), "wk": wk.astype(BF16), "wv": wv.astype(BF16),
        "place": place,
    }


def _even_mixers(x, g1n, sc1, sh1, ew, s5m, s5_d, s5_w_glu, s5_b_glu):
    b, s, _ = x.shape
    pa = _normmod_mm(x, g1n, sc1, sh1, ew["w_in_a"], tn=A_COLS // 3)
    u = _normmod_mm(x, g1n, sc1, sh1, ew["w_in_b"], tn=B_WIDTH)
    r, v, kk, lw, kd, bd, bv, g = _rwkv_prep(pa, ew)
    yf, yr = _rwkv_scan(r, v, kk, lw, kd, bd)
    ya = _rwkv_post(yf, yr, bv, g, ew["lnx_g"], ew["lnx_b"], ew["ones_bd"])
    nrows = s // S5_CHUNK
    u_g = jnp.transpose(u.astype(BF16).reshape(b, nrows, S5_CHUNK, B_GROUPS, B_GROUP), (0, 3, 1, 2, 4))
    u_g = u_g.reshape(b, B_GROUPS, nrows, S5_CHUNK * B_GROUP)
    ys_g = _s5_scan(u_g, s5m[0], False) + _s5_scan(u_g, s5m[1], True)
    ys = jnp.transpose(ys_g.reshape(b, B_GROUPS, nrows, S5_CHUNK, B_GROUP), (0, 2, 3, 1, 4)).reshape(b, s, B_WIDTH)
    yb = _s5_glu(ys, u, s5_d, s5_w_glu, s5_b_glu)
    return ya, yb


def _odd_mixers(x, g1n, sc1, sh1, ow, tabs, diff_w, lambda_init):
    s = x.shape[1]
    p = _normmod_mm(x, g1n, sc1, sh1, ow["w_in"], tn=1024)
    mq, mk, mv, q0, q1, dk, dv = _odd_prep(p, tabs["cos"][:s], tabs["sin"][:s], ow)
    yc = _mla_flash(mq, mk, mv)
    yd = _diff_flash(q0, q1, dk, dv, tabs["bias"], diff_w["lq1"], diff_w["lk1"], diff_w["lq2"], diff_w["lk2"],
                     diff_w["subln_g"], lambda_init)
    return yc, yd


def kernel(x_prompt, x_sample, c_prompt, c_sample, ada_w, ada_b, norm1_g, norm2_g, even_w_in, even_w_out, rwkv_mu, rwkv_w0, rwkv_w_up, rwkv_a0, rwkv_a_up, rwkv_g_up, rwkv_k_k, rwkv_k_a, rwkv_r_k, rwkv_lnx_g, rwkv_lnx_b, s5_lam_re, s5_lam_im, s5_log_step, s5_b_re, s5_b_im, s5_c_re, s5_c_im, s5_d, s5_w_glu, s5_b_glu, odd_w_in, odd_w_out, mla_q_norm_g, mla_kv_norm_g, mla_w_uq, mla_w_ukv, diff_lq1, diff_lk1, diff_lq2, diff_lk2, diff_subln_g, rel_bias, ffn_w_up, ffn_conv_w, ffn_conv_b, ffn_w_down, final_g):
    d = D_MODEL
    groups = [(x_prompt, c_prompt), (x_sample, c_sample)]
    nb = [g[0].shape[0] for g in groups]
    c_all = jnp.concatenate([g[1] for g in groups] + [jnp.zeros((SUBLANES - sum(nb), d), F32)], axis=0)
    mod = _ada_mod(c_all, ada_w, ada_b)

    max_s = max(g[0].shape[1] for g in groups)
    cos_t, sin_t = _rope_tables(max_s)
    rel = jnp.arange(-BIAS_HALF, BIAS_HALF, dtype=jnp.int32)
    bias_tab = (jnp.transpose(rel_bias.astype(F32)[_t5_bucket(rel)]) * LOG2E).reshape(D_HEADS, 1, 2 * BIAS_HALF)
    tabs = {"cos": cos_t, "sin": sin_t, "bias": bias_tab}

    xs = [g[0] for g in groups]
    for i in range(DEPTH):
        j = i // 2
        if i % 2 == 0:
            ew = _pack_even(j, even_w_in, even_w_out, rwkv_mu, rwkv_w0, rwkv_w_up, rwkv_a0, rwkv_a_up, rwkv_g_up,
                            rwkv_k_k, rwkv_k_a, rwkv_r_k, rwkv_lnx_g, rwkv_lnx_b)
            s5m = [_s5_matrices(s5_lam_re[j, dr], s5_lam_im[j, dr], s5_log_step[j, dr], s5_b_re[j, dr],
                                s5_b_im[j, dr], s5_c_re[j, dr], s5_c_im[j, dr], dr == 1) for dr in range(2)]
        else:
            ow = _pack_odd(j, odd_w_in, odd_w_out, mla_q_norm_g, mla_kv_norm_g, mla_w_uq, mla_w_ukv)
            diff_w = {"lq1": diff_lq1[j].reshape(1, D_HEAD), "lk1": diff_lk1[j].reshape(1, D_HEAD),
                      "lq2": diff_lq2[j].reshape(1, D_HEAD), "lk2": diff_lk2[j].reshape(1, D_HEAD),
                      "subln_g": diff_subln_g[j].reshape(1, D_V)}
        w_up = ffn_w_up[i].astype(BF16)
        w_down = ffn_w_down[i].astype(BF16)
        row0 = 0
        for gi in range(len(groups)):
            x = xs[gi]
            m = mod[i, row0:row0 + nb[gi]]
            row0 += nb[gi]
            sh1, sc1, g1, sh2, sc2, g2 = [m[:, None, k * d:(k + 1) * d] for k in range(N_MOD)]
            if i % 2 == 0:
                ya, yb = _even_mixers(x, norm1_g[i], sc1, sh1, ew, s5m, s5_d[j].reshape(1, B_WIDTH),
                                      s5_w_glu[j].astype(BF16), s5_b_glu[j].reshape(1, B_WIDTH))
                x = _out_proj(ya, yb, ew["w_out_a"], ew["w_out_b"], x, g1)
            else:
                yc, yd = _odd_mixers(x, norm1_g[i], sc1, sh1, ow, tabs, diff_w, 0.8 - 0.6 * math.exp(-0.3 * i))
                x = _out_proj(yc, yd, ow["w_out_a"], ow["w_out_b"], x, g1)
            x = _ffn(x, norm2_g[i], sc2, sh2, w_up, ffn_conv_w[i], ffn_conv_b[i].reshape(1, 2 * FFN_HIDDEN), w_down,
                     g2, final_g.reshape(1, d), final=(i == DEPTH - 1))
            xs[gi] = x
    return (xs[0], xs[1])
```

```python
import functools
import math

import jax
import jax.numpy as jnp
from jax import lax
from jax.experimental import pallas as pl
from jax.experimental.pallas import tpu as pltpu

F32 = jnp.float32
BF16 = jnp.bfloat16

D_MODEL = 2048
DEPTH = 2
EPS = 1e-6
A_WIDTH = 1024
A_HEAD = 64
A_HEADS = 16
A_LORA = 64
A_GATE_LORA = 128
A_COLS = 3 * A_WIDTH + 2 * A_LORA + 2 * A_LORA + A_GATE_LORA
RWKV_GN_EPS = 64e-5
DECAY_SCALE = math.exp(-0.5)
B_WIDTH = 1024
B_GROUP = 16
B_GROUPS = 64
B_STATE = 64
C_HEADS = 8
C_NOPE = 64
C_ROPE = 32
C_V = 128
C_Q_RANK = 512
C_KV_RANK = 256
ROPE_THETA = 10000.0
D_HEADS = 8
D_HEAD = 64
D_V = 128
SUBLN_EPS = 1e-5
N_BUCKETS = 32
MAX_DISTANCE = 128
FFN_HIDDEN = 5632
N_MOD = 6

LANES = 128
SUBLANES = 8
VMEM_LIMIT_BYTES = 56 * 1024 * 1024

MM_TOKENS = 1024
FFN_TOKENS = 1024
ATT_TQ = 1024
ATT_TK = 1024

RWKV_CHUNK = 64
S5_CHUNK = 16
LOG2E = 1.4426950408889634

NN = (((1,), (0,)), ((), ()))
NT = (((1,), (1,)), ((), ()))
BATCH_NN = (((2,), (1,)), ((0,), (0,)))


def _params(*sem):
    return pltpu.CompilerParams(dimension_semantics=sem, vmem_limit_bytes=VMEM_LIMIT_BYTES)


def _dot(a, b, dims=NN):
    return lax.dot_general(a, b, dims, preferred_element_type=F32)


def _split2(x):
    hi = x.astype(BF16)
    lo = (x - hi.astype(F32)).astype(BF16)
    return hi, lo


def _dot3(a, b, dims=NN):
    ah, al = _split2(a)
    bh, bl = _split2(b)
    return _dot(ah, bh, dims) + _dot(ah, bl, dims) + _dot(al, bh, dims)


def _dot_exact_rhs(a, b):
    ah, al = _split2(a)
    return _dot(ah, b) + _dot(al, b)


def _ada_kernel(c_ref, w_ref, b_ref, o_ref):
    c = c_ref[...]
    cs = c * jax.nn.sigmoid(c)
    o_ref[0] = _dot(cs.astype(BF16), w_ref[0].astype(BF16)) + b_ref[0]


def _ada_mod(c_all, ada_w, ada_b):
    n = N_MOD * D_MODEL
    tn = 1024
    return pl.pallas_call(
        _ada_kernel,
        grid=(DEPTH, n // tn),
        in_specs=[
            pl.BlockSpec((SUBLANES, D_MODEL), lambda l, j: (0, 0)),
            pl.BlockSpec((1, D_MODEL, tn), lambda l, j: (l, 0, j)),
            pl.BlockSpec((1, 1, tn), lambda l, j: (l, 0, j)),
        ],
        out_specs=pl.BlockSpec((1, SUBLANES, tn), lambda l, j: (l, 0, j)),
        out_shape=jax.ShapeDtypeStruct((DEPTH, SUBLANES, n), F32),
        compiler_params=_params("parallel", "parallel"),
        name="ada_mod",
    )(c_all, ada_w, ada_b.reshape(DEPTH, 1, n))


def _modnorm(x, g, sc, sh):
    y = x * lax.rsqrt(jnp.mean(x * x, axis=-1, keepdims=True) + EPS)
    return (y * g) * (1.0 + sc) + sh


def _normmod_mm_kernel(x_ref, g_ref, sc_ref, sh_ref, w_ref, o_ref, h_scr):
    @pl.when(pl.program_id(2) == 0)
    def _():
        h_scr[...] = _modnorm(x_ref[0], g_ref[...], sc_ref[0], sh_ref[0]).astype(BF16)

    o_ref[0] = _dot(h_scr[...], w_ref[...]).astype(o_ref.dtype)


def _normmod_mm(x, g, sc, sh, w, tn, out_dtype=F32):
    b, s, d = x.shape
    n = w.shape[1]
    tm = min(MM_TOKENS, s)
    return pl.pallas_call(
        _normmod_mm_kernel,
        grid=(b, s // tm, n // tn),
        in_specs=[
            pl.BlockSpec((1, tm, d), lambda bi, i, j: (bi, i, 0)),
            pl.BlockSpec((1, d), lambda bi, i, j: (0, 0)),
            pl.BlockSpec((1, 1, d), lambda bi, i, j: (bi, 0, 0)),
            pl.BlockSpec((1, 1, d), lambda bi, i, j: (bi, 0, 0)),
            pl.BlockSpec((d, tn), lambda bi, i, j: (0, j)),
        ],
        out_specs=pl.BlockSpec((1, tm, tn), lambda bi, i, j: (bi, i, j)),
        out_shape=jax.ShapeDtypeStruct((b, s, n), out_dtype),
        scratch_shapes=[pltpu.VMEM((tm, d), BF16)],
        compiler_params=_params("parallel", "parallel", "arbitrary"),
        name="normmod_mm",
    )(x, g.reshape(1, d), sc, sh, w)


def _rwkv_prep_kernel(p_ref, pp_ref, pn_ref, mu_ref, w0_ref, wup_ref, a0_ref, aup_ref, gup_ref, kk_ref, ka_ref,
                      rk_ref, ones_ref, r_out, v_out, kkn_out, lw_out, kd_out, bd_out, bv_out, g_out):
    i = pl.program_id(1)
    last = pl.num_programs(1) - 1
    pa = p_ref[0]
    tm = pa.shape[0]
    row = lax.broadcasted_iota(jnp.int32, (tm, 1), 0)
    prev_row = jnp.where(i == 0, 0.0, pp_ref[0][SUBLANES - 1:SUBLANES, :])
    next_row = jnp.where(i == last, 0.0, pn_ref[0][0:1, :])
    p_prev = jnp.where(row == 0, prev_row, pltpu.roll(pa, 1, 0))
    p_next = jnp.where(row == tm - 1, next_row, pltpu.roll(pa, tm - 1, 0))
    pa = pa + mu_ref[...] * (0.5 * (p_prev + p_next) - pa)

    w = A_WIDTH
    r = pa[:, 0:w]
    k = pa[:, w:2 * w]
    v = pa[:, 2 * w:3 * w]
    dw = pa[:, 3 * w:3 * w + 2 * A_LORA]
    da = pa[:, 3 * w + 2 * A_LORA:3 * w + 4 * A_LORA]
    dg = pa[:, 3 * w + 4 * A_LORA:A_COLS]

    lw = -DECAY_SCALE * jax.nn.sigmoid(w0_ref[...] + _dot(jnp.tanh(dw).astype(BF16), wup_ref[...]))
    icl = jax.nn.sigmoid(a0_ref[...] + _dot(da.astype(BF16), aup_ref[...]))
    g = _dot(jax.nn.sigmoid(dg).astype(BF16), gup_ref[...])

    ones_bd = ones_ref[...]
    kkr = k * kk_ref[...]
    ss = _dot_exact_rhs(kkr * kkr, ones_bd)
    kkn = kkr / jnp.maximum(jnp.sqrt(ss), 1e-12)

    r_out[0] = r.astype(r_out.dtype)
    v_out[0] = v.astype(v_out.dtype)
    kkn_out[0] = kkn.astype(kkn_out.dtype)
    lw_out[0] = lw
    g_out[0] = g
    bonus = jnp.zeros_like(r)
    for d in range(2):
        icl_d = icl[:, d * w:(d + 1) * w]
        k_d = k * (1.0 + (icl_d - 1.0) * ka_ref[...])
        kd_out[0, :, d * w:(d + 1) * w] = k_d.astype(kd_out.dtype)
        bd_out[0, :, d * w:(d + 1) * w] = (icl_d * kkn).astype(bd_out.dtype)
        bonus = bonus + _dot_exact_rhs(r * k_d * rk_ref[...], ones_bd)
    bv_out[0] = bonus * v


def _rwkv_prep(pa, wts):
    b, s, _ = pa.shape
    tm = min(256, s)
    nh = tm // SUBLANES
    w = A_WIDTH
    full = lambda shape: pl.BlockSpec(shape, lambda bi, i: (0,) * len(shape))
    tok = lambda n: pl.BlockSpec((1, tm, n), lambda bi, i: (bi, i, 0))
    out_shapes = [jax.ShapeDtypeStruct((b, s, n), dt) for n, dt in (
        (w, BF16), (w, BF16), (w, BF16), (2 * w, F32), (2 * w, BF16), (2 * w, BF16), (w, F32), (w, F32))]
    return pl.pallas_call(
        _rwkv_prep_kernel,
        grid=(b, s // tm),
        in_specs=[
            tok(A_COLS),
            pl.BlockSpec((1, SUBLANES, A_COLS), lambda bi, i: (bi, jnp.maximum(i * nh - 1, 0), 0)),
            pl.BlockSpec((1, SUBLANES, A_COLS), lambda bi, i: (bi, jnp.minimum((i + 1) * nh, s // SUBLANES - 1), 0)),
            full((1, A_COLS)), full((1, 2 * w)), full((2 * A_LORA, 2 * w)), full((1, 2 * w)),
            full((2 * A_LORA, 2 * w)), full((A_GATE_LORA, w)), full((1, w)), full((1, w)), full((1, w)),
            full((w, w)),
        ],
        out_specs=[tok(sd.shape[-1]) for sd in out_shapes],
        out_shape=out_shapes,
        compiler_params=_params("parallel", "parallel"),
        name="rwkv_prep",
    )(pa, pa, pa, wts["mu"], wts["w0"], wts["wup"], wts["a0"], wts["aup"], wts["gup"], wts["k_k"], wts["k_a"],
      wts["r_k"], wts["ones_bd"])


def _rwkv_chunk(r, v, kk, lw, kd, bd, s2, reverse):
    t = RWKV_CHUNK
    assert t == A_HEAD
    sign = -1 if reverse else 1
    ri = lax.broadcasted_iota(jnp.int32, (t, t), 0)
    ci = lax.broadcasted_iota(jnp.int32, (t, t), 1)
    tri = jnp.where((ri - ci) * sign >= 0, 1.0, 0.0).astype(BF16)
    lw_hi = lw.astype(BF16)
    rem = lw - lw_hi.astype(F32)
    lw_mid = rem.astype(BF16)
    lw_lo = (rem - lw_mid.astype(F32)).astype(BF16)
    cum = _dot(tri, lw_hi) + _dot(tri, lw_mid) + _dot(tri, lw_lo)
    yield
    tot = jnp.sum(lw, axis=0, keepdims=True)
    gam = jnp.exp(cum)
    gam_ex = jnp.exp(cum - lw)
    ginv = jnp.exp(-cum)
    gend = jnp.exp(tot - cum)
    gtot = jnp.exp(tot)

    width = r.shape[1]
    heads = width // A_HEAD
    lane_head = lax.broadcasted_iota(jnp.int32, (1, width), 1) // A_HEAD

    def stack(x):
        return jnp.concatenate([jnp.where(lane_head == h, x, 0.0) for h in range(heads)], axis=0).astype(BF16)

    def mm(x_wide, y_stack):
        return _dot(x_wide.astype(BF16), y_stack)

    ti = lax.broadcasted_iota(jnp.int32, (t, width), 0)
    si = lax.broadcasted_iota(jnp.int32, (t, width), 1) % t
    order = (ti - si) * sign
    incl = order >= 0
    strict = order > 0

    q_rk = jnp.concatenate([r * gam, kk * gam_ex], axis=0).astype(BF16)
    a = _dot(q_rk, jnp.concatenate([stack(kd * ginv), stack(bd * ginv)], axis=0), NT)
    yield
    n = heads * t
    a_rk = jnp.where(incl, a[:t, :n], 0.0)
    a_rb = jnp.where(incl, a[:t, n:], 0.0)
    a_kk = jnp.where(strict, a[t:, :n], 0.0)
    a_bk = jnp.where(strict, a[t:, n:], 0.0)

    qs = _dot(q_rk, s2.astype(BF16), NT)
    yield
    akv = mm(jnp.concatenate([a_rk, a_kk], axis=0), stack(v))
    yield

    s_n = stack(a_bk)
    n2 = mm(a_bk, s_n)
    yield
    n3 = mm(n2, s_n)
    yield
    n4 = mm(n2, stack(n2))
    yield
    s_n4 = stack(n4)
    n8 = mm(n4, s_n4)
    yield
    n12 = mm(n8, s_n4)
    yield
    f1 = jnp.where(ti == si, 1.0, 0.0) - a_bk + n2 - n3
    n16 = mm(n8, stack(n8))
    yield
    f12 = f1 + mm(f1, stack(n4 + n8 + n12))
    yield
    s_n16 = stack(n16)
    n32 = mm(n16, s_n16)
    yield
    n48 = mm(n32, s_n16)
    yield
    inv = f12 + mm(f12, stack(n16 + n32 + n48))
    yield

    ps = mm(inv, stack(-qs[t:] - akv[t:]))
    yield
    y = qs[:t] + akv[:t] + mm(a_rb, stack(ps))
    yield
    upd = _dot(jnp.concatenate([v, ps], axis=0).T.astype(BF16),
               jnp.concatenate([kd * gend, bd * gend], axis=0).astype(BF16))
    row_head = lax.broadcasted_iota(jnp.int32, (width, 1), 0) // A_HEAD
    s2_new = s2 * gtot + jnp.where(row_head == lane_head, upd, 0.0)
    return y, s2_new


def _run_lockstep(gens):
    results = [None] * len(gens)
    active = list(range(len(gens)))
    while active:
        for i in list(active):
            try:
                next(gens[i])
            except StopIteration as e:
                results[i] = e.value
                active.remove(i)
    return results


RWKV_CHAIN_LANES = 256
RWKV_CHAINS_PER_STEP = 4


def _rwkv_scan_kernel(rf_ref, vf_ref, kkf_ref, lwf_ref, kdf_ref, bdf_ref,
                      rb_ref, vb_ref, kkb_ref, lwb_ref, kdb_ref, bdb_ref, yf_ref, yb_ref, s_scr):
    @pl.when(pl.program_id(2) == 0)
    def _():
        s_scr[...] = jnp.zeros_like(s_scr)

    dirs = ((rf_ref, vf_ref, kkf_ref, lwf_ref, kdf_ref, bdf_ref, yf_ref),
            (rb_ref, vb_ref, kkb_ref, lwb_ref, kdb_ref, bdb_ref, yb_ref))
    chains = []
    for d, (r_ref, v_ref, kk_ref, lw_ref, kd_ref, bd_ref, y_ref) in enumerate(dirs):
        for p in range(RWKV_CHAINS_PER_STEP):
            sl = slice(p * RWKV_CHAIN_LANES, (p + 1) * RWKV_CHAIN_LANES)
            chains.append((d, p, sl, y_ref, _rwkv_chunk(
                r_ref[0, :, sl], v_ref[0, :, sl], kk_ref[0, :, sl], lw_ref[0, :, sl], kd_ref[0, :, sl],
                bd_ref[0, :, sl], s_scr[d, p], reverse=(d == 1))))
    results = _run_lockstep([c[4] for c in chains])
    for (d, p, sl, y_ref, _), (y, s_new) in zip(chains, results):
        y_ref[0, :, sl] = y
        s_scr[d, p] = s_new


def _rwkv_scan(r, v, kk, lw, kd, bd):
    b, s, w = r.shape
    t = RWKV_CHUNK
    nc = s // t
    pw = RWKV_CHAINS_PER_STEP * RWKV_CHAIN_LANES
    ngrp = w // pw
    fwd = lambda off: pl.BlockSpec((1, t, pw), lambda bi, g, c: (bi, c, g + off))
    bwd = lambda off: pl.BlockSpec((1, t, pw), lambda bi, g, c: (bi, nc - 1 - c, g + off))
    return pl.pallas_call(
        _rwkv_scan_kernel,
        grid=(b, ngrp, nc),
        in_specs=[fwd(0)] * 6 + [bwd(0)] * 3 + [bwd(ngrp)] * 3,
        out_specs=[fwd(0), bwd(0)],
        out_shape=[jax.ShapeDtypeStruct((b, s, w), F32)] * 2,
        scratch_shapes=[pltpu.VMEM((2, RWKV_CHAINS_PER_STEP, RWKV_CHAIN_LANES, RWKV_CHAIN_LANES), F32)],
        compiler_params=_params("parallel", "parallel", "arbitrary"),
        name="rwkv_scan",
    )(r, v, kk, lw, kd, bd, r, v, kk, lw, kd, bd)


def _rwkv_post_kernel(yf_ref, yb_ref, bv_ref, g_ref, lng_ref, lnb_ref, ones_ref, o_ref):
    y = yf_ref[0] + yb_ref[0]
    ones_bd = ones_ref[...]
    mean = _dot_exact_rhs(y, ones_bd) * (1.0 / A_HEAD)
    yc = y - mean
    var = _dot_exact_rhs(yc * yc, ones_bd) * (1.0 / A_HEAD)
    yn = yc * lax.rsqrt(var + RWKV_GN_EPS) * lng_ref[...] + lnb_ref[...]
    o_ref[0] = ((yn + bv_ref[0]) * g_ref[0]).astype(o_ref.dtype)


def _rwkv_post(yf, yb, bv, g, lng, lnb, ones_bd):
    b, s, w = bv.shape
    tm = min(512, s)
    full = lambda shape: pl.BlockSpec(shape, lambda bi, i: (0,) * len(shape))
    tok = lambda n: pl.BlockSpec((1, tm, n), lambda bi, i: (bi, i, 0))
    return pl.pallas_call(
        _rwkv_post_kernel,
        grid=(b, s // tm),
        in_specs=[tok(w), tok(w), tok(w), tok(w), full((1, w)), full((1, w)), full((w, w))],
        out_specs=tok(w),
        out_shape=jax.ShapeDtypeStruct((b, s, w), BF16),
        compiler_params=_params("parallel", "parallel"),
        name="rwkv_post",
    )(yf, yb, bv, g, lng, lnb, ones_bd)


def _s5_kernel(u_ref, m_ref, w_ref, ws_ref, v_ref, a_ref, y_ref, x_scr, xs_scr, h_scr, carry_scr, *, rows, reverse):
    g = B_GROUPS

    @pl.when(pl.program_id(1) == 0)
    def _():
        carry_scr[...] = jnp.zeros_like(carry_scr)

    u = u_ref[0]
    x_scr[...] = lax.dot_general(u, w_ref[...], BATCH_NN, preferred_element_type=F32).reshape(g * rows, LANES)
    xs_scr[...] = lax.dot_general(u, ws_ref[...], BATCH_NN, preferred_element_type=F32).reshape(g * rows, LANES)
    a1 = a_ref[0]
    a2 = a_ref[1]
    a2s = a_ref[2]

    def step(i, carry):
        h, hs = carry
        r = (rows - 1 - i) if reverse else i
        idx = pl.ds(r, g, stride=rows)
        h_scr[idx, :] = h
        hn = a1 * h + a2 * hs + x_scr[idx, :]
        hsn = a1 * hs + a2s * h + xs_scr[idx, :]
        return hn, hsn

    h, hs = lax.fori_loop(0, rows, step, (carry_scr[0], carry_scr[1]))
    carry_scr[0] = h
    carry_scr[1] = hs
    hprev = h_scr[...].reshape(g, rows, LANES).astype(BF16)
    y_ref[0] = (lax.dot_general(u, m_ref[...], BATCH_NN, preferred_element_type=F32)
                + lax.dot_general(hprev, v_ref[...], BATCH_NN, preferred_element_type=F32)).astype(y_ref.dtype)


def _s5_scan(u_g, mats, reverse):
    b, g, nrows, cw = u_g.shape
    rows = min(64, nrows)
    nsb = nrows // rows
    whole = pl.BlockSpec(memory_space=pltpu.VMEM)
    idx = (lambda bi, i: (bi, 0, nsb - 1 - i, 0)) if reverse else (lambda bi, i: (bi, 0, i, 0))
    return pl.pallas_call(
        functools.partial(_s5_kernel, rows=rows, reverse=reverse),
        grid=(b, nsb),
        in_specs=[pl.BlockSpec((1, g, rows, cw), idx), whole, whole, whole, whole, whole],
        out_specs=pl.BlockSpec((1, g, rows, cw), idx),
        out_shape=jax.ShapeDtypeStruct((b, g, nrows, cw), BF16),
        scratch_shapes=[pltpu.VMEM((g * rows, LANES), F32), pltpu.VMEM((g * rows, LANES), F32),
                        pltpu.VMEM((g * rows, LANES), F32), pltpu.VMEM((2, g, LANES), F32)],
        compiler_params=_params("parallel", "arbitrary"),
        name="s5_scan_bwd" if reverse else "s5_scan_fwd",
    )(u_g, mats["m"], mats["w"], mats["ws"], mats["v"], mats["a"])


def _s5_matrices(lam_re, lam_im, log_step, b_re, b_im, c_re, c_im, reverse):
    hp = lax.Precision.HIGHEST
    t = S5_CHUNK
    g, p, c = B_GROUPS, B_STATE, B_GROUP
    lr, li = lam_re.astype(F32), lam_im.astype(F32)
    step = jnp.exp(log_step.astype(F32))[:, None]
    ar, ai = jnp.exp(lr * step) * jnp.cos(li * step), jnp.exp(lr * step) * jnp.sin(li * step)
    den = lr * lr + li * li
    nr, ni = ar - 1.0, ai
    fr, fi = (nr * lr + ni * li) / den, (ni * lr - nr * li) / den
    br, bi = b_re.astype(F32), b_im.astype(F32)
    bbr = fr[..., None] * br - fi[..., None] * bi
    bbi = fr[..., None] * bi + fi[..., None] * br
    cr, cim = c_re.astype(F32), c_im.astype(F32)
    taus = jnp.arange(t + 1, dtype=F32)[:, None, None]
    mag = jnp.exp(lr * step * taus)
    pr, pi = mag * jnp.cos(li * step * taus), mag * jnp.sin(li * step * taus)
    cpr = cr[None] * pr[:, :, None, :] - cim[None] * pi[:, :, None, :]
    cpi = cr[None] * pi[:, :, None, :] + cim[None] * pr[:, :, None, :]
    kern = (jnp.einsum("tgcp,gpd->tgcd", cpr, bbr, precision=hp)
            - jnp.einsum("tgcp,gpd->tgcd", cpi, bbi, precision=hp))
    s_idx = jnp.arange(t)[:, None]
    t_idx = jnp.arange(t)[None, :]
    lag = (s_idx - t_idx) if reverse else (t_idx - s_idx)
    kk = jnp.where((lag >= 0)[:, :, None, None, None], kern[jnp.clip(lag, 0, t)], 0.0)
    m = jnp.transpose(kk, (2, 0, 4, 1, 3)).reshape(g, t * c, t * c)
    e = jnp.arange(t) if reverse else (t - 1 - jnp.arange(t))
    pre, pie = pr[e], pi[e]
    wre = pre[..., None] * bbr[None] - pie[..., None] * bbi[None]
    wim = pre[..., None] * bbi[None] + pie[..., None] * bbr[None]
    wre = jnp.transpose(wre, (1, 0, 3, 2)).reshape(g, t * c, p)
    wim = jnp.transpose(wim, (1, 0, 3, 2)).reshape(g, t * c, p)
    w = jnp.concatenate([wre, wim], axis=-1)
    ws = jnp.concatenate([wim, wre], axis=-1)
    f = (t - jnp.arange(t)) if reverse else (jnp.arange(t) + 1)
    vre = jnp.transpose(cpr[f], (1, 3, 0, 2)).reshape(g, p, t * c)
    vim = jnp.transpose(-cpi[f], (1, 3, 0, 2)).reshape(g, p, t * c)
    v = jnp.concatenate([vre, vim], axis=1)
    atr, ati = pr[t], pi[t]
    a = jnp.stack([jnp.concatenate([atr, atr], -1), jnp.concatenate([-ati, ati], -1),
                   jnp.concatenate([ati, -ati], -1)])
    return {"m": m.astype(BF16), "w": w.astype(BF16), "ws": ws.astype(BF16), "v": v.astype(BF16), "a": a}


def _s5_glu_kernel(ys_ref, u_ref, d_ref, w_ref, b_ref, o_ref):
    y = ys_ref[0].astype(F32) + u_ref[0] * d_ref[...]
    z = jax.nn.gelu(y)
    gate = jax.nn.sigmoid(_dot(z.astype(BF16), w_ref[...]) + b_ref[...])
    o_ref[0] = (z * gate).astype(o_ref.dtype)


def _s5_glu(ys, u, d_skip, w_glu, b_glu):
    b, s, w = u.shape
    tm = min(512, s)
    full = lambda shape: pl.BlockSpec(shape, lambda bi, i: (0,) * len(shape))
    tok = pl.BlockSpec((1, tm, w), lambda bi, i: (bi, i, 0))
    return pl.pallas_call(
        _s5_glu_kernel,
        grid=(b, s // tm),
        in_specs=[tok, tok, full((1, w)), full((w, w)), full((1, w))],
        out_specs=tok,
        out_shape=jax.ShapeDtypeStruct((b, s, w), BF16),
        compiler_params=_params("parallel", "parallel"),
        name="s5_glu",
    )(ys, u, d_skip, w_glu, b_glu)


def _out_proj_kernel(a_ref, b_ref, wa_ref, wb_ref, x_ref, g_ref, o_ref):
    mix = _dot(a_ref[0], wa_ref[...]) + _dot(b_ref[0], wb_ref[...])
    o_ref[0] = x_ref[0] + g_ref[0] * mix


def _out_proj(a, bb, wa, wb, x, gate):
    b, s, d = x.shape
    k = a.shape[-1]
    tm = min(MM_TOKENS, s)
    tn = 1024
    return pl.pallas_call(
        _out_proj_kernel,
        grid=(b, s // tm, d // tn),
        in_specs=[
            pl.BlockSpec((1, tm, k), lambda bi, i, j: (bi, i, 0)),
            pl.BlockSpec((1, tm, k), lambda bi, i, j: (bi, i, 0)),
            pl.BlockSpec((k, tn), lambda bi, i, j: (0, j)),
            pl.BlockSpec((k, tn), lambda bi, i, j: (0, j)),
            pl.BlockSpec((1, tm, tn), lambda bi, i, j: (bi, i, j)),
            pl.BlockSpec((1, 1, tn), lambda bi, i, j: (bi, 0, j)),
        ],
        out_specs=pl.BlockSpec((1, tm, tn), lambda bi, i, j: (bi, i, j)),
        out_shape=jax.ShapeDtypeStruct((b, s, d), F32),
        compiler_params=_params("parallel", "parallel", "parallel"),
        name="out_proj",
    )(a, bb, wa, wb, x, gate)


FFN_HALO = 2 * SUBLANES


def _ffn_kernel(x_ref, xp_ref, xn_ref, ng_ref, sc_ref, sh_ref, wv_ref, wg_ref, cwv_ref, cwg_ref, cbv_ref, cbg_ref,
                wd_ref, g_ref, fg_ref, o_ref, h_scr, *, final):
    i = pl.program_id(1)
    j = pl.program_id(2)
    tm = x_ref.shape[1]
    hl = FFN_HALO

    @pl.when(j == 0)
    def _():
        g, sc, sh = ng_ref[...], sc_ref[0], sh_ref[0]
        h_scr[hl:hl + tm] = _modnorm(x_ref[0], g, sc, sh).astype(BF16)
        before = jnp.where(i == 0, 0.0, _modnorm(xp_ref[0], g, sc, sh))
        after = jnp.where(i == pl.num_programs(1) - 1, 0.0, _modnorm(xn_ref[0], g, sc, sh))
        h_scr[0:hl] = before.astype(BF16)
        h_scr[hl + tm:2 * hl + tm] = after.astype(BF16)
        o_ref[...] = jnp.zeros_like(o_ref)

    h = h_scr[...]
    rows = tm + 2 * hl

    def conv(w_ref, cw_ref, cb_ref):
        u = _dot(h, w_ref[...])
        cw = cw_ref[...]
        u_prev = pltpu.roll(u, 1, 0)[hl:hl + tm]
        u_next = pltpu.roll(u, rows - 1, 0)[hl:hl + tm]
        return u_prev * cw[0:1] + u[hl:hl + tm] * cw[1:2] + u_next * cw[2:3] + cb_ref[...]

    val = conv(wv_ref, cwv_ref, cbv_ref)
    gate = conv(wg_ref, cwg_ref, cbg_ref)
    act = (gate * jax.nn.sigmoid(gate)) * val
    o_ref[0] += _dot(act.astype(BF16), wd_ref[...])

    @pl.when(j == pl.num_programs(2) - 1)
    def _():
        xn = x_ref[0] + g_ref[0] * o_ref[0]
        if final:
            xn = xn * lax.rsqrt(jnp.mean(xn * xn, axis=-1, keepdims=True) + EPS) * fg_ref[...]
        o_ref[0] = xn


def _ffn(x, norm_g, sc, sh, w_up, conv_w, conv_b, w_down, gate, final_g, final):
    b, s, d = x.shape
    f = FFN_HIDDEN
    tm = min(FFN_TOKENS, s)
    tf = 512
    nf = f // tf
    nh = tm // FFN_HALO
    nhalo = s // FFN_HALO
    const = lambda shape: pl.BlockSpec(shape, lambda bi, i, j: (0,) * len(shape))
    per_b = pl.BlockSpec((1, 1, d), lambda bi, i, j: (bi, 0, 0))
    up = lambda off: pl.BlockSpec((d, tf), lambda bi, i, j: (0, j + off))
    cw = lambda off: pl.BlockSpec((3, tf), lambda bi, i, j: (0, j + off))
    cb = lambda off: pl.BlockSpec((1, tf), lambda bi, i, j: (0, j + off))
    once = pl.Buffered(1)
    return pl.pallas_call(
        functools.partial(_ffn_kernel, final=final),
        grid=(b, s // tm, nf),
        in_specs=[
            pl.BlockSpec((1, tm, d), lambda bi, i, j: (bi, i, 0)),
            pl.BlockSpec((1, FFN_HALO, d), lambda bi, i, j: (bi, jnp.maximum(i * nh - 1, 0), 0)),
            pl.BlockSpec((1, FFN_HALO, d), lambda bi, i, j: (bi, jnp.minimum((i + 1) * nh, nhalo - 1), 0)),
            const((1, d)), per_b, per_b,
            up(0), up(nf), cw(0), cw(nf), cb(0), cb(nf),
            pl.BlockSpec((tf, d), lambda bi, i, j: (j, 0)),
            per_b, const((1, d)),
        ],
        out_specs=pl.BlockSpec((1, tm, d), lambda bi, i, j: (bi, i, 0), pipeline_mode=once),
        out_shape=jax.ShapeDtypeStruct((b, s, d), F32),
        scratch_shapes=[pltpu.VMEM((tm + 2 * FFN_HALO, d), BF16)],
        compiler_params=_params("parallel", "parallel", "arbitrary"),
        name="ffn",
    )(x, x, x, norm_g.reshape(1, d), sc, sh, w_up, w_up, conv_w, conv_w, conv_b, conv_b, w_down, gate, final_g)


def _rms(x, g, eps):
    return x * lax.rsqrt(jnp.mean(x * x, axis=-1, keepdims=True) + eps) * g


def _odd_prep_kernel(cq_ref, ckv_ref, kr_ref, dq_ref, dk_ref, dv_ref, cos_ref, sin_ref, qg_ref, kvg_ref,
                     wqa_ref, wqb_ref, wk_ref, wv_ref, place_ref,
                     mq_out, mk_out, mv_out, q0_out, q1_out, dk_out, dv_out):
    cosq = cos_ref[...]
    sinq = sin_ref[...]
    qn = _rms(cq_ref[0], qg_ref[...], EPS).astype(BF16)
    qa = _dot(qn, wqa_ref[...])
    qb = _dot(qn, wqb_ref[...])
    mla_scale = (C_NOPE + C_ROPE) ** -0.5 * LOG2E
    for h in range(C_HEADS):
        sl = slice(h * LANES, (h + 1) * LANES)
        mq_out[0, :, sl] = ((qa[:, sl] * cosq + qb[:, sl] * sinq) * mla_scale).astype(BF16)
    kvn = _rms(ckv_ref[0], kvg_ref[...], EPS).astype(BF16)
    kr = kr_ref[0]
    cos_k = pltpu.roll(cosq, LANES - C_NOPE, 1)
    sin_k = pltpu.roll(sinq, LANES - C_NOPE, 1)
    partner = pltpu.roll(kr, LANES - C_ROPE, 1)
    lane = lax.broadcasted_iota(jnp.int32, (1, LANES), 1)
    kr_rope = jnp.where(lane < C_ROPE, kr * cos_k + partner * sin_k, 0.0)
    mk_out[0] = (_dot(kvn, wk_ref[...]) + _dot(kr_rope.astype(BF16), place_ref[...])).astype(BF16)
    mv_out[0] = _dot(kvn, wv_ref[...]).astype(BF16)
    dq = dq_ref[0] * (D_HEAD ** -0.5 * LOG2E)
    lane_w = lax.broadcasted_iota(jnp.int32, (1, dq.shape[1]), 1)
    first_map = (lane_w % LANES) < D_HEAD
    q0_out[0] = jnp.where(first_map, dq, 0.0).astype(BF16)
    q1_out[0] = jnp.where(first_map, 0.0, dq).astype(BF16)
    dk_out[0] = dk_ref[0].astype(BF16)
    dv_out[0] = dv_ref[0].astype(BF16)


ODD_DQ = 0
ODD_DK = ODD_DQ + D_HEADS * 2 * D_HEAD
ODD_DV = ODD_DK + D_HEADS * 2 * D_HEAD
ODD_CQ = ODD_DV + D_HEADS * D_V
ODD_CKV = ODD_CQ + C_Q_RANK
ODD_KR = ODD_CKV + C_KV_RANK
ODD_COLS = 4096


def _odd_prep(p, cos_t, sin_t, wts):
    b, s, _ = p.shape
    tm = min(256, s)
    hw = C_HEADS * LANES
    full = lambda shape: pl.BlockSpec(shape, lambda bi, i: (0,) * len(shape))
    col = lambda off, n: pl.BlockSpec((1, tm, n), lambda bi, i: (bi, i, off // n))
    tok = pl.BlockSpec((1, tm, hw), lambda bi, i: (bi, i, 0))
    tab = pl.BlockSpec((tm, LANES), lambda bi, i: (i, 0))
    outs = [jax.ShapeDtypeStruct((b, s, hw), BF16)] * 7
    return pl.pallas_call(
        _odd_prep_kernel,
        grid=(b, s // tm),
        in_specs=[
            col(ODD_CQ, C_Q_RANK), col(ODD_CKV, C_KV_RANK), col(ODD_KR, LANES),
            col(ODD_DQ, hw), col(ODD_DK, hw), col(ODD_DV, hw), tab, tab,
            full((1, C_Q_RANK)), full((1, C_KV_RANK)),
            full((C_Q_RANK, hw)), full((C_Q_RANK, hw)), full((C_KV_RANK, hw)), full((C_KV_RANK, hw)),
            full((LANES, hw)),
        ],
        out_specs=[tok] * 7,
        out_shape=outs,
        compiler_params=_params("parallel", "parallel"),
        name="odd_prep",
    )(p, p, p, p, p, p, cos_t, sin_t, wts["q_norm_g"], wts["kv_norm_g"], wts["wqa"], wts["wqb"], wts["wk"],
      wts["wv"], wts["place"])


def _with_ones(v):
    return jnp.concatenate([v, jnp.ones_like(v)], axis=1)


def _flash_chain(q, k, v1, bias, shift, m_scr, acc_scr):
    s = _dot(q, k, NT)
    yield
    if isinstance(bias, tuple):
        row0, strip = bias
        row1 = row0 + strip.shape[0]
        parts = [s[:row0]] * (row0 > 0) + [s[row0:row1] + strip] + [s[row1:]] * (row1 < s.shape[0])
        s = jnp.concatenate(parts, axis=0)
    elif bias is not None:
        s = s + bias
    m_prev = m_scr[...]
    m_new = jnp.maximum(m_prev, jnp.max(s, axis=-1, keepdims=True) + shift)
    alpha = jnp.exp2(m_prev - m_new)
    p = jnp.exp2(s - jnp.tile(m_new - shift, (1, s.shape[1] // LANES))).astype(BF16)
    yield
    acc_scr[...] = jnp.tile(alpha, (1, 2)) * acc_scr[...] + _dot(p, v1)
    m_scr[...] = m_new


MLA_HEADS_PER_STEP = 4


def _mla_flash_kernel(q_ref, k_ref, v_ref, o_ref, m_scr, acc_scr):
    kj = pl.program_id(3)

    @pl.when(kj == 0)
    def _():
        m_scr[...] = jnp.full_like(m_scr, -jnp.inf)
        acc_scr[...] = jnp.zeros_like(acc_scr)

    heads = [slice(h * LANES, (h + 1) * LANES) for h in range(MLA_HEADS_PER_STEP)]
    _run_lockstep([_flash_chain(q_ref[0, :, sl], k_ref[0, :, sl], _with_ones(v_ref[0, :, sl]), None, 0.0,
                                m_scr.at[h], acc_scr.at[h]) for h, sl in enumerate(heads)])

    @pl.when(kj == pl.num_programs(3) - 1)
    def _():
        for h, sl in enumerate(heads):
            acc = acc_scr[h]
            o_ref[0, :, sl] = (acc[:, :LANES] / acc[:, LANES:]).astype(o_ref.dtype)


def _mla_flash(q, k, v):
    b, s, hw = q.shape
    gw = MLA_HEADS_PER_STEP * LANES
    tq = min(ATT_TQ, s)
    tk = min(ATT_TK, s)
    qspec = pl.BlockSpec((1, tq, gw), lambda bi, h, i, j: (bi, i, h))
    kspec = pl.BlockSpec((1, tk, gw), lambda bi, h, i, j: (bi, j, h))
    return pl.pallas_call(
        _mla_flash_kernel,
        grid=(b, hw // gw, s // tq, s // tk),
        in_specs=[qspec, kspec, kspec],
        out_specs=qspec,
        out_shape=jax.ShapeDtypeStruct((b, s, hw), BF16),
        scratch_shapes=[pltpu.VMEM((MLA_HEADS_PER_STEP, tq, LANES), F32),
                        pltpu.VMEM((MLA_HEADS_PER_STEP, tq, 2 * LANES), F32)],
        compiler_params=_params("parallel", "parallel", "parallel", "arbitrary"),
        name="mla_flash",
    )(q, k, v)


BIAS_HALF = 256


DIFF_HEADS_PER_STEP = 2


def _diff_flash_kernel(q0_ref, q1_ref, k_ref, v_ref, tab_ref, lq1_ref, lk1_ref, lq2_ref, lk2_ref, sg_ref, o_ref,
                       m_scr, acc_scr, bias_scr, *, tq, tk, near, lambda_init):
    kj = pl.program_id(3)
    off = kj * tk - pl.program_id(2) * tq
    heads = [slice(h * LANES, (h + 1) * LANES) for h in range(DIFF_HEADS_PER_STEP)]

    @pl.when(kj == 0)
    def _():
        m_scr[...] = jnp.full_like(m_scr, -jnp.inf)
        acc_scr[...] = jnp.zeros_like(acc_scr)

    tabs = [tab_ref[h] for h in range(DIFF_HEADS_PER_STEP)]
    far_left = [tab[:, 0:1] for tab in tabs]
    far_right = [tab[:, 2 * BIAS_HALF - 1:2 * BIAS_HALF] for tab in tabs]

    def update(biases, shifts):
        chains = []
        for h, sl in enumerate(heads):
            k = k_ref[0, :, sl]
            v1 = _with_ones(v_ref[0, :, sl])
            for m, q_ref in enumerate((q0_ref, q1_ref)):
                chains.append(_flash_chain(q_ref[0, :, sl], k, v1, biases[h], shifts[h], m_scr.at[h, m],
                                           acc_scr.at[h, m]))
        _run_lockstep(chains)

    def toeplitz(tab, d):
        r = tab[:, d + LANES:d + 3 * LANES]
        rows = jnp.broadcast_to(r, (LANES, 2 * LANES))
        return pltpu.roll(rows, LANES, 1, stride=1, stride_axis=0)[:, :LANES]

    def block_kinds(d0):
        kinds = {}
        for ri in range(tq // LANES):
            for cj in range(tk // LANES):
                d = d0 + (cj - ri) * LANES
                kinds[ri, cj] = 'L' if d <= -BIAS_HALF else 'R' if d >= BIAS_HALF else d
        return kinds

    for d0 in near:
        kinds = block_kinds(d0)
        band_rows = {ri for (ri, _), kind in kinds.items() if not isinstance(kind, str)}
        sides = {kind for kind in kinds.values() if isinstance(kind, str)}

        if len(band_rows) == 1 and len(sides) == 1:
            @pl.when(off == d0)
            def _(kinds=kinds, ri=band_rows.pop(), side=sides.pop()):
                const = far_left if side == 'L' else far_right
                for h in range(DIFF_HEADS_PER_STEP):
                    for cj in range(tk // LANES):
                        kind = kinds[ri, cj]
                        corr = (jnp.zeros((LANES, LANES), F32) if isinstance(kind, str)
                                else toeplitz(tabs[h], kind) - const[h])
                        bias_scr[h, 0:LANES, cj * LANES:(cj + 1) * LANES] = corr
                update([(ri * LANES, bias_scr[h, 0:LANES, :]) for h in range(DIFF_HEADS_PER_STEP)], const)
        else:
            @pl.when(off == d0)
            def _(kinds=kinds):
                for h in range(DIFF_HEADS_PER_STEP):
                    blocks = {}
                    for (ri, cj), kind in kinds.items():
                        sl = (h, slice(ri * LANES, (ri + 1) * LANES), slice(cj * LANES, (cj + 1) * LANES))
                        if kind == 'L':
                            bias_scr[sl] = jnp.broadcast_to(far_left[h], (LANES, LANES))
                        elif kind == 'R':
                            bias_scr[sl] = jnp.broadcast_to(far_right[h], (LANES, LANES))
                        else:
                            if kind not in blocks:
                                blocks[kind] = toeplitz(tabs[h], kind)
                            bias_scr[sl] = blocks[kind]
                update([bias_scr[h] for h in range(DIFF_HEADS_PER_STEP)], [0.0] * DIFF_HEADS_PER_STEP)

    @pl.when(jnp.logical_or(off < near[0], off > near[-1]))
    def _():
        update([None] * DIFF_HEADS_PER_STEP,
               [jnp.where(off < near[0], lo, hi) for lo, hi in zip(far_left, far_right)])

    @pl.when(kj == pl.num_programs(3) - 1)
    def _():
        lam = (jnp.exp(jnp.sum(lq1_ref[...] * lk1_ref[...], axis=-1, keepdims=True))
               - jnp.exp(jnp.sum(lq2_ref[...] * lk2_ref[...], axis=-1, keepdims=True)) + lambda_init)
        for h, sl in enumerate(heads):
            a0 = acc_scr[h, 0]
            a1 = acc_scr[h, 1]
            o = a0[:, :LANES] / a0[:, LANES:] - lam * (a1[:, :LANES] / a1[:, LANES:])
            o = _rms(o, sg_ref[...], SUBLN_EPS) * (1.0 - lambda_init)
            o_ref[0, :, sl] = o.astype(o_ref.dtype)


def _diff_flash(q0, q1, k, v, tab, lq1, lk1, lq2, lk2, subln_g, lambda_init):
    b, s, hw = q0.shape
    hps = DIFF_HEADS_PER_STEP
    gw = hps * LANES
    tq = min(ATT_TQ, s)
    tk = min(ATT_TK, s)
    offs = sorted({j * tk - i * tq for i in range(s // tq) for j in range(s // tk)})
    near = tuple(d for d in offs if d - (tq - 1) < BIAS_HALF and d + tk - 1 > -BIAS_HALF)
    assert near == tuple(d for d in offs if near[0] <= d <= near[-1])
    qspec = pl.BlockSpec((1, tq, gw), lambda bi, h, i, j: (bi, i, h))
    kspec = pl.BlockSpec((1, tk, gw), lambda bi, h, i, j: (bi, j, h))
    vec = lambda n: pl.BlockSpec((1, n), lambda bi, h, i, j: (0, 0))
    return pl.pallas_call(
        functools.partial(_diff_flash_kernel, tq=tq, tk=tk, near=near, lambda_init=lambda_init),
        grid=(b, hw // gw, s // tq, s // tk),
        in_specs=[qspec, qspec, kspec, kspec,
                  pl.BlockSpec((hps, 1, 2 * BIAS_HALF), lambda bi, h, i, j: (h, 0, 0)),
                  vec(D_HEAD), vec(D_HEAD), vec(D_HEAD), vec(D_HEAD), vec(D_V)],
        out_specs=qspec,
        out_shape=jax.ShapeDtypeStruct((b, s, hw), BF16),
        scratch_shapes=[pltpu.VMEM((hps, 2, tq, LANES), F32), pltpu.VMEM((hps, 2, tq, 2 * LANES), F32),
                        pltpu.VMEM((hps, tq, tk), F32)],
        compiler_params=_params("parallel", "parallel", "parallel", "arbitrary"),
        name="diff_flash",
    )(q0, q1, k, v, tab, lq1, lk1, lq2, lk2, subln_g)


def _t5_bucket(rel):
    half = N_BUCKETS // 2
    max_exact = half // 2
    n = jnp.abs(rel)
    large = max_exact + (jnp.log(jnp.maximum(n, 1).astype(jnp.float32) / max_exact)
                         / math.log(MAX_DISTANCE / max_exact) * (half - max_exact)).astype(jnp.int32)
    large = jnp.minimum(large, half - 1)
    return jnp.where(rel > 0, half, 0) + jnp.where(n < max_exact, n, large)


def _rope_tables(s):
    inv = 1.0 / (ROPE_THETA ** (jnp.arange(0, C_ROPE, 2, dtype=F32) / C_ROPE))
    ang = jnp.arange(s, dtype=F32)[:, None] * inv[None, :]
    cos, sin = jnp.cos(ang), jnp.sin(ang)
    pad = LANES - C_NOPE - C_ROPE
    cos_t = jnp.concatenate([jnp.ones((s, C_NOPE), F32), cos, cos, jnp.zeros((s, pad), F32)], axis=-1)
    sin_t = jnp.concatenate([jnp.zeros((s, C_NOPE), F32), sin, sin, jnp.zeros((s, pad), F32)], axis=-1)
    return cos_t, sin_t


def _rot_half_cols(w):
    h = w.shape[-1] // 2
    return jnp.concatenate([-w[..., h:], w[..., :h]], axis=-1)


def _pack_even(j, even_w_in, even_w_out, rwkv_mu, rwkv_w0, rwkv_w_up, rwkv_a0, rwkv_a_up, rwkv_g_up, rwkv_k_k,
               rwkv_k_a, rwkv_r_k, rwkv_lnx_g, rwkv_lnx_b):
    w = A_WIDTH
    z = jnp.zeros((A_LORA, w), F32)
    blockdiag = lambda m: jnp.concatenate(
        [jnp.concatenate([m[0], z], axis=1), jnp.concatenate([z, m[1]], axis=1)], axis=0)
    head = jnp.arange(w) // A_HEAD
    return {
        "w_in_a": even_w_in[j][:, :A_COLS].astype(BF16),
        "w_in_b": even_w_in[j][:, A_COLS:].astype(BF16),
        "w_out_a": even_w_out[j][:w].astype(BF16),
        "w_out_b": even_w_out[j][w:].astype(BF16),
        "mu": rwkv_mu[j].reshape(1, A_COLS),
        "w0": rwkv_w0[j].reshape(1, 2 * w),
        "wup": blockdiag(rwkv_w_up[j]).astype(BF16),
        "a0": rwkv_a0[j].reshape(1, 2 * w),
        "aup": blockdiag(rwkv_a_up[j]).astype(BF16),
        "gup": rwkv_g_up[j].astype(BF16),
        "k_k": rwkv_k_k[j].reshape(1, w),
        "k_a": rwkv_k_a[j].reshape(1, w),
        "r_k": rwkv_r_k[j].reshape(1, w),
        "lnx_g": rwkv_lnx_g[j].reshape(1, w),
        "lnx_b": rwkv_lnx_b[j].reshape(1, w),
        "ones_bd": (head[:, None] == head[None, :]).astype(BF16),
    }


def _pack_odd(j, odd_w_in, odd_w_out, mla_q_norm_g, mla_kv_norm_g, mla_w_uq, mla_w_ukv):
    d = D_MODEL
    w_in = odd_w_in[j]
    o_cq, o_ckv = 0, C_Q_RANK
    o_kr = o_ckv + C_KV_RANK
    o_dq = o_kr + C_ROPE
    n_d = D_HEADS * 2 * D_HEAD
    w_kr = w_in[:, o_kr:o_kr + C_ROPE]
    packed = jnp.concatenate([
        w_in[:, o_dq:o_dq + 3 * n_d], w_in[:, o_cq:o_kr], w_kr, _rot_half_cols(w_kr),
        jnp.zeros((d, ODD_COLS - (C_Q_RANK + C_KV_RANK + 3 * n_d + 2 * C_ROPE)), F32)], axis=1)
    pad = LANES - C_NOPE - C_ROPE
    wq = mla_w_uq[j].reshape(C_Q_RANK, C_HEADS, C_NOPE + C_ROPE)
    zq = jnp.zeros((C_Q_RANK, C_HEADS, pad), F32)
    wqa = jnp.concatenate([wq, zq], axis=-1).reshape(C_Q_RANK, C_HEADS * LANES)
    wqb = jnp.concatenate([jnp.zeros((C_Q_RANK, C_HEADS, C_NOPE), F32), _rot_half_cols(wq[..., C_NOPE:]), zq],
                          axis=-1).reshape(C_Q_RANK, C_HEADS * LANES)
    wkv = mla_w_ukv[j].reshape(C_KV_RANK, C_HEADS, C_NOPE + C_V)
    wk = jnp.concatenate([wkv[..., :C_NOPE], jnp.zeros((C_KV_RANK, C_HEADS, LANES - C_NOPE), F32)],
                         axis=-1).reshape(C_KV_RANK, C_HEADS * LANES)
    wv = wkv[..., C_NOPE:].reshape(C_KV_RANK, C_HEADS * C_V)
    src = jnp.arange(LANES)[:, None]
    dst = jnp.arange(C_HEADS * LANES)[None, :] % LANES
    place = ((dst >= C_NOPE) & (dst < C_NOPE + C_ROPE) & (dst - C_NOPE == src)).astype(BF16)
    hv = C_HEADS * C_V
    return {
        "w_in": packed.astype(BF16),
        "w_out_a": odd_w_out[j][:hv].astype(BF16),
        "w_out_b": odd_w_out[j][hv:].astype(BF16),
        "q_norm_g": mla_q_norm_g[j].reshape(1, C_Q_RANK),
        "kv_norm_g": mla_kv_norm_g[j].reshape(1, C_KV_RANK),
        "wqa": wqa.astype(BF16), "wqb": wqb.astype(BF16), "wk": wk.astype(BF16), "wv": wv.astype(BF16),
        "place": place,
    }


def _even_mixers(x, g1n, sc1, sh1, ew, s5m, s5_d, s5_w_glu, s5_b_glu):
    b, s, _ = x.shape
    pa = _normmod_mm(x, g1n, sc1, sh1, ew["w_in_a"], tn=A_COLS // 3)
    u = _normmod_mm(x, g1n, sc1, sh1, ew["w_in_b"], tn=B_WIDTH)
    r, v, kk, lw, kd, bd, bv, g = _rwkv_prep(pa, ew)
    yf, yr = _rwkv_scan(r, v, kk, lw, kd, bd)
    ya = _rwkv_post(yf, yr, bv, g, ew["lnx_g"], ew["lnx_b"], ew["ones_bd"])
    nrows = s // S5_CHUNK
    u_g = jnp.transpose(u.astype(BF16).reshape(b, nrows, S5_CHUNK, B_GROUPS, B_GROUP), (0, 3, 1, 2, 4))
    u_g = u_g.reshape(b, B_GROUPS, nrows, S5_CHUNK * B_GROUP)
    ys_g = (_s5_scan(u_g, s5m[0], False).astype(F32) + _s5_scan(u_g, s5m[1], True).astype(F32)).astype(BF16)
    ys = jnp.transpose(ys_g.reshape(b, B_GROUPS, nrows, S5_CHUNK, B_GROUP), (0, 2, 3, 1, 4)).reshape(b, s, B_WIDTH)
    yb = _s5_glu(ys, u, s5_d, s5_w_glu, s5_b_glu)
    return ya, yb


def _odd_mixers(x, g1n, sc1, sh1, ow, tabs, diff_w, lambda_init):
    s = x.shape[1]
    p = _normmod_mm(x, g1n, sc1, sh1, ow["w_in"], tn=1024)
    mq, mk, mv, q0, q1, dk, dv = _odd_prep(p, tabs["cos"][:s], tabs["sin"][:s], ow)
    yc = _mla_flash(mq, mk, mv)
    yd = _diff_flash(q0, q1, dk, dv, tabs["bias"], diff_w["lq1"], diff_w["lk1"], diff_w["lq2"], diff_w["lk2"],
                     diff_w["subln_g"], lambda_init)
    return yc, yd


def kernel(x_prompt, x_sample, c_prompt, c_sample, ada_w, ada_b, norm1_g, norm2_g, even_w_in, even_w_out, rwkv_mu, rwkv_w0, rwkv_w_up, rwkv_a0, rwkv_a_up, rwkv_g_up, rwkv_k_k, rwkv_k_a, rwkv_r_k, rwkv_lnx_g, rwkv_lnx_b, s5_lam_re, s5_lam_im, s5_log_step, s5_b_re, s5_b_im, s5_c_re, s5_c_im, s5_d, s5_w_glu, s5_b_glu, odd_w_in, odd_w_out, mla_q_norm_g, mla_kv_norm_g, mla_w_uq, mla_w_ukv, diff_lq1, diff_lk1, diff_lq2, diff_lk2, diff_subln_g, rel_bias, ffn_w_up, ffn_conv_w, ffn_conv_b, ffn_w_down, final_g):
    d = D_MODEL
    groups = [(x_prompt, c_prompt), (x_sample, c_sample)]
    nb = [g[0].shape[0] for g in groups]
    c_all = jnp.concatenate([g[1] for g in groups] + [jnp.zeros((SUBLANES - sum(nb), d), F32)], axis=0)
    mod = _ada_mod(c_all, ada_w, ada_b)

    max_s = max(g[0].shape[1] for g in groups)
    cos_t, sin_t = _rope_tables(max_s)
    rel = jnp.arange(-BIAS_HALF, BIAS_HALF, dtype=jnp.int32)
    bias_tab = (jnp.transpose(rel_bias.astype(F32)[_t5_bucket(rel)]) * LOG2E).reshape(D_HEADS, 1, 2 * BIAS_HALF)
    tabs = {"cos": cos_t, "sin": sin_t, "bias": bias_tab}

    xs = [g[0] for g in groups]
    for i in range(DEPTH):
        j = i // 2
        if i % 2 == 0:
            ew = _pack_even(j, even_w_in, even_w_out, rwkv_mu, rwkv_w0, rwkv_w_up, rwkv_a0, rwkv_a_up, rwkv_g_up,
                            rwkv_k_k, rwkv_k_a, rwkv_r_k, rwkv_lnx_g, rwkv_lnx_b)
            s5m = [_s5_matrices(s5_lam_re[j, dr], s5_lam_im[j, dr], s5_log_step[j, dr], s5_b_re[j, dr],
                                s5_b_im[j, dr], s5_c_re[j, dr], s5_c_im[j, dr], dr == 1) for dr in range(2)]
        else:
            ow = _pack_odd(j, odd_w_in, odd_w_out, mla_q_norm_g, mla_kv_norm_g, mla_w_uq, mla_w_ukv)
            diff_w = {"lq1": diff_lq1[j].reshape(1, D_HEAD), "lk1": diff_lk1[j].reshape(1, D_HEAD),
                      "lq2": diff_lq2[j].reshape(1, D_HEAD), "lk2": diff_lk2[j].reshape(1, D_HEAD),
                      "subln_g": diff_subln_g[j].reshape(1, D_V)}
        w_up = ffn_w_up[i].astype(BF16)
        w_down = ffn_w_down[i].astype(BF16)
        row0 = 0
        for gi in range(len(groups)):
            x = xs[gi]
            m = mod[i, row0:row0 + nb[gi]]
            row0 += nb[gi]
            sh1, sc1, g1, sh2, sc2, g2 = [m[:, None, k * d:(k + 1) * d] for k in range(N_MOD)]
            if i % 2 == 0:
                ya, yb = _even_mixers(x, norm1_g[i], sc1, sh1, ew, s5m, s5_d[j].reshape(1, B_WIDTH),
                                      s5_w_glu[j].astype(BF16), s5_b_glu[j].reshape(1, B_WIDTH))
                x = _out_proj(ya, yb, ew["w_out_a"], ew["w_out_b"], x, g1)
            else:
                yc, yd = _odd_mixers(x, norm1_g[i], sc1, sh1, ow, tabs, diff_w, 0.8 - 0.6 * math.exp(-0.3 * i))
                x = _out_proj(yc, yd, ow["w_out_a"], ow["w_out_b"], x, g1)
            x = _ffn(x, norm2_g[i], sc2, sh2, w_up, ffn_conv_w[i], ffn_conv_b[i].reshape(1, 2 * FFN_HIDDEN), w_down,
                     g2, final_g.reshape(1, d), final=(i == DEPTH - 1))
            xs[gi] = x
    return (xs[0], xs[1])
```

```python
import functools
import math

import jax
import jax.numpy as jnp
from jax import lax
from jax.experimental import pallas as pl
from jax.experimental.pallas import tpu as pltpu

F32 = jnp.float32
BF16 = jnp.bfloat16

D_MODEL = 2048
DEPTH = 2
EPS = 1e-6
A_WIDTH = 1024
A_HEAD = 64
A_HEADS = 16
A_LORA = 64
A_GATE_LORA = 128
A_COLS = 3 * A_WIDTH + 2 * A_LORA + 2 * A_LORA + A_GATE_LORA
RWKV_GN_EPS = 64e-5
DECAY_SCALE = math.exp(-0.5)
B_WIDTH = 1024
B_GROUP = 16
B_GROUPS = 64
B_STATE = 64
C_HEADS = 8
C_NOPE = 64
C_ROPE = 32
C_V = 128
C_Q_RANK = 512
C_KV_RANK = 256
ROPE_THETA = 10000.0
D_HEADS = 8
D_HEAD = 64
D_V = 128
SUBLN_EPS = 1e-5
N_BUCKETS = 32
MAX_DISTANCE = 128
FFN_HIDDEN = 5632
N_MOD = 6

LANES = 128
SUBLANES = 8
VMEM_LIMIT_BYTES = 56 * 1024 * 1024

MM_TOKENS = 1024
FFN_TOKENS = 1024
FFN_HIDDEN_TILE = 512
ATT_TQ = 1024
ATT_TK = 1024

RWKV_CHUNK = 64
S5_CHUNK = 16
LOG2E = 1.4426950408889634

NN = (((1,), (0,)), ((), ()))
NT = (((1,), (1,)), ((), ()))
BATCH_NN = (((2,), (1,)), ((0,), (0,)))


def _params(*sem):
    return pltpu.CompilerParams(dimension_semantics=sem, vmem_limit_bytes=VMEM_LIMIT_BYTES)


def _dot(a, b, dims=NN):
    return lax.dot_general(a, b, dims, preferred_element_type=F32)


def _split2(x):
    hi = x.astype(BF16)
    lo = (x - hi.astype(F32)).astype(BF16)
    return hi, lo


def _dot_exact_rhs(a, b):
    ah, al = _split2(a)
    return _dot(ah, b) + _dot(al, b)


def _ada_kernel(c_ref, w_ref, b_ref, o_ref):
    c = c_ref[...]
    cs = c * jax.nn.sigmoid(c)
    o_ref[0] = _dot(cs.astype(BF16), w_ref[0].astype(BF16)) + b_ref[0]


def _ada_mod(c_all, ada_w, ada_b):
    n = N_MOD * D_MODEL
    tn = 1024
    return pl.pallas_call(
        _ada_kernel,
        grid=(DEPTH, n // tn),
        in_specs=[
            pl.BlockSpec((SUBLANES, D_MODEL), lambda l, j: (0, 0)),
            pl.BlockSpec((1, D_MODEL, tn), lambda l, j: (l, 0, j)),
            pl.BlockSpec((1, 1, tn), lambda l, j: (l, 0, j)),
        ],
        out_specs=pl.BlockSpec((1, SUBLANES, tn), lambda l, j: (l, 0, j)),
        out_shape=jax.ShapeDtypeStruct((DEPTH, SUBLANES, n), F32),
        compiler_params=_params("parallel", "parallel"),
        name="ada_mod",
    )(c_all, ada_w, ada_b.reshape(DEPTH, 1, n))


def _modnorm(x, g, sc, sh):
    y = x * lax.rsqrt(jnp.mean(x * x, axis=-1, keepdims=True) + EPS)
    return (y * g) * (1.0 + sc) + sh


def _normmod_mm_kernel(x_ref, g_ref, sc_ref, sh_ref, w_ref, o_ref, h_scr):
    @pl.when(pl.program_id(2) == 0)
    def _():
        h_scr[...] = _modnorm(x_ref[0], g_ref[...], sc_ref[0], sh_ref[0]).astype(BF16)

    o_ref[0] = _dot(h_scr[...], w_ref[...]).astype(o_ref.dtype)


def _normmod_mm(x, g, sc, sh, w, tn, out_dtype=F32):
    b, s, d = x.shape
    n = w.shape[1]
    tm = min(MM_TOKENS, s)
    return pl.pallas_call(
        _normmod_mm_kernel,
        grid=(b, s // tm, n // tn),
        in_specs=[
            pl.BlockSpec((1, tm, d), lambda bi, i, j: (bi, i, 0)),
            pl.BlockSpec((1, d), lambda bi, i, j: (0, 0)),
            pl.BlockSpec((1, 1, d), lambda bi, i, j: (bi, 0, 0)),
            pl.BlockSpec((1, 1, d), lambda bi, i, j: (bi, 0, 0)),
            pl.BlockSpec((d, tn), lambda bi, i, j: (0, j)),
        ],
        out_specs=pl.BlockSpec((1, tm, tn), lambda bi, i, j: (bi, i, j)),
        out_shape=jax.ShapeDtypeStruct((b, s, n), out_dtype),
        scratch_shapes=[pltpu.VMEM((tm, d), BF16)],
        compiler_params=_params("parallel", "parallel", "arbitrary"),
        name="normmod_mm",
    )(x, g.reshape(1, d), sc, sh, w)


def _rwkv_prep_kernel(p_ref, pp_ref, pn_ref, mu_ref, w0_ref, wup_ref, a0_ref, aup_ref, gup_ref, kk_ref, ka_ref,
                      rk_ref, ones_ref, r_out, v_out, kkn_out, lw_out, kd_out, bd_out, bv_out, g_out):
    i = pl.program_id(1)
    last = pl.num_programs(1) - 1
    pa = p_ref[0]
    tm = pa.shape[0]
    row = lax.broadcasted_iota(jnp.int32, (tm, 1), 0)
    prev_row = jnp.where(i == 0, 0.0, pp_ref[0][SUBLANES - 1:SUBLANES, :])
    next_row = jnp.where(i == last, 0.0, pn_ref[0][0:1, :])
    p_prev = jnp.where(row == 0, prev_row, pltpu.roll(pa, 1, 0))
    p_next = jnp.where(row == tm - 1, next_row, pltpu.roll(pa, tm - 1, 0))
    pa = pa + mu_ref[...] * (0.5 * (p_prev + p_next) - pa)

    w = A_WIDTH
    r = pa[:, 0:w]
    k = pa[:, w:2 * w]
    v = pa[:, 2 * w:3 * w]
    dw = pa[:, 3 * w:3 * w + 2 * A_LORA]
    da = pa[:, 3 * w + 2 * A_LORA:3 * w + 4 * A_LORA]
    dg = pa[:, 3 * w + 4 * A_LORA:A_COLS]

    lw = -DECAY_SCALE * jax.nn.sigmoid(w0_ref[...] + _dot(jnp.tanh(dw).astype(BF16), wup_ref[...]))
    icl = jax.nn.sigmoid(a0_ref[...] + _dot(da.astype(BF16), aup_ref[...]))
    g = _dot(jax.nn.sigmoid(dg).astype(BF16), gup_ref[...])

    ones_bd = ones_ref[...]
    kkr = k * kk_ref[...]
    ss = _dot_exact_rhs(kkr * kkr, ones_bd)
    kkn = kkr / jnp.maximum(jnp.sqrt(ss), 1e-12)

    r_out[0] = r.astype(r_out.dtype)
    v_out[0] = v.astype(v_out.dtype)
    kkn_out[0] = kkn.astype(kkn_out.dtype)
    lw_out[0] = lw
    g_out[0] = g
    bonus = jnp.zeros_like(r)
    for d in range(2):
        icl_d = icl[:, d * w:(d + 1) * w]
        k_d = k * (1.0 + (icl_d - 1.0) * ka_ref[...])
        kd_out[0, :, d * w:(d + 1) * w] = k_d.astype(kd_out.dtype)
        bd_out[0, :, d * w:(d + 1) * w] = (icl_d * kkn).astype(bd_out.dtype)
        bonus = bonus + _dot_exact_rhs(r * k_d * rk_ref[...], ones_bd)
    bv_out[0] = bonus * v


def _rwkv_prep(pa, wts):
    b, s, _ = pa.shape
    tm = min(256, s)
    nh = tm // SUBLANES
    w = A_WIDTH
    full = lambda shape: pl.BlockSpec(shape, lambda bi, i: (0,) * len(shape))
    tok = lambda n: pl.BlockSpec((1, tm, n), lambda bi, i: (bi, i, 0))
    out_shapes = [jax.ShapeDtypeStruct((b, s, n), dt) for n, dt in (
        (w, BF16), (w, BF16), (w, BF16), (2 * w, F32), (2 * w, BF16), (2 * w, BF16), (w, F32), (w, F32))]
    return pl.pallas_call(
        _rwkv_prep_kernel,
        grid=(b, s // tm),
        in_specs=[
            tok(A_COLS),
            pl.BlockSpec((1, SUBLANES, A_COLS), lambda bi, i: (bi, jnp.maximum(i * nh - 1, 0), 0)),
            pl.BlockSpec((1, SUBLANES, A_COLS), lambda bi, i: (bi, jnp.minimum((i + 1) * nh, s // SUBLANES - 1), 0)),
            full((1, A_COLS)), full((1, 2 * w)), full((2 * A_LORA, 2 * w)), full((1, 2 * w)),
            full((2 * A_LORA, 2 * w)), full((A_GATE_LORA, w)), full((1, w)), full((1, w)), full((1, w)),
            full((w, w)),
        ],
        out_specs=[tok(sd.shape[-1]) for sd in out_shapes],
        out_shape=out_shapes,
        compiler_params=_params("parallel", "parallel"),
        name="rwkv_prep",
    )(pa, pa, pa, wts["mu"], wts["w0"], wts["wup"], wts["a0"], wts["aup"], wts["gup"], wts["k_k"], wts["k_a"],
      wts["r_k"], wts["ones_bd"])


def _rwkv_chunk(r, v, kk, lw, kd, bd, s2, reverse):
    t = RWKV_CHUNK
    assert t == A_HEAD
    sign = -1 if reverse else 1
    ri = lax.broadcasted_iota(jnp.int32, (t, t), 0)
    ci = lax.broadcasted_iota(jnp.int32, (t, t), 1)
    tri = jnp.where((ri - ci) * sign >= 0, 1.0, 0.0).astype(BF16)
    lw_hi = lw.astype(BF16)
    rem = lw - lw_hi.astype(F32)
    lw_mid = rem.astype(BF16)
    lw_lo = (rem - lw_mid.astype(F32)).astype(BF16)
    cum = _dot(tri, lw_hi) + _dot(tri, lw_mid) + _dot(tri, lw_lo)
    yield
    tot = jnp.sum(lw, axis=0, keepdims=True)
    gam = jnp.exp(cum)
    gam_ex = jnp.exp(cum - lw)
    ginv = jnp.exp(-cum)
    gend = jnp.exp(tot - cum)
    gtot = jnp.exp(tot)

    width = r.shape[1]
    heads = width // A_HEAD
    lane_head = lax.broadcasted_iota(jnp.int32, (1, width), 1) // A_HEAD

    def stack(x):
        return jnp.concatenate([jnp.where(lane_head == h, x, 0.0) for h in range(heads)], axis=0).astype(BF16)

    def mm(x_wide, y_stack):
        return _dot(x_wide.astype(BF16), y_stack)

    ti = lax.broadcasted_iota(jnp.int32, (t, width), 0)
    si = lax.broadcasted_iota(jnp.int32, (t, width), 1) % t
    order = (ti - si) * sign
    incl = order >= 0
    strict = order > 0

    q_rk = jnp.concatenate([r * gam, kk * gam_ex], axis=0).astype(BF16)
    a = _dot(q_rk, jnp.concatenate([stack(kd * ginv), stack(bd * ginv)], axis=0), NT)
    yield
    n = heads * t
    a_rk = jnp.where(incl, a[:t, :n], 0.0)
    a_rb = jnp.where(incl, a[:t, n:], 0.0)
    a_kk = jnp.where(strict, a[t:, :n], 0.0)
    a_bk = jnp.where(strict, a[t:, n:], 0.0)

    qs = _dot(q_rk, s2.astype(BF16), NT)
    yield
    akv = mm(jnp.concatenate([a_rk, a_kk], axis=0), stack(v))
    yield

    s_n = stack(a_bk)
    n2 = mm(a_bk, s_n)
    yield
    n3 = mm(n2, s_n)
    yield
    n4 = mm(n2, stack(n2))
    yield
    s_n4 = stack(n4)
    n8 = mm(n4, s_n4)
    yield
    n12 = mm(n8, s_n4)
    yield
    f1 = jnp.where(ti == si, 1.0, 0.0) - a_bk + n2 - n3
    n16 = mm(n8, stack(n8))
    yield
    f12 = f1 + mm(f1, stack(n4 + n8 + n12))
    yield
    s_n16 = stack(n16)
    n32 = mm(n16, s_n16)
    yield
    n48 = mm(n32, s_n16)
    yield
    inv = f12 + mm(f12, stack(n16 + n32 + n48))
    yield

    ps = mm(inv, stack(-qs[t:] - akv[t:]))
    yield
    y = qs[:t] + akv[:t] + mm(a_rb, stack(ps))
    yield
    upd = _dot(jnp.concatenate([v, ps], axis=0).T.astype(BF16),
               jnp.concatenate([kd * gend, bd * gend], axis=0).astype(BF16))
    row_head = lax.broadcasted_iota(jnp.int32, (width, 1), 0) // A_HEAD
    s2_new = s2 * gtot + jnp.where(row_head == lane_head, upd, 0.0)
    return y, s2_new


def _run_lockstep(gens):
    results = [None] * len(gens)
    active = list(range(len(gens)))
    while active:
        for i in list(active):
            try:
                next(gens[i])
            except StopIteration as e:
                results[i] = e.value
                active.remove(i)
    return results


RWKV_CHAIN_LANES = 256
RWKV_CHAINS_PER_STEP = 4


def _rwkv_scan_kernel(rf_ref, vf_ref, kkf_ref, lwf_ref, kdf_ref, bdf_ref,
                      rb_ref, vb_ref, kkb_ref, lwb_ref, kdb_ref, bdb_ref, yf_ref, yb_ref, s_scr):
    @pl.when(pl.program_id(2) == 0)
    def _():
        s_scr[...] = jnp.zeros_like(s_scr)

    dirs = ((rf_ref, vf_ref, kkf_ref, lwf_ref, kdf_ref, bdf_ref, yf_ref),
            (rb_ref, vb_ref, kkb_ref, lwb_ref, kdb_ref, bdb_ref, yb_ref))
    chains = []
    for d, (r_ref, v_ref, kk_ref, lw_ref, kd_ref, bd_ref, y_ref) in enumerate(dirs):
        for p in range(RWKV_CHAINS_PER_STEP):
            sl = slice(p * RWKV_CHAIN_LANES, (p + 1) * RWKV_CHAIN_LANES)
            chains.append((d, p, sl, y_ref, _rwkv_chunk(
                r_ref[0, :, sl], v_ref[0, :, sl], kk_ref[0, :, sl], lw_ref[0, :, sl], kd_ref[0, :, sl],
                bd_ref[0, :, sl], s_scr[d, p], reverse=(d == 1))))
    results = _run_lockstep([c[4] for c in chains])
    for (d, p, sl, y_ref, _), (y, s_new) in zip(chains, results):
        y_ref[0, :, sl] = y
        s_scr[d, p] = s_new


def _rwkv_scan(r, v, kk, lw, kd, bd):
    b, s, w = r.shape
    t = RWKV_CHUNK
    nc = s // t
    pw = RWKV_CHAINS_PER_STEP * RWKV_CHAIN_LANES
    ngrp = w // pw
    fwd = lambda off: pl.BlockSpec((1, t, pw), lambda bi, g, c: (bi, c, g + off))
    bwd = lambda off: pl.BlockSpec((1, t, pw), lambda bi, g, c: (bi, nc - 1 - c, g + off))
    return pl.pallas_call(
        _rwkv_scan_kernel,
        grid=(b, ngrp, nc),
        in_specs=[fwd(0)] * 6 + [bwd(0)] * 3 + [bwd(ngrp)] * 3,
        out_specs=[fwd(0), bwd(0)],
        out_shape=[jax.ShapeDtypeStruct((b, s, w), F32)] * 2,
        scratch_shapes=[pltpu.VMEM((2, RWKV_CHAINS_PER_STEP, RWKV_CHAIN_LANES, RWKV_CHAIN_LANES), F32)],
        compiler_params=_params("parallel", "parallel", "arbitrary"),
        name="rwkv_scan",
    )(r, v, kk, lw, kd, bd, r, v, kk, lw, kd, bd)


def _rwkv_post_kernel(yf_ref, yb_ref, bv_ref, g_ref, lng_ref, lnb_ref, ones_ref, o_ref):
    y = yf_ref[0] + yb_ref[0]
    ones_bd = ones_ref[...]
    mean = _dot_exact_rhs(y, ones_bd) * (1.0 / A_HEAD)
    yc = y - mean
    var = _dot_exact_rhs(yc * yc, ones_bd) * (1.0 / A_HEAD)
    yn = yc * lax.rsqrt(var + RWKV_GN_EPS) * lng_ref[...] + lnb_ref[...]
    o_ref[0] = ((yn + bv_ref[0]) * g_ref[0]).astype(o_ref.dtype)


def _rwkv_post(yf, yb, bv, g, lng, lnb, ones_bd):
    b, s, w = bv.shape
    tm = min(512, s)
    full = lambda shape: pl.BlockSpec(shape, lambda bi, i: (0,) * len(shape))
    tok = lambda n: pl.BlockSpec((1, tm, n), lambda bi, i: (bi, i, 0))
    return pl.pallas_call(
        _rwkv_post_kernel,
        grid=(b, s // tm),
        in_specs=[tok(w), tok(w), tok(w), tok(w), full((1, w)), full((1, w)), full((w, w))],
        out_specs=tok(w),
        out_shape=jax.ShapeDtypeStruct((b, s, w), BF16),
        compiler_params=_params("parallel", "parallel"),
        name="rwkv_post",
    )(yf, yb, bv, g, lng, lnb, ones_bd)


def _s5_kernel(u_ref, m_ref, w_ref, ws_ref, v_ref, a_ref, y_ref, x_scr, xs_scr, h_scr, carry_scr, *, rows, reverse):
    g = B_GROUPS

    @pl.when(pl.program_id(1) == 0)
    def _():
        carry_scr[...] = jnp.zeros_like(carry_scr)

    u = u_ref[0]
    x_scr[...] = lax.dot_general(u, w_ref[...], BATCH_NN, preferred_element_type=F32).reshape(g * rows, LANES)
    xs_scr[...] = lax.dot_general(u, ws_ref[...], BATCH_NN, preferred_element_type=F32).reshape(g * rows, LANES)
    a1 = a_ref[0]
    a2 = a_ref[1]
    a2s = a_ref[2]

    def step(i, carry):
        h, hs = carry
        r = (rows - 1 - i) if reverse else i
        idx = pl.ds(r, g, stride=rows)
        h_scr[idx, :] = h
        hn = a1 * h + a2 * hs + x_scr[idx, :]
        hsn = a1 * hs + a2s * h + xs_scr[idx, :]
        return hn, hsn

    h, hs = lax.fori_loop(0, rows, step, (carry_scr[0], carry_scr[1]))
    carry_scr[0] = h
    carry_scr[1] = hs
    hprev = h_scr[...].reshape(g, rows, LANES).astype(BF16)
    y_ref[0] = (lax.dot_general(u, m_ref[...], BATCH_NN, preferred_element_type=F32)
                + lax.dot_general(hprev, v_ref[...], BATCH_NN, preferred_element_type=F32)).astype(y_ref.dtype)


def _s5_scan(u_g, mats, reverse):
    b, g, nrows, cw = u_g.shape
    rows = min(64, nrows)
    nsb = nrows // rows
    whole = pl.BlockSpec(memory_space=pltpu.VMEM)
    idx = (lambda bi, i: (bi, 0, nsb - 1 - i, 0)) if reverse else (lambda bi, i: (bi, 0, i, 0))
    return pl.pallas_call(
        functools.partial(_s5_kernel, rows=rows, reverse=reverse),
        grid=(b, nsb),
        in_specs=[pl.BlockSpec((1, g, rows, cw), idx), whole, whole, whole, whole, whole],
        out_specs=pl.BlockSpec((1, g, rows, cw), idx),
        out_shape=jax.ShapeDtypeStruct((b, g, nrows, cw), BF16),
        scratch_shapes=[pltpu.VMEM((g * rows, LANES), F32), pltpu.VMEM((g * rows, LANES), F32),
                        pltpu.VMEM((g * rows, LANES), F32), pltpu.VMEM((2, g, LANES), F32)],
        compiler_params=_params("parallel", "arbitrary"),
        name="s5_scan_bwd" if reverse else "s5_scan_fwd",
    )(u_g, mats["m"], mats["w"], mats["ws"], mats["v"], mats["a"])


def _s5_matrices(lam_re, lam_im, log_step, b_re, b_im, c_re, c_im, reverse):
    hp = lax.Precision.HIGHEST
    t = S5_CHUNK
    g, p, c = B_GROUPS, B_STATE, B_GROUP
    lr, li = lam_re.astype(F32), lam_im.astype(F32)
    step = jnp.exp(log_step.astype(F32))[:, None]
    ar, ai = jnp.exp(lr * step) * jnp.cos(li * step), jnp.exp(lr * step) * jnp.sin(li * step)
    den = lr * lr + li * li
    nr, ni = ar - 1.0, ai
    fr, fi = (nr * lr + ni * li) / den, (ni * lr - nr * li) / den
    br, bi = b_re.astype(F32), b_im.astype(F32)
    bbr = fr[..., None] * br - fi[..., None] * bi
    bbi = fr[..., None] * bi + fi[..., None] * br
    cr, cim = c_re.astype(F32), c_im.astype(F32)
    taus = jnp.arange(t + 1, dtype=F32)[:, None, None]
    mag = jnp.exp(lr * step * taus)
    pr, pi = mag * jnp.cos(li * step * taus), mag * jnp.sin(li * step * taus)
    cpr = cr[None] * pr[:, :, None, :] - cim[None] * pi[:, :, None, :]
    cpi = cr[None] * pi[:, :, None, :] + cim[None] * pr[:, :, None, :]
    kern = (jnp.einsum("tgcp,gpd->tgcd", cpr, bbr, precision=hp)
            - jnp.einsum("tgcp,gpd->tgcd", cpi, bbi, precision=hp))
    s_idx = jnp.arange(t)[:, None]
    t_idx = jnp.arange(t)[None, :]
    lag = (s_idx - t_idx) if reverse else (t_idx - s_idx)
    kk = jnp.where((lag >= 0)[:, :, None, None, None], kern[jnp.clip(lag, 0, t)], 0.0)
    m = jnp.transpose(kk, (2, 0, 4, 1, 3)).reshape(g, t * c, t * c)
    e = jnp.arange(t) if reverse else (t - 1 - jnp.arange(t))
    pre, pie = pr[e], pi[e]
    wre = pre[..., None] * bbr[None] - pie[..., None] * bbi[None]
    wim = pre[..., None] * bbi[None] + pie[..., None] * bbr[None]
    wre = jnp.transpose(wre, (1, 0, 3, 2)).reshape(g, t * c, p)
    wim = jnp.transpose(wim, (1, 0, 3, 2)).reshape(g, t * c, p)
    w = jnp.concatenate([wre, wim], axis=-1)
    ws = jnp.concatenate([wim, wre], axis=-1)
    f = (t - jnp.arange(t)) if reverse else (jnp.arange(t) + 1)
    vre = jnp.transpose(cpr[f], (1, 3, 0, 2)).reshape(g, p, t * c)
    vim = jnp.transpose(-cpi[f], (1, 3, 0, 2)).reshape(g, p, t * c)
    v = jnp.concatenate([vre, vim], axis=1)
    atr, ati = pr[t], pi[t]
    a = jnp.stack([jnp.concatenate([atr, atr], -1), jnp.concatenate([-ati, ati], -1),
                   jnp.concatenate([ati, -ati], -1)])
    return {"m": m.astype(BF16), "w": w.astype(BF16), "ws": ws.astype(BF16), "v": v.astype(BF16), "a": a}


def _s5_glu_kernel(ys_ref, u_ref, d_ref, w_ref, b_ref, o_ref):
    y = ys_ref[0].astype(F32) + u_ref[0] * d_ref[...]
    z = jax.nn.gelu(y)
    gate = jax.nn.sigmoid(_dot(z.astype(BF16), w_ref[...]) + b_ref[...])
    o_ref[0] = (z * gate).astype(o_ref.dtype)


def _s5_glu(ys, u, d_skip, w_glu, b_glu):
    b, s, w = u.shape
    tm = min(512, s)
    full = lambda shape: pl.BlockSpec(shape, lambda bi, i: (0,) * len(shape))
    tok = pl.BlockSpec((1, tm, w), lambda bi, i: (bi, i, 0))
    return pl.pallas_call(
        _s5_glu_kernel,
        grid=(b, s // tm),
        in_specs=[tok, tok, full((1, w)), full((w, w)), full((1, w))],
        out_specs=tok,
        out_shape=jax.ShapeDtypeStruct((b, s, w), BF16),
        compiler_params=_params("parallel", "parallel"),
        name="s5_glu",
    )(ys, u, d_skip, w_glu, b_glu)


def _out_proj_kernel(a_ref, b_ref, wa_ref, wb_ref, x_ref, g_ref, o_ref):
    mix = _dot(a_ref[0], wa_ref[...]) + _dot(b_ref[0], wb_ref[...])
    o_ref[0] = x_ref[0] + g_ref[0] * mix


def _out_proj(a, bb, wa, wb, x, gate):
    b, s, d = x.shape
    k = a.shape[-1]
    tm = min(MM_TOKENS, s)
    tn = 1024
    return pl.pallas_call(
        _out_proj_kernel,
        grid=(b, s // tm, d // tn),
        in_specs=[
            pl.BlockSpec((1, tm, k), lambda bi, i, j: (bi, i, 0)),
            pl.BlockSpec((1, tm, k), lambda bi, i, j: (bi, i, 0)),
            pl.BlockSpec((k, tn), lambda bi, i, j: (0, j)),
            pl.BlockSpec((k, tn), lambda bi, i, j: (0, j)),
            pl.BlockSpec((1, tm, tn), lambda bi, i, j: (bi, i, j)),
            pl.BlockSpec((1, 1, tn), lambda bi, i, j: (bi, 0, j)),
        ],
        out_specs=pl.BlockSpec((1, tm, tn), lambda bi, i, j: (bi, i, j)),
        out_shape=jax.ShapeDtypeStruct((b, s, d), F32),
        compiler_params=_params("parallel", "parallel", "parallel"),
        name="out_proj",
    )(a, bb, wa, wb, x, gate)


FFN_HALO = 2 * SUBLANES


def _ffn_kernel(x_ref, xp_ref, xn_ref, ng_ref, sc_ref, sh_ref, wv_ref, wg_ref, cwv_ref, cwg_ref, cbv_ref, cbg_ref,
                wd_ref, g_ref, fg_ref, o_ref, h_scr, *, final):
    i = pl.program_id(1)
    j = pl.program_id(2)
    tm = x_ref.shape[1]
    hl = FFN_HALO

    @pl.when(j == 0)
    def _():
        g, sc, sh = ng_ref[...], sc_ref[0], sh_ref[0]
        h_scr[hl:hl + tm] = _modnorm(x_ref[0], g, sc, sh).astype(BF16)
        before = jnp.where(i == 0, 0.0, _modnorm(xp_ref[0], g, sc, sh))
        after = jnp.where(i == pl.num_programs(1) - 1, 0.0, _modnorm(xn_ref[0], g, sc, sh))
        h_scr[0:hl] = before.astype(BF16)
        h_scr[hl + tm:2 * hl + tm] = after.astype(BF16)
        o_ref[...] = jnp.zeros_like(o_ref)

    h = h_scr[...]
    rows = tm + 2 * hl

    def conv(w_ref, cw_ref, cb_ref):
        u = _dot(h, w_ref[...])
        cw = cw_ref[...]
        u_prev = pltpu.roll(u, 1, 0)[hl:hl + tm]
        u_next = pltpu.roll(u, rows - 1, 0)[hl:hl + tm]
        return u_prev * cw[0:1] + u[hl:hl + tm] * cw[1:2] + u_next * cw[2:3] + cb_ref[...]

    val = conv(wv_ref, cwv_ref, cbv_ref)
    gate = conv(wg_ref, cwg_ref, cbg_ref)
    act = (gate * jax.nn.sigmoid(gate)) * val
    o_ref[0] += _dot(act.astype(BF16), wd_ref[...])

    @pl.when(j == pl.num_programs(2) - 1)
    def _():
        xn = x_ref[0] + g_ref[0] * o_ref[0]
        if final:
            xn = xn * lax.rsqrt(jnp.mean(xn * xn, axis=-1, keepdims=True) + EPS) * fg_ref[...]
        o_ref[0] = xn


def _ffn(x, norm_g, sc, sh, w_up, conv_w, conv_b, w_down, gate, final_g, final):
    b, s, d = x.shape
    f = FFN_HIDDEN
    tm = min(FFN_TOKENS, s)
    tf = FFN_HIDDEN_TILE
    nf = f // tf
    nh = tm // FFN_HALO
    nhalo = s // FFN_HALO
    const = lambda shape: pl.BlockSpec(shape, lambda bi, i, j: (0,) * len(shape))
    per_b = pl.BlockSpec((1, 1, d), lambda bi, i, j: (bi, 0, 0))
    up = lambda off: pl.BlockSpec((d, tf), lambda bi, i, j: (0, j + off))
    cw = lambda off: pl.BlockSpec((3, tf), lambda bi, i, j: (0, j + off))
    cb = lambda off: pl.BlockSpec((1, tf), lambda bi, i, j: (0, j + off))
    once = pl.Buffered(1)
    return pl.pallas_call(
        functools.partial(_ffn_kernel, final=final),
        grid=(b, s // tm, nf),
        in_specs=[
            pl.BlockSpec((1, tm, d), lambda bi, i, j: (bi, i, 0)),
            pl.BlockSpec((1, FFN_HALO, d), lambda bi, i, j: (bi, jnp.maximum(i * nh - 1, 0), 0)),
            pl.BlockSpec((1, FFN_HALO, d), lambda bi, i, j: (bi, jnp.minimum((i + 1) * nh, nhalo - 1), 0)),
            const((1, d)), per_b, per_b,
            up(0), up(nf), cw(0), cw(nf), cb(0), cb(nf),
            pl.BlockSpec((tf, d), lambda bi, i, j: (j, 0)),
            per_b, const((1, d)),
        ],
        out_specs=pl.BlockSpec((1, tm, d), lambda bi, i, j: (bi, i, 0), pipeline_mode=once),
        out_shape=jax.ShapeDtypeStruct((b, s, d), F32),
        scratch_shapes=[pltpu.VMEM((tm + 2 * FFN_HALO, d), BF16)],
        compiler_params=_params("parallel", "parallel", "arbitrary"),
        name="ffn",
    )(x, x, x, norm_g.reshape(1, d), sc, sh, w_up, w_up, conv_w, conv_w, conv_b, conv_b, w_down, gate, final_g)


def _rms(x, g, eps):
    return x * lax.rsqrt(jnp.mean(x * x, axis=-1, keepdims=True) + eps) * g


def _odd_prep_kernel(cq_ref, ckv_ref, kr_ref, dq_ref, dk_ref, dv_ref, cos_ref, sin_ref, qg_ref, kvg_ref,
                     wqa_ref, wqb_ref, wk_ref, wv_ref, place_ref,
                     mq_out, mk_out, mv_out, q0_out, q1_out, dk_out, dv_out):
    cosq = cos_ref[...]
    sinq = sin_ref[...]
    qn = _rms(cq_ref[0], qg_ref[...], EPS).astype(BF16)
    qa = _dot(qn, wqa_ref[...])
    qb = _dot(qn, wqb_ref[...])
    mla_scale = (C_NOPE + C_ROPE) ** -0.5 * LOG2E
    for h in range(C_HEADS):
        sl = slice(h * LANES, (h + 1) * LANES)
        mq_out[0, :, sl] = ((qa[:, sl] * cosq + qb[:, sl] * sinq) * mla_scale).astype(BF16)
    kvn = _rms(ckv_ref[0], kvg_ref[...], EPS).astype(BF16)
    kr = kr_ref[0]
    cos_k = pltpu.roll(cosq, LANES - C_NOPE, 1)
    sin_k = pltpu.roll(sinq, LANES - C_NOPE, 1)
    partner = pltpu.roll(kr, LANES - C_ROPE, 1)
    lane = lax.broadcasted_iota(jnp.int32, (1, LANES), 1)
    kr_rope = jnp.where(lane < C_ROPE, kr * cos_k + partner * sin_k, 0.0)
    mk_out[0] = (_dot(kvn, wk_ref[...]) + _dot(kr_rope.astype(BF16), place_ref[...])).astype(BF16)
    mv_out[0] = _dot(kvn, wv_ref[...]).astype(BF16)
    dq = dq_ref[0] * (D_HEAD ** -0.5 * LOG2E)
    lane_w = lax.broadcasted_iota(jnp.int32, (1, dq.shape[1]), 1)
    first_map = (lane_w % LANES) < D_HEAD
    q0_out[0] = jnp.where(first_map, dq, 0.0).astype(BF16)
    q1_out[0] = jnp.where(first_map, 0.0, dq).astype(BF16)
    dk_out[0] = dk_ref[0].astype(BF16)
    dv_out[0] = dv_ref[0].astype(BF16)


ODD_DQ = 0
ODD_DK = ODD_DQ + D_HEADS * 2 * D_HEAD
ODD_DV = ODD_DK + D_HEADS * 2 * D_HEAD
ODD_CQ = ODD_DV + D_HEADS * D_V
ODD_CKV = ODD_CQ + C_Q_RANK
ODD_KR = ODD_CKV + C_KV_RANK
ODD_COLS = 4096


def _odd_prep(p, cos_t, sin_t, wts):
    b, s, _ = p.shape
    tm = min(256, s)
    hw = C_HEADS * LANES
    full = lambda shape: pl.BlockSpec(shape, lambda bi, i: (0,) * len(shape))
    col = lambda off, n: pl.BlockSpec((1, tm, n), lambda bi, i: (bi, i, off // n))
    tok = pl.BlockSpec((1, tm, hw), lambda bi, i: (bi, i, 0))
    tab = pl.BlockSpec((tm, LANES), lambda bi, i: (i, 0))
    outs = [jax.ShapeDtypeStruct((b, s, hw), BF16)] * 7
    return pl.pallas_call(
        _odd_prep_kernel,
        grid=(b, s // tm),
        in_specs=[
            col(ODD_CQ, C_Q_RANK), col(ODD_CKV, C_KV_RANK), col(ODD_KR, LANES),
            col(ODD_DQ, hw), col(ODD_DK, hw), col(ODD_DV, hw), tab, tab,
            full((1, C_Q_RANK)), full((1, C_KV_RANK)),
            full((C_Q_RANK, hw)), full((C_Q_RANK, hw)), full((C_KV_RANK, hw)), full((C_KV_RANK, hw)),
            full((LANES, hw)),
        ],
        out_specs=[tok] * 7,
        out_shape=outs,
        compiler_params=_params("parallel", "parallel"),
        name="odd_prep",
    )(p, p, p, p, p, p, cos_t, sin_t, wts["q_norm_g"], wts["kv_norm_g"], wts["wqa"], wts["wqb"], wts["wk"],
      wts["wv"], wts["place"])


def _with_ones(v):
    return jnp.concatenate([v, jnp.ones_like(v)], axis=1)


def _flash_chain(q, k, v1, bias, shift, m_scr, acc_scr):
    s = _dot(q, k, NT)
    yield
    if isinstance(bias, tuple):
        row0, strip = bias
        row1 = row0 + strip.shape[0]
        parts = [s[:row0]] * (row0 > 0) + [s[row0:row1] + strip] + [s[row1:]] * (row1 < s.shape[0])
        s = jnp.concatenate(parts, axis=0)
    elif bias is not None:
        s = s + bias
    m_prev = m_scr[...]
    m_new = jnp.maximum(m_prev, jnp.max(s, axis=-1, keepdims=True) + shift)
    alpha = jnp.exp2(m_prev - m_new)
    p = jnp.exp2(s - jnp.tile(m_new - shift, (1, s.shape[1] // LANES))).astype(BF16)
    yield
    acc_scr[...] = jnp.tile(alpha, (1, 2)) * acc_scr[...] + _dot(p, v1)
    m_scr[...] = m_new


MLA_HEADS_PER_STEP = 4


def _mla_flash_kernel(q_ref, k_ref, v_ref, o_ref, m_scr, acc_scr):
    kj = pl.program_id(3)

    @pl.when(kj == 0)
    def _():
        m_scr[...] = jnp.full_like(m_scr, -jnp.inf)
        acc_scr[...] = jnp.zeros_like(acc_scr)

    heads = [slice(h * LANES, (h + 1) * LANES) for h in range(MLA_HEADS_PER_STEP)]
    _run_lockstep([_flash_chain(q_ref[0, :, sl], k_ref[0, :, sl], _with_ones(v_ref[0, :, sl]), None, 0.0,
                                m_scr.at[h], acc_scr.at[h]) for h, sl in enumerate(heads)])

    @pl.when(kj == pl.num_programs(3) - 1)
    def _():
        for h, sl in enumerate(heads):
            acc = acc_scr[h]
            o_ref[0, :, sl] = (acc[:, :LANES] / acc[:, LANES:]).astype(o_ref.dtype)


def _mla_flash(q, k, v):
    b, s, hw = q.shape
    gw = MLA_HEADS_PER_STEP * LANES
    tq = min(ATT_TQ, s)
    tk = min(ATT_TK, s)
    qspec = pl.BlockSpec((1, tq, gw), lambda bi, h, i, j: (bi, i, h))
    kspec = pl.BlockSpec((1, tk, gw), lambda bi, h, i, j: (bi, j, h))
    return pl.pallas_call(
        _mla_flash_kernel,
        grid=(b, hw // gw, s // tq, s // tk),
        in_specs=[qspec, kspec, kspec],
        out_specs=qspec,
        out_shape=jax.ShapeDtypeStruct((b, s, hw), BF16),
        scratch_shapes=[pltpu.VMEM((MLA_HEADS_PER_STEP, tq, LANES), F32),
                        pltpu.VMEM((MLA_HEADS_PER_STEP, tq, 2 * LANES), F32)],
        compiler_params=_params("parallel", "parallel", "parallel", "arbitrary"),
        name="mla_flash",
    )(q, k, v)


BIAS_HALF = 256


DIFF_HEADS_PER_STEP = 2


def _diff_flash_kernel(q0_ref, q1_ref, k_ref, v_ref, tab_ref, lq1_ref, lk1_ref, lq2_ref, lk2_ref, sg_ref, o_ref,
                       m_scr, acc_scr, bias_scr, *, tq, tk, near, lambda_init):
    kj = pl.program_id(3)
    off = kj * tk - pl.program_id(2) * tq
    heads = [slice(h * LANES, (h + 1) * LANES) for h in range(DIFF_HEADS_PER_STEP)]

    @pl.when(kj == 0)
    def _():
        m_scr[...] = jnp.full_like(m_scr, -jnp.inf)
        acc_scr[...] = jnp.zeros_like(acc_scr)

    tabs = [tab_ref[h] for h in range(DIFF_HEADS_PER_STEP)]
    far_left = [tab[:, 0:1] for tab in tabs]
    far_right = [tab[:, 2 * BIAS_HALF - 1:2 * BIAS_HALF] for tab in tabs]

    def update(biases, shifts):
        chains = []
        for h, sl in enumerate(heads):
            k = k_ref[0, :, sl]
            v1 = _with_ones(v_ref[0, :, sl])
            for m, q_ref in enumerate((q0_ref, q1_ref)):
                chains.append(_flash_chain(q_ref[0, :, sl], k, v1, biases[h], shifts[h], m_scr.at[h, m],
                                           acc_scr.at[h, m]))
        _run_lockstep(chains)

    def toeplitz(tab, d):
        r = tab[:, d + LANES:d + 3 * LANES]
        rows = jnp.broadcast_to(r, (LANES, 2 * LANES))
        return pltpu.roll(rows, LANES, 1, stride=1, stride_axis=0)[:, :LANES]

    def block_kinds(d0):
        kinds = {}
        for ri in range(tq // LANES):
            for cj in range(tk // LANES):
                d = d0 + (cj - ri) * LANES
                kinds[ri, cj] = 'L' if d <= -BIAS_HALF else 'R' if d >= BIAS_HALF else d
        return kinds

    for d0 in near:
        kinds = block_kinds(d0)
        band_rows = {ri for (ri, _), kind in kinds.items() if not isinstance(kind, str)}
        sides = {kind for kind in kinds.values() if isinstance(kind, str)}

        if len(band_rows) == 1 and len(sides) == 1:
            @pl.when(off == d0)
            def _(kinds=kinds, ri=band_rows.pop(), side=sides.pop()):
                const = far_left if side == 'L' else far_right
                for h in range(DIFF_HEADS_PER_STEP):
                    for cj in range(tk // LANES):
                        kind = kinds[ri, cj]
                        corr = (jnp.zeros((LANES, LANES), F32) if isinstance(kind, str)
                                else toeplitz(tabs[h], kind) - const[h])
                        bias_scr[h, 0:LANES, cj * LANES:(cj + 1) * LANES] = corr
                update([(ri * LANES, bias_scr[h, 0:LANES, :]) for h in range(DIFF_HEADS_PER_STEP)], const)
        else:
            @pl.when(off == d0)
            def _(kinds=kinds):
                for h in range(DIFF_HEADS_PER_STEP):
                    blocks = {}
                    for (ri, cj), kind in kinds.items():
                        sl = (h, slice(ri * LANES, (ri + 1) * LANES), slice(cj * LANES, (cj + 1) * LANES))
                        if kind == 'L':
                            bias_scr[sl] = jnp.broadcast_to(far_left[h], (LANES, LANES))
                        elif kind == 'R':
                            bias_scr[sl] = jnp.broadcast_to(far_right[h], (LANES, LANES))
                        else:
                            if kind not in blocks:
                                blocks[kind] = toeplitz(tabs[h], kind)
                            bias_scr[sl] = blocks[kind]
                update([bias_scr[h] for h in range(DIFF_HEADS_PER_STEP)], [0.0] * DIFF_HEADS_PER_STEP)

    @pl.when(jnp.logical_or(off < near[0], off > near[-1]))
    def _():
        update([None] * DIFF_HEADS_PER_STEP,
               [jnp.where(off < near[0], lo, hi) for lo, hi in zip(far_left, far_right)])

    @pl.when(kj == pl.num_programs(3) - 1)
    def _():
        lam = (jnp.exp(jnp.sum(lq1_ref[...] * lk1_ref[...], axis=-1, keepdims=True))
               - jnp.exp(jnp.sum(lq2_ref[...] * lk2_ref[...], axis=-1, keepdims=True)) + lambda_init)
        for h, sl in enumerate(heads):
            a0 = acc_scr[h, 0]
            a1 = acc_scr[h, 1]
            o = a0[:, :LANES] / a0[:, LANES:] - lam * (a1[:, :LANES] / a1[:, LANES:])
            o = _rms(o, sg_ref[...], SUBLN_EPS) * (1.0 - lambda_init)
            o_ref[0, :, sl] = o.astype(o_ref.dtype)


def _diff_flash(q0, q1, k, v, tab, lq1, lk1, lq2, lk2, subln_g, lambda_init):
    b, s, hw = q0.shape
    hps = DIFF_HEADS_PER_STEP
    gw = hps * LANES
    tq = min(ATT_TQ, s)
    tk = min(ATT_TK, s)
    offs = sorted({j * tk - i * tq for i in range(s // tq) for j in range(s // tk)})
    near = tuple(d for d in offs if d - (tq - 1) < BIAS_HALF and d + tk - 1 > -BIAS_HALF)
    assert near == tuple(d for d in offs if near[0] <= d <= near[-1])
    qspec = pl.BlockSpec((1, tq, gw), lambda bi, h, i, j: (bi, i, h))
    kspec = pl.BlockSpec((1, tk, gw), lambda bi, h, i, j: (bi, j, h))
    vec = lambda n: pl.BlockSpec((1, n), lambda bi, h, i, j: (0, 0))
    return pl.pallas_call(
        functools.partial(_diff_flash_kernel, tq=tq, tk=tk, near=near, lambda_init=lambda_init),
        grid=(b, hw // gw, s // tq, s // tk),
        in_specs=[qspec, qspec, kspec, kspec,
                  pl.BlockSpec((hps, 1, 2 * BIAS_HALF), lambda bi, h, i, j: (h, 0, 0)),
                  vec(D_HEAD), vec(D_HEAD), vec(D_HEAD), vec(D_HEAD), vec(D_V)],
        out_specs=qspec,
        out_shape=jax.ShapeDtypeStruct((b, s, hw), BF16),
        scratch_shapes=[pltpu.VMEM((hps, 2, tq, LANES), F32), pltpu.VMEM((hps, 2, tq, 2 * LANES), F32),
                        pltpu.VMEM((hps, tq, tk), F32)],
        compiler_params=_params("parallel", "parallel", "parallel", "arbitrary"),
        name="diff_flash",
    )(q0, q1, k, v, tab, lq1, lk1, lq2, lk2, subln_g)


def _t5_bucket(rel):
    half = N_BUCKETS // 2
    max_exact = half // 2
    n = jnp.abs(rel)
    large = max_exact + (jnp.log(jnp.maximum(n, 1).astype(jnp.float32) / max_exact)
                         / math.log(MAX_DISTANCE / max_exact) * (half - max_exact)).astype(jnp.int32)
    large = jnp.minimum(large, half - 1)
    return jnp.where(rel > 0, half, 0) + jnp.where(n < max_exact, n, large)


def _rope_tables(s):
    inv = 1.0 / (ROPE_THETA ** (jnp.arange(0, C_ROPE, 2, dtype=F32) / C_ROPE))
    ang = jnp.arange(s, dtype=F32)[:, None] * inv[None, :]
    cos, sin = jnp.cos(ang), jnp.sin(ang)
    pad = LANES - C_NOPE - C_ROPE
    cos_t = jnp.concatenate([jnp.ones((s, C_NOPE), F32), cos, cos, jnp.zeros((s, pad), F32)], axis=-1)
    sin_t = jnp.concatenate([jnp.zeros((s, C_NOPE), F32), sin, sin, jnp.zeros((s, pad), F32)], axis=-1)
    return cos_t, sin_t


def _rot_half_cols(w):
    h = w.shape[-1] // 2
    return jnp.concatenate([-w[..., h:], w[..., :h]], axis=-1)


def _pack_even(j, even_w_in, even_w_out, rwkv_mu, rwkv_w0, rwkv_w_up, rwkv_a0, rwkv_a_up, rwkv_g_up, rwkv_k_k,
               rwkv_k_a, rwkv_r_k, rwkv_lnx_g, rwkv_lnx_b):
    w = A_WIDTH
    z = jnp.zeros((A_LORA, w), F32)
    blockdiag = lambda m: jnp.concatenate(
        [jnp.concatenate([m[0], z], axis=1), jnp.concatenate([z, m[1]], axis=1)], axis=0)
    head = jnp.arange(w) // A_HEAD
    return {
        "w_in_a": even_w_in[j][:, :A_COLS].astype(BF16),
        "w_in_b": even_w_in[j][:, A_COLS:].astype(BF16),
        "w_out_a": even_w_out[j][:w].astype(BF16),
        "w_out_b": even_w_out[j][w:].astype(BF16),
        "mu": rwkv_mu[j].reshape(1, A_COLS),
        "w0": rwkv_w0[j].reshape(1, 2 * w),
        "wup": blockdiag(rwkv_w_up[j]).astype(BF16),
        "a0": rwkv_a0[j].reshape(1, 2 * w),
        "aup": blockdiag(rwkv_a_up[j]).astype(BF16),
        "gup": rwkv_g_up[j].astype(BF16),
        "k_k": rwkv_k_k[j].reshape(1, w),
        "k_a": rwkv_k_a[j].reshape(1, w),
        "r_k": rwkv_r_k[j].reshape(1, w),
        "lnx_g": rwkv_lnx_g[j].reshape(1, w),
        "lnx_b": rwkv_lnx_b[j].reshape(1, w),
        "ones_bd": (head[:, None] == head[None, :]).astype(BF16),
    }


def _pack_odd(j, odd_w_in, odd_w_out, mla_q_norm_g, mla_kv_norm_g, mla_w_uq, mla_w_ukv):
    d = D_MODEL
    w_in = odd_w_in[j]
    o_cq, o_ckv = 0, C_Q_RANK
    o_kr = o_ckv + C_KV_RANK
    o_dq = o_kr + C_ROPE
    n_d = D_HEADS * 2 * D_HEAD
    w_kr = w_in[:, o_kr:o_kr + C_ROPE]
    packed = jnp.concatenate([
        w_in[:, o_dq:o_dq + 3 * n_d], w_in[:, o_cq:o_kr], w_kr, _rot_half_cols(w_kr),
        jnp.zeros((d, ODD_COLS - (C_Q_RANK + C_KV_RANK + 3 * n_d + 2 * C_ROPE)), F32)], axis=1)
    pad = LANES - C_NOPE - C_ROPE
    wq = mla_w_uq[j].reshape(C_Q_RANK, C_HEADS, C_NOPE + C_ROPE)
    zq = jnp.zeros((C_Q_RANK, C_HEADS, pad), F32)
    wqa = jnp.concatenate([wq, zq], axis=-1).reshape(C_Q_RANK, C_HEADS * LANES)
    wqb = jnp.concatenate([jnp.zeros((C_Q_RANK, C_HEADS, C_NOPE), F32), _rot_half_cols(wq[..., C_NOPE:]), zq],
                          axis=-1).reshape(C_Q_RANK, C_HEADS * LANES)
    wkv = mla_w_ukv[j].reshape(C_KV_RANK, C_HEADS, C_NOPE + C_V)
    wk = jnp.concatenate([wkv[..., :C_NOPE], jnp.zeros((C_KV_RANK, C_HEADS, LANES - C_NOPE), F32)],
                         axis=-1).reshape(C_KV_RANK, C_HEADS * LANES)
    wv = wkv[..., C_NOPE:].reshape(C_KV_RANK, C_HEADS * C_V)
    src = jnp.arange(LANES)[:, None]
    dst = jnp.arange(C_HEADS * LANES)[None, :] % LANES
    place = ((dst >= C_NOPE) & (dst < C_NOPE + C_ROPE) & (dst - C_NOPE == src)).astype(BF16)
    hv = C_HEADS * C_V
    return {
        "w_in": packed.astype(BF16),
        "w_out_a": odd_w_out[j][:hv].astype(BF16),
        "w_out_b": odd_w_out[j][hv:].astype(BF16),
        "q_norm_g": mla_q_norm_g[j].reshape(1, C_Q_RANK),
        "kv_norm_g": mla_kv_norm_g[j].reshape(1, C_KV_RANK),
        "wqa": wqa.astype(BF16), "wqb": wqb.astype(BF16), "wk": wk.astype(BF16), "wv": wv.astype(BF16),
        "place": place,
    }


def _even_mixers(x, g1n, sc1, sh1, ew, s5m, s5_d, s5_w_glu, s5_b_glu):
    b, s, _ = x.shape
    pa = _normmod_mm(x, g1n, sc1, sh1, ew["w_in_a"], tn=A_COLS // 3)
    u = _normmod_mm(x, g1n, sc1, sh1, ew["w_in_b"], tn=B_WIDTH)
    r, v, kk, lw, kd, bd, bv, g = _rwkv_prep(pa, ew)
    yf, yr = _rwkv_scan(r, v, kk, lw, kd, bd)
    ya = _rwkv_post(yf, yr, bv, g, ew["lnx_g"], ew["lnx_b"], ew["ones_bd"])
    nrows = s // S5_CHUNK
    u_g = jnp.transpose(u.astype(BF16).reshape(b, nrows, S5_CHUNK, B_GROUPS, B_GROUP), (0, 3, 1, 2, 4))
    u_g = u_g.reshape(b, B_GROUPS, nrows, S5_CHUNK * B_GROUP)
    ys_g = (_s5_scan(u_g, s5m[0], False).astype(F32) + _s5_scan(u_g, s5m[1], True).astype(F32)).astype(BF16)
    ys = jnp.transpose(ys_g.reshape(b, B_GROUPS, nrows, S5_CHUNK, B_GROUP), (0, 2, 3, 1, 4)).reshape(b, s, B_WIDTH)
    yb = _s5_glu(ys, u, s5_d, s5_w_glu, s5_b_glu)
    return ya, yb


def _odd_mixers(x, g1n, sc1, sh1, ow, tabs, diff_w, lambda_init):
    s = x.shape[1]
    p = _normmod_mm(x, g1n, sc1, sh1, ow["w_in"], tn=1024)
    mq, mk, mv, q0, q1, dk, dv = _odd_prep(p, tabs["cos"][:s], tabs["sin"][:s], ow)
    yc = _mla_flash(mq, mk, mv)
    yd = _diff_flash(q0, q1, dk, dv, tabs["bias"], diff_w["lq1"], diff_w["lk1"], diff_w["lq2"], diff_w["lk2"],
                     diff_w["subln_g"], lambda_init)
    return yc, yd


def kernel(x_prompt, x_sample, c_prompt, c_sample, ada_w, ada_b, norm1_g, norm2_g, even_w_in, even_w_out, rwkv_mu, rwkv_w0, rwkv_w_up, rwkv_a0, rwkv_a_up, rwkv_g_up, rwkv_k_k, rwkv_k_a, rwkv_r_k, rwkv_lnx_g, rwkv_lnx_b, s5_lam_re, s5_lam_im, s5_log_step, s5_b_re, s5_b_im, s5_c_re, s5_c_im, s5_d, s5_w_glu, s5_b_glu, odd_w_in, odd_w_out, mla_q_norm_g, mla_kv_norm_g, mla_w_uq, mla_w_ukv, diff_lq1, diff_lk1, diff_lq2, diff_lk2, diff_subln_g, rel_bias, ffn_w_up, ffn_conv_w, ffn_conv_b, ffn_w_down, final_g):
    d = D_MODEL
    groups = [(x_prompt, c_prompt), (x_sample, c_sample)]
    nb = [g[0].shape[0] for g in groups]
    c_all = jnp.concatenate([g[1] for g in groups] + [jnp.zeros((SUBLANES - sum(nb), d), F32)], axis=0)
    mod = _ada_mod(c_all, ada_w, ada_b)

    max_s = max(g[0].shape[1] for g in groups)
    cos_t, sin_t = _rope_tables(max_s)
    rel = jnp.arange(-BIAS_HALF, BIAS_HALF, dtype=jnp.int32)
    bias_tab = (jnp.transpose(rel_bias.astype(F32)[_t5_bucket(rel)]) * LOG2E).reshape(D_HEADS, 1, 2 * BIAS_HALF)
    tabs = {"cos": cos_t, "sin": sin_t, "bias": bias_tab}

    xs = [g[0] for g in groups]
    for i in range(DEPTH):
        j = i // 2
        if i % 2 == 0:
            ew = _pack_even(j, even_w_in, even_w_out, rwkv_mu, rwkv_w0, rwkv_w_up, rwkv_a0, rwkv_a_up, rwkv_g_up,
                            rwkv_k_k, rwkv_k_a, rwkv_r_k, rwkv_lnx_g, rwkv_lnx_b)
            s5m = [_s5_matrices(s5_lam_re[j, dr], s5_lam_im[j, dr], s5_log_step[j, dr], s5_b_re[j, dr],
                                s5_b_im[j, dr], s5_c_re[j, dr], s5_c_im[j, dr], dr == 1) for dr in range(2)]
        else:
            ow = _pack_odd(j, odd_w_in, odd_w_out, mla_q_norm_g, mla_kv_norm_g, mla_w_uq, mla_w_ukv)
            diff_w = {"lq1": diff_lq1[j].reshape(1, D_HEAD), "lk1": diff_lk1[j].reshape(1, D_HEAD),
                      "lq2": diff_lq2[j].reshape(1, D_HEAD), "lk2": diff_lk2[j].reshape(1, D_HEAD),
                      "subln_g": diff_subln_g[j].reshape(1, D_V)}
        w_up = ffn_w_up[i].astype(BF16)
        w_down = ffn_w_down[i].astype(BF16)
        row0 = 0
        for gi in range(len(groups)):
            x = xs[gi]
            m = mod[i, row0:row0 + nb[gi]]
            row0 += nb[gi]
            sh1, sc1, g1, sh2, sc2, g2 = [m[:, None, k * d:(k + 1) * d] for k in range(N_MOD)]
            if i % 2 == 0:
                ya, yb = _even_mixers(x, norm1_g[i], sc1, sh1, ew, s5m, s5_d[j].reshape(1, B_WIDTH),
                                      s5_w_glu[j].astype(BF16), s5_b_glu[j].reshape(1, B_WIDTH))
                x = _out_proj(ya, yb, ew["w_out_a"], ew["w_out_b"], x, g1)
            else:
                yc, yd = _odd_mixers(x, norm1_g[i], sc1, sh1, ow, tabs, diff_w, 0.8 - 0.6 * math.exp(-0.3 * i))
                x = _out_proj(yc, yd, ow["w_out_a"], ow["w_out_b"], x, g1)
            x = _ffn(x, norm2_g[i], sc2, sh2, w_up, ffn_conv_w[i], ffn_conv_b[i].reshape(1, 2 * FFN_HIDDEN), w_down,
                     g2, final_g.reshape(1, d), final=(i == DEPTH - 1))
            xs[gi] = x
    return (xs[0], xs[1])
```

```python
import functools
import math

import jax
import jax.numpy as jnp
from jax import lax
from jax.experimental import pallas as pl
from jax.experimental.pallas import tpu as pltpu

F32 = jnp.float32
BF16 = jnp.bfloat16

D_MODEL = 2048
DEPTH = 2
EPS = 1e-6
A_WIDTH = 1024
A_HEAD = 64
A_HEADS = 16
A_LORA = 64
A_GATE_LORA = 128
A_COLS = 3 * A_WIDTH + 2 * A_LORA + 2 * A_LORA + A_GATE_LORA
RWKV_GN_EPS = 64e-5
DECAY_SCALE = math.exp(-0.5)
B_WIDTH = 1024
B_GROUP = 16
B_GROUPS = 64
B_STATE = 64
C_HEADS = 8
C_NOPE = 64
C_ROPE = 32
C_V = 128
C_Q_RANK = 512
C_KV_RANK = 256
ROPE_THETA = 10000.0
D_HEADS = 8
D_HEAD = 64
D_V = 128
SUBLN_EPS = 1e-5
N_BUCKETS = 32
MAX_DISTANCE = 128
FFN_HIDDEN = 5632
N_MOD = 6

LANES = 128
SUBLANES = 8
VMEM_LIMIT_BYTES = 56 * 1024 * 1024
S5_VMEM_LIMIT_BYTES = 60 * 1024 * 1024

MM_TOKENS = 1024
FFN_TOKENS = 1024
FFN_HIDDEN_TILE = 512
ATT_TQ = 1024
ATT_TK = 1024

RWKV_CHUNK = 64
S5_CHUNK = 16
S5_ROWS = 128
LOG2E = 1.4426950408889634

NN = (((1,), (0,)), ((), ()))
NT = (((1,), (1,)), ((), ()))
BATCH_NN = (((2,), (1,)), ((0,), (0,)))


def _params(*sem, vmem_limit_bytes=VMEM_LIMIT_BYTES):
    return pltpu.CompilerParams(dimension_semantics=sem, vmem_limit_bytes=vmem_limit_bytes)


def _dot(a, b, dims=NN):
    return lax.dot_general(a, b, dims, preferred_element_type=F32)


def _split2(x):
    hi = x.astype(BF16)
    lo = (x - hi.astype(F32)).astype(BF16)
    return hi, lo


def _dot_exact_rhs(a, b):
    ah, al = _split2(a)
    return _dot(ah, b) + _dot(al, b)


def _ada_kernel(c_ref, w_ref, b_ref, o_ref):
    c = c_ref[...]
    cs = c * jax.nn.sigmoid(c)
    o_ref[0] = _dot(cs.astype(BF16), w_ref[0].astype(BF16)) + b_ref[0]


def _ada_mod(c_all, ada_w, ada_b):
    n = N_MOD * D_MODEL
    tn = 1024
    return pl.pallas_call(
        _ada_kernel,
        grid=(DEPTH, n // tn),
        in_specs=[
            pl.BlockSpec((SUBLANES, D_MODEL), lambda l, j: (0, 0)),
            pl.BlockSpec((1, D_MODEL, tn), lambda l, j: (l, 0, j)),
            pl.BlockSpec((1, 1, tn), lambda l, j: (l, 0, j)),
        ],
        out_specs=pl.BlockSpec((1, SUBLANES, tn), lambda l, j: (l, 0, j)),
        out_shape=jax.ShapeDtypeStruct((DEPTH, SUBLANES, n), F32),
        compiler_params=_params("parallel", "parallel"),
        name="ada_mod",
    )(c_all, ada_w, ada_b.reshape(DEPTH, 1, n))


def _modnorm(x, g, sc, sh):
    y = x * lax.rsqrt(jnp.mean(x * x, axis=-1, keepdims=True) + EPS)
    return (y * g) * (1.0 + sc) + sh


def _normmod_mm_kernel(x_ref, g_ref, sc_ref, sh_ref, w_ref, o_ref, h_scr):
    @pl.when(pl.program_id(2) == 0)
    def _():
        h_scr[...] = _modnorm(x_ref[0], g_ref[...], sc_ref[0], sh_ref[0]).astype(BF16)

    o_ref[0] = _dot(h_scr[...], w_ref[...]).astype(o_ref.dtype)


def _normmod_mm(x, g, sc, sh, w, tn, out_dtype=F32):
    b, s, d = x.shape
    n = w.shape[1]
    tm = min(MM_TOKENS, s)
    return pl.pallas_call(
        _normmod_mm_kernel,
        grid=(b, s // tm, n // tn),
        in_specs=[
            pl.BlockSpec((1, tm, d), lambda bi, i, j: (bi, i, 0)),
            pl.BlockSpec((1, d), lambda bi, i, j: (0, 0)),
            pl.BlockSpec((1, 1, d), lambda bi, i, j: (bi, 0, 0)),
            pl.BlockSpec((1, 1, d), lambda bi, i, j: (bi, 0, 0)),
            pl.BlockSpec((d, tn), lambda bi, i, j: (0, j)),
        ],
        out_specs=pl.BlockSpec((1, tm, tn), lambda bi, i, j: (bi, i, j)),
        out_shape=jax.ShapeDtypeStruct((b, s, n), out_dtype),
        scratch_shapes=[pltpu.VMEM((tm, d), BF16)],
        compiler_params=_params("parallel", "parallel", "arbitrary"),
        name="normmod_mm",
    )(x, g.reshape(1, d), sc, sh, w)


def _rwkv_prep_kernel(p_ref, pp_ref, pn_ref, mu_ref, w0_ref, wup_ref, a0_ref, aup_ref, gup_ref, kk_ref, ka_ref,
                      rk_ref, ones_ref, r_out, v_out, kkn_out, lw_out, kd_out, bd_out, bv_out, g_out):
    i = pl.program_id(1)
    last = pl.num_programs(1) - 1
    pa = p_ref[0]
    tm = pa.shape[0]
    row = lax.broadcasted_iota(jnp.int32, (tm, 1), 0)
    prev_row = jnp.where(i == 0, 0.0, pp_ref[0][SUBLANES - 1:SUBLANES, :])
    next_row = jnp.where(i == last, 0.0, pn_ref[0][0:1, :])
    p_prev = jnp.where(row == 0, prev_row, pltpu.roll(pa, 1, 0))
    p_next = jnp.where(row == tm - 1, next_row, pltpu.roll(pa, tm - 1, 0))
    pa = pa + mu_ref[...] * (0.5 * (p_prev + p_next) - pa)

    w = A_WIDTH
    r = pa[:, 0:w]
    k = pa[:, w:2 * w]
    v = pa[:, 2 * w:3 * w]
    dw = pa[:, 3 * w:3 * w + 2 * A_LORA]
    da = pa[:, 3 * w + 2 * A_LORA:3 * w + 4 * A_LORA]
    dg = pa[:, 3 * w + 4 * A_LORA:A_COLS]

    lw = -DECAY_SCALE * jax.nn.sigmoid(w0_ref[...] + _dot(jnp.tanh(dw).astype(BF16), wup_ref[...]))
    icl = jax.nn.sigmoid(a0_ref[...] + _dot(da.astype(BF16), aup_ref[...]))
    g = _dot(jax.nn.sigmoid(dg).astype(BF16), gup_ref[...])

    ones_bd = ones_ref[...]
    kkr = k * kk_ref[...]
    ss = _dot_exact_rhs(kkr * kkr, ones_bd)
    kkn = kkr / jnp.maximum(jnp.sqrt(ss), 1e-12)

    r_out[0] = r.astype(r_out.dtype)
    v_out[0] = v.astype(v_out.dtype)
    kkn_out[0] = kkn.astype(kkn_out.dtype)
    lw_out[0] = lw
    g_out[0] = g
    bonus = jnp.zeros_like(r)
    for d in range(2):
        icl_d = icl[:, d * w:(d + 1) * w]
        k_d = k * (1.0 + (icl_d - 1.0) * ka_ref[...])
        kd_out[0, :, d * w:(d + 1) * w] = k_d.astype(kd_out.dtype)
        bd_out[0, :, d * w:(d + 1) * w] = (icl_d * kkn).astype(bd_out.dtype)
        bonus = bonus + _dot_exact_rhs(r * k_d * rk_ref[...], ones_bd)
    bv_out[0] = bonus * v


def _rwkv_prep(pa, wts):
    b, s, _ = pa.shape
    tm = min(256, s)
    nh = tm // SUBLANES
    w = A_WIDTH
    full = lambda shape: pl.BlockSpec(shape, lambda bi, i: (0,) * len(shape))
    tok = lambda n: pl.BlockSpec((1, tm, n), lambda bi, i: (bi, i, 0))
    out_shapes = [jax.ShapeDtypeStruct((b, s, n), dt) for n, dt in (
        (w, BF16), (w, BF16), (w, BF16), (2 * w, F32), (2 * w, BF16), (2 * w, BF16), (w, F32), (w, F32))]
    return pl.pallas_call(
        _rwkv_prep_kernel,
        grid=(b, s // tm),
        in_specs=[
            tok(A_COLS),
            pl.BlockSpec((1, SUBLANES, A_COLS), lambda bi, i: (bi, jnp.maximum(i * nh - 1, 0), 0)),
            pl.BlockSpec((1, SUBLANES, A_COLS), lambda bi, i: (bi, jnp.minimum((i + 1) * nh, s // SUBLANES - 1), 0)),
            full((1, A_COLS)), full((1, 2 * w)), full((2 * A_LORA, 2 * w)), full((1, 2 * w)),
            full((2 * A_LORA, 2 * w)), full((A_GATE_LORA, w)), full((1, w)), full((1, w)), full((1, w)),
            full((w, w)),
        ],
        out_specs=[tok(sd.shape[-1]) for sd in out_shapes],
        out_shape=out_shapes,
        compiler_params=_params("parallel", "parallel"),
        name="rwkv_prep",
    )(pa, pa, pa, wts["mu"], wts["w0"], wts["wup"], wts["a0"], wts["aup"], wts["gup"], wts["k_k"], wts["k_a"],
      wts["r_k"], wts["ones_bd"])


def _rwkv_chunk(r, v, kk, lw, kd, bd, s2, reverse):
    t = RWKV_CHUNK
    assert t == A_HEAD
    sign = -1 if reverse else 1
    ri = lax.broadcasted_iota(jnp.int32, (t, t), 0)
    ci = lax.broadcasted_iota(jnp.int32, (t, t), 1)
    tri = jnp.where((ri - ci) * sign >= 0, 1.0, 0.0).astype(BF16)
    lw_hi = lw.astype(BF16)
    rem = lw - lw_hi.astype(F32)
    lw_mid = rem.astype(BF16)
    lw_lo = (rem - lw_mid.astype(F32)).astype(BF16)
    cum = _dot(tri, lw_hi) + _dot(tri, lw_mid) + _dot(tri, lw_lo)
    yield
    tot = jnp.sum(lw, axis=0, keepdims=True)
    gam = jnp.exp(cum)
    gam_ex = jnp.exp(cum - lw)
    ginv = jnp.exp(-cum)
    gend = jnp.exp(tot - cum)
    gtot = jnp.exp(tot)

    width = r.shape[1]
    heads = width // A_HEAD
    lane_head = lax.broadcasted_iota(jnp.int32, (1, width), 1) // A_HEAD

    def stack(x):
        return jnp.concatenate([jnp.where(lane_head == h, x, 0.0) for h in range(heads)], axis=0).astype(BF16)

    def mm(x_wide, y_stack):
        return _dot(x_wide.astype(BF16), y_stack)

    ti = lax.broadcasted_iota(jnp.int32, (t, width), 0)
    si = lax.broadcasted_iota(jnp.int32, (t, width), 1) % t
    order = (ti - si) * sign
    incl = order >= 0
    strict = order > 0

    q_rk = jnp.concatenate([r * gam, kk * gam_ex], axis=0).astype(BF16)
    a = _dot(q_rk, jnp.concatenate([stack(kd * ginv), stack(bd * ginv)], axis=0), NT)
    yield
    n = heads * t
    a_rk = jnp.where(incl, a[:t, :n], 0.0)
    a_rb = jnp.where(incl, a[:t, n:], 0.0)
    a_kk = jnp.where(strict, a[t:, :n], 0.0)
    a_bk = jnp.where(strict, a[t:, n:], 0.0)

    qs = _dot(q_rk, s2.astype(BF16), NT)
    yield
    akv = mm(jnp.concatenate([a_rk, a_kk], axis=0), stack(v))
    yield

    s_n = stack(a_bk)
    n2 = mm(a_bk, s_n)
    yield
    n3 = mm(n2, s_n)
    yield
    n4 = mm(n2, stack(n2))
    yield
    s_n4 = stack(n4)
    n8 = mm(n4, s_n4)
    yield
    n12 = mm(n8, s_n4)
    yield
    f1 = jnp.where(ti == si, 1.0, 0.0) - a_bk + n2 - n3
    n16 = mm(n8, stack(n8))
    yield
    f12 = f1 + mm(f1, stack(n4 + n8 + n12))
    yield
    s_n16 = stack(n16)
    n32 = mm(n16, s_n16)
    yield
    n48 = mm(n32, s_n16)
    yield
    inv = f12 + mm(f12, stack(n16 + n32 + n48))
    yield

    ps = mm(inv, stack(-qs[t:] - akv[t:]))
    yield
    y = qs[:t] + akv[:t] + mm(a_rb, stack(ps))
    yield
    upd = _dot(jnp.concatenate([v, ps], axis=0).T.astype(BF16),
               jnp.concatenate([kd * gend, bd * gend], axis=0).astype(BF16))
    row_head = lax.broadcasted_iota(jnp.int32, (width, 1), 0) // A_HEAD
    s2_new = s2 * gtot + jnp.where(row_head == lane_head, upd, 0.0)
    return y, s2_new


def _run_lockstep(gens):
    results = [None] * len(gens)
    active = list(range(len(gens)))
    while active:
        for i in list(active):
            try:
                next(gens[i])
            except StopIteration as e:
                results[i] = e.value
                active.remove(i)
    return results


RWKV_CHAIN_LANES = 256
RWKV_CHAINS_PER_STEP = 4


def _rwkv_scan_kernel(rf_ref, vf_ref, kkf_ref, lwf_ref, kdf_ref, bdf_ref,
                      rb_ref, vb_ref, kkb_ref, lwb_ref, kdb_ref, bdb_ref, yf_ref, yb_ref, s_scr):
    @pl.when(pl.program_id(2) == 0)
    def _():
        s_scr[...] = jnp.zeros_like(s_scr)

    dirs = ((rf_ref, vf_ref, kkf_ref, lwf_ref, kdf_ref, bdf_ref, yf_ref),
            (rb_ref, vb_ref, kkb_ref, lwb_ref, kdb_ref, bdb_ref, yb_ref))
    chains = []
    for d, (r_ref, v_ref, kk_ref, lw_ref, kd_ref, bd_ref, y_ref) in enumerate(dirs):
        for p in range(RWKV_CHAINS_PER_STEP):
            sl = slice(p * RWKV_CHAIN_LANES, (p + 1) * RWKV_CHAIN_LANES)
            chains.append((d, p, sl, y_ref, _rwkv_chunk(
                r_ref[0, :, sl], v_ref[0, :, sl], kk_ref[0, :, sl], lw_ref[0, :, sl], kd_ref[0, :, sl],
                bd_ref[0, :, sl], s_scr[d, p], reverse=(d == 1))))
    results = _run_lockstep([c[4] for c in chains])
    for (d, p, sl, y_ref, _), (y, s_new) in zip(chains, results):
        y_ref[0, :, sl] = y
        s_scr[d, p] = s_new


def _rwkv_scan(r, v, kk, lw, kd, bd):
    b, s, w = r.shape
    t = RWKV_CHUNK
    nc = s // t
    pw = RWKV_CHAINS_PER_STEP * RWKV_CHAIN_LANES
    ngrp = w // pw
    fwd = lambda off: pl.BlockSpec((1, t, pw), lambda bi, g, c: (bi, c, g + off))
    bwd = lambda off: pl.BlockSpec((1, t, pw), lambda bi, g, c: (bi, nc - 1 - c, g + off))
    return pl.pallas_call(
        _rwkv_scan_kernel,
        grid=(b, ngrp, nc),
        in_specs=[fwd(0)] * 6 + [bwd(0)] * 3 + [bwd(ngrp)] * 3,
        out_specs=[fwd(0), bwd(0)],
        out_shape=[jax.ShapeDtypeStruct((b, s, w), F32)] * 2,
        scratch_shapes=[pltpu.VMEM((2, RWKV_CHAINS_PER_STEP, RWKV_CHAIN_LANES, RWKV_CHAIN_LANES), F32)],
        compiler_params=_params("parallel", "parallel", "arbitrary"),
        name="rwkv_scan",
    )(r, v, kk, lw, kd, bd, r, v, kk, lw, kd, bd)


def _rwkv_post_kernel(yf_ref, yb_ref, bv_ref, g_ref, lng_ref, lnb_ref, ones_ref, o_ref):
    y = yf_ref[0] + yb_ref[0]
    ones_bd = ones_ref[...]
    mean = _dot_exact_rhs(y, ones_bd) * (1.0 / A_HEAD)
    yc = y - mean
    var = _dot_exact_rhs(yc * yc, ones_bd) * (1.0 / A_HEAD)
    yn = yc * lax.rsqrt(var + RWKV_GN_EPS) * lng_ref[...] + lnb_ref[...]
    o_ref[0] = ((yn + bv_ref[0]) * g_ref[0]).astype(o_ref.dtype)


def _rwkv_post(yf, yb, bv, g, lng, lnb, ones_bd):
    b, s, w = bv.shape
    tm = min(512, s)
    full = lambda shape: pl.BlockSpec(shape, lambda bi, i: (0,) * len(shape))
    tok = lambda n: pl.BlockSpec((1, tm, n), lambda bi, i: (bi, i, 0))
    return pl.pallas_call(
        _rwkv_post_kernel,
        grid=(b, s // tm),
        in_specs=[tok(w), tok(w), tok(w), tok(w), full((1, w)), full((1, w)), full((w, w))],
        out_specs=tok(w),
        out_shape=jax.ShapeDtypeStruct((b, s, w), BF16),
        compiler_params=_params("parallel", "parallel"),
        name="rwkv_post",
    )(yf, yb, bv, g, lng, lnb, ones_bd)


def _s5_kernel(u_ref, m_ref, w_ref, ws_ref, v_ref, a_ref, y_ref, x_scr, xs_scr, h_scr, carry_scr, *, rows, reverse):
    g = B_GROUPS

    @pl.when(pl.program_id(1) == 0)
    def _():
        carry_scr[...] = jnp.zeros_like(carry_scr)

    u = u_ref[0]
    x_scr[...] = lax.dot_general(u, w_ref[...], BATCH_NN, preferred_element_type=F32).reshape(g * rows, LANES)
    xs_scr[...] = lax.dot_general(u, ws_ref[...], BATCH_NN, preferred_element_type=F32).reshape(g * rows, LANES)
    a1 = a_ref[0]
    a2 = a_ref[1]
    a2s = a_ref[2]

    def step(i, carry):
        h, hs = carry
        r = (rows - 1 - i) if reverse else i
        idx = pl.ds(r, g, stride=rows)
        h_scr[idx, :] = h
        hn = a1 * h + a2 * hs + x_scr[idx, :]
        hsn = a1 * hs + a2s * h + xs_scr[idx, :]
        return hn, hsn

    h, hs = lax.fori_loop(0, rows, step, (carry_scr[0], carry_scr[1]))
    carry_scr[0] = h
    carry_scr[1] = hs
    hprev = h_scr[...].reshape(g, rows, LANES).astype(BF16)
    y_ref[0] = (lax.dot_general(u, m_ref[...], BATCH_NN, preferred_element_type=F32)
                + lax.dot_general(hprev, v_ref[...], BATCH_NN, preferred_element_type=F32)).astype(y_ref.dtype)


def _s5_scan(u_g, mats, reverse):
    b, g, nrows, cw = u_g.shape
    rows = min(S5_ROWS, nrows)
    nsb = nrows // rows
    whole = pl.BlockSpec(memory_space=pltpu.VMEM)
    idx = (lambda bi, i: (bi, 0, nsb - 1 - i, 0)) if reverse else (lambda bi, i: (bi, 0, i, 0))
    return pl.pallas_call(
        functools.partial(_s5_kernel, rows=rows, reverse=reverse),
        grid=(b, nsb),
        in_specs=[pl.BlockSpec((1, g, rows, cw), idx), whole, whole, whole, whole, whole],
        out_specs=pl.BlockSpec((1, g, rows, cw), idx),
        out_shape=jax.ShapeDtypeStruct((b, g, nrows, cw), BF16),
        scratch_shapes=[pltpu.VMEM((g * rows, LANES), F32), pltpu.VMEM((g * rows, LANES), F32),
                        pltpu.VMEM((g * rows, LANES), F32), pltpu.VMEM((2, g, LANES), F32)],
        compiler_params=_params("parallel", "arbitrary", vmem_limit_bytes=S5_VMEM_LIMIT_BYTES),
        name="s5_scan_bwd" if reverse else "s5_scan_fwd",
    )(u_g, mats["m"], mats["w"], mats["ws"], mats["v"], mats["a"])


def _s5_matrices(lam_re, lam_im, log_step, b_re, b_im, c_re, c_im, reverse):
    hp = lax.Precision.HIGHEST
    t = S5_CHUNK
    g, p, c = B_GROUPS, B_STATE, B_GROUP
    lr, li = lam_re.astype(F32), lam_im.astype(F32)
    step = jnp.exp(log_step.astype(F32))[:, None]
    ar, ai = jnp.exp(lr * step) * jnp.cos(li * step), jnp.exp(lr * step) * jnp.sin(li * step)
    den = lr * lr + li * li
    nr, ni = ar - 1.0, ai
    fr, fi = (nr * lr + ni * li) / den, (ni * lr - nr * li) / den
    br, bi = b_re.astype(F32), b_im.astype(F32)
    bbr = fr[..., None] * br - fi[..., None] * bi
    bbi = fr[..., None] * bi + fi[..., None] * br
    cr, cim = c_re.astype(F32), c_im.astype(F32)
    taus = jnp.arange(t + 1, dtype=F32)[:, None, None]
    mag = jnp.exp(lr * step * taus)
    pr, pi = mag * jnp.cos(li * step * taus), mag * jnp.sin(li * step * taus)
    cpr = cr[None] * pr[:, :, None, :] - cim[None] * pi[:, :, None, :]
    cpi = cr[None] * pi[:, :, None, :] + cim[None] * pr[:, :, None, :]
    kern = (jnp.einsum("tgcp,gpd->tgcd", cpr, bbr, precision=hp)
            - jnp.einsum("tgcp,gpd->tgcd", cpi, bbi, precision=hp))
    s_idx = jnp.arange(t)[:, None]
    t_idx = jnp.arange(t)[None, :]
    lag = (s_idx - t_idx) if reverse else (t_idx - s_idx)
    kk = jnp.where((lag >= 0)[:, :, None, None, None], kern[jnp.clip(lag, 0, t)], 0.0)
    m = jnp.transpose(kk, (2, 0, 4, 1, 3)).reshape(g, t * c, t * c)
    e = jnp.arange(t) if reverse else (t - 1 - jnp.arange(t))
    pre, pie = pr[e], pi[e]
    wre = pre[..., None] * bbr[None] - pie[..., None] * bbi[None]
    wim = pre[..., None] * bbi[None] + pie[..., None] * bbr[None]
    wre = jnp.transpose(wre, (1, 0, 3, 2)).reshape(g, t * c, p)
    wim = jnp.transpose(wim, (1, 0, 3, 2)).reshape(g, t * c, p)
    w = jnp.concatenate([wre, wim], axis=-1)
    ws = jnp.concatenate([wim, wre], axis=-1)
    f = (t - jnp.arange(t)) if reverse else (jnp.arange(t) + 1)
    vre = jnp.transpose(cpr[f], (1, 3, 0, 2)).reshape(g, p, t * c)
    vim = jnp.transpose(-cpi[f], (1, 3, 0, 2)).reshape(g, p, t * c)
    v = jnp.concatenate([vre, vim], axis=1)
    atr, ati = pr[t], pi[t]
    a = jnp.stack([jnp.concatenate([atr, atr], -1), jnp.concatenate([-ati, ati], -1),
                   jnp.concatenate([ati, -ati], -1)])
    return {"m": m.astype(BF16), "w": w.astype(BF16), "ws": ws.astype(BF16), "v": v.astype(BF16), "a": a}


def _s5_glu_kernel(ys_ref, u_ref, d_ref, w_ref, b_ref, o_ref):
    y = ys_ref[0].astype(F32) + u_ref[0] * d_ref[...]
    z = jax.nn.gelu(y)
    gate = jax.nn.sigmoid(_dot(z.astype(BF16), w_ref[...]) + b_ref[...])
    o_ref[0] = (z * gate).astype(o_ref.dtype)


def _s5_glu(ys, u, d_skip, w_glu, b_glu):
    b, s, w = u.shape
    tm = min(512, s)
    full = lambda shape: pl.BlockSpec(shape, lambda bi, i: (0,) * len(shape))
    tok = pl.BlockSpec((1, tm, w), lambda bi, i: (bi, i, 0))
    return pl.pallas_call(
        _s5_glu_kernel,
        grid=(b, s // tm),
        in_specs=[tok, tok, full((1, w)), full((w, w)), full((1, w))],
        out_specs=tok,
        out_shape=jax.ShapeDtypeStruct((b, s, w), BF16),
        compiler_params=_params("parallel", "parallel"),
        name="s5_glu",
    )(ys, u, d_skip, w_glu, b_glu)


def _out_proj_kernel(a_ref, b_ref, wa_ref, wb_ref, x_ref, g_ref, o_ref):
    mix = _dot(a_ref[0], wa_ref[...]) + _dot(b_ref[0], wb_ref[...])
    o_ref[0] = x_ref[0] + g_ref[0] * mix


def _out_proj(a, bb, wa, wb, x, gate):
    b, s, d = x.shape
    k = a.shape[-1]
    tm = min(MM_TOKENS, s)
    tn = 1024
    return pl.pallas_call(
        _out_proj_kernel,
        grid=(b, s // tm, d // tn),
        in_specs=[
            pl.BlockSpec((1, tm, k), lambda bi, i, j: (bi, i, 0)),
            pl.BlockSpec((1, tm, k), lambda bi, i, j: (bi, i, 0)),
            pl.BlockSpec((k, tn), lambda bi, i, j: (0, j)),
            pl.BlockSpec((k, tn), lambda bi, i, j: (0, j)),
            pl.BlockSpec((1, tm, tn), lambda bi, i, j: (bi, i, j)),
            pl.BlockSpec((1, 1, tn), lambda bi, i, j: (bi, 0, j)),
        ],
        out_specs=pl.BlockSpec((1, tm, tn), lambda bi, i, j: (bi, i, j)),
        out_shape=jax.ShapeDtypeStruct((b, s, d), F32),
        compiler_params=_params("parallel", "parallel", "parallel"),
        name="out_proj",
    )(a, bb, wa, wb, x, gate)


FFN_HALO = 2 * SUBLANES


def _ffn_kernel(x_ref, xp_ref, xn_ref, ng_ref, sc_ref, sh_ref, wv_ref, wg_ref, cwv_ref, cwg_ref, cbv_ref, cbg_ref,
                wd_ref, g_ref, fg_ref, o_ref, h_scr, *, final):
    i = pl.program_id(1)
    j = pl.program_id(2)
    tm = x_ref.shape[1]
    hl = FFN_HALO

    @pl.when(j == 0)
    def _():
        g, sc, sh = ng_ref[...], sc_ref[0], sh_ref[0]
        h_scr[hl:hl + tm] = _modnorm(x_ref[0], g, sc, sh).astype(BF16)
        before = jnp.where(i == 0, 0.0, _modnorm(xp_ref[0], g, sc, sh))
        after = jnp.where(i == pl.num_programs(1) - 1, 0.0, _modnorm(xn_ref[0], g, sc, sh))
        h_scr[0:hl] = before.astype(BF16)
        h_scr[hl + tm:2 * hl + tm] = after.astype(BF16)
        o_ref[...] = jnp.zeros_like(o_ref)

    h = h_scr[...]
    rows = tm + 2 * hl

    def conv(w_ref, cw_ref, cb_ref):
        u = _dot(h, w_ref[...])
        cw = cw_ref[...]
        u_prev = pltpu.roll(u, 1, 0)[hl:hl + tm]
        u_next = pltpu.roll(u, rows - 1, 0)[hl:hl + tm]
        return u_prev * cw[0:1] + u[hl:hl + tm] * cw[1:2] + u_next * cw[2:3] + cb_ref[...]

    val = conv(wv_ref, cwv_ref, cbv_ref)
    gate = conv(wg_ref, cwg_ref, cbg_ref)
    act = (gate * jax.nn.sigmoid(gate)) * val
    o_ref[0] += _dot(act.astype(BF16), wd_ref[...])

    @pl.when(j == pl.num_programs(2) - 1)
    def _():
        xn = x_ref[0] + g_ref[0] * o_ref[0]
        if final:
            xn = xn * lax.rsqrt(jnp.mean(xn * xn, axis=-1, keepdims=True) + EPS) * fg_ref[...]
        o_ref[0] = xn


def _ffn(x, norm_g, sc, sh, w_up, conv_w, conv_b, w_down, gate, final_g, final):
    b, s, d = x.shape
    f = FFN_HIDDEN
    tm = min(FFN_TOKENS, s)
    tf = FFN_HIDDEN_TILE
    nf = f // tf
    nh = tm // FFN_HALO
    nhalo = s // FFN_HALO
    const = lambda shape: pl.BlockSpec(shape, lambda bi, i, j: (0,) * len(shape))
    per_b = pl.BlockSpec((1, 1, d), lambda bi, i, j: (bi, 0, 0))
    up = lambda off: pl.BlockSpec((d, tf), lambda bi, i, j: (0, j + off))
    cw = lambda off: pl.BlockSpec((3, tf), lambda bi, i, j: (0, j + off))
    cb = lambda off: pl.BlockSpec((1, tf), lambda bi, i, j: (0, j + off))
    once = pl.Buffered(1)
    return pl.pallas_call(
        functools.partial(_ffn_kernel, final=final),
        grid=(b, s // tm, nf),
        in_specs=[
            pl.BlockSpec((1, tm, d), lambda bi, i, j: (bi, i, 0)),
            pl.BlockSpec((1, FFN_HALO, d), lambda bi, i, j: (bi, jnp.maximum(i * nh - 1, 0), 0)),
            pl.BlockSpec((1, FFN_HALO, d), lambda bi, i, j: (bi, jnp.minimum((i + 1) * nh, nhalo - 1), 0)),
            const((1, d)), per_b, per_b,
            up(0), up(nf), cw(0), cw(nf), cb(0), cb(nf),
            pl.BlockSpec((tf, d), lambda bi, i, j: (j, 0)),
            per_b, const((1, d)),
        ],
        out_specs=pl.BlockSpec((1, tm, d), lambda bi, i, j: (bi, i, 0), pipeline_mode=once),
        out_shape=jax.ShapeDtypeStruct((b, s, d), F32),
        scratch_shapes=[pltpu.VMEM((tm + 2 * FFN_HALO, d), BF16)],
        compiler_params=_params("parallel", "parallel", "arbitrary"),
        name="ffn",
    )(x, x, x, norm_g.reshape(1, d), sc, sh, w_up, w_up, conv_w, conv_w, conv_b, conv_b, w_down, gate, final_g)


def _rms(x, g, eps):
    return x * lax.rsqrt(jnp.mean(x * x, axis=-1, keepdims=True) + eps) * g


def _odd_prep_kernel(cq_ref, ckv_ref, kr_ref, dq_ref, dk_ref, dv_ref, cos_ref, sin_ref, qg_ref, kvg_ref,
                     wqa_ref, wqb_ref, wk_ref, wv_ref, place_ref,
                     mq_out, mk_out, mv_out, q0_out, q1_out, dk_out, dv_out):
    cosq = cos_ref[...]
    sinq = sin_ref[...]
    qn = _rms(cq_ref[0], qg_ref[...], EPS).astype(BF16)
    qa = _dot(qn, wqa_ref[...])
    qb = _dot(qn, wqb_ref[...])
    mla_scale = (C_NOPE + C_ROPE) ** -0.5 * LOG2E
    for h in range(C_HEADS):
        sl = slice(h * LANES, (h + 1) * LANES)
        mq_out[0, :, sl] = ((qa[:, sl] * cosq + qb[:, sl] * sinq) * mla_scale).astype(BF16)
    kvn = _rms(ckv_ref[0], kvg_ref[...], EPS).astype(BF16)
    kr = kr_ref[0]
    cos_k = pltpu.roll(cosq, LANES - C_NOPE, 1)
    sin_k = pltpu.roll(sinq, LANES - C_NOPE, 1)
    partner = pltpu.roll(kr, LANES - C_ROPE, 1)
    lane = lax.broadcasted_iota(jnp.int32, (1, LANES), 1)
    kr_rope = jnp.where(lane < C_ROPE, kr * cos_k + partner * sin_k, 0.0)
    mk_out[0] = (_dot(kvn, wk_ref[...]) + _dot(kr_rope.astype(BF16), place_ref[...])).astype(BF16)
    mv_out[0] = _dot(kvn, wv_ref[...]).astype(BF16)
    dq = dq_ref[0] * (D_HEAD ** -0.5 * LOG2E)
    lane_w = lax.broadcasted_iota(jnp.int32, (1, dq.shape[1]), 1)
    first_map = (lane_w % LANES) < D_HEAD
    q0_out[0] = jnp.where(first_map, dq, 0.0).astype(BF16)
    q1_out[0] = jnp.where(first_map, 0.0, dq).astype(BF16)
    dk_out[0] = dk_ref[0].astype(BF16)
    dv_out[0] = dv_ref[0].astype(BF16)


ODD_DQ = 0
ODD_DK = ODD_DQ + D_HEADS * 2 * D_HEAD
ODD_DV = ODD_DK + D_HEADS * 2 * D_HEAD
ODD_CQ = ODD_DV + D_HEADS * D_V
ODD_CKV = ODD_CQ + C_Q_RANK
ODD_KR = ODD_CKV + C_KV_RANK
ODD_COLS = 4096


def _odd_prep(p, cos_t, sin_t, wts):
    b, s, _ = p.shape
    tm = min(512, s)
    hw = C_HEADS * LANES
    full = lambda shape: pl.BlockSpec(shape, lambda bi, i: (0,) * len(shape))
    col = lambda off, n: pl.BlockSpec((1, tm, n), lambda bi, i: (bi, i, off // n))
    tok = pl.BlockSpec((1, tm, hw), lambda bi, i: (bi, i, 0))
    tab = pl.BlockSpec((tm, LANES), lambda bi, i: (i, 0))
    outs = [jax.ShapeDtypeStruct((b, s, hw), BF16)] * 7
    return pl.pallas_call(
        _odd_prep_kernel,
        grid=(b, s // tm),
        in_specs=[
            col(ODD_CQ, C_Q_RANK), col(ODD_CKV, C_KV_RANK), col(ODD_KR, LANES),
            col(ODD_DQ, hw), col(ODD_DK, hw), col(ODD_DV, hw), tab, tab,
            full((1, C_Q_RANK)), full((1, C_KV_RANK)),
            full((C_Q_RANK, hw)), full((C_Q_RANK, hw)), full((C_KV_RANK, hw)), full((C_KV_RANK, hw)),
            full((LANES, hw)),
        ],
        out_specs=[tok] * 7,
        out_shape=outs,
        compiler_params=_params("parallel", "parallel"),
        name="odd_prep",
    )(p, p, p, p, p, p, cos_t, sin_t, wts["q_norm_g"], wts["kv_norm_g"], wts["wqa"], wts["wqb"], wts["wk"],
      wts["wv"], wts["place"])


def _with_ones(v):
    return jnp.concatenate([v, jnp.ones_like(v)], axis=1)


def _flash_chain(q, k, v1, bias, shift, m_scr, acc_scr):
    s = _dot(q, k, NT)
    yield
    if isinstance(bias, tuple):
        row0, strip = bias
        row1 = row0 + strip.shape[0]
        parts = [s[:row0]] * (row0 > 0) + [s[row0:row1] + strip] + [s[row1:]] * (row1 < s.shape[0])
        s = jnp.concatenate(parts, axis=0)
    elif bias is not None:
        s = s + bias
    m_prev = m_scr[...]
    m_new = jnp.maximum(m_prev, jnp.max(s, axis=-1, keepdims=True) + shift)
    alpha = jnp.exp2(m_prev - m_new)
    p = jnp.exp2(s - jnp.tile(m_new - shift, (1, s.shape[1] // LANES))).astype(BF16)
    yield
    acc_scr[...] = jnp.tile(alpha, (1, 2)) * acc_scr[...] + _dot(p, v1)
    m_scr[...] = m_new


MLA_HEADS_PER_STEP = 4


def _mla_flash_kernel(q_ref, k_ref, v_ref, o_ref, m_scr, acc_scr):
    kj = pl.program_id(3)

    @pl.when(kj == 0)
    def _():
        m_scr[...] = jnp.full_like(m_scr, -jnp.inf)
        acc_scr[...] = jnp.zeros_like(acc_scr)

    heads = [slice(h * LANES, (h + 1) * LANES) for h in range(MLA_HEADS_PER_STEP)]
    _run_lockstep([_flash_chain(q_ref[0, :, sl], k_ref[0, :, sl], _with_ones(v_ref[0, :, sl]), None, 0.0,
                                m_scr.at[h], acc_scr.at[h]) for h, sl in enumerate(heads)])

    @pl.when(kj == pl.num_programs(3) - 1)
    def _():
        for h, sl in enumerate(heads):
            acc = acc_scr[h]
            o_ref[0, :, sl] = (acc[:, :LANES] / acc[:, LANES:]).astype(o_ref.dtype)


def _mla_flash(q, k, v):
    b, s, hw = q.shape
    gw = MLA_HEADS_PER_STEP * LANES
    tq = min(ATT_TQ, s)
    tk = min(ATT_TK, s)
    qspec = pl.BlockSpec((1, tq, gw), lambda bi, h, i, j: (bi, i, h))
    kspec = pl.BlockSpec((1, tk, gw), lambda bi, h, i, j: (bi, j, h))
    return pl.pallas_call(
        _mla_flash_kernel,
        grid=(b, hw // gw, s // tq, s // tk),
        in_specs=[qspec, kspec, kspec],
        out_specs=qspec,
        out_shape=jax.ShapeDtypeStruct((b, s, hw), BF16),
        scratch_shapes=[pltpu.VMEM((MLA_HEADS_PER_STEP, tq, LANES), F32),
                        pltpu.VMEM((MLA_HEADS_PER_STEP, tq, 2 * LANES), F32)],
        compiler_params=_params("parallel", "parallel", "parallel", "arbitrary"),
        name="mla_flash",
    )(q, k, v)


BIAS_HALF = 256


DIFF_HEADS_PER_STEP = 2


def _diff_flash_kernel(q0_ref, q1_ref, k_ref, v_ref, tab_ref, lq1_ref, lk1_ref, lq2_ref, lk2_ref, sg_ref, o_ref,
                       m_scr, acc_scr, bias_scr, *, tq, tk, near, lambda_init):
    kj = pl.program_id(3)
    off = kj * tk - pl.program_id(2) * tq
    heads = [slice(h * LANES, (h + 1) * LANES) for h in range(DIFF_HEADS_PER_STEP)]

    @pl.when(kj == 0)
    def _():
        m_scr[...] = jnp.full_like(m_scr, -jnp.inf)
        acc_scr[...] = jnp.zeros_like(acc_scr)

    tabs = [tab_ref[h] for h in range(DIFF_HEADS_PER_STEP)]
    far_left = [tab[:, 0:1] for tab in tabs]
    far_right = [tab[:, 2 * BIAS_HALF - 1:2 * BIAS_HALF] for tab in tabs]

    def update(biases, shifts):
        chains = []
        for h, sl in enumerate(heads):
            k = k_ref[0, :, sl]
            v1 = _with_ones(v_ref[0, :, sl])
            for m, q_ref in enumerate((q0_ref, q1_ref)):
                chains.append(_flash_chain(q_ref[0, :, sl], k, v1, biases[h], shifts[h], m_scr.at[h, m],
                                           acc_scr.at[h, m]))
        _run_lockstep(chains)

    def toeplitz(tab, d):
        r = tab[:, d + LANES:d + 3 * LANES]
        rows = jnp.broadcast_to(r, (LANES, 2 * LANES))
        return pltpu.roll(rows, LANES, 1, stride=1, stride_axis=0)[:, :LANES]

    def block_kinds(d0):
        kinds = {}
        for ri in range(tq // LANES):
            for cj in range(tk // LANES):
                d = d0 + (cj - ri) * LANES
                kinds[ri, cj] = 'L' if d <= -BIAS_HALF else 'R' if d >= BIAS_HALF else d
        return kinds

    for d0 in near:
        kinds = block_kinds(d0)
        band_rows = {ri for (ri, _), kind in kinds.items() if not isinstance(kind, str)}
        sides = {kind for kind in kinds.values() if isinstance(kind, str)}

        if len(band_rows) == 1 and len(sides) == 1:
            @pl.when(off == d0)
            def _(kinds=kinds, ri=band_rows.pop(), side=sides.pop()):
                const = far_left if side == 'L' else far_right
                for h in range(DIFF_HEADS_PER_STEP):
                    for cj in range(tk // LANES):
                        kind = kinds[ri, cj]
                        corr = (jnp.zeros((LANES, LANES), F32) if isinstance(kind, str)
                                else toeplitz(tabs[h], kind) - const[h])
                        bias_scr[h, 0:LANES, cj * LANES:(cj + 1) * LANES] = corr
                update([(ri * LANES, bias_scr[h, 0:LANES, :]) for h in range(DIFF_HEADS_PER_STEP)], const)
        else:
            @pl.when(off == d0)
            def _(kinds=kinds):
                for h in range(DIFF_HEADS_PER_STEP):
                    blocks = {}
                    for (ri, cj), kind in kinds.items():
                        sl = (h, slice(ri * LANES, (ri + 1) * LANES), slice(cj * LANES, (cj + 1) * LANES))
                        if kind == 'L':
                            bias_scr[sl] = jnp.broadcast_to(far_left[h], (LANES, LANES))
                        elif kind == 'R':
                            bias_scr[sl] = jnp.broadcast_to(far_right[h], (LANES, LANES))
                        else:
                            if kind not in blocks:
                                blocks[kind] = toeplitz(tabs[h], kind)
                            bias_scr[sl] = blocks[kind]
                update([bias_scr[h] for h in range(DIFF_HEADS_PER_STEP)], [0.0] * DIFF_HEADS_PER_STEP)

    @pl.when(jnp.logical_or(off < near[0], off > near[-1]))
    def _():
        update([None] * DIFF_HEADS_PER_STEP,
               [jnp.where(off < near[0], lo, hi) for lo, hi in zip(far_left, far_right)])

    @pl.when(kj == pl.num_programs(3) - 1)
    def _():
        lam = (jnp.exp(jnp.sum(lq1_ref[...] * lk1_ref[...], axis=-1, keepdims=True))
               - jnp.exp(jnp.sum(lq2_ref[...] * lk2_ref[...], axis=-1, keepdims=True)) + lambda_init)
        for h, sl in enumerate(heads):
            a0 = acc_scr[h, 0]
            a1 = acc_scr[h, 1]
            o = a0[:, :LANES] / a0[:, LANES:] - lam * (a1[:, :LANES] / a1[:, LANES:])
            o = _rms(o, sg_ref[...], SUBLN_EPS) * (1.0 - lambda_init)
            o_ref[0, :, sl] = o.astype(o_ref.dtype)


def _diff_flash(q0, q1, k, v, tab, lq1, lk1, lq2, lk2, subln_g, lambda_init):
    b, s, hw = q0.shape
    hps = DIFF_HEADS_PER_STEP
    gw = hps * LANES
    tq = min(ATT_TQ, s)
    tk = min(ATT_TK, s)
    offs = sorted({j * tk - i * tq for i in range(s // tq) for j in range(s // tk)})
    near = tuple(d for d in offs if d - (tq - 1) < BIAS_HALF and d + tk - 1 > -BIAS_HALF)
    assert near == tuple(d for d in offs if near[0] <= d <= near[-1])
    qspec = pl.BlockSpec((1, tq, gw), lambda bi, h, i, j: (bi, i, h))
    kspec = pl.BlockSpec((1, tk, gw), lambda bi, h, i, j: (bi, j, h))
    vec = lambda n: pl.BlockSpec((1, n), lambda bi, h, i, j: (0, 0))
    return pl.pallas_call(
        functools.partial(_diff_flash_kernel, tq=tq, tk=tk, near=near, lambda_init=lambda_init),
        grid=(b, hw // gw, s // tq, s // tk),
        in_specs=[qspec, qspec, kspec, kspec,
                  pl.BlockSpec((hps, 1, 2 * BIAS_HALF), lambda bi, h, i, j: (h, 0, 0)),
                  vec(D_HEAD), vec(D_HEAD), vec(D_HEAD), vec(D_HEAD), vec(D_V)],
        out_specs=qspec,
        out_shape=jax.ShapeDtypeStruct((b, s, hw), BF16),
        scratch_shapes=[pltpu.VMEM((hps, 2, tq, LANES), F32), pltpu.VMEM((hps, 2, tq, 2 * LANES), F32),
                        pltpu.VMEM((hps, tq, tk), F32)],
        compiler_params=_params("parallel", "parallel", "parallel", "arbitrary"),
        name="diff_flash",
    )(q0, q1, k, v, tab, lq1, lk1, lq2, lk2, subln_g)


def _t5_bucket(rel):
    half = N_BUCKETS // 2
    max_exact = half // 2
    n = jnp.abs(rel)
    large = max_exact + (jnp.log(jnp.maximum(n, 1).astype(jnp.float32) / max_exact)
                         / math.log(MAX_DISTANCE / max_exact) * (half - max_exact)).astype(jnp.int32)
    large = jnp.minimum(large, half - 1)
    return jnp.where(rel > 0, half, 0) + jnp.where(n < max_exact, n, large)


def _rope_tables(s):
    inv = 1.0 / (ROPE_THETA ** (jnp.arange(0, C_ROPE, 2, dtype=F32) / C_ROPE))
    ang = jnp.arange(s, dtype=F32)[:, None] * inv[None, :]
    cos, sin = jnp.cos(ang), jnp.sin(ang)
    pad = LANES - C_NOPE - C_ROPE
    cos_t = jnp.concatenate([jnp.ones((s, C_NOPE), F32), cos, cos, jnp.zeros((s, pad), F32)], axis=-1)
    sin_t = jnp.concatenate([jnp.zeros((s, C_NOPE), F32), sin, sin, jnp.zeros((s, pad), F32)], axis=-1)
    return cos_t, sin_t


def _rot_half_cols(w):
    h = w.shape[-1] // 2
    return jnp.concatenate([-w[..., h:], w[..., :h]], axis=-1)


def _pack_even(j, even_w_in, even_w_out, rwkv_mu, rwkv_w0, rwkv_w_up, rwkv_a0, rwkv_a_up, rwkv_g_up, rwkv_k_k,
               rwkv_k_a, rwkv_r_k, rwkv_lnx_g, rwkv_lnx_b):
    w = A_WIDTH
    z = jnp.zeros((A_LORA, w), F32)
    blockdiag = lambda m: jnp.concatenate(
        [jnp.concatenate([m[0], z], axis=1), jnp.concatenate([z, m[1]], axis=1)], axis=0)
    head = jnp.arange(w) // A_HEAD
    return {
        "w_in_a": even_w_in[j][:, :A_COLS].astype(BF16),
        "w_in_b": even_w_in[j][:, A_COLS:].astype(BF16),
        "w_out_a": even_w_out[j][:w].astype(BF16),
        "w_out_b": even_w_out[j][w:].astype(BF16),
        "mu": rwkv_mu[j].reshape(1, A_COLS),
        "w0": rwkv_w0[j].reshape(1, 2 * w),
        "wup": blockdiag(rwkv_w_up[j]).astype(BF16),
        "a0": rwkv_a0[j].reshape(1, 2 * w),
        "aup": blockdiag(rwkv_a_up[j]).astype(BF16),
        "gup": rwkv_g_up[j].astype(BF16),
        "k_k": rwkv_k_k[j].reshape(1, w),
        "k_a": rwkv_k_a[j].reshape(1, w),
        "r_k": rwkv_r_k[j].reshape(1, w),
        "lnx_g": rwkv_lnx_g[j].reshape(1, w),
        "lnx_b": rwkv_lnx_b[j].reshape(1, w),
        "ones_bd": (head[:, None] == head[None, :]).astype(BF16),
    }


def _pack_odd(j, odd_w_in, odd_w_out, mla_q_norm_g, mla_kv_norm_g, mla_w_uq, mla_w_ukv):
    d = D_MODEL
    w_in = odd_w_in[j]
    o_cq, o_ckv = 0, C_Q_RANK
    o_kr = o_ckv + C_KV_RANK
    o_dq = o_kr + C_ROPE
    n_d = D_HEADS * 2 * D_HEAD
    w_kr = w_in[:, o_kr:o_kr + C_ROPE]
    packed = jnp.concatenate([
        w_in[:, o_dq:o_dq + 3 * n_d], w_in[:, o_cq:o_kr], w_kr, _rot_half_cols(w_kr),
        jnp.zeros((d, ODD_COLS - (C_Q_RANK + C_KV_RANK + 3 * n_d + 2 * C_ROPE)), F32)], axis=1)
    pad = LANES - C_NOPE - C_ROPE
    wq = mla_w_uq[j].reshape(C_Q_RANK, C_HEADS, C_NOPE + C_ROPE)
    zq = jnp.zeros((C_Q_RANK, C_HEADS, pad), F32)
    wqa = jnp.concatenate([wq, zq], axis=-1).reshape(C_Q_RANK, C_HEADS * LANES)
    wqb = jnp.concatenate([jnp.zeros((C_Q_RANK, C_HEADS, C_NOPE), F32), _rot_half_cols(wq[..., C_NOPE:]), zq],
                          axis=-1).reshape(C_Q_RANK, C_HEADS * LANES)
    wkv = mla_w_ukv[j].reshape(C_KV_RANK, C_HEADS, C_NOPE + C_V)
    wk = jnp.concatenate([wkv[..., :C_NOPE], jnp.zeros((C_KV_RANK, C_HEADS, LANES - C_NOPE), F32)],
                         axis=-1).reshape(C_KV_RANK, C_HEADS * LANES)
    wv = wkv[..., C_NOPE:].reshape(C_KV_RANK, C_HEADS * C_V)
    src = jnp.arange(LANES)[:, None]
    dst = jnp.arange(C_HEADS * LANES)[None, :] % LANES
    place = ((dst >= C_NOPE) & (dst < C_NOPE + C_ROPE) & (dst - C_NOPE == src)).astype(BF16)
    hv = C_HEADS * C_V
    return {
        "w_in": packed.astype(BF16),
        "w_out_a": odd_w_out[j][:hv].astype(BF16),
        "w_out_b": odd_w_out[j][hv:].astype(BF16),
        "q_norm_g": mla_q_norm_g[j].reshape(1, C_Q_RANK),
        "kv_norm_g": mla_kv_norm_g[j].reshape(1, C_KV_RANK),
        "wqa": wqa.astype(BF16), "wqb": wqb.astype(BF16), "wk": wk.astype(BF16), "wv": wv.astype(BF16),
        "place": place,
    }


def _even_mixers(x, g1n, sc1, sh1, ew, s5m, s5_d, s5_w_glu, s5_b_glu):
    b, s, _ = x.shape
    pa = _normmod_mm(x, g1n, sc1, sh1, ew["w_in_a"], tn=A_COLS // 3)
    u = _normmod_mm(x, g1n, sc1, sh1, ew["w_in_b"], tn=B_WIDTH)
    r, v, kk, lw, kd, bd, bv, g = _rwkv_prep(pa, ew)
    yf, yr = _rwkv_scan(r, v, kk, lw, kd, bd)
    ya = _rwkv_post(yf, yr, bv, g, ew["lnx_g"], ew["lnx_b"], ew["ones_bd"])
    nrows = s // S5_CHUNK
    u_g = jnp.transpose(u.astype(BF16).reshape(b, nrows, S5_CHUNK, B_GROUPS, B_GROUP), (0, 3, 1, 2, 4))
    u_g = u_g.reshape(b, B_GROUPS, nrows, S5_CHUNK * B_GROUP)
    ys_g = (_s5_scan(u_g, s5m[0], False).astype(F32) + _s5_scan(u_g, s5m[1], True).astype(F32)).astype(BF16)
    ys = jnp.transpose(ys_g.reshape(b, B_GROUPS, nrows, S5_CHUNK, B_GROUP), (0, 2, 3, 1, 4)).reshape(b, s, B_WIDTH)
    yb = _s5_glu(ys, u, s5_d, s5_w_glu, s5_b_glu)
    return ya, yb


def _odd_mixers(x, g1n, sc1, sh1, ow, tabs, diff_w, lambda_init):
    s = x.shape[1]
    p = _normmod_mm(x, g1n, sc1, sh1, ow["w_in"], tn=1024)
    mq, mk, mv, q0, q1, dk, dv = _odd_prep(p, tabs["cos"][:s], tabs["sin"][:s], ow)
    yc = _mla_flash(mq, mk, mv)
    yd = _diff_flash(q0, q1, dk, dv, tabs["bias"], diff_w["lq1"], diff_w["lk1"], diff_w["lq2"], diff_w["lk2"],
                     diff_w["subln_g"], lambda_init)
    return yc, yd


def kernel(x_prompt, x_sample, c_prompt, c_sample, ada_w, ada_b, norm1_g, norm2_g, even_w_in, even_w_out, rwkv_mu, rwkv_w0, rwkv_w_up, rwkv_a0, rwkv_a_up, rwkv_g_up, rwkv_k_k, rwkv_k_a, rwkv_r_k, rwkv_lnx_g, rwkv_lnx_b, s5_lam_re, s5_lam_im, s5_log_step, s5_b_re, s5_b_im, s5_c_re, s5_c_im, s5_d, s5_w_glu, s5_b_glu, odd_w_in, odd_w_out, mla_q_norm_g, mla_kv_norm_g, mla_w_uq, mla_w_ukv, diff_lq1, diff_lk1, diff_lq2, diff_lk2, diff_subln_g, rel_bias, ffn_w_up, ffn_conv_w, ffn_conv_b, ffn_w_down, final_g):
    d = D_MODEL
    groups = [(x_prompt, c_prompt), (x_sample, c_sample)]
    nb = [g[0].shape[0] for g in groups]
    c_all = jnp.concatenate([g[1] for g in groups] + [jnp.zeros((SUBLANES - sum(nb), d), F32)], axis=0)
    mod = _ada_mod(c_all, ada_w, ada_b)

    max_s = max(g[0].shape[1] for g in groups)
    cos_t, sin_t = _rope_tables(max_s)
    rel = jnp.arange(-BIAS_HALF, BIAS_HALF, dtype=jnp.int32)
    bias_tab = (jnp.transpose(rel_bias.astype(F32)[_t5_bucket(rel)]) * LOG2E).reshape(D_HEADS, 1, 2 * BIAS_HALF)
    tabs = {"cos": cos_t, "sin": sin_t, "bias": bias_tab}

    xs = [g[0] for g in groups]
    for i in range(DEPTH):
        j = i // 2
        if i % 2 == 0:
            ew = _pack_even(j, even_w_in, even_w_out, rwkv_mu, rwkv_w0, rwkv_w_up, rwkv_a0, rwkv_a_up, rwkv_g_up,
                            rwkv_k_k, rwkv_k_a, rwkv_r_k, rwkv_lnx_g, rwkv_lnx_b)
            s5m = [_s5_matrices(s5_lam_re[j, dr], s5_lam_im[j, dr], s5_log_step[j, dr], s5_b_re[j, dr],
                                s5_b_im[j, dr], s5_c_re[j, dr], s5_c_im[j, dr], dr == 1) for dr in range(2)]
        else:
            ow = _pack_odd(j, odd_w_in, odd_w_out, mla_q_norm_g, mla_kv_norm_g, mla_w_uq, mla_w_ukv)
            diff_w = {"lq1": diff_lq1[j].reshape(1, D_HEAD), "lk1": diff_lk1[j].reshape(1, D_HEAD),
                      "lq2": diff_lq2[j].reshape(1, D_HEAD), "lk2": diff_lk2[j].reshape(1, D_HEAD),
                      "subln_g": diff_subln_g[j].reshape(1, D_V)}
        w_up = ffn_w_up[i].astype(BF16)
        w_down = ffn_w_down[i].astype(BF16)
        row0 = 0
        for gi in range(len(groups)):
            x = xs[gi]
            m = mod[i, row0:row0 + nb[gi]]
            row0 += nb[gi]
            sh1, sc1, g1, sh2, sc2, g2 = [m[:, None, k * d:(k + 1) * d] for k in range(N_MOD)]
            if i % 2 == 0:
                ya, yb = _even_mixers(x, norm1_g[i], sc1, sh1, ew, s5m, s5_d[j].reshape(1, B_WIDTH),
                                      s5_w_glu[j].astype(BF16), s5_b_glu[j].reshape(1, B_WIDTH))
                x = _out_proj(ya, yb, ew["w_out_a"], ew["w_out_b"], x, g1)
            else:
                yc, yd = _odd_mixers(x, norm1_g[i], sc1, sh1, ow, tabs, diff_w, 0.8 - 0.6 * math.exp(-0.3 * i))
                x = _out_proj(yc, yd, ow["w_out_a"], ow["w_out_b"], x, g1)
            x = _ffn(x, norm2_g[i], sc2, sh2, w_up, ffn_conv_w[i], ffn_conv_b[i].reshape(1, 2 * FFN_HIDDEN), w_down,
                     g2, final_g.reshape(1, d), final=(i == DEPTH - 1))
            xs[gi] = x
    return (xs[0], xs[1])
```

```python
import functools
import math

import jax
import jax.numpy as jnp
from jax import lax
from jax.experimental import pallas as pl
from jax.experimental.pallas import tpu as pltpu

F32 = jnp.float32
BF16 = jnp.bfloat16

D_MODEL = 2048
DEPTH = 2
EPS = 1e-6
A_WIDTH = 1024
A_HEAD = 64
A_HEADS = 16
A_LORA = 64
A_GATE_LORA = 128
A_COLS = 3 * A_WIDTH + 2 * A_LORA + 2 * A_LORA + A_GATE_LORA
RWKV_GN_EPS = 64e-5
DECAY_SCALE = math.exp(-0.5)
B_WIDTH = 1024
B_GROUP = 16
B_GROUPS = 64
B_STATE = 64
C_HEADS = 8
C_NOPE = 64
C_ROPE = 32
C_V = 128
C_Q_RANK = 512
C_KV_RANK = 256
ROPE_THETA = 10000.0
D_HEADS = 8
D_HEAD = 64
D_V = 128
SUBLN_EPS = 1e-5
N_BUCKETS = 32
MAX_DISTANCE = 128
FFN_HIDDEN = 5632
N_MOD = 6

LANES = 128
SUBLANES = 8
VMEM_LIMIT_BYTES = 56 * 1024 * 1024
S5_VMEM_LIMIT_BYTES = 63 * 1024 * 1024

MM_TOKENS = 1024
FFN_TOKENS = 1024
FFN_HIDDEN_TILE = 512
ATT_TQ = 1024
ATT_TK = 1024

RWKV_CHUNK = 64
S5_CHUNK = 16
S5_ROWS = 128
LOG2E = 1.4426950408889634

NN = (((1,), (0,)), ((), ()))
NT = (((1,), (1,)), ((), ()))
BATCH_NN = (((2,), (1,)), ((0,), (0,)))


def _params(*sem, vmem_limit_bytes=VMEM_LIMIT_BYTES):
    return pltpu.CompilerParams(dimension_semantics=sem, vmem_limit_bytes=vmem_limit_bytes)


def _dot(a, b, dims=NN):
    return lax.dot_general(a, b, dims, preferred_element_type=F32)


def _split2(x):
    hi = x.astype(BF16)
    lo = (x - hi.astype(F32)).astype(BF16)
    return hi, lo


def _dot_exact_rhs(a, b):
    ah, al = _split2(a)
    return _dot(ah, b) + _dot(al, b)


def _ada_kernel(c_ref, w_ref, b_ref, o_ref):
    c = c_ref[...]
    cs = c * jax.nn.sigmoid(c)
    o_ref[0] = _dot(cs.astype(BF16), w_ref[0].astype(BF16)) + b_ref[0]


def _ada_mod(c_all, ada_w, ada_b):
    n = N_MOD * D_MODEL
    tn = 1024
    return pl.pallas_call(
        _ada_kernel,
        grid=(DEPTH, n // tn),
        in_specs=[
            pl.BlockSpec((SUBLANES, D_MODEL), lambda l, j: (0, 0)),
            pl.BlockSpec((1, D_MODEL, tn), lambda l, j: (l, 0, j)),
            pl.BlockSpec((1, 1, tn), lambda l, j: (l, 0, j)),
        ],
        out_specs=pl.BlockSpec((1, SUBLANES, tn), lambda l, j: (l, 0, j)),
        out_shape=jax.ShapeDtypeStruct((DEPTH, SUBLANES, n), F32),
        compiler_params=_params("parallel", "parallel"),
        name="ada_mod",
    )(c_all, ada_w, ada_b.reshape(DEPTH, 1, n))


def _modnorm(x, g, sc, sh):
    y = x * lax.rsqrt(jnp.mean(x * x, axis=-1, keepdims=True) + EPS)
    return (y * g) * (1.0 + sc) + sh


def _normmod_mm_kernel(x_ref, g_ref, sc_ref, sh_ref, w_ref, o_ref, h_scr):
    @pl.when(pl.program_id(2) == 0)
    def _():
        h_scr[...] = _modnorm(x_ref[0], g_ref[...], sc_ref[0], sh_ref[0]).astype(BF16)

    o_ref[0] = _dot(h_scr[...], w_ref[...]).astype(o_ref.dtype)


def _normmod_mm(x, g, sc, sh, w, tn, out_dtype=F32):
    b, s, d = x.shape
    n = w.shape[1]
    tm = min(MM_TOKENS, s)
    return pl.pallas_call(
        _normmod_mm_kernel,
        grid=(b, s // tm, n // tn),
        in_specs=[
            pl.BlockSpec((1, tm, d), lambda bi, i, j: (bi, i, 0)),
            pl.BlockSpec((1, d), lambda bi, i, j: (0, 0)),
            pl.BlockSpec((1, 1, d), lambda bi, i, j: (bi, 0, 0)),
            pl.BlockSpec((1, 1, d), lambda bi, i, j: (bi, 0, 0)),
            pl.BlockSpec((d, tn), lambda bi, i, j: (0, j)),
        ],
        out_specs=pl.BlockSpec((1, tm, tn), lambda bi, i, j: (bi, i, j)),
        out_shape=jax.ShapeDtypeStruct((b, s, n), out_dtype),
        scratch_shapes=[pltpu.VMEM((tm, d), BF16)],
        compiler_params=_params("parallel", "parallel", "arbitrary"),
        name="normmod_mm",
    )(x, g.reshape(1, d), sc, sh, w)


def _rwkv_prep_kernel(p_ref, pp_ref, pn_ref, mu_ref, w0_ref, wup_ref, a0_ref, aup_ref, gup_ref, kk_ref, ka_ref,
                      rk_ref, ones_ref, r_out, v_out, kkn_out, lw_out, kd_out, bd_out, bv_out, g_out):
    i = pl.program_id(1)
    last = pl.num_programs(1) - 1
    pa = p_ref[0]
    tm = pa.shape[0]
    row = lax.broadcasted_iota(jnp.int32, (tm, 1), 0)
    prev_row = jnp.where(i == 0, 0.0, pp_ref[0][SUBLANES - 1:SUBLANES, :])
    next_row = jnp.where(i == last, 0.0, pn_ref[0][0:1, :])
    p_prev = jnp.where(row == 0, prev_row, pltpu.roll(pa, 1, 0))
    p_next = jnp.where(row == tm - 1, next_row, pltpu.roll(pa, tm - 1, 0))
    pa = pa + mu_ref[...] * (0.5 * (p_prev + p_next) - pa)

    w = A_WIDTH
    r = pa[:, 0:w]
    k = pa[:, w:2 * w]
    v = pa[:, 2 * w:3 * w]
    dw = pa[:, 3 * w:3 * w + 2 * A_LORA]
    da = pa[:, 3 * w + 2 * A_LORA:3 * w + 4 * A_LORA]
    dg = pa[:, 3 * w + 4 * A_LORA:A_COLS]

    lw = -DECAY_SCALE * jax.nn.sigmoid(w0_ref[...] + _dot(jnp.tanh(dw).astype(BF16), wup_ref[...]))
    icl = jax.nn.sigmoid(a0_ref[...] + _dot(da.astype(BF16), aup_ref[...]))
    g = _dot(jax.nn.sigmoid(dg).astype(BF16), gup_ref[...])

    ones_bd = ones_ref[...]
    kkr = k * kk_ref[...]
    ss = _dot_exact_rhs(kkr * kkr, ones_bd)
    kkn = kkr / jnp.maximum(jnp.sqrt(ss), 1e-12)

    r_out[0] = r.astype(r_out.dtype)
    v_out[0] = v.astype(v_out.dtype)
    kkn_out[0] = kkn.astype(kkn_out.dtype)
    lw_out[0] = lw
    g_out[0] = g
    bonus = jnp.zeros_like(r)
    for d in range(2):
        icl_d = icl[:, d * w:(d + 1) * w]
        k_d = k * (1.0 + (icl_d - 1.0) * ka_ref[...])
        kd_out[0, :, d * w:(d + 1) * w] = k_d.astype(kd_out.dtype)
        bd_out[0, :, d * w:(d + 1) * w] = (icl_d * kkn).astype(bd_out.dtype)
        bonus = bonus + _dot_exact_rhs(r * k_d * rk_ref[...], ones_bd)
    bv_out[0] = bonus * v


def _rwkv_prep(pa, wts):
    b, s, _ = pa.shape
    tm = min(256, s)
    nh = tm // SUBLANES
    w = A_WIDTH
    full = lambda shape: pl.BlockSpec(shape, lambda bi, i: (0,) * len(shape))
    tok = lambda n: pl.BlockSpec((1, tm, n), lambda bi, i: (bi, i, 0))
    out_shapes = [jax.ShapeDtypeStruct((b, s, n), dt) for n, dt in (
        (w, BF16), (w, BF16), (w, BF16), (2 * w, F32), (2 * w, BF16), (2 * w, BF16), (w, F32), (w, F32))]
    return pl.pallas_call(
        _rwkv_prep_kernel,
        grid=(b, s // tm),
        in_specs=[
            tok(A_COLS),
            pl.BlockSpec((1, SUBLANES, A_COLS), lambda bi, i: (bi, jnp.maximum(i * nh - 1, 0), 0)),
            pl.BlockSpec((1, SUBLANES, A_COLS), lambda bi, i: (bi, jnp.minimum((i + 1) * nh, s // SUBLANES - 1), 0)),
            full((1, A_COLS)), full((1, 2 * w)), full((2 * A_LORA, 2 * w)), full((1, 2 * w)),
            full((2 * A_LORA, 2 * w)), full((A_GATE_LORA, w)), full((1, w)), full((1, w)), full((1, w)),
            full((w, w)),
        ],
        out_specs=[tok(sd.shape[-1]) for sd in out_shapes],
        out_shape=out_shapes,
        compiler_params=_params("parallel", "parallel"),
        name="rwkv_prep",
    )(pa, pa, pa, wts["mu"], wts["w0"], wts["wup"], wts["a0"], wts["aup"], wts["gup"], wts["k_k"], wts["k_a"],
      wts["r_k"], wts["ones_bd"])


def _rwkv_chunk(r, v, kk, lw, kd, bd, s2, reverse):
    t = RWKV_CHUNK
    assert t == A_HEAD
    sign = -1 if reverse else 1
    ri = lax.broadcasted_iota(jnp.int32, (t, t), 0)
    ci = lax.broadcasted_iota(jnp.int32, (t, t), 1)
    tri = jnp.where((ri - ci) * sign >= 0, 1.0, 0.0).astype(BF16)
    lw_hi = lw.astype(BF16)
    rem = lw - lw_hi.astype(F32)
    lw_mid = rem.astype(BF16)
    lw_lo = (rem - lw_mid.astype(F32)).astype(BF16)
    cum = _dot(tri, lw_hi) + _dot(tri, lw_mid) + _dot(tri, lw_lo)
    yield
    tot = jnp.sum(lw, axis=0, keepdims=True)
    gam = jnp.exp(cum)
    gam_ex = jnp.exp(cum - lw)
    ginv = jnp.exp(-cum)
    gend = jnp.exp(tot - cum)
    gtot = jnp.exp(tot)

    width = r.shape[1]
    heads = width // A_HEAD
    lane_head = lax.broadcasted_iota(jnp.int32, (1, width), 1) // A_HEAD

    def stack(x):
        return jnp.concatenate([jnp.where(lane_head == h, x, 0.0) for h in range(heads)], axis=0).astype(BF16)

    def mm(x_wide, y_stack):
        return _dot(x_wide.astype(BF16), y_stack)

    ti = lax.broadcasted_iota(jnp.int32, (t, width), 0)
    si = lax.broadcasted_iota(jnp.int32, (t, width), 1) % t
    order = (ti - si) * sign
    incl = order >= 0
    strict = order > 0

    q_rk = jnp.concatenate([r * gam, kk * gam_ex], axis=0).astype(BF16)
    a = _dot(q_rk, jnp.concatenate([stack(kd * ginv), stack(bd * ginv)], axis=0), NT)
    yield
    n = heads * t
    a_rk = jnp.where(incl, a[:t, :n], 0.0)
    a_rb = jnp.where(incl, a[:t, n:], 0.0)
    a_kk = jnp.where(strict, a[t:, :n], 0.0)
    a_bk = jnp.where(strict, a[t:, n:], 0.0)

    qs = _dot(q_rk, s2.astype(BF16), NT)
    yield
    akv = mm(jnp.concatenate([a_rk, a_kk], axis=0), stack(v))
    yield

    s_n = stack(a_bk)
    n2 = mm(a_bk, s_n)
    yield
    n3 = mm(n2, s_n)
    yield
    n4 = mm(n2, stack(n2))
    yield
    s_n4 = stack(n4)
    n8 = mm(n4, s_n4)
    yield
    n12 = mm(n8, s_n4)
    yield
    f1 = jnp.where(ti == si, 1.0, 0.0) - a_bk + n2 - n3
    n16 = mm(n8, stack(n8))
    yield
    f12 = f1 + mm(f1, stack(n4 + n8 + n12))
    yield
    s_n16 = stack(n16)
    n32 = mm(n16, s_n16)
    yield
    n48 = mm(n32, s_n16)
    yield
    inv = f12 + mm(f12, stack(n16 + n32 + n48))
    yield

    ps = mm(inv, stack(-qs[t:] - akv[t:]))
    yield
    y = qs[:t] + akv[:t] + mm(a_rb, stack(ps))
    yield
    upd = _dot(jnp.concatenate([v, ps], axis=0).T.astype(BF16),
               jnp.concatenate([kd * gend, bd * gend], axis=0).astype(BF16))
    row_head = lax.broadcasted_iota(jnp.int32, (width, 1), 0) // A_HEAD
    s2_new = s2 * gtot + jnp.where(row_head == lane_head, upd, 0.0)
    return y, s2_new


def _run_lockstep(gens):
    results = [None] * len(gens)
    active = list(range(len(gens)))
    while active:
        for i in list(active):
            try:
                next(gens[i])
            except StopIteration as e:
                results[i] = e.value
                active.remove(i)
    return results


RWKV_CHAIN_LANES = 256
RWKV_CHAINS_PER_STEP = 4


def _rwkv_scan_kernel(rf_ref, vf_ref, kkf_ref, lwf_ref, kdf_ref, bdf_ref,
                      rb_ref, vb_ref, kkb_ref, lwb_ref, kdb_ref, bdb_ref, yf_ref, yb_ref, s_scr):
    @pl.when(pl.program_id(2) == 0)
    def _():
        s_scr[...] = jnp.zeros_like(s_scr)

    dirs = ((rf_ref, vf_ref, kkf_ref, lwf_ref, kdf_ref, bdf_ref, yf_ref),
            (rb_ref, vb_ref, kkb_ref, lwb_ref, kdb_ref, bdb_ref, yb_ref))
    chains = []
    for d, (r_ref, v_ref, kk_ref, lw_ref, kd_ref, bd_ref, y_ref) in enumerate(dirs):
        for p in range(RWKV_CHAINS_PER_STEP):
            sl = slice(p * RWKV_CHAIN_LANES, (p + 1) * RWKV_CHAIN_LANES)
            chains.append((d, p, sl, y_ref, _rwkv_chunk(
                r_ref[0, :, sl], v_ref[0, :, sl], kk_ref[0, :, sl], lw_ref[0, :, sl], kd_ref[0, :, sl],
                bd_ref[0, :, sl], s_scr[d, p], reverse=(d == 1))))
    results = _run_lockstep([c[4] for c in chains])
    for (d, p, sl, y_ref, _), (y, s_new) in zip(chains, results):
        y_ref[0, :, sl] = y
        s_scr[d, p] = s_new


def _rwkv_scan(r, v, kk, lw, kd, bd):
    b, s, w = r.shape
    t = RWKV_CHUNK
    nc = s // t
    pw = RWKV_CHAINS_PER_STEP * RWKV_CHAIN_LANES
    ngrp = w // pw
    fwd = lambda off: pl.BlockSpec((1, t, pw), lambda bi, g, c: (bi, c, g + off))
    bwd = lambda off: pl.BlockSpec((1, t, pw), lambda bi, g, c: (bi, nc - 1 - c, g + off))
    return pl.pallas_call(
        _rwkv_scan_kernel,
        grid=(b, ngrp, nc),
        in_specs=[fwd(0)] * 6 + [bwd(0)] * 3 + [bwd(ngrp)] * 3,
        out_specs=[fwd(0), bwd(0)],
        out_shape=[jax.ShapeDtypeStruct((b, s, w), F32)] * 2,
        scratch_shapes=[pltpu.VMEM((2, RWKV_CHAINS_PER_STEP, RWKV_CHAIN_LANES, RWKV_CHAIN_LANES), F32)],
        compiler_params=_params("parallel", "parallel", "arbitrary"),
        name="rwkv_scan",
    )(r, v, kk, lw, kd, bd, r, v, kk, lw, kd, bd)


def _rwkv_post_kernel(yf_ref, yb_ref, bv_ref, g_ref, lng_ref, lnb_ref, ones_ref, o_ref):
    y = yf_ref[0] + yb_ref[0]
    ones_bd = ones_ref[...]
    mean = _dot_exact_rhs(y, ones_bd) * (1.0 / A_HEAD)
    yc = y - mean
    var = _dot_exact_rhs(yc * yc, ones_bd) * (1.0 / A_HEAD)
    yn = yc * lax.rsqrt(var + RWKV_GN_EPS) * lng_ref[...] + lnb_ref[...]
    o_ref[0] = ((yn + bv_ref[0]) * g_ref[0]).astype(o_ref.dtype)


def _rwkv_post(yf, yb, bv, g, lng, lnb, ones_bd):
    b, s, w = bv.shape
    tm = min(512, s)
    full = lambda shape: pl.BlockSpec(shape, lambda bi, i: (0,) * len(shape))
    tok = lambda n: pl.BlockSpec((1, tm, n), lambda bi, i: (bi, i, 0))
    return pl.pallas_call(
        _rwkv_post_kernel,
        grid=(b, s // tm),
        in_specs=[tok(w), tok(w), tok(w), tok(w), full((1, w)), full((1, w)), full((w, w))],
        out_specs=tok(w),
        out_shape=jax.ShapeDtypeStruct((b, s, w), BF16),
        compiler_params=_params("parallel", "parallel"),
        name="rwkv_post",
    )(yf, yb, bv, g, lng, lnb, ones_bd)


def _s5_kernel(u_ref, *rest, rows, reverse, has_prev):
    prev_ref = rest[0] if has_prev else None
    m_ref, w_ref, ws_ref, v_ref, a_ref, y_ref, x_scr, xs_scr, h_scr, carry_scr = rest[1:] if has_prev else rest
    g = B_GROUPS

    @pl.when(pl.program_id(1) == 0)
    def _():
        carry_scr[...] = jnp.zeros_like(carry_scr)

    u = u_ref[0]
    x_scr[...] = lax.dot_general(u, w_ref[...], BATCH_NN, preferred_element_type=F32).reshape(g * rows, LANES)
    xs_scr[...] = lax.dot_general(u, ws_ref[...], BATCH_NN, preferred_element_type=F32).reshape(g * rows, LANES)
    a1 = a_ref[0]
    a2 = a_ref[1]
    a2s = a_ref[2]

    def step(i, carry):
        h, hs = carry
        r = (rows - 1 - i) if reverse else i
        idx = pl.ds(r, g, stride=rows)
        h_scr[idx, :] = h
        hn = a1 * h + a2 * hs + x_scr[idx, :]
        hsn = a1 * hs + a2s * h + xs_scr[idx, :]
        return hn, hsn

    h, hs = lax.fori_loop(0, rows, step, (carry_scr[0], carry_scr[1]))
    carry_scr[0] = h
    carry_scr[1] = hs
    hprev = h_scr[...].reshape(g, rows, LANES).astype(BF16)
    y = (lax.dot_general(u, m_ref[...], BATCH_NN, preferred_element_type=F32)
         + lax.dot_general(hprev, v_ref[...], BATCH_NN, preferred_element_type=F32))
    if has_prev:
        y = y + prev_ref[0].astype(F32)
    y_ref[0] = y.astype(y_ref.dtype)


def _s5_scan(u_g, mats, reverse, prev=None):
    b, g, nrows, cw = u_g.shape
    rows = min(S5_ROWS, nrows)
    nsb = nrows // rows
    whole = pl.BlockSpec(memory_space=pltpu.VMEM)
    idx = (lambda bi, i: (bi, 0, nsb - 1 - i, 0)) if reverse else (lambda bi, i: (bi, 0, i, 0))
    return pl.pallas_call(
        functools.partial(_s5_kernel, rows=rows, reverse=reverse, has_prev=prev is not None),
        grid=(b, nsb),
        in_specs=([pl.BlockSpec((1, g, rows, cw), idx)]
                  + [pl.BlockSpec((1, g, rows, cw), idx, pipeline_mode=pl.Buffered(1))] * (prev is not None)
                  + [whole] * 5),
        out_specs=pl.BlockSpec((1, g, rows, cw), idx),
        out_shape=jax.ShapeDtypeStruct((b, g, nrows, cw), BF16),
        scratch_shapes=[pltpu.VMEM((g * rows, LANES), F32), pltpu.VMEM((g * rows, LANES), F32),
                        pltpu.VMEM((g * rows, LANES), F32), pltpu.VMEM((2, g, LANES), F32)],
        compiler_params=_params("parallel", "arbitrary", vmem_limit_bytes=S5_VMEM_LIMIT_BYTES),
        name="s5_scan_bwd" if reverse else "s5_scan_fwd",
    )(u_g, *([prev] if prev is not None else []), mats["m"], mats["w"], mats["ws"], mats["v"], mats["a"])


def _s5_matrices(lam_re, lam_im, log_step, b_re, b_im, c_re, c_im, reverse):
    hp = lax.Precision.HIGHEST
    t = S5_CHUNK
    g, p, c = B_GROUPS, B_STATE, B_GROUP
    lr, li = lam_re.astype(F32), lam_im.astype(F32)
    step = jnp.exp(log_step.astype(F32))[:, None]
    ar, ai = jnp.exp(lr * step) * jnp.cos(li * step), jnp.exp(lr * step) * jnp.sin(li * step)
    den = lr * lr + li * li
    nr, ni = ar - 1.0, ai
    fr, fi = (nr * lr + ni * li) / den, (ni * lr - nr * li) / den
    br, bi = b_re.astype(F32), b_im.astype(F32)
    bbr = fr[..., None] * br - fi[..., None] * bi
    bbi = fr[..., None] * bi + fi[..., None] * br
    cr, cim = c_re.astype(F32), c_im.astype(F32)
    taus = jnp.arange(t + 1, dtype=F32)[:, None, None]
    mag = jnp.exp(lr * step * taus)
    pr, pi = mag * jnp.cos(li * step * taus), mag * jnp.sin(li * step * taus)
    cpr = cr[None] * pr[:, :, None, :] - cim[None] * pi[:, :, None, :]
    cpi = cr[None] * pi[:, :, None, :] + cim[None] * pr[:, :, None, :]
    kern = (jnp.einsum("tgcp,gpd->tgcd", cpr, bbr, precision=hp)
            - jnp.einsum("tgcp,gpd->tgcd", cpi, bbi, precision=hp))
    s_idx = jnp.arange(t)[:, None]
    t_idx = jnp.arange(t)[None, :]
    lag = (s_idx - t_idx) if reverse else (t_idx - s_idx)
    kk = jnp.where((lag >= 0)[:, :, None, None, None], kern[jnp.clip(lag, 0, t)], 0.0)
    m = jnp.transpose(kk, (2, 0, 4, 1, 3)).reshape(g, t * c, t * c)
    e = jnp.arange(t) if reverse else (t - 1 - jnp.arange(t))
    pre, pie = pr[e], pi[e]
    wre = pre[..., None] * bbr[None] - pie[..., None] * bbi[None]
    wim = pre[..., None] * bbi[None] + pie[..., None] * bbr[None]
    wre = jnp.transpose(wre, (1, 0, 3, 2)).reshape(g, t * c, p)
    wim = jnp.transpose(wim, (1, 0, 3, 2)).reshape(g, t * c, p)
    w = jnp.concatenate([wre, wim], axis=-1)
    ws = jnp.concatenate([wim, wre], axis=-1)
    f = (t - jnp.arange(t)) if reverse else (jnp.arange(t) + 1)
    vre = jnp.transpose(cpr[f], (1, 3, 0, 2)).reshape(g, p, t * c)
    vim = jnp.transpose(-cpi[f], (1, 3, 0, 2)).reshape(g, p, t * c)
    v = jnp.concatenate([vre, vim], axis=1)
    atr, ati = pr[t], pi[t]
    a = jnp.stack([jnp.concatenate([atr, atr], -1), jnp.concatenate([-ati, ati], -1),
                   jnp.concatenate([ati, -ati], -1)])
    return {"m": m.astype(BF16), "w": w.astype(BF16), "ws": ws.astype(BF16), "v": v.astype(BF16), "a": a}


def _s5_glu_kernel(ys_ref, u_ref, d_ref, w_ref, b_ref, o_ref):
    y = ys_ref[0].astype(F32) + u_ref[0] * d_ref[...]
    z = jax.nn.gelu(y)
    gate = jax.nn.sigmoid(_dot(z.astype(BF16), w_ref[...]) + b_ref[...])
    o_ref[0] = (z * gate).astype(o_ref.dtype)


def _s5_glu(ys, u, d_skip, w_glu, b_glu):
    b, s, w = u.shape
    tm = min(512, s)
    full = lambda shape: pl.BlockSpec(shape, lambda bi, i: (0,) * len(shape))
    tok = pl.BlockSpec((1, tm, w), lambda bi, i: (bi, i, 0))
    return pl.pallas_call(
        _s5_glu_kernel,
        grid=(b, s // tm),
        in_specs=[tok, tok, full((1, w)), full((w, w)), full((1, w))],
        out_specs=tok,
        out_shape=jax.ShapeDtypeStruct((b, s, w), BF16),
        compiler_params=_params("parallel", "parallel"),
        name="s5_glu",
    )(ys, u, d_skip, w_glu, b_glu)


def _out_proj_kernel(a_ref, b_ref, wa_ref, wb_ref, x_ref, g_ref, o_ref):
    mix = _dot(a_ref[0], wa_ref[...]) + _dot(b_ref[0], wb_ref[...])
    o_ref[0] = x_ref[0] + g_ref[0] * mix


def _out_proj(a, bb, wa, wb, x, gate):
    b, s, d = x.shape
    k = a.shape[-1]
    tm = min(MM_TOKENS, s)
    tn = 1024
    return pl.pallas_call(
        _out_proj_kernel,
        grid=(b, s // tm, d // tn),
        in_specs=[
            pl.BlockSpec((1, tm, k), lambda bi, i, j: (bi, i, 0)),
            pl.BlockSpec((1, tm, k), lambda bi, i, j: (bi, i, 0)),
            pl.BlockSpec((k, tn), lambda bi, i, j: (0, j)),
            pl.BlockSpec((k, tn), lambda bi, i, j: (0, j)),
            pl.BlockSpec((1, tm, tn), lambda bi, i, j: (bi, i, j)),
            pl.BlockSpec((1, 1, tn), lambda bi, i, j: (bi, 0, j)),
        ],
        out_specs=pl.BlockSpec((1, tm, tn), lambda bi, i, j: (bi, i, j)),
        out_shape=jax.ShapeDtypeStruct((b, s, d), F32),
        compiler_params=_params("parallel", "parallel", "parallel"),
        name="out_proj",
    )(a, bb, wa, wb, x, gate)


FFN_HALO = 2 * SUBLANES


def _ffn_kernel(x_ref, xp_ref, xn_ref, ng_ref, sc_ref, sh_ref, wv_ref, wg_ref, cwv_ref, cwg_ref, cbv_ref, cbg_ref,
                wd_ref, g_ref, fg_ref, o_ref, h_scr, *, final):
    i = pl.program_id(1)
    j = pl.program_id(2)
    tm = x_ref.shape[1]
    hl = FFN_HALO

    @pl.when(j == 0)
    def _():
        g, sc, sh = ng_ref[...], sc_ref[0], sh_ref[0]
        h_scr[hl:hl + tm] = _modnorm(x_ref[0], g, sc, sh).astype(BF16)
        before = jnp.where(i == 0, 0.0, _modnorm(xp_ref[0], g, sc, sh))
        after = jnp.where(i == pl.num_programs(1) - 1, 0.0, _modnorm(xn_ref[0], g, sc, sh))
        h_scr[0:hl] = before.astype(BF16)
        h_scr[hl + tm:2 * hl + tm] = after.astype(BF16)
        o_ref[...] = jnp.zeros_like(o_ref)

    h = h_scr[...]
    rows = tm + 2 * hl

    def conv(w_ref, cw_ref, cb_ref):
        u = _dot(h, w_ref[...])
        cw = cw_ref[...]
        u_prev = pltpu.roll(u, 1, 0)[hl:hl + tm]
        u_next = pltpu.roll(u, rows - 1, 0)[hl:hl + tm]
        return u_prev * cw[0:1] + u[hl:hl + tm] * cw[1:2] + u_next * cw[2:3] + cb_ref[...]

    val = conv(wv_ref, cwv_ref, cbv_ref)
    gate = conv(wg_ref, cwg_ref, cbg_ref)
    act = (gate * jax.nn.sigmoid(gate)) * val
    o_ref[0] += _dot(act.astype(BF16), wd_ref[...])

    @pl.when(j == pl.num_programs(2) - 1)
    def _():
        xn = x_ref[0] + g_ref[0] * o_ref[0]
        if final:
            xn = xn * lax.rsqrt(jnp.mean(xn * xn, axis=-1, keepdims=True) + EPS) * fg_ref[...]
        o_ref[0] = xn


def _ffn(x, norm_g, sc, sh, w_up, conv_w, conv_b, w_down, gate, final_g, final):
    b, s, d = x.shape
    f = FFN_HIDDEN
    tm = min(FFN_TOKENS, s)
    tf = FFN_HIDDEN_TILE
    nf = f // tf
    nh = tm // FFN_HALO
    nhalo = s // FFN_HALO
    const = lambda shape: pl.BlockSpec(shape, lambda bi, i, j: (0,) * len(shape))
    per_b = pl.BlockSpec((1, 1, d), lambda bi, i, j: (bi, 0, 0))
    up = lambda off: pl.BlockSpec((d, tf), lambda bi, i, j: (0, j + off))
    cw = lambda off: pl.BlockSpec((3, tf), lambda bi, i, j: (0, j + off))
    cb = lambda off: pl.BlockSpec((1, tf), lambda bi, i, j: (0, j + off))
    once = pl.Buffered(1)
    return pl.pallas_call(
        functools.partial(_ffn_kernel, final=final),
        grid=(b, s // tm, nf),
        in_specs=[
            pl.BlockSpec((1, tm, d), lambda bi, i, j: (bi, i, 0)),
            pl.BlockSpec((1, FFN_HALO, d), lambda bi, i, j: (bi, jnp.maximum(i * nh - 1, 0), 0)),
            pl.BlockSpec((1, FFN_HALO, d), lambda bi, i, j: (bi, jnp.minimum((i + 1) * nh, nhalo - 1), 0)),
            const((1, d)), per_b, per_b,
            up(0), up(nf), cw(0), cw(nf), cb(0), cb(nf),
            pl.BlockSpec((tf, d), lambda bi, i, j: (j, 0)),
            per_b, const((1, d)),
        ],
        out_specs=pl.BlockSpec((1, tm, d), lambda bi, i, j: (bi, i, 0), pipeline_mode=once),
        out_shape=jax.ShapeDtypeStruct((b, s, d), F32),
        scratch_shapes=[pltpu.VMEM((tm + 2 * FFN_HALO, d), BF16)],
        compiler_params=_params("parallel", "parallel", "arbitrary"),
        name="ffn",
    )(x, x, x, norm_g.reshape(1, d), sc, sh, w_up, w_up, conv_w, conv_w, conv_b, conv_b, w_down, gate, final_g)


def _rms(x, g, eps):
    return x * lax.rsqrt(jnp.mean(x * x, axis=-1, keepdims=True) + eps) * g


def _odd_prep_kernel(cq_ref, ckv_ref, kr_ref, dq_ref, dk_ref, dv_ref, cos_ref, sin_ref, qg_ref, kvg_ref,
                     wqa_ref, wqb_ref, wk_ref, wv_ref, place_ref,
                     mq_out, mk_out, mv_out, q0_out, q1_out, dk_out, dv_out):
    cosq = cos_ref[...]
    sinq = sin_ref[...]
    qn = _rms(cq_ref[0], qg_ref[...], EPS).astype(BF16)
    qa = _dot(qn, wqa_ref[...])
    qb = _dot(qn, wqb_ref[...])
    mla_scale = (C_NOPE + C_ROPE) ** -0.5 * LOG2E
    for h in range(C_HEADS):
        sl = slice(h * LANES, (h + 1) * LANES)
        mq_out[0, :, sl] = ((qa[:, sl] * cosq + qb[:, sl] * sinq) * mla_scale).astype(BF16)
    kvn = _rms(ckv_ref[0], kvg_ref[...], EPS).astype(BF16)
    kr = kr_ref[0]
    cos_k = pltpu.roll(cosq, LANES - C_NOPE, 1)
    sin_k = pltpu.roll(sinq, LANES - C_NOPE, 1)
    partner = pltpu.roll(kr, LANES - C_ROPE, 1)
    lane = lax.broadcasted_iota(jnp.int32, (1, LANES), 1)
    kr_rope = jnp.where(lane < C_ROPE, kr * cos_k + partner * sin_k, 0.0)
    mk_out[0] = (_dot(kvn, wk_ref[...]) + _dot(kr_rope.astype(BF16), place_ref[...])).astype(BF16)
    mv_out[0] = _dot(kvn, wv_ref[...]).astype(BF16)
    dq = dq_ref[0] * (D_HEAD ** -0.5 * LOG2E)
    lane_w = lax.broadcasted_iota(jnp.int32, (1, dq.shape[1]), 1)
    first_map = (lane_w % LANES) < D_HEAD
    q0_out[0] = jnp.where(first_map, dq, 0.0).astype(BF16)
    q1_out[0] = jnp.where(first_map, 0.0, dq).astype(BF16)
    dk_out[0] = dk_ref[0].astype(BF16)
    dv_out[0] = dv_ref[0].astype(BF16)


ODD_DQ = 0
ODD_DK = ODD_DQ + D_HEADS * 2 * D_HEAD
ODD_DV = ODD_DK + D_HEADS * 2 * D_HEAD
ODD_CQ = ODD_DV + D_HEADS * D_V
ODD_CKV = ODD_CQ + C_Q_RANK
ODD_KR = ODD_CKV + C_KV_RANK
ODD_COLS = 4096


def _odd_prep(p, cos_t, sin_t, wts):
    b, s, _ = p.shape
    tm = min(512, s)
    hw = C_HEADS * LANES
    full = lambda shape: pl.BlockSpec(shape, lambda bi, i: (0,) * len(shape))
    col = lambda off, n: pl.BlockSpec((1, tm, n), lambda bi, i: (bi, i, off // n))
    tok = pl.BlockSpec((1, tm, hw), lambda bi, i: (bi, i, 0))
    tab = pl.BlockSpec((tm, LANES), lambda bi, i: (i, 0))
    outs = [jax.ShapeDtypeStruct((b, s, hw), BF16)] * 7
    return pl.pallas_call(
        _odd_prep_kernel,
        grid=(b, s // tm),
        in_specs=[
            col(ODD_CQ, C_Q_RANK), col(ODD_CKV, C_KV_RANK), col(ODD_KR, LANES),
            col(ODD_DQ, hw), col(ODD_DK, hw), col(ODD_DV, hw), tab, tab,
            full((1, C_Q_RANK)), full((1, C_KV_RANK)),
            full((C_Q_RANK, hw)), full((C_Q_RANK, hw)), full((C_KV_RANK, hw)), full((C_KV_RANK, hw)),
            full((LANES, hw)),
        ],
        out_specs=[tok] * 7,
        out_shape=outs,
        compiler_params=_params("parallel", "parallel"),
        name="odd_prep",
    )(p, p, p, p, p, p, cos_t, sin_t, wts["q_norm_g"], wts["kv_norm_g"], wts["wqa"], wts["wqb"], wts["wk"],
      wts["wv"], wts["place"])


def _with_ones(v):
    return jnp.concatenate([v, jnp.ones_like(v)], axis=1)


def _flash_chain(q, k, v1, bias, shift, m_scr, acc_scr):
    s = _dot(q, k, NT)
    yield
    if isinstance(bias, tuple):
        row0, strip = bias
        row1 = row0 + strip.shape[0]
        parts = [s[:row0]] * (row0 > 0) + [s[row0:row1] + strip] + [s[row1:]] * (row1 < s.shape[0])
        s = jnp.concatenate(parts, axis=0)
    elif bias is not None:
        s = s + bias
    m_prev = m_scr[...]
    m_new = jnp.maximum(m_prev, jnp.max(s, axis=-1, keepdims=True) + shift)
    alpha = jnp.exp2(m_prev - m_new)
    p = jnp.exp2(s - jnp.tile(m_new - shift, (1, s.shape[1] // LANES))).astype(BF16)
    yield
    acc_scr[...] = jnp.tile(alpha, (1, 2)) * acc_scr[...] + _dot(p, v1)
    m_scr[...] = m_new


MLA_HEADS_PER_STEP = 4


def _mla_flash_kernel(q_ref, k_ref, v_ref, o_ref, m_scr, acc_scr):
    kj = pl.program_id(3)

    @pl.when(kj == 0)
    def _():
        m_scr[...] = jnp.full_like(m_scr, -jnp.inf)
        acc_scr[...] = jnp.zeros_like(acc_scr)

    heads = [slice(h * LANES, (h + 1) * LANES) for h in range(MLA_HEADS_PER_STEP)]
    _run_lockstep([_flash_chain(q_ref[0, :, sl], k_ref[0, :, sl], _with_ones(v_ref[0, :, sl]), None, 0.0,
                                m_scr.at[h], acc_scr.at[h]) for h, sl in enumerate(heads)])

    @pl.when(kj == pl.num_programs(3) - 1)
    def _():
        for h, sl in enumerate(heads):
            acc = acc_scr[h]
            o_ref[0, :, sl] = (acc[:, :LANES] / acc[:, LANES:]).astype(o_ref.dtype)


def _mla_flash(q, k, v):
    b, s, hw = q.shape
    gw = MLA_HEADS_PER_STEP * LANES
    tq = min(ATT_TQ, s)
    tk = min(ATT_TK, s)
    qspec = pl.BlockSpec((1, tq, gw), lambda bi, h, i, j: (bi, i, h))
    kspec = pl.BlockSpec((1, tk, gw), lambda bi, h, i, j: (bi, j, h))
    return pl.pallas_call(
        _mla_flash_kernel,
        grid=(b, hw // gw, s // tq, s // tk),
        in_specs=[qspec, kspec, kspec],
        out_specs=qspec,
        out_shape=jax.ShapeDtypeStruct((b, s, hw), BF16),
        scratch_shapes=[pltpu.VMEM((MLA_HEADS_PER_STEP, tq, LANES), F32),
                        pltpu.VMEM((MLA_HEADS_PER_STEP, tq, 2 * LANES), F32)],
        compiler_params=_params("parallel", "parallel", "parallel", "arbitrary"),
        name="mla_flash",
    )(q, k, v)


BIAS_HALF = 256


DIFF_HEADS_PER_STEP = 2


def _diff_flash_kernel(q0_ref, q1_ref, k_ref, v_ref, tab_ref, lq1_ref, lk1_ref, lq2_ref, lk2_ref, sg_ref, o_ref,
                       m_scr, acc_scr, bias_scr, *, tq, tk, near, lambda_init):
    kj = pl.program_id(3)
    off = kj * tk - pl.program_id(2) * tq
    heads = [slice(h * LANES, (h + 1) * LANES) for h in range(DIFF_HEADS_PER_STEP)]

    @pl.when(kj == 0)
    def _():
        m_scr[...] = jnp.full_like(m_scr, -jnp.inf)
        acc_scr[...] = jnp.zeros_like(acc_scr)

    tabs = [tab_ref[h] for h in range(DIFF_HEADS_PER_STEP)]
    far_left = [tab[:, 0:1] for tab in tabs]
    far_right = [tab[:, 2 * BIAS_HALF - 1:2 * BIAS_HALF] for tab in tabs]

    def update(biases, shifts):
        chains = []
        for h, sl in enumerate(heads):
            k = k_ref[0, :, sl]
            v1 = _with_ones(v_ref[0, :, sl])
            for m, q_ref in enumerate((q0_ref, q1_ref)):
                chains.append(_flash_chain(q_ref[0, :, sl], k, v1, biases[h], shifts[h], m_scr.at[h, m],
                                           acc_scr.at[h, m]))
        _run_lockstep(chains)

    def toeplitz(tab, d):
        r = tab[:, d + LANES:d + 3 * LANES]
        rows = jnp.broadcast_to(r, (LANES, 2 * LANES))
        return pltpu.roll(rows, LANES, 1, stride=1, stride_axis=0)[:, :LANES]

    def block_kinds(d0):
        kinds = {}
        for ri in range(tq // LANES):
            for cj in range(tk // LANES):
                d = d0 + (cj - ri) * LANES
                kinds[ri, cj] = 'L' if d <= -BIAS_HALF else 'R' if d >= BIAS_HALF else d
        return kinds

    for d0 in near:
        kinds = block_kinds(d0)
        band_rows = {ri for (ri, _), kind in kinds.items() if not isinstance(kind, str)}
        sides = {kind for kind in kinds.values() if isinstance(kind, str)}

        if len(band_rows) == 1 and len(sides) == 1:
            @pl.when(off == d0)
            def _(kinds=kinds, ri=band_rows.pop(), side=sides.pop()):
                const = far_left if side == 'L' else far_right
                for h in range(DIFF_HEADS_PER_STEP):
                    for cj in range(tk // LANES):
                        kind = kinds[ri, cj]
                        corr = (jnp.zeros((LANES, LANES), F32) if isinstance(kind, str)
                                else toeplitz(tabs[h], kind) - const[h])
                        bias_scr[h, 0:LANES, cj * LANES:(cj + 1) * LANES] = corr
                update([(ri * LANES, bias_scr[h, 0:LANES, :]) for h in range(DIFF_HEADS_PER_STEP)], const)
        else:
            @pl.when(off == d0)
            def _(kinds=kinds):
                for h in range(DIFF_HEADS_PER_STEP):
                    blocks = {}
                    for (ri, cj), kind in kinds.items():
                        sl = (h, slice(ri * LANES, (ri + 1) * LANES), slice(cj * LANES, (cj + 1) * LANES))
                        if kind == 'L':
                            bias_scr[sl] = jnp.broadcast_to(far_left[h], (LANES, LANES))
                        elif kind == 'R':
                            bias_scr[sl] = jnp.broadcast_to(far_right[h], (LANES, LANES))
                        else:
                            if kind not in blocks:
                                blocks[kind] = toeplitz(tabs[h], kind)
                            bias_scr[sl] = blocks[kind]
                update([bias_scr[h] for h in range(DIFF_HEADS_PER_STEP)], [0.0] * DIFF_HEADS_PER_STEP)

    @pl.when(jnp.logical_or(off < near[0], off > near[-1]))
    def _():
        update([None] * DIFF_HEADS_PER_STEP,
               [jnp.where(off < near[0], lo, hi) for lo, hi in zip(far_left, far_right)])

    @pl.when(kj == pl.num_programs(3) - 1)
    def _():
        lam = (jnp.exp(jnp.sum(lq1_ref[...] * lk1_ref[...], axis=-1, keepdims=True))
               - jnp.exp(jnp.sum(lq2_ref[...] * lk2_ref[...], axis=-1, keepdims=True)) + lambda_init)
        for h, sl in enumerate(heads):
            a0 = acc_scr[h, 0]
            a1 = acc_scr[h, 1]
            o = a0[:, :LANES] / a0[:, LANES:] - lam * (a1[:, :LANES] / a1[:, LANES:])
            o = _rms(o, sg_ref[...], SUBLN_EPS) * (1.0 - lambda_init)
            o_ref[0, :, sl] = o.astype(o_ref.dtype)


def _diff_flash(q0, q1, k, v, tab, lq1, lk1, lq2, lk2, subln_g, lambda_init):
    b, s, hw = q0.shape
    hps = DIFF_HEADS_PER_STEP
    gw = hps * LANES
    tq = min(ATT_TQ, s)
    tk = min(ATT_TK, s)
    offs = sorted({j * tk - i * tq for i in range(s // tq) for j in range(s // tk)})
    near = tuple(d for d in offs if d - (tq - 1) < BIAS_HALF and d + tk - 1 > -BIAS_HALF)
    assert near == tuple(d for d in offs if near[0] <= d <= near[-1])
    qspec = pl.BlockSpec((1, tq, gw), lambda bi, h, i, j: (bi, i, h))
    kspec = pl.BlockSpec((1, tk, gw), lambda bi, h, i, j: (bi, j, h))
    vec = lambda n: pl.BlockSpec((1, n), lambda bi, h, i, j: (0, 0))
    return pl.pallas_call(
        functools.partial(_diff_flash_kernel, tq=tq, tk=tk, near=near, lambda_init=lambda_init),
        grid=(b, hw // gw, s // tq, s // tk),
        in_specs=[qspec, qspec, kspec, kspec,
                  pl.BlockSpec((hps, 1, 2 * BIAS_HALF), lambda bi, h, i, j: (h, 0, 0)),
                  vec(D_HEAD), vec(D_HEAD), vec(D_HEAD), vec(D_HEAD), vec(D_V)],
        out_specs=qspec,
        out_shape=jax.ShapeDtypeStruct((b, s, hw), BF16),
        scratch_shapes=[pltpu.VMEM((hps, 2, tq, LANES), F32), pltpu.VMEM((hps, 2, tq, 2 * LANES), F32),
                        pltpu.VMEM((hps, tq, tk), F32)],
        compiler_params=_params("parallel", "parallel", "parallel", "arbitrary"),
        name="diff_flash",
    )(q0, q1, k, v, tab, lq1, lk1, lq2, lk2, subln_g)


def _t5_bucket(rel):
    half = N_BUCKETS // 2
    max_exact = half // 2
    n = jnp.abs(rel)
    large = max_exact + (jnp.log(jnp.maximum(n, 1).astype(jnp.float32) / max_exact)
                         / math.log(MAX_DISTANCE / max_exact) * (half - max_exact)).astype(jnp.int32)
    large = jnp.minimum(large, half - 1)
    return jnp.where(rel > 0, half, 0) + jnp.where(n < max_exact, n, large)


def _rope_tables(s):
    inv = 1.0 / (ROPE_THETA ** (jnp.arange(0, C_ROPE, 2, dtype=F32) / C_ROPE))
    ang = jnp.arange(s, dtype=F32)[:, None] * inv[None, :]
    cos, sin = jnp.cos(ang), jnp.sin(ang)
    pad = LANES - C_NOPE - C_ROPE
    cos_t = jnp.concatenate([jnp.ones((s, C_NOPE), F32), cos, cos, jnp.zeros((s, pad), F32)], axis=-1)
    sin_t = jnp.concatenate([jnp.zeros((s, C_NOPE), F32), sin, sin, jnp.zeros((s, pad), F32)], axis=-1)
    return cos_t, sin_t


def _rot_half_cols(w):
    h = w.shape[-1] // 2
    return jnp.concatenate([-w[..., h:], w[..., :h]], axis=-1)


def _pack_even(j, even_w_in, even_w_out, rwkv_mu, rwkv_w0, rwkv_w_up, rwkv_a0, rwkv_a_up, rwkv_g_up, rwkv_k_k,
               rwkv_k_a, rwkv_r_k, rwkv_lnx_g, rwkv_lnx_b):
    w = A_WIDTH
    z = jnp.zeros((A_LORA, w), F32)
    blockdiag = lambda m: jnp.concatenate(
        [jnp.concatenate([m[0], z], axis=1), jnp.concatenate([z, m[1]], axis=1)], axis=0)
    head = jnp.arange(w) // A_HEAD
    return {
        "w_in_a": even_w_in[j][:, :A_COLS].astype(BF16),
        "w_in_b": even_w_in[j][:, A_COLS:].astype(BF16),
        "w_out_a": even_w_out[j][:w].astype(BF16),
        "w_out_b": even_w_out[j][w:].astype(BF16),
        "mu": rwkv_mu[j].reshape(1, A_COLS),
        "w0": rwkv_w0[j].reshape(1, 2 * w),
        "wup": blockdiag(rwkv_w_up[j]).astype(BF16),
        "a0": rwkv_a0[j].reshape(1, 2 * w),
        "aup": blockdiag(rwkv_a_up[j]).astype(BF16),
        "gup": rwkv_g_up[j].astype(BF16),
        "k_k": rwkv_k_k[j].reshape(1, w),
        "k_a": rwkv_k_a[j].reshape(1, w),
        "r_k": rwkv_r_k[j].reshape(1, w),
        "lnx_g": rwkv_lnx_g[j].reshape(1, w),
        "lnx_b": rwkv_lnx_b[j].reshape(1, w),
        "ones_bd": (head[:, None] == head[None, :]).astype(BF16),
    }


def _pack_odd(j, odd_w_in, odd_w_out, mla_q_norm_g, mla_kv_norm_g, mla_w_uq, mla_w_ukv):
    d = D_MODEL
    w_in = odd_w_in[j]
    o_cq, o_ckv = 0, C_Q_RANK
    o_kr = o_ckv + C_KV_RANK
    o_dq = o_kr + C_ROPE
    n_d = D_HEADS * 2 * D_HEAD
    w_kr = w_in[:, o_kr:o_kr + C_ROPE]
    packed = jnp.concatenate([
        w_in[:, o_dq:o_dq + 3 * n_d], w_in[:, o_cq:o_kr], w_kr, _rot_half_cols(w_kr),
        jnp.zeros((d, ODD_COLS - (C_Q_RANK + C_KV_RANK + 3 * n_d + 2 * C_ROPE)), F32)], axis=1)
    pad = LANES - C_NOPE - C_ROPE
    wq = mla_w_uq[j].reshape(C_Q_RANK, C_HEADS, C_NOPE + C_ROPE)
    zq = jnp.zeros((C_Q_RANK, C_HEADS, pad), F32)
    wqa = jnp.concatenate([wq, zq], axis=-1).reshape(C_Q_RANK, C_HEADS * LANES)
    wqb = jnp.concatenate([jnp.zeros((C_Q_RANK, C_HEADS, C_NOPE), F32), _rot_half_cols(wq[..., C_NOPE:]), zq],
                          axis=-1).reshape(C_Q_RANK, C_HEADS * LANES)
    wkv = mla_w_ukv[j].reshape(C_KV_RANK, C_HEADS, C_NOPE + C_V)
    wk = jnp.concatenate([wkv[..., :C_NOPE], jnp.zeros((C_KV_RANK, C_HEADS, LANES - C_NOPE), F32)],
                         axis=-1).reshape(C_KV_RANK, C_HEADS * LANES)
    wv = wkv[..., C_NOPE:].reshape(C_KV_RANK, C_HEADS * C_V)
    src = jnp.arange(LANES)[:, None]
    dst = jnp.arange(C_HEADS * LANES)[None, :] % LANES
    place = ((dst >= C_NOPE) & (dst < C_NOPE + C_ROPE) & (dst - C_NOPE == src)).astype(BF16)
    hv = C_HEADS * C_V
    return {
        "w_in": packed.astype(BF16),
        "w_out_a": odd_w_out[j][:hv].astype(BF16),
        "w_out_b": odd_w_out[j][hv:].astype(BF16),
        "q_norm_g": mla_q_norm_g[j].reshape(1, C_Q_RANK),
        "kv_norm_g": mla_kv_norm_g[j].reshape(1, C_KV_RANK),
        "wqa": wqa.astype(BF16), "wqb": wqb.astype(BF16), "wk": wk.astype(BF16), "wv": wv.astype(BF16),
        "place": place,
    }


def _even_mixers(x, g1n, sc1, sh1, ew, s5m, s5_d, s5_w_glu, s5_b_glu):
    b, s, _ = x.shape
    pa = _normmod_mm(x, g1n, sc1, sh1, ew["w_in_a"], tn=A_COLS // 3)
    u = _normmod_mm(x, g1n, sc1, sh1, ew["w_in_b"], tn=B_WIDTH)
    r, v, kk, lw, kd, bd, bv, g = _rwkv_prep(pa, ew)
    yf, yr = _rwkv_scan(r, v, kk, lw, kd, bd)
    ya = _rwkv_post(yf, yr, bv, g, ew["lnx_g"], ew["lnx_b"], ew["ones_bd"])
    nrows = s // S5_CHUNK
    u_g = jnp.transpose(u.astype(BF16).reshape(b, nrows, S5_CHUNK, B_GROUPS, B_GROUP), (0, 3, 1, 2, 4))
    u_g = u_g.reshape(b, B_GROUPS, nrows, S5_CHUNK * B_GROUP)
    ys_g = _s5_scan(u_g, s5m[1], True, prev=_s5_scan(u_g, s5m[0], False))
    ys = jnp.transpose(ys_g.reshape(b, B_GROUPS, nrows, S5_CHUNK, B_GROUP), (0, 2, 3, 1, 4)).reshape(b, s, B_WIDTH)
    yb = _s5_glu(ys, u, s5_d, s5_w_glu, s5_b_glu)
    return ya, yb


def _odd_mixers(x, g1n, sc1, sh1, ow, tabs, diff_w, lambda_init):
    s = x.shape[1]
    p = _normmod_mm(x, g1n, sc1, sh1, ow["w_in"], tn=1024)
    mq, mk, mv, q0, q1, dk, dv = _odd_prep(p, tabs["cos"][:s], tabs["sin"][:s], ow)
    yc = _mla_flash(mq, mk, mv)
    yd = _diff_flash(q0, q1, dk, dv, tabs["bias"], diff_w["lq1"], diff_w["lk1"], diff_w["lq2"], diff_w["lk2"],
                     diff_w["subln_g"], lambda_init)
    return yc, yd


def kernel(x_prompt, x_sample, c_prompt, c_sample, ada_w, ada_b, norm1_g, norm2_g, even_w_in, even_w_out, rwkv_mu, rwkv_w0, rwkv_w_up, rwkv_a0, rwkv_a_up, rwkv_g_up, rwkv_k_k, rwkv_k_a, rwkv_r_k, rwkv_lnx_g, rwkv_lnx_b, s5_lam_re, s5_lam_im, s5_log_step, s5_b_re, s5_b_im, s5_c_re, s5_c_im, s5_d, s5_w_glu, s5_b_glu, odd_w_in, odd_w_out, mla_q_norm_g, mla_kv_norm_g, mla_w_uq, mla_w_ukv, diff_lq1, diff_lk1, diff_lq2, diff_lk2, diff_subln_g, rel_bias, ffn_w_up, ffn_conv_w, ffn_conv_b, ffn_w_down, final_g):
    d = D_MODEL
    groups = [(x_prompt, c_prompt), (x_sample, c_sample)]
    nb = [g[0].shape[0] for g in groups]
    c_all = jnp.concatenate([g[1] for g in groups] + [jnp.zeros((SUBLANES - sum(nb), d), F32)], axis=0)
    mod = _ada_mod(c_all, ada_w, ada_b)

    max_s = max(g[0].shape[1] for g in groups)
    cos_t, sin_t = _rope_tables(max_s)
    rel = jnp.arange(-BIAS_HALF, BIAS_HALF, dtype=jnp.int32)
    bias_tab = (jnp.transpose(rel_bias.astype(F32)[_t5_bucket(rel)]) * LOG2E).reshape(D_HEADS, 1, 2 * BIAS_HALF)
    tabs = {"cos": cos_t, "sin": sin_t, "bias": bias_tab}

    xs = [g[0] for g in groups]
    for i in range(DEPTH):
        j = i // 2
        if i % 2 == 0:
            ew = _pack_even(j, even_w_in, even_w_out, rwkv_mu, rwkv_w0, rwkv_w_up, rwkv_a0, rwkv_a_up, rwkv_g_up,
                            rwkv_k_k, rwkv_k_a, rwkv_r_k, rwkv_lnx_g, rwkv_lnx_b)
            s5m = [_s5_matrices(s5_lam_re[j, dr], s5_lam_im[j, dr], s5_log_step[j, dr], s5_b_re[j, dr],
                                s5_b_im[j, dr], s5_c_re[j, dr], s5_c_im[j, dr], dr == 1) for dr in range(2)]
        else:
            ow = _pack_odd(j, odd_w_in, odd_w_out, mla_q_norm_g, mla_kv_norm_g, mla_w_uq, mla_w_ukv)
            diff_w = {"lq1": diff_lq1[j].reshape(1, D_HEAD), "lk1": diff_lk1[j].reshape(1, D_HEAD),
                      "lq2": diff_lq2[j].reshape(1, D_HEAD), "lk2": diff_lk2[j].reshape(1, D_HEAD),
                      "subln_g": diff_subln_g[j].reshape(1, D_V)}
        w_up = ffn_w_up[i].astype(BF16)
        w_down = ffn_w_down[i].astype(BF16)
        row0 = 0
        for gi in range(len(groups)):
            x = xs[gi]
            m = mod[i, row0:row0 + nb[gi]]
            row0 += nb[gi]
            sh1, sc1, g1, sh2, sc2, g2 = [m[:, None, k * d:(k + 1) * d] for k in range(N_MOD)]
            if i % 2 == 0:
                ya, yb = _even_mixers(x, norm1_g[i], sc1, sh1, ew, s5m, s5_d[j].reshape(1, B_WIDTH),
                                      s5_w_glu[j].astype(BF16), s5_b_glu[j].reshape(1, B_WIDTH))
                x = _out_proj(ya, yb, ew["w_out_a"], ew["w_out_b"], x, g1)
            else:
                yc, yd = _odd_mixers(x, norm1_g[i], sc1, sh1, ow, tabs, diff_w, 0.8 - 0.6 * math.exp(-0.3 * i))
                x = _out_proj(yc, yd, ow["w_out_a"], ow["w_out_b"], x, g1)
            x = _ffn(x, norm2_g[i], sc2, sh2, w_up, ffn_conv_w[i], ffn_conv_b[i].reshape(1, 2 * FFN_HIDDEN), w_down,
                     g2, final_g.reshape(1, d), final=(i == DEPTH - 1))
            xs[gi] = x
    return (xs[0], xs[1])
```
